```python
import jax, jax.numpy as jnp
from jax import lax
import numpy as np

D_MODEL = 1024
BATCH = 4
SEQ = 4096
DEPTH = 2

HEAD_DIM = 64
BLK = 128
NEG = -1e30
RMS_EPS = 1e-6
FOX_HEADS = 8
DIL_HEADS = 8
DIL_BRANCHES = ((128, 1), (512, 4), (2048, 16))
MLA_HEADS = 8
MLA_Q_RANK = 384
MLA_KV_RANK = 256
MLA_NOPE = 64
MLA_ROPE = 32
MLA_V = 64
ROPE_THETA = 10000.0
SWA_Q_HEADS = 8
SWA_KV_HEADS = 2
SWA_WINDOW = 128
N_GROUPS = 4
EXPERTS_PER_GROUP = 8
N_EXPERTS = N_GROUPS * EXPERTS_PER_GROUP
EXPERT_TOPK = 2
D_EXPERT = 256

EVEN_SPLIT = [FOX_HEADS * HEAD_DIM] * 3 + [FOX_HEADS] + [DIL_HEADS * HEAD_DIM] * 3
EVEN_IN = sum(EVEN_SPLIT)
EVEN_MIX = (FOX_HEADS + DIL_HEADS) * HEAD_DIM
ODD_SPLIT = [MLA_Q_RANK, MLA_KV_RANK, MLA_ROPE, SWA_Q_HEADS * HEAD_DIM, SWA_KV_HEADS * HEAD_DIM, SWA_KV_HEADS * HEAD_DIM]
ODD_IN = sum(ODD_SPLIT)
ODD_MIX = MLA_HEADS * MLA_V + SWA_Q_HEADS * HEAD_DIM
N_EVEN = (DEPTH + 1) // 2
N_ODD = DEPTH // 2

kernel_name = "hybrid_fox_dilated_mla_swa_hmoe"

F32 = jnp.float32


def rmsnorm(x, g):
    xf = x.astype(F32)
    y = xf * lax.rsqrt(jnp.mean(xf * xf, axis=-1, keepdims=True) + RMS_EPS)
    return (y * g.astype(F32)).astype(x.dtype)


def split_cols(z, sizes):
    idx = [int(v) for v in np.cumsum(sizes)[:-1]]
    return jnp.split(z, idx, axis=-1)


def to_heads(t, n):
    b, s, _ = t.shape
    return t.reshape(b, s, n, -1).transpose(0, 2, 1, 3)


def from_heads(o):
    b, h, s, e = o.shape
    return o.transpose(0, 2, 1, 3).reshape(b, s, h * e)


def alibi_slopes(n):
    return jnp.exp2(-8.0 * jnp.arange(1, n + 1, dtype=F32) / n)


def rope_tables(s):
    inv = ROPE_THETA ** (-jnp.arange(0, MLA_ROPE, 2, dtype=F32) / MLA_ROPE)
    ang = jnp.arange(s, dtype=F32)[:, None] * inv[None, :]
    return jnp.cos(ang), jnp.sin(ang)


def apply_rope(t, cos, sin):
    half = t.shape[-1] // 2
    t1, t2 = t[..., :half].astype(F32), t[..., half:].astype(F32)
    return jnp.concatenate([t1 * cos - t2 * sin, t1 * sin + t2 * cos], axis=-1).astype(t.dtype)


def causal_block_sweep(block_scores, v):
    b, h, s, e = v.shape
    kpos = jnp.arange(s)
    vf = v.astype(F32)

    def one(i):
        sc = block_scores(i)
        qpos = i * BLK + jnp.arange(BLK)
        sc = jnp.where(kpos[None, :] <= qpos[:, None], sc, NEG)
        p = jax.nn.softmax(sc, axis=-1)
        return jnp.einsum("bhqk,bhke->bhqe", p, vf)

    out = lax.map(one, jnp.arange(s // BLK))
    return out.transpose(1, 2, 0, 3, 4).reshape(b, h, s, e).astype(v.dtype)


def banded_attention(q, k, v, max_steps, step_dist, slopes, sinks=None):
    n, hk, g, L, e = q.shape
    nb = L // BLK
    qb = q.reshape(n, hk, g, nb, BLK, e)

    def with_prev(t):
        t = t.reshape(n, hk, nb, BLK, e)
        prev = jnp.concatenate([jnp.zeros_like(t[:, :, :1]), t[:, :, :-1]], axis=2)
        return jnp.concatenate([prev, t], axis=3)

    kc, vc = with_prev(k), with_prev(v)
    s = jnp.einsum("nhgbqe,nhbke->nhgbqk", qb, kc, preferred_element_type=F32) * (e ** -0.5)
    steps = (jnp.arange(BLK)[:, None] + BLK) - jnp.arange(2 * BLK)[None, :]
    in_band = (steps >= 0) & (steps <= max_steps)
    has_prev = (jnp.arange(nb)[:, None, None] > 0) | (jnp.arange(2 * BLK)[None, None, :] >= BLK)
    mask = in_band[None] & has_prev
    dist = (steps * step_dist).astype(F32)
    s = s - slopes.astype(F32)[None, :, :, None, None, None] * dist
    s = jnp.where(mask, s, NEG)
    m = s.max(axis=-1)
    if sinks is not None:
        sk = sinks.astype(F32)[None, :, :, None, None]
        m = jnp.maximum(m, sk)
    p = jnp.exp(s - m[..., None])
    den = p.sum(axis=-1)
    if sinks is not None:
        den = den + jnp.exp(sk - m)
    o = jnp.einsum("nhgbqk,nhbke->nhgbqe", p, vc.astype(F32)) / den[..., None]
    lse = m + jnp.log(den)
    return o.reshape(n, hk, g, L, e).astype(q.dtype), lse.reshape(n, hk, g, L)


def forgetting_attention(q, k, v, log_f):
    c = jnp.cumsum(log_f, axis=-1)
    scale = q.shape[-1] ** -0.5

    def scores(i):
        qi = lax.dynamic_slice_in_dim(q, i * BLK, BLK, axis=2)
        ci = lax.dynamic_slice_in_dim(c, i * BLK, BLK, axis=2)
        sc = jnp.einsum("bhqe,bhke->bhqk", qi, k, preferred_element_type=F32) * scale
        return sc + ci[..., None] - c[:, :, None, :]

    return causal_block_sweep(scores, v)


def dilated_branch(q, k, v, window, dil, slopes):
    b, h, s, e = q.shape
    L = s // dil
    Lp = -(-L // BLK) * BLK

    def split(t):
        t = t.reshape(b, h, L, dil, e).transpose(0, 3, 1, 2, 4).reshape(b * dil, h, L, e)
        return jnp.pad(t, ((0, 0), (0, 0), (0, Lp - L), (0, 0)))

    o, lse = banded_attention(split(q)[:, :, None], split(k), split(v), window // dil, dil, slopes[:, None])
    o = o[:, :, 0, :L].reshape(b, dil, h, L, e).transpose(0, 2, 3, 1, 4).reshape(b, h, s, e)
    lse = lse[:, :, 0, :L].reshape(b, dil, h, L).transpose(0, 2, 3, 1).reshape(b, h, s)
    return o, lse


def dilated_attention(q, k, v, slopes):
    res = [dilated_branch(q, k, v, w, d, slopes) for (w, d) in DIL_BRANCHES]
    outs = jnp.stack([r[0] for r in res]).astype(F32)
    wts = jax.nn.softmax(jnp.stack([r[1] for r in res]), axis=0)
    return jnp.einsum("rbhs,rbhse->bhse", wts, outs).astype(q.dtype)


def even_mixer(hn, w_in, b_f, w_out, dil_slopes):
    zq, zk, zv, zf, dq, dk, dv = split_cols(hn @ w_in, EVEN_SPLIT)
    log_f = jax.nn.log_sigmoid((zf + b_f).astype(F32)).transpose(0, 2, 1)
    o_a = forgetting_attention(to_heads(zq, FOX_HEADS), to_heads(zk, FOX_HEADS), to_heads(zv, FOX_HEADS), log_f)
    o_b = dilated_attention(to_heads(dq, DIL_HEADS), to_heads(dk, DIL_HEADS), to_heads(dv, DIL_HEADS), dil_slopes)
    o = jnp.concatenate([o_a, o_b.astype(o_a.dtype)], axis=1)
    return from_heads(o) @ w_out


def odd_mixer(hn, w_in, q_norm, kv_norm, w_uq, w_ukv, sinks, w_out, cos, sin, swa_slopes):
    b, s, _ = hn.shape
    c_q, c_kv, k_pe, q_s, k_s, v_s = split_cols(hn @ w_in, ODD_SPLIT)
    q = to_heads(rmsnorm(c_q, q_norm) @ w_uq, MLA_HEADS)
    q_nope, q_pe = q[..., :MLA_NOPE], apply_rope(q[..., MLA_NOPE:], cos, sin)
    kv = to_heads(rmsnorm(c_kv, kv_norm) @ w_ukv, MLA_HEADS)
    k_nope, v_c = kv[..., :MLA_NOPE], kv[..., MLA_NOPE:]
    k_pe = apply_rope(k_pe, cos, sin)
    scale = (MLA_NOPE + MLA_ROPE) ** -0.5

    def mla_scores(i):
        qn = lax.dynamic_slice_in_dim(q_nope, i * BLK, BLK, axis=2)
        qp = lax.dynamic_slice_in_dim(q_pe, i * BLK, BLK, axis=2)
        sc = jnp.einsum("bhqe,bhke->bhqk", qn, k_nope, preferred_element_type=F32)
        sc = sc + jnp.einsum("bhqr,bkr->bhqk", qp, k_pe, preferred_element_type=F32)
        return sc * scale

    o_c = causal_block_sweep(mla_scores, v_c)
    grp = SWA_Q_HEADS // SWA_KV_HEADS
    qd = to_heads(q_s, SWA_Q_HEADS).reshape(b, SWA_KV_HEADS, grp, s, HEAD_DIM)
    o_d, _ = banded_attention(qd, to_heads(k_s, SWA_KV_HEADS), to_heads(v_s, SWA_KV_HEADS),
                              SWA_WINDOW - 1, 1, swa_slopes, sinks.reshape(SWA_KV_HEADS, grp))
    o_d = o_d.reshape(b, SWA_Q_HEADS, s, HEAD_DIM)
    o = jnp.concatenate([o_c, o_d.astype(o_c.dtype)], axis=1)
    return from_heads(o) @ w_out


def hierarchical_moe(hn, w_group, b_group, w_router, b_router, w_gate, w_up, w_down):
    b, s, d = hn.shape
    t = hn.reshape(-1, d)
    g_logits = (t @ w_group + b_group).astype(F32)
    g_prob = jax.nn.softmax(g_logits, axis=-1)
    g_idx = jnp.argmax(g_logits, axis=-1)
    g_w = jnp.take_along_axis(g_prob, g_idx[:, None], axis=-1)[:, 0]
    e_logits = (t @ w_router + b_router).astype(F32).reshape(-1, N_GROUPS, EXPERTS_PER_GROUP)
    e_sel = jnp.take_along_axis(e_logits, g_idx[:, None, None], axis=1)[:, 0]
    top_v, top_i = lax.top_k(e_sel, EXPERT_TOPK)
    top_w = jax.nn.softmax(top_v, axis=-1) * g_w[:, None]
    expert_id = g_idx[:, None] * EXPERTS_PER_GROUP + top_i
    gates = jnp.sum(jax.nn.one_hot(expert_id, N_EXPERTS, dtype=F32) * top_w[..., None], axis=1)
    gates = gates.reshape(-1, N_GROUPS, EXPERTS_PER_GROUP)
    y = jnp.zeros((t.shape[0], d), F32)
    for grp in range(N_GROUPS):
        a = jnp.einsum("nd,edf->nef", t, w_gate[grp])
        u = jnp.einsum("nd,edf->nef", t, w_up[grp])
        act = jax.nn.silu(a) * u * gates[:, grp, :, None]
        y = y + jnp.einsum("nef,efd->nd", act, w_down[grp])
    return y.reshape(b, s, d).astype(hn.dtype)


def setup_inputs(seed: int = 0) -> dict:
    key = jax.random.key(seed)
    ks = jax.random.split(key, 24)

    def nrm(k, shape, scale):
        return scale * jax.random.normal(k, shape, F32)

    def gain(k, shape):
        return 1.0 + 0.01 * jax.random.normal(k, shape, F32)

    return {
        "x": nrm(ks[0], (BATCH, SEQ, D_MODEL), 1.0),
        "attn_norm": gain(ks[1], (DEPTH, D_MODEL)),
        "ffn_norm": gain(ks[2], (DEPTH, D_MODEL)),
        "final_norm": gain(ks[3], (D_MODEL,)),
        "e_w_in": nrm(ks[4], (N_EVEN, D_MODEL, EVEN_IN), D_MODEL ** -0.5),
        "e_b_f": jax.random.uniform(ks[5], (N_EVEN, FOX_HEADS), F32, 1.0, 4.0),
        "e_w_out": nrm(ks[6], (N_EVEN, EVEN_MIX, D_MODEL), EVEN_MIX ** -0.5),
        "o_w_in": nrm(ks[7], (N_ODD, D_MODEL, ODD_IN), D_MODEL ** -0.5),
        "o_q_norm": gain(ks[8], (N_ODD, MLA_Q_RANK)),
        "o_kv_norm": gain(ks[9], (N_ODD, MLA_KV_RANK)),
        "o_w_uq": nrm(ks[10], (N_ODD, MLA_Q_RANK, MLA_HEADS * (MLA_NOPE + MLA_ROPE)), MLA_Q_RANK ** -0.5),
        "o_w_ukv": nrm(ks[11], (N_ODD, MLA_KV_RANK, MLA_HEADS * (MLA_NOPE + MLA_V)), MLA_KV_RANK ** -0.5),
        "o_sinks": nrm(ks[12], (N_ODD, SWA_Q_HEADS), 1.0),
        "o_w_out": nrm(ks[13], (N_ODD, ODD_MIX, D_MODEL), ODD_MIX ** -0.5),
        "moe_w_group": nrm(ks[14], (DEPTH, D_MODEL, N_GROUPS), D_MODEL ** -0.5),
        "moe_b_group": nrm(ks[15], (DEPTH, N_GROUPS), 0.01),
        "moe_w_router": nrm(ks[16], (DEPTH, D_MODEL, N_EXPERTS), D_MODEL ** -0.5),
        "moe_b_router": nrm(ks[17], (DEPTH, N_EXPERTS), 0.01),
        "moe_w_gate": nrm(ks[18], (DEPTH, N_GROUPS, EXPERTS_PER_GROUP, D_MODEL, D_EXPERT), D_MODEL ** -0.5),
        "moe_w_up": nrm(ks[19], (DEPTH, N_GROUPS, EXPERTS_PER_GROUP, D_MODEL, D_EXPERT), D_MODEL ** -0.5),
        "moe_w_down": nrm(ks[20], (DEPTH, N_GROUPS, EXPERTS_PER_GROUP, D_EXPERT, D_MODEL), D_EXPERT ** -0.5),
    }


def reference(x, attn_norm, ffn_norm, final_norm, e_w_in, e_b_f, e_w_out, o_w_in, o_q_norm, o_kv_norm,
              o_w_uq, o_w_ukv, o_sinks, o_w_out, moe_w_group, moe_b_group, moe_w_router, moe_b_router,
              moe_w_gate, moe_w_up, moe_w_down):
    s = x.shape[1]
    cos, sin = rope_tables(s)
    dil_slopes = alibi_slopes(DIL_HEADS)
    swa_slopes = alibi_slopes(SWA_Q_HEADS).reshape(SWA_KV_HEADS, SWA_Q_HEADS // SWA_KV_HEADS)
    h = x
    for layer in range(DEPTH):
        hn = rmsnorm(h, attn_norm[layer])
        if layer % 2 == 0:
            i = layer // 2
            h = h + even_mixer(hn, e_w_in[i], e_b_f[i], e_w_out[i], dil_slopes)
        else:
            i = layer // 2
            h = h + odd_mixer(hn, o_w_in[i], o_q_norm[i], o_kv_norm[i], o_w_uq[i], o_w_ukv[i], o_sinks[i],
                              o_w_out[i], cos, sin, swa_slopes)
        h = h + hierarchical_moe(rmsnorm(h, ffn_norm[layer]), moe_w_group[layer], moe_b_group[layer],
                                 moe_w_router[layer], moe_b_router[layer], moe_w_gate[layer],
                                 moe_w_up[layer], moe_w_down[layer])
    return rmsnorm(h, final_norm)
```

```python
import functools

import numpy as np
import jax
import jax.numpy as jnp
from jax import lax
from jax.experimental import pallas as pl
from jax.experimental.pallas import tpu as pltpu

F32 = jnp.float32
BF16 = jnp.bfloat16

D_MODEL = 1024
HEAD_DIM = 64
BLK = 128
NEG = -1e30
RMS_EPS = 1e-6
FOX_HEADS = 8
DIL_HEADS = 8
DIL_BRANCHES = ((128, 1), (512, 4), (2048, 16))
MLA_HEADS = 8
MLA_Q_RANK = 384
MLA_KV_RANK = 256
MLA_NOPE = 64
MLA_ROPE = 32
MLA_V = 64
ROPE_THETA = 10000.0
SWA_Q_HEADS = 8
SWA_KV_HEADS = 2
SWA_WINDOW = 128
N_GROUPS = 4
EXPERTS_PER_GROUP = 8
N_EXPERTS = N_GROUPS * EXPERTS_PER_GROUP
D_EXPERT = 256

LANES = 128
VMEM_LIMIT = 56 * 1024 * 1024

NT_DIMS = (((1,), (1,)), ((), ()))


def _params(*sem):
    return pltpu.CompilerParams(dimension_semantics=sem, vmem_limit_bytes=VMEM_LIMIT)


def _dot(a, b):
    return jnp.dot(a, b, preferred_element_type=F32)


def _rms(x, g):
    return x * lax.rsqrt(jnp.mean(x * x, axis=-1, keepdims=True) + RMS_EPS) * g


def _hi_lo(w):
    hi = w.astype(BF16)
    return hi, (w - hi.astype(F32)).astype(BF16)


def _dot_hi(x, w_hi, w_lo):
    x_hi = x.astype(BF16)
    x_lo = (x - x_hi.astype(F32)).astype(BF16)
    return _dot(x_hi, w_hi) + (_dot(x_hi, w_lo) + _dot(x_lo, w_hi))


def _proj_even_kernel(x_ref, g_ref, w_ref, wf_hi_ref, wf_lo_ref, bf_ref, out_ref, logf_ref):
    xn = _rms(x_ref[...], g_ref[...])
    xb = xn.astype(BF16)
    n_out = out_ref.shape[1]
    for c in range(0, n_out, 512):
        out_ref[:, c:c + 512] = _dot(xb, w_ref[:, c:c + 512]).astype(out_ref.dtype)
    z = _dot_hi(xn, wf_hi_ref[...], wf_lo_ref[...]) + bf_ref[...]
    logf_ref[...] = jnp.minimum(z, 0.0) - jnp.log1p(jnp.exp(-jnp.abs(z)))


def _proj_even(h, g, w, wf_hi, wf_lo, bf, tm=512):
    n, d = h.shape
    n_out = w.shape[1]
    return pl.pallas_call(
        _proj_even_kernel,
        grid=(n // tm,),
        in_specs=[
            pl.BlockSpec((tm, d), lambda i: (i, 0)),
            pl.BlockSpec((1, d), lambda i: (0, 0)),
            pl.BlockSpec((d, n_out), lambda i: (0, 0)),
            pl.BlockSpec((d, LANES), lambda i: (0, 0)),
            pl.BlockSpec((d, LANES), lambda i: (0, 0)),
            pl.BlockSpec((1, LANES), lambda i: (0, 0)),
        ],
        out_specs=[
            pl.BlockSpec((tm, n_out), lambda i: (i, 0)),
            pl.BlockSpec((tm, LANES), lambda i: (i, 0)),
        ],
        out_shape=[
            jax.ShapeDtypeStruct((n, n_out), BF16),
            jax.ShapeDtypeStruct((n, LANES), F32),
        ],
        compiler_params=_params("parallel"),
        name="proj_even",
    )(h, g, w, wf_hi, wf_lo, bf)


def _split3(x):
    x1 = x.astype(BF16)
    r1 = x - x1.astype(F32)
    x2 = r1.astype(BF16)
    x3 = (r1 - x2.astype(F32)).astype(BF16)
    return x1, x2, x3


def _cumsum_kernel(x_ref, o_ref, *, rows_per_seq):
    x = x_ref[...]
    r = x.shape[0]
    ki = lax.broadcasted_iota(jnp.int32, (LANES, LANES), 0)
    ji = lax.broadcasted_iota(jnp.int32, (LANES, LANES), 1)
    upper = jnp.where(ki <= ji, 1.0, 0.0).astype(BF16)
    x1, x2, x3 = _split3(x)
    y = _dot(x1, upper) + (_dot(x2, upper) + _dot(x3, upper))
    ri = lax.broadcasted_iota(jnp.int32, (r, r), 0)
    ci = lax.broadcasted_iota(jnp.int32, (r, r), 1)
    same = (ri // rows_per_seq) == (ci // rows_per_seq)
    prev = jnp.where(same & (ci < ri), 1.0, 0.0).astype(BF16)
    y1, y2, y3 = _split3(y)
    z = _dot(prev, y1) + (_dot(prev, y2) + _dot(prev, y3))
    o_ref[...] = y + z[:, LANES - 1:LANES]


def _cumsum_rows(x, rows_per_seq):
    r = x.shape[0]
    return pl.pallas_call(
        functools.partial(_cumsum_kernel, rows_per_seq=rows_per_seq),
        grid=(1,),
        in_specs=[pl.BlockSpec((r, LANES), lambda i: (0, 0))],
        out_specs=pl.BlockSpec((r, LANES), lambda i: (0, 0)),
        out_shape=jax.ShapeDtypeStruct((r, LANES), F32),
        compiler_params=_params("arbitrary"),
        name="logf_cumsum",
    )(x)


def _causal_kernel(*refs, dqk, tq, tk, has_decay):
    if has_decay:
        q_ref, k_ref, v_ref, c_ref, o_ref = refs
    else:
        q_ref, k_ref, v_ref, o_ref = refs
        c_ref = None
    qi = pl.program_id(2)
    row = lax.broadcasted_iota(jnp.int32, (tq, tk), 0)
    col = lax.broadcasted_iota(jnp.int32, (tq, tk), 1)
    outs = []
    for h in range(2):
        q = q_ref[0, :, h * dqk:(h + 1) * dqk]

        def block(j, carry, masked, q=q, h=h):
            m, l, acc = carry
            start = pl.multiple_of(j * tk, tk)
            kj = k_ref[0, pl.ds(start, tk), h * dqk:(h + 1) * dqk]
            vj = v_ref[0, pl.ds(start, tk), h * HEAD_DIM:(h + 1) * HEAD_DIM]
            s = lax.dot_general(q, kj, NT_DIMS, preferred_element_type=F32)
            if c_ref is not None:
                s = s - c_ref[0, h, j]
            if masked:
                s = jnp.where(col <= row, s, NEG)
            m_new = jnp.maximum(m, jnp.max(s, axis=-1, keepdims=True))
            p = jnp.exp(s - m_new)
            alpha = jnp.exp(m - m_new)
            l = alpha * l + jnp.sum(p, axis=-1, keepdims=True)
            acc = alpha * acc + _dot(p.astype(BF16), vj)
            return m_new, l, acc

        carry = (jnp.full((tq, 1), NEG, F32), jnp.zeros((tq, 1), F32), jnp.zeros((tq, HEAD_DIM), F32))
        carry = lax.fori_loop(0, qi, lambda j, c, block=block: block(j, c, False), carry)
        _, l, acc = block(qi, carry, True)
        outs.append(acc / l)
    o_ref[0] = jnp.concatenate(outs, axis=-1).astype(o_ref.dtype)


def _causal_attention(q_arr, k_arr, v_arr, c_arr, *, dqk, q_blk0, k_blk0, v_blk0, n_pairs, tq=256):
    b, s, _ = q_arr.shape
    tk = tq
    in_specs = [
        pl.BlockSpec((1, tq, 2 * dqk), lambda bi, p, qi: (bi, qi, q_blk0 + p)),
        pl.BlockSpec((1, s, 2 * dqk), lambda bi, p, qi: (bi, 0, k_blk0 + p)),
        pl.BlockSpec((1, s, 2 * HEAD_DIM), lambda bi, p, qi: (bi, 0, v_blk0 + p)),
    ]
    args = [q_arr, k_arr, v_arr]
    if c_arr is not None:
        in_specs.append(pl.BlockSpec((1, 2, s // tk, 1, tk), lambda bi, p, qi: (bi, p, 0, 0, 0)))
        args.append(c_arr)
    return pl.pallas_call(
        functools.partial(_causal_kernel, dqk=dqk, tq=tq, tk=tk, has_decay=c_arr is not None),
        grid=(b, n_pairs, s // tq),
        in_specs=in_specs,
        out_specs=pl.BlockSpec((1, tq, 2 * HEAD_DIM), lambda bi, p, qi: (bi, qi, p)),
        out_shape=jax.ShapeDtypeStruct((b, s, n_pairs * 2 * HEAD_DIM), BF16),
        compiler_params=_params("parallel", "parallel", "arbitrary"),
        name="causal_attention",
    )(*args)


def _pair_cols(a, b):
    rows = a.shape[0]
    return jnp.concatenate([jnp.broadcast_to(a, (rows, HEAD_DIM)), jnp.broadcast_to(b, (rows, HEAD_DIM))], axis=-1)


def _dilated_kernel(q_ref, k_ref, v_ref, bias_ref, o_ref, qf, kf, vf, acc_s, m_s, l_s, *, seq):
    qf[...] = q_ref[0].astype(F32)
    kf[...] = k_ref[0].astype(F32)
    vf[...] = v_ref[0].astype(F32)

    def band_block(br, dil, q_start, k_start, n_keys):
        q2 = qf[pl.ds(q_start, BLK, stride=dil), :]
        k2 = kf[pl.ds(k_start, n_keys, stride=dil), :]
        v2 = vf[pl.ds(k_start, n_keys, stride=dil), :]
        ms, ls, accs = [], [], []
        for h in range(2):
            lo, hi = h * HEAD_DIM, (h + 1) * HEAD_DIM
            s = lax.dot_general(q2[:, lo:hi].astype(BF16), k2[:, lo:hi].astype(BF16), NT_DIMS,
                                preferred_element_type=F32)
            s = s + bias_ref[br, h, :, 2 * BLK - n_keys:]
            m = jnp.max(s, axis=-1, keepdims=True)
            p = jnp.exp(s - m)
            ms.append(m)
            ls.append(jnp.sum(p, axis=-1, keepdims=True))
            accs.append(_dot(p.astype(BF16), v2[:, lo:hi].astype(BF16)))
        return _pair_cols(*ms), _pair_cols(*ls), jnp.concatenate(accs, axis=-1)

    def merge(dil, q_start, m_b, l_b, acc_b, first):
        idx = pl.ds(q_start, BLK, stride=dil)
        if first:
            m_s[idx, :] = m_b
            l_s[idx, :] = l_b
            acc_s[idx, :] = acc_b
            return
        m_old = m_s[idx, :]
        m_new = jnp.maximum(m_old, m_b)
        a_old = jnp.exp(m_old - m_new)
        a_b = jnp.exp(m_b - m_new)
        m_s[idx, :] = m_new
        l_s[idx, :] = a_old * l_s[idx, :] + a_b * l_b
        acc_s[idx, :] = a_old * acc_s[idx, :] + a_b * acc_b

    for br, (_, dil) in enumerate(DIL_BRANCHES):
        n_blocks = seq // (BLK * dil)
        first = br == 0

        def per_class(r, carry, br=br, dil=dil, n_blocks=n_blocks, first=first):
            merge(dil, r, *band_block(br, dil, r, r, BLK), first)

            def per_block(bi, c):
                q_start = r + dil * BLK * bi
                merge(dil, q_start, *band_block(br, dil, q_start, q_start - dil * BLK, 2 * BLK), first)
                return c

            return lax.fori_loop(1, n_blocks, per_block, carry)

        lax.fori_loop(0, dil, per_class, 0)

    o_ref[0] = (acc_s[...] / l_s[...]).astype(o_ref.dtype)


def _dilated_attention(proj, bias, *, q_blk0, k_blk0, v_blk0):
    b, s, _ = proj.shape
    n_pairs = DIL_HEADS // 2
    blk = lambda off: pl.BlockSpec((1, s, LANES), lambda bi, p: (bi, 0, off + p))
    return pl.pallas_call(
        functools.partial(_dilated_kernel, seq=s),
        grid=(b, n_pairs),
        in_specs=[blk(q_blk0), blk(k_blk0), blk(v_blk0),
                  pl.BlockSpec((len(DIL_BRANCHES), 2, BLK, 2 * BLK), lambda bi, p: (0, p, 0, 0))],
        out_specs=pl.BlockSpec((1, s, LANES), lambda bi, p: (bi, 0, p)),
        out_shape=jax.ShapeDtypeStruct((b, s, n_pairs * LANES), BF16),
        scratch_shapes=[pltpu.VMEM((s, LANES), F32) for _ in range(6)],
        compiler_params=_params("parallel", "parallel"),
        name="dilated_attention",
    )(proj, proj, proj, bias)


def _swa_kernel(q_ref, k_ref, v_ref, bias_ref, sink_ref, o_ref, *, tc):
    ci = pl.program_id(1)
    grp = SWA_Q_HEADS // SWA_KV_HEADS
    col = lax.broadcasted_iota(jnp.int32, (grp * BLK, 2 * BLK), 1)
    for bb in range(tc // BLK):
        own = pl.multiple_of(ci * tc + bb * BLK, BLK)
        prev = pl.multiple_of(jnp.maximum(own - BLK, 0), BLK)
        no_prev = jnp.where(own == 0, NEG, 0.0).astype(F32)
        pieces = []
        for kv in range(SWA_KV_HEADS):
            lo, hi = kv * HEAD_DIM, (kv + 1) * HEAD_DIM
            k2 = jnp.concatenate([k_ref[0, pl.ds(prev, BLK), lo:hi], k_ref[0, pl.ds(own, BLK), lo:hi]], axis=0)
            v2 = jnp.concatenate([v_ref[0, pl.ds(prev, BLK), lo:hi], v_ref[0, pl.ds(own, BLK), lo:hi]], axis=0)
            q4 = jnp.concatenate(
                [q_ref[0, bb * BLK:(bb + 1) * BLK, (kv * grp + g) * HEAD_DIM:(kv * grp + g + 1) * HEAD_DIM]
                 for g in range(grp)], axis=0)
            s = lax.dot_general(q4, k2, NT_DIMS, preferred_element_type=F32)
            s = s + bias_ref[kv] + jnp.where(col < BLK, no_prev, 0.0)
            sink = sink_ref[kv]
            m = jnp.maximum(jnp.max(s, axis=-1, keepdims=True), sink)
            p = jnp.exp(s - m)
            den = jnp.sum(p, axis=-1, keepdims=True) + jnp.exp(sink - m)
            o = _dot(p.astype(BF16), v2) / den
            pieces += [o[g * BLK:(g + 1) * BLK] for g in range(grp)]
        o_ref[0, bb * BLK:(bb + 1) * BLK, :] = jnp.concatenate(pieces, axis=-1).astype(o_ref.dtype)


def _swa_attention(proj, bias, sink_col, *, q_blk0, k_blk0, v_blk0, tc=512):
    b, s, _ = proj.shape
    qw = SWA_Q_HEADS * HEAD_DIM
    return pl.pallas_call(
        functools.partial(_swa_kernel, tc=tc),
        grid=(b, s // tc),
        in_specs=[
            pl.BlockSpec((1, tc, qw), lambda bi, ci: (bi, ci, q_blk0 * LANES // qw)),
            pl.BlockSpec((1, s, LANES), lambda bi, ci: (bi, 0, k_blk0)),
            pl.BlockSpec((1, s, LANES), lambda bi, ci: (bi, 0, v_blk0)),
            pl.BlockSpec(bias.shape, lambda bi, ci: (0, 0, 0)),
            pl.BlockSpec(sink_col.shape, lambda bi, ci: (0, 0, 0)),
        ],
        out_specs=pl.BlockSpec((1, tc, qw), lambda bi, ci: (bi, ci, 0)),
        out_shape=jax.ShapeDtypeStruct((b, s, qw), BF16),
        compiler_params=_params("parallel", "parallel"),
        name="swa_attention",
    )(proj, proj, proj, bias, sink_col)


def _rope128(x, a, bm, bp):
    return x * a + pltpu.roll(x, LANES - 16, axis=1) * bm + pltpu.roll(x, 16, axis=1) * bp


def _proj_odd_kernel(x_ref, g_ref, w_ref, qn_ref, kvn_ref, wuq_ref, wuk_ref, wuv_ref, a_ref, bm_ref, bp_ref,
                     q_out, k_out, v_out, swa_out):
    xb = _rms(x_ref[...], g_ref[...]).astype(BF16)
    w = w_ref
    c_q = _dot(xb, w[:, 0:MLA_Q_RANK])
    c_kv = _dot(xb, w[:, MLA_Q_RANK:MLA_Q_RANK + MLA_KV_RANK])
    o1 = MLA_Q_RANK + MLA_KV_RANK
    kpe = _dot(xb, w[:, o1:o1 + LANES])
    o2 = o1 + LANES
    swa_out[...] = _dot(xb, w[:, o2:]).astype(swa_out.dtype)
    a, bm, bp = a_ref[...], bm_ref[...], bp_ref[...]
    kpe = _rope128(kpe, a, bm, bp)
    cqn = _rms(c_q, qn_ref[...]).astype(BF16)
    ckvn = _rms(c_kv, kvn_ref[...]).astype(BF16)
    v_out[...] = _dot(ckvn, wuv_ref[...]).astype(v_out.dtype)
    for h in range(MLA_HEADS):
        sl = slice(h * LANES, (h + 1) * LANES)
        q_out[:, sl] = _rope128(_dot(cqn, wuq_ref[:, sl]), a, bm, bp).astype(q_out.dtype)
        k_out[:, sl] = (_dot(ckvn, wuk_ref[:, sl]) + kpe).astype(k_out.dtype)


def _proj_odd(h, g, w, qn, kvn, wuq, wuk, wuv, rope_a, rope_bm, rope_bp, seq, tm=512):
    n, d = h.shape
    n_in = w.shape[1]
    n_swa = n_in - (MLA_Q_RANK + MLA_KV_RANK + LANES)
    tiles_per_seq = seq // tm
    full = lambda shape: pl.BlockSpec(shape, lambda i: (0, 0))
    rope = pl.BlockSpec((tm, LANES), lambda i: (i % tiles_per_seq, 0))
    return pl.pallas_call(
        _proj_odd_kernel,
        grid=(n // tm,),
        in_specs=[
            pl.BlockSpec((tm, d), lambda i: (i, 0)), full((1, d)), full(w.shape),
            full(qn.shape), full(kvn.shape), full(wuq.shape), full(wuk.shape), full(wuv.shape),
            rope, rope, rope,
        ],
        out_specs=[
            pl.BlockSpec((tm, MLA_HEADS * LANES), lambda i: (i, 0)),
            pl.BlockSpec((tm, MLA_HEADS * LANES), lambda i: (i, 0)),
            pl.BlockSpec((tm, MLA_HEADS * MLA_V), lambda i: (i, 0)),
            pl.BlockSpec((tm, n_swa), lambda i: (i, 0)),
        ],
        out_shape=[
            jax.ShapeDtypeStruct((n, MLA_HEADS * LANES), BF16),
            jax.ShapeDtypeStruct((n, MLA_HEADS * LANES), BF16),
            jax.ShapeDtypeStruct((n, MLA_HEADS * MLA_V), BF16),
            jax.ShapeDtypeStruct((n, n_swa), BF16),
        ],
        compiler_params=_params("parallel"),
        name="proj_odd",
    )(h, g, w, qn, kvn, wuq, wuk, wuv, rope_a, rope_bm, rope_bp)


def _out_router_kernel(h_ref, oa_ref, ob_ref, wo_ref, g_ref, wr_hi_ref, wr_lo_ref, br_ref,
                       h_out, hn_out, gates_out):
    half = oa_ref.shape[1]
    h = h_ref[...] + _dot(oa_ref[...], wo_ref[0:half, :]) + _dot(ob_ref[...], wo_ref[half:, :])
    h_out[...] = h
    hn = _rms(h, g_ref[...])
    hn_out[...] = hn.astype(hn_out.dtype)
    z = _dot_hi(hn, wr_hi_ref[...], wr_lo_ref[...]) + br_ref[...]
    lane = lax.broadcasted_iota(jnp.int32, z.shape, 1)
    big = jnp.int32(LANES)
    is_grp = (lane >= N_EXPERTS) & (lane < N_EXPERTS + N_GROUPS)
    zg = jnp.where(is_grp, z, -jnp.inf)
    g_max = jnp.max(zg, axis=-1, keepdims=True)
    g_w = 1.0 / jnp.sum(jnp.exp(zg - g_max), axis=-1, keepdims=True)
    g_idx = jnp.min(jnp.where(zg == g_max, lane - N_EXPERTS, big), axis=-1, keepdims=True)
    in_grp = (lane < N_EXPERTS) & ((lane // EXPERTS_PER_GROUP) == g_idx)
    ze = jnp.where(in_grp, z, -jnp.inf)
    v1 = jnp.max(ze, axis=-1, keepdims=True)
    i1 = jnp.min(jnp.where(ze == v1, lane, big), axis=-1, keepdims=True)
    ze2 = jnp.where(lane == i1, -jnp.inf, ze)
    v2 = jnp.max(ze2, axis=-1, keepdims=True)
    i2 = jnp.min(jnp.where(ze2 == v2, lane, big), axis=-1, keepdims=True)
    e2 = jnp.exp(v2 - v1)
    w1 = g_w / (1.0 + e2)
    w2 = g_w * e2 / (1.0 + e2)
    gates_out[...] = jnp.where(lane == i1, w1, 0.0) + jnp.where(lane == i2, w2, 0.0)


def _out_router(h, oa, ob, wo, g, wr_hi, wr_lo, br, tm=512):
    n, d = h.shape
    half = oa.shape[1]
    full = lambda shape: pl.BlockSpec(shape, lambda i: (0, 0))
    tile = lambda w: pl.BlockSpec((tm, w), lambda i: (i, 0))
    return pl.pallas_call(
        _out_router_kernel,
        grid=(n // tm,),
        in_specs=[tile(d), tile(half), tile(half), full(wo.shape), full((1, d)),
                  full(wr_hi.shape), full(wr_lo.shape), full((1, LANES))],
        out_specs=[tile(d), tile(d), tile(LANES)],
        out_shape=[
            jax.ShapeDtypeStruct((n, d), F32),
            jax.ShapeDtypeStruct((n, d), BF16),
            jax.ShapeDtypeStruct((n, LANES), F32),
        ],
        compiler_params=_params("parallel"),
        name="out_router",
    )(h, oa, ob, wo, g, wr_hi, wr_lo, br)


def _moe_kernel(hn_ref, gates_ref, h_ref, wgu_ref, wd_ref, fg_ref, o_ref, acc_ref, *, final_norm):
    e = pl.program_id(1)

    @pl.when(e == 0)
    def _():
        acc_ref[...] = jnp.zeros_like(acc_ref)

    gates = gates_ref[...]
    lane = lax.broadcasted_iota(jnp.int32, gates.shape, 1)
    gate = jnp.sum(jnp.where(lane == e, gates, 0.0), axis=-1, keepdims=True)
    au = _dot(hn_ref[...], wgu_ref[0])
    a, u = au[:, :D_EXPERT], au[:, D_EXPERT:]
    act = (a * jax.nn.sigmoid(a)) * u * gate
    acc_ref[...] += _dot(act.astype(BF16), wd_ref[0])

    @pl.when(e == pl.num_programs(1) - 1)
    def _():
        h = h_ref[...] + acc_ref[...]
        o_ref[...] = _rms(h, fg_ref[...]) if final_norm else h


def _moe(hn, gates, h, wgu, wd, fg, final_norm, tm=1024):
    n, d = h.shape
    ne = wgu.shape[0]
    return pl.pallas_call(
        functools.partial(_moe_kernel, final_norm=final_norm),
        grid=(n // tm, ne),
        in_specs=[
            pl.BlockSpec((tm, d), lambda i, e: (i, 0)),
            pl.BlockSpec((tm, LANES), lambda i, e: (i, 0)),
            pl.BlockSpec((tm, d), lambda i, e: (i, 0)),
            pl.BlockSpec((1, d, 2 * D_EXPERT), lambda i, e: (e, 0, 0)),
            pl.BlockSpec((1, D_EXPERT, d), lambda i, e: (e, 0, 0)),
            pl.BlockSpec((1, d), lambda i, e: (0, 0)),
        ],
        out_specs=pl.BlockSpec((tm, d), lambda i, e: (i, 0)),
        out_shape=jax.ShapeDtypeStruct((n, d), F32),
        scratch_shapes=[pltpu.VMEM((tm, d), F32)],
        compiler_params=_params("parallel", "arbitrary"),
        name="moe_experts",
    )(hn, gates, h, wgu, wd, fg)


def _alibi_slopes(n):
    return jnp.exp2(-8.0 * jnp.arange(1, n + 1, dtype=F32) / n)


def _band_bias(slopes, max_steps, step_dist):
    steps = (jnp.arange(BLK)[:, None] + BLK) - jnp.arange(2 * BLK)[None, :]
    in_band = (steps >= 0) & (steps <= max_steps)
    dist = (steps * step_dist).astype(F32)
    return jnp.where(in_band[None], -slopes.astype(F32)[:, None, None] * dist[None], NEG)


def _rope_tables(s):
    inv = ROPE_THETA ** (-jnp.arange(0, MLA_ROPE, 2, dtype=F32) / MLA_ROPE)
    ang = jnp.arange(s, dtype=F32)[:, None] * inv[None, :]
    cos, sin = jnp.cos(ang), jnp.sin(ang)
    half = MLA_ROPE // 2
    zeros_tail = jnp.zeros((s, LANES - MLA_NOPE - MLA_ROPE), F32)
    a = jnp.concatenate([jnp.ones((s, MLA_NOPE), F32), cos, cos, zeros_tail], axis=1)
    zeros_nope = jnp.zeros((s, MLA_NOPE), F32)
    zeros_half = jnp.zeros((s, half), F32)
    bm = jnp.concatenate([zeros_nope, -sin, zeros_half, zeros_tail], axis=1)
    bp = jnp.concatenate([zeros_nope, zeros_half, sin, zeros_tail], axis=1)
    return a, bm, bp


def _pad_cols(w, width):
    return jnp.pad(w, ((0, 0), (0, width - w.shape[1])))


def _router_weights(w_group, b_group, w_router, b_router):
    w = _pad_cols(jnp.concatenate([w_router, w_group], axis=1), LANES)
    b = _pad_cols(jnp.concatenate([b_router, b_group])[None, :], LANES)
    hi, lo = _hi_lo(w)
    return hi, lo, b


def _moe_weights(w_gate, w_up, w_down):
    d = w_gate.shape[-2]
    wgu = jnp.concatenate([w_gate.reshape(N_EXPERTS, d, D_EXPERT), w_up.reshape(N_EXPERTS, d, D_EXPERT)], axis=-1)
    return wgu.astype(BF16), w_down.reshape(N_EXPERTS, D_EXPERT, d).astype(BF16)


def kernel(x, attn_norm, ffn_norm, final_norm, e_w_in, e_b_f, e_w_out, o_w_in, o_q_norm, o_kv_norm, o_w_uq,
           o_w_ukv, o_sinks, o_w_out, moe_w_group, moe_b_group, moe_w_router, moe_b_router, moe_w_gate,
           moe_w_up, moe_w_down):
    b, s, d = x.shape
    n = b * s
    depth = attn_norm.shape[0]
    assert s % (BLK * DIL_BRANCHES[-1][1]) == 0 and d == D_MODEL
    h = x.reshape(n, d)
    tq = 256

    dil_slopes = _alibi_slopes(DIL_HEADS)
    dil_bias = jnp.stack([_band_bias(dil_slopes, w // dl, dl) for (w, dl) in DIL_BRANCHES])
    swa_bias = _band_bias(_alibi_slopes(SWA_Q_HEADS), SWA_WINDOW - 1, 1)
    swa_bias = swa_bias.reshape(SWA_KV_HEADS, (SWA_Q_HEADS // SWA_KV_HEADS) * BLK, 2 * BLK)
    rope_a, rope_bm, rope_bp = _rope_tables(s)

    for layer in range(depth):
        i = layer // 2
        g_attn = attn_norm[layer][None, :]
        if layer % 2 == 0:
            w_in = e_w_in[i]
            hq = FOX_HEADS * HEAD_DIM
            scale = HEAD_DIM ** -0.5
            cols = [w_in[:, 0:hq] * scale, w_in[:, hq:2 * hq], w_in[:, 2 * hq:3 * hq]]
            o = 3 * hq + FOX_HEADS
            cols += [w_in[:, o:o + hq] * scale, w_in[:, o + hq:o + 2 * hq], w_in[:, o + 2 * hq:o + 3 * hq]]
            w_main = jnp.concatenate(cols, axis=1).astype(BF16)
            wf_hi, wf_lo = _hi_lo(_pad_cols(w_in[:, 3 * hq:o], LANES))
            b_f = _pad_cols(e_b_f[i][None, :], LANES)
            proj, logf = _proj_even(h, g_attn, w_main, wf_hi, wf_lo, b_f)
            lf = logf[:, :FOX_HEADS].reshape(b, s, FOX_HEADS).transpose(0, 2, 1)
            c = _cumsum_rows(lf.reshape(b * FOX_HEADS * (s // LANES), LANES), s // LANES)
            c = c.reshape(b, FOX_HEADS, s // tq, 1, tq)
            proj = proj.reshape(b, s, -1)
            o_a = _causal_attention(proj, proj, proj, c, dqk=HEAD_DIM, q_blk0=0, k_blk0=4, v_blk0=8,
                                    n_pairs=FOX_HEADS // 2, tq=tq)
            o_b = _dilated_attention(proj, dil_bias, q_blk0=12, k_blk0=16, v_blk0=20)
            w_out = e_w_out[i].astype(BF16)
        else:
            w_in = o_w_in[i]
            o1 = MLA_Q_RANK + MLA_KV_RANK
            o2 = o1 + MLA_ROPE
            sq = SWA_Q_HEADS * HEAD_DIM
            kpe_cols = jnp.pad(w_in[:, o1:o2], ((0, 0), (MLA_NOPE, LANES - MLA_NOPE - MLA_ROPE)))
            w_main = jnp.concatenate(
                [w_in[:, :o1], kpe_cols, w_in[:, o2:o2 + sq] * (HEAD_DIM ** -0.5), w_in[:, o2 + sq:]],
                axis=1).astype(BF16)
            dq = MLA_NOPE + MLA_ROPE
            wuq = o_w_uq[i].reshape(MLA_Q_RANK, MLA_HEADS, dq) * (dq ** -0.5)
            wuq = jnp.pad(wuq, ((0, 0), (0, 0), (0, LANES - dq))).reshape(MLA_Q_RANK, MLA_HEADS * LANES)
            wukv = o_w_ukv[i].reshape(MLA_KV_RANK, MLA_HEADS, MLA_NOPE + MLA_V)
            wuk = jnp.pad(wukv[:, :, :MLA_NOPE], ((0, 0), (0, 0), (0, LANES - MLA_NOPE)))
            wuk = wuk.reshape(MLA_KV_RANK, MLA_HEADS * LANES)
            wuv = wukv[:, :, MLA_NOPE:].reshape(MLA_KV_RANK, MLA_HEADS * MLA_V)
            q_full, k_full, v_mla, swa = _proj_odd(
                h, g_attn, w_main, o_q_norm[i][None, :], o_kv_norm[i][None, :], wuq.astype(BF16),
                wuk.astype(BF16), wuv.astype(BF16), rope_a, rope_bm, rope_bp, s)
            o_a = _causal_attention(q_full.reshape(b, s, -1), k_full.reshape(b, s, -1), v_mla.reshape(b, s, -1),
                                    None, dqk=LANES, q_blk0=0, k_blk0=0, v_blk0=0, n_pairs=MLA_HEADS // 2, tq=tq)
            grp = SWA_Q_HEADS // SWA_KV_HEADS
            sink_col = jnp.repeat(o_sinks[i].reshape(SWA_KV_HEADS, grp), BLK, axis=1)[:, :, None]
            o_b = _swa_attention(swa.reshape(b, s, -1), swa_bias, sink_col, q_blk0=0, k_blk0=4, v_blk0=5)
            w_out = o_w_out[i].astype(BF16)

        wr_hi, wr_lo, b_r = _router_weights(moe_w_group[layer], moe_b_group[layer], moe_w_router[layer],
                                            moe_b_router[layer])
        h, hn, gates = _out_router(h, o_a.reshape(n, -1), o_b.reshape(n, -1), w_out, ffn_norm[layer][None, :],
                                   wr_hi, wr_lo, b_r)
        wgu, wd = _moe_weights(moe_w_gate[layer], moe_w_up[layer], moe_w_down[layer])
        h = _moe(hn, gates, h, wgu, wd, final_norm[None, :], final_norm=layer == depth - 1)
    return h.reshape(b, s, d)
```

```python
import functools

import numpy as np
import jax
import jax.numpy as jnp
from jax import lax
from jax.experimental import pallas as pl
from jax.experimental.pallas import tpu as pltpu

F32 = jnp.float32
BF16 = jnp.bfloat16

D_MODEL = 1024
HEAD_DIM = 64
BLK = 128
NEG = -1e30
RMS_EPS = 1e-6
FOX_HEADS = 8
DIL_HEADS = 8
DIL_BRANCHES = ((128, 1), (512, 4), (2048, 16))
MLA_HEADS = 8
MLA_Q_RANK = 384
MLA_KV_RANK = 256
MLA_NOPE = 64
MLA_ROPE = 32
MLA_V = 64
ROPE_THETA = 10000.0
SWA_Q_HEADS = 8
SWA_KV_HEADS = 2
SWA_WINDOW = 128
N_GROUPS = 4
EXPERTS_PER_GROUP = 8
N_EXPERTS = N_GROUPS * EXPERTS_PER_GROUP
D_EXPERT = 256

LANES = 128
VMEM_LIMIT = 56 * 1024 * 1024

NT_DIMS = (((1,), (1,)), ((), ()))
LOG2E = 1.4426950408889634


def _params(*sem):
    return pltpu.CompilerParams(dimension_semantics=sem, vmem_limit_bytes=VMEM_LIMIT)


def _dot(a, b):
    return jnp.dot(a, b, preferred_element_type=F32)


def _rms(x, g):
    return x * lax.rsqrt(jnp.mean(x * x, axis=-1, keepdims=True) + RMS_EPS) * g


def _hi_lo(w):
    hi = w.astype(BF16)
    return hi, (w - hi.astype(F32)).astype(BF16)


def _dot_hi(x, w_hi, w_lo):
    x_hi = x.astype(BF16)
    x_lo = (x - x_hi.astype(F32)).astype(BF16)
    return _dot(x_hi, w_hi) + (_dot(x_hi, w_lo) + _dot(x_lo, w_hi))


def _split3(x):
    x1 = x.astype(BF16)
    r1 = x - x1.astype(F32)
    x2 = r1.astype(BF16)
    x3 = (r1 - x2.astype(F32)).astype(BF16)
    return x1, x2, x3


def _proj_even_kernel(x_ref, g_ref, w_ref, wf_hi_ref, wf_lo_ref, bf_ref, out_ref, c_ref, carry_ref, *, tiles_per_seq):
    xn = _rms(x_ref[...], g_ref[...])
    xb = xn.astype(BF16)
    n_out = out_ref.shape[1]
    for c in range(0, n_out, 512):
        out_ref[:, c:c + 512] = _dot(xb, w_ref[:, c:c + 512]).astype(out_ref.dtype)
    z = _dot_hi(xn, wf_hi_ref[...], wf_lo_ref[...]) + bf_ref[...]
    logf = jnp.minimum(z, 0.0) - jnp.log1p(jnp.exp(-jnp.abs(z)))

    @pl.when(pl.program_id(0) % tiles_per_seq == 0)
    def _():
        carry_ref[...] = jnp.zeros_like(carry_ref)

    tm = logf.shape[0]
    ri = lax.broadcasted_iota(jnp.int32, (tm, tm), 0)
    ci = lax.broadcasted_iota(jnp.int32, (tm, tm), 1)
    lower = jnp.where(ci <= ri, 1.0, 0.0).astype(BF16)
    l1, l2, l3 = _split3(logf)
    c = (_dot(lower, l1) + (_dot(lower, l2) + _dot(lower, l3))) + carry_ref[...]
    c_ref[...] = c
    carry_ref[...] = c[tm - 1:tm, :]


def _proj_even(h, g, w, wf_hi, wf_lo, bf, seq, tm=512):
    n, d = h.shape
    n_out = w.shape[1]
    return pl.pallas_call(
        functools.partial(_proj_even_kernel, tiles_per_seq=seq // tm),
        grid=(n // tm,),
        in_specs=[
            pl.BlockSpec((tm, d), lambda i: (i, 0)),
            pl.BlockSpec((1, d), lambda i: (0, 0)),
            pl.BlockSpec((d, n_out), lambda i: (0, 0)),
            pl.BlockSpec((d, LANES), lambda i: (0, 0)),
            pl.BlockSpec((d, LANES), lambda i: (0, 0)),
            pl.BlockSpec((1, LANES), lambda i: (0, 0)),
        ],
        out_specs=[
            pl.BlockSpec((tm, n_out), lambda i: (i, 0)),
            pl.BlockSpec((tm, LANES), lambda i: (i, 0)),
        ],
        out_shape=[
            jax.ShapeDtypeStruct((n, n_out), BF16),
            jax.ShapeDtypeStruct((n, LANES), F32),
        ],
        scratch_shapes=[pltpu.VMEM((1, LANES), F32)],
        compiler_params=_params("arbitrary"),
        name="proj_even",
    )(h, g, w, wf_hi, wf_lo, bf)


def _own_lanes(lane, h):
    return lane < HEAD_DIM if h == 0 else lane >= HEAD_DIM


def _causal_kernel(*refs, fox, tq, tk, chunk):
    if fox:
        q_ref, k_ref, v_ref, c_ref, o_ref, vaug, kaug = refs
    else:
        q_ref, k_ref, v_ref, o_ref, vaug = refs
    pair = pl.program_id(1)
    qi = pl.program_id(2)
    seq = v_ref.shape[1]

    @pl.when(qi == 0)
    def _():
        lane = lax.broadcasted_iota(jnp.int32, (chunk, LANES), 1)

        def fill(t, carry):
            rows = pl.ds(pl.multiple_of(t * chunk, chunk), chunk)
            vp = v_ref[0, rows, :].astype(F32)
            if fox:
                kp = k_ref[0, rows, :].astype(F32)
                c = c_ref[0, rows, :]
            for h in range(2):
                own = _own_lanes(lane, h)
                vaug[h, rows, :] = jnp.where(own, vp, 1.0).astype(BF16)
                if fox:
                    ch = jnp.sum(jnp.where(lane == 2 * pair + h, c, 0.0), axis=-1, keepdims=True)
                    c1, c2, c3 = _split3(ch * (-LOG2E))
                    base = HEAD_DIM if h == 0 else 0
                    extra = jnp.where(lane == base, c1.astype(F32),
                                      jnp.where(lane == base + 1, c2.astype(F32),
                                                jnp.where(lane == base + 2, c3.astype(F32), 0.0)))
                    kaug[h, rows, :] = jnp.where(own, kp, extra).astype(BF16)
            return carry

        lax.fori_loop(0, seq // chunk, fill, 0)

    lane_q = lax.broadcasted_iota(jnp.int32, (tq, LANES), 1)
    qs = []
    for h in range(2):
        if fox:
            base = HEAD_DIM if h == 0 else 0
            ones = jnp.where((lane_q >= base) & (lane_q < base + 3), 1.0, 0.0)
            qs.append(jnp.where(_own_lanes(lane_q, h), q_ref[0].astype(F32), ones).astype(BF16))
        else:
            qs.append(q_ref[0, :, h * LANES:(h + 1) * LANES])

    row = lax.broadcasted_iota(jnp.int32, (tq, tk), 0)
    col = lax.broadcasted_iota(jnp.int32, (tq, tk), 1)

    def block(j, carry, masked):
        start = pl.multiple_of(j * tk, tk)
        new = []
        for h in range(2):
            m, acc = carry[h]
            if fox:
                kj = kaug[h, pl.ds(start, tk), :]
            else:
                kj = k_ref[0, pl.ds(start, tk), h * LANES:(h + 1) * LANES]
            s = lax.dot_general(qs[h], kj, NT_DIMS, preferred_element_type=F32)
            if masked:
                s = jnp.where(col <= row, s, NEG)
            m_new = jnp.maximum(m, jnp.max(s, axis=-1, keepdims=True))
            p = jnp.exp2(s - m_new)
            acc = jnp.exp2(m - m_new) * acc + _dot(p.astype(BF16), vaug[h, pl.ds(start, tk), :])
            new.append((m_new, acc))
        return tuple(new)

    init = tuple((jnp.full((tq, 1), NEG, F32), jnp.zeros((tq, LANES), F32)) for _ in range(2))
    carry = lax.fori_loop(0, qi, lambda j, c: block(j, c, False), init)
    (_, acc0), (_, acc1) = block(qi, carry, True)
    out = jnp.where(lane_q < HEAD_DIM, acc0 / acc0[:, HEAD_DIM:HEAD_DIM + 1], acc1 / acc1[:, 0:1])
    o_ref[0] = out.astype(o_ref.dtype)


def _causal_attention(q_arr, k_arr, v_arr, c_arr, *, q_blk0, k_blk0, v_blk0, n_pairs, tq=512):
    b, s, _ = q_arr.shape
    fox = c_arr is not None
    qk_w = LANES if fox else 2 * LANES
    in_specs = [
        pl.BlockSpec((1, tq, qk_w), lambda bi, p, qi: (bi, qi, q_blk0 + p)),
        pl.BlockSpec((1, s, qk_w), lambda bi, p, qi: (bi, 0, k_blk0 + p)),
        pl.BlockSpec((1, s, LANES), lambda bi, p, qi: (bi, 0, v_blk0 + p)),
    ]
    args = [q_arr, k_arr, v_arr]
    scratch = [pltpu.VMEM((2, s, LANES), BF16)]
    if fox:
        in_specs.append(pl.BlockSpec((1, s, LANES), lambda bi, p, qi: (bi, 0, 0)))
        args.append(c_arr)
        scratch.append(pltpu.VMEM((2, s, LANES), BF16))
    return pl.pallas_call(
        functools.partial(_causal_kernel, fox=fox, tq=tq, tk=tq, chunk=512),
        grid=(b, n_pairs, s // tq),
        in_specs=in_specs,
        out_specs=pl.BlockSpec((1, tq, LANES), lambda bi, p, qi: (bi, qi, p)),
        out_shape=jax.ShapeDtypeStruct((b, s, n_pairs * LANES), BF16),
        scratch_shapes=scratch,
        compiler_params=_params("parallel", "parallel", "arbitrary"),
        name="causal_attention",
    )(*args)


def _pair_cols(a, b):
    rows = a.shape[0]
    return jnp.concatenate([jnp.broadcast_to(a, (rows, HEAD_DIM)), jnp.broadcast_to(b, (rows, HEAD_DIM))], axis=-1)


def _dilated_kernel(q_ref, k_ref, v_ref, bias_ref, o_ref, qf, kf, vf, acc_s, m_s, l_s, *, seq):
    qf[...] = q_ref[0].astype(F32)
    kf[...] = k_ref[0].astype(F32)
    vf[...] = v_ref[0].astype(F32)

    def band_block(br, dil, q_start, k_start, n_keys):
        q2 = qf[pl.ds(q_start, BLK, stride=dil), :]
        k2 = kf[pl.ds(k_start, n_keys, stride=dil), :]
        v2 = vf[pl.ds(k_start, n_keys, stride=dil), :]
        ms, ls, accs = [], [], []
        for h in range(2):
            lo, hi = h * HEAD_DIM, (h + 1) * HEAD_DIM
            s = lax.dot_general(q2[:, lo:hi].astype(BF16), k2[:, lo:hi].astype(BF16), NT_DIMS,
                                preferred_element_type=F32)
            s = s + bias_ref[br, h, :, 2 * BLK - n_keys:]
            m = jnp.max(s, axis=-1, keepdims=True)
            p = jnp.exp(s - m)
            ms.append(m)
            ls.append(jnp.sum(p, axis=-1, keepdims=True))
            accs.append(_dot(p.astype(BF16), v2[:, lo:hi].astype(BF16)))
        return _pair_cols(*ms), _pair_cols(*ls), jnp.concatenate(accs, axis=-1)

    def merge(dil, q_start, m_b, l_b, acc_b, first):
        idx = pl.ds(q_start, BLK, stride=dil)
        if first:
            m_s[idx, :] = m_b
            l_s[idx, :] = l_b
            acc_s[idx, :] = acc_b
            return
        m_old = m_s[idx, :]
        m_new = jnp.maximum(m_old, m_b)
        a_old = jnp.exp(m_old - m_new)
        a_b = jnp.exp(m_b - m_new)
        m_s[idx, :] = m_new
        l_s[idx, :] = a_old * l_s[idx, :] + a_b * l_b
        acc_s[idx, :] = a_old * acc_s[idx, :] + a_b * acc_b

    for br, (_, dil) in enumerate(DIL_BRANCHES):
        n_blocks = seq // (BLK * dil)
        first = br == 0

        def per_class(r, carry, br=br, dil=dil, n_blocks=n_blocks, first=first):
            merge(dil, r, *band_block(br, dil, r, r, BLK), first)

            def per_block(bi, c):
                q_start = r + dil * BLK * bi
                merge(dil, q_start, *band_block(br, dil, q_start, q_start - dil * BLK, 2 * BLK), first)
                return c

            return lax.fori_loop(1, n_blocks, per_block, carry)

        lax.fori_loop(0, dil, per_class, 0)

    o_ref[0] = (acc_s[...] / l_s[...]).astype(o_ref.dtype)


def _dilated_attention(proj, bias, *, q_blk0, k_blk0, v_blk0):
    b, s, _ = proj.shape
    n_pairs = DIL_HEADS // 2
    blk = lambda off: pl.BlockSpec((1, s, LANES), lambda bi, p: (bi, 0, off + p))
    return pl.pallas_call(
        functools.partial(_dilated_kernel, seq=s),
        grid=(b, n_pairs),
        in_specs=[blk(q_blk0), blk(k_blk0), blk(v_blk0),
                  pl.BlockSpec((len(DIL_BRANCHES), 2, BLK, 2 * BLK), lambda bi, p: (0, p, 0, 0))],
        out_specs=pl.BlockSpec((1, s, LANES), lambda bi, p: (bi, 0, p)),
        out_shape=jax.ShapeDtypeStruct((b, s, n_pairs * LANES), BF16),
        scratch_shapes=[pltpu.VMEM((s, LANES), F32) for _ in range(6)],
        compiler_params=_params("parallel", "parallel"),
        name="dilated_attention",
    )(proj, proj, proj, bias)


def _swa_kernel(q_ref, k_ref, v_ref, bias_ref, sink_ref, o_ref, *, tc):
    ci = pl.program_id(1)
    grp = SWA_Q_HEADS // SWA_KV_HEADS
    col = lax.broadcasted_iota(jnp.int32, (grp * BLK, 2 * BLK), 1)
    for bb in range(tc // BLK):
        own = pl.multiple_of(ci * tc + bb * BLK, BLK)
        prev = pl.multiple_of(jnp.maximum(own - BLK, 0), BLK)
        no_prev = jnp.where(own == 0, NEG, 0.0).astype(F32)
        pieces = []
        for kv in range(SWA_KV_HEADS):
            lo, hi = kv * HEAD_DIM, (kv + 1) * HEAD_DIM
            k2 = jnp.concatenate([k_ref[0, pl.ds(prev, BLK), lo:hi], k_ref[0, pl.ds(own, BLK), lo:hi]], axis=0)
            v2 = jnp.concatenate([v_ref[0, pl.ds(prev, BLK), lo:hi], v_ref[0, pl.ds(own, BLK), lo:hi]], axis=0)
            q4 = jnp.concatenate(
                [q_ref[0, bb * BLK:(bb + 1) * BLK, (kv * grp + g) * HEAD_DIM:(kv * grp + g + 1) * HEAD_DIM]
                 for g in range(grp)], axis=0)
            s = lax.dot_general(q4, k2, NT_DIMS, preferred_element_type=F32)
            s = s + bias_ref[kv] + jnp.where(col < BLK, no_prev, 0.0)
            sink = sink_ref[kv]
            m = jnp.maximum(jnp.max(s, axis=-1, keepdims=True), sink)
            p = jnp.exp(s - m)
            den = jnp.sum(p, axis=-1, keepdims=True) + jnp.exp(sink - m)
            o = _dot(p.astype(BF16), v2) / den
            pieces += [o[g * BLK:(g + 1) * BLK] for g in range(grp)]
        o_ref[0, bb * BLK:(bb + 1) * BLK, :] = jnp.concatenate(pieces, axis=-1).astype(o_ref.dtype)


def _swa_attention(proj, bias, sink_col, *, q_blk0, k_blk0, v_blk0, tc=512):
    b, s, _ = proj.shape
    qw = SWA_Q_HEADS * HEAD_DIM
    return pl.pallas_call(
        functools.partial(_swa_kernel, tc=tc),
        grid=(b, s // tc),
        in_specs=[
            pl.BlockSpec((1, tc, qw), lambda bi, ci: (bi, ci, q_blk0 * LANES // qw)),
            pl.BlockSpec((1, s, LANES), lambda bi, ci: (bi, 0, k_blk0)),
            pl.BlockSpec((1, s, LANES), lambda bi, ci: (bi, 0, v_blk0)),
            pl.BlockSpec(bias.shape, lambda bi, ci: (0, 0, 0)),
            pl.BlockSpec(sink_col.shape, lambda bi, ci: (0, 0, 0)),
        ],
        out_specs=pl.BlockSpec((1, tc, qw), lambda bi, ci: (bi, ci, 0)),
        out_shape=jax.ShapeDtypeStruct((b, s, qw), BF16),
        compiler_params=_params("parallel", "parallel"),
        name="swa_attention",
    )(proj, proj, proj, bias, sink_col)


def _rope128(x, a, bm, bp):
    return x * a + pltpu.roll(x, LANES - 16, axis=1) * bm + pltpu.roll(x, 16, axis=1) * bp


def _proj_odd_kernel(x_ref, g_ref, w_ref, qn_ref, kvn_ref, wuq_ref, wuk_ref, wuv_ref, a_ref, bm_ref, bp_ref,
                     q_out, k_out, v_out, swa_out):
    xb = _rms(x_ref[...], g_ref[...]).astype(BF16)
    w = w_ref
    c_q = _dot(xb, w[:, 0:MLA_Q_RANK])
    c_kv = _dot(xb, w[:, MLA_Q_RANK:MLA_Q_RANK + MLA_KV_RANK])
    o1 = MLA_Q_RANK + MLA_KV_RANK
    kpe = _dot(xb, w[:, o1:o1 + LANES])
    o2 = o1 + LANES
    swa_out[...] = _dot(xb, w[:, o2:]).astype(swa_out.dtype)
    a, bm, bp = a_ref[...], bm_ref[...], bp_ref[...]
    kpe = _rope128(kpe, a, bm, bp)
    cqn = _rms(c_q, qn_ref[...]).astype(BF16)
    ckvn = _rms(c_kv, kvn_ref[...]).astype(BF16)
    v_out[...] = _dot(ckvn, wuv_ref[...]).astype(v_out.dtype)
    for h in range(MLA_HEADS):
        sl = slice(h * LANES, (h + 1) * LANES)
        q_out[:, sl] = _rope128(_dot(cqn, wuq_ref[:, sl]), a, bm, bp).astype(q_out.dtype)
        k_out[:, sl] = (_dot(ckvn, wuk_ref[:, sl]) + kpe).astype(k_out.dtype)


def _proj_odd(h, g, w, qn, kvn, wuq, wuk, wuv, rope_a, rope_bm, rope_bp, seq, tm=512):
    n, d = h.shape
    n_in = w.shape[1]
    n_swa = n_in - (MLA_Q_RANK + MLA_KV_RANK + LANES)
    tiles_per_seq = seq // tm
    full = lambda shape: pl.BlockSpec(shape, lambda i: (0, 0))
    rope = pl.BlockSpec((tm, LANES), lambda i: (i % tiles_per_seq, 0))
    return pl.pallas_call(
        _proj_odd_kernel,
        grid=(n // tm,),
        in_specs=[
            pl.BlockSpec((tm, d), lambda i: (i, 0)), full((1, d)), full(w.shape),
            full(qn.shape), full(kvn.shape), full(wuq.shape), full(wuk.shape), full(wuv.shape),
            rope, rope, rope,
        ],
        out_specs=[
            pl.BlockSpec((tm, MLA_HEADS * LANES), lambda i: (i, 0)),
            pl.BlockSpec((tm, MLA_HEADS * LANES), lambda i: (i, 0)),
            pl.BlockSpec((tm, MLA_HEADS * MLA_V), lambda i: (i, 0)),
            pl.BlockSpec((tm, n_swa), lambda i: (i, 0)),
        ],
        out_shape=[
            jax.ShapeDtypeStruct((n, MLA_HEADS * LANES), BF16),
            jax.ShapeDtypeStruct((n, MLA_HEADS * LANES), BF16),
            jax.ShapeDtypeStruct((n, MLA_HEADS * MLA_V), BF16),
            jax.ShapeDtypeStruct((n, n_swa), BF16),
        ],
        compiler_params=_params("parallel"),
        name="proj_odd",
    )(h, g, w, qn, kvn, wuq, wuk, wuv, rope_a, rope_bm, rope_bp)


def _out_router_kernel(h_ref, oa_ref, ob_ref, wo_ref, g_ref, wr_hi_ref, wr_lo_ref, br_ref,
                       h_out, hn_out, gates_out):
    half = oa_ref.shape[1]
    h = h_ref[...] + _dot(oa_ref[...], wo_ref[0:half, :]) + _dot(ob_ref[...], wo_ref[half:, :])
    h_out[...] = h
    hn = _rms(h, g_ref[...])
    hn_out[...] = hn.astype(hn_out.dtype)
    z = _dot_hi(hn, wr_hi_ref[...], wr_lo_ref[...]) + br_ref[...]
    lane = lax.broadcasted_iota(jnp.int32, z.shape, 1)
    big = jnp.int32(LANES)
    is_grp = (lane >= N_EXPERTS) & (lane < N_EXPERTS + N_GROUPS)
    zg = jnp.where(is_grp, z, -jnp.inf)
    g_max = jnp.max(zg, axis=-1, keepdims=True)
    g_w = 1.0 / jnp.sum(jnp.exp(zg - g_max), axis=-1, keepdims=True)
    g_idx = jnp.min(jnp.where(zg == g_max, lane - N_EXPERTS, big), axis=-1, keepdims=True)
    in_grp = (lane < N_EXPERTS) & ((lane // EXPERTS_PER_GROUP) == g_idx)
    ze = jnp.where(in_grp, z, -jnp.inf)
    v1 = jnp.max(ze, axis=-1, keepdims=True)
    i1 = jnp.min(jnp.where(ze == v1, lane, big), axis=-1, keepdims=True)
    ze2 = jnp.where(lane == i1, -jnp.inf, ze)
    v2 = jnp.max(ze2, axis=-1, keepdims=True)
    i2 = jnp.min(jnp.where(ze2 == v2, lane, big), axis=-1, keepdims=True)
    e2 = jnp.exp(v2 - v1)
    w1 = g_w / (1.0 + e2)
    w2 = g_w * e2 / (1.0 + e2)
    gates_out[...] = jnp.where(lane == i1, w1, 0.0) + jnp.where(lane == i2, w2, 0.0)


def _out_router(h, oa, ob, wo, g, wr_hi, wr_lo, br, tm=512):
    n, d = h.shape
    half = oa.shape[1]
    full = lambda shape: pl.BlockSpec(shape, lambda i: (0, 0))
    tile = lambda w: pl.BlockSpec((tm, w), lambda i: (i, 0))
    return pl.pallas_call(
        _out_router_kernel,
        grid=(n // tm,),
        in_specs=[tile(d), tile(half), tile(half), full(wo.shape), full((1, d)),
                  full(wr_hi.shape), full(wr_lo.shape), full((1, LANES))],
        out_specs=[tile(d), tile(d), tile(LANES)],
        out_shape=[
            jax.ShapeDtypeStruct((n, d), F32),
            jax.ShapeDtypeStruct((n, d), BF16),
            jax.ShapeDtypeStruct((n, LANES), F32),
        ],
        compiler_params=_params("parallel"),
        name="out_router",
    )(h, oa, ob, wo, g, wr_hi, wr_lo, br)


def _moe_kernel(hn_ref, gates_ref, h_ref, wgu_ref, wd_ref, fg_ref, o_ref, acc_ref, *, final_norm):
    e = pl.program_id(1)

    @pl.when(e == 0)
    def _():
        acc_ref[...] = jnp.zeros_like(acc_ref)

    gates = gates_ref[...]
    lane = lax.broadcasted_iota(jnp.int32, gates.shape, 1)
    gate = jnp.sum(jnp.where(lane == e, gates, 0.0), axis=-1, keepdims=True)
    au = _dot(hn_ref[...], wgu_ref[0])
    a, u = au[:, :D_EXPERT], au[:, D_EXPERT:]
    act = (a * jax.nn.sigmoid(a)) * u * gate
    acc_ref[...] += _dot(act.astype(BF16), wd_ref[0])

    @pl.when(e == pl.num_programs(1) - 1)
    def _():
        h = h_ref[...] + acc_ref[...]
        o_ref[...] = _rms(h, fg_ref[...]) if final_norm else h


def _moe(hn, gates, h, wgu, wd, fg, final_norm, tm=1024):
    n, d = h.shape
    ne = wgu.shape[0]
    return pl.pallas_call(
        functools.partial(_moe_kernel, final_norm=final_norm),
        grid=(n // tm, ne),
        in_specs=[
            pl.BlockSpec((tm, d), lambda i, e: (i, 0)),
            pl.BlockSpec((tm, LANES), lambda i, e: (i, 0)),
            pl.BlockSpec((tm, d), lambda i, e: (i, 0)),
            pl.BlockSpec((1, d, 2 * D_EXPERT), lambda i, e: (e, 0, 0)),
            pl.BlockSpec((1, D_EXPERT, d), lambda i, e: (e, 0, 0)),
            pl.BlockSpec((1, d), lambda i, e: (0, 0)),
        ],
        out_specs=pl.BlockSpec((tm, d), lambda i, e: (i, 0)),
        out_shape=jax.ShapeDtypeStruct((n, d), F32),
        scratch_shapes=[pltpu.VMEM((tm, d), F32)],
        compiler_params=_params("parallel", "arbitrary"),
        name="moe_experts",
    )(hn, gates, h, wgu, wd, fg)


def _alibi_slopes(n):
    return jnp.exp2(-8.0 * jnp.arange(1, n + 1, dtype=F32) / n)


def _band_bias(slopes, max_steps, step_dist):
    steps = (jnp.arange(BLK)[:, None] + BLK) - jnp.arange(2 * BLK)[None, :]
    in_band = (steps >= 0) & (steps <= max_steps)
    dist = (steps * step_dist).astype(F32)
    return jnp.where(in_band[None], -slopes.astype(F32)[:, None, None] * dist[None], NEG)


def _rope_tables(s):
    inv = ROPE_THETA ** (-jnp.arange(0, MLA_ROPE, 2, dtype=F32) / MLA_ROPE)
    ang = jnp.arange(s, dtype=F32)[:, None] * inv[None, :]
    cos, sin = jnp.cos(ang), jnp.sin(ang)
    half = MLA_ROPE // 2
    zeros_tail = jnp.zeros((s, LANES - MLA_NOPE - MLA_ROPE), F32)
    a = jnp.concatenate([jnp.ones((s, MLA_NOPE), F32), cos, cos, zeros_tail], axis=1)
    zeros_nope = jnp.zeros((s, MLA_NOPE), F32)
    zeros_half = jnp.zeros((s, half), F32)
    bm = jnp.concatenate([zeros_nope, -sin, zeros_half, zeros_tail], axis=1)
    bp = jnp.concatenate([zeros_nope, zeros_half, sin, zeros_tail], axis=1)
    return a, bm, bp


def _pad_cols(w, width):
    return jnp.pad(w, ((0, 0), (0, width - w.shape[1])))


def _router_weights(w_group, b_group, w_router, b_router):
    w = _pad_cols(jnp.concatenate([w_router, w_group], axis=1), LANES)
    b = _pad_cols(jnp.concatenate([b_router, b_group])[None, :], LANES)
    hi, lo = _hi_lo(w)
    return hi, lo, b


def _moe_weights(w_gate, w_up, w_down):
    d = w_gate.shape[-2]
    wgu = jnp.concatenate([w_gate.reshape(N_EXPERTS, d, D_EXPERT), w_up.reshape(N_EXPERTS, d, D_EXPERT)], axis=-1)
    return wgu.astype(BF16), w_down.reshape(N_EXPERTS, D_EXPERT, d).astype(BF16)


def kernel(x, attn_norm, ffn_norm, final_norm, e_w_in, e_b_f, e_w_out, o_w_in, o_q_norm, o_kv_norm, o_w_uq,
           o_w_ukv, o_sinks, o_w_out, moe_w_group, moe_b_group, moe_w_router, moe_b_router, moe_w_gate,
           moe_w_up, moe_w_down):
    b, s, d = x.shape
    n = b * s
    depth = attn_norm.shape[0]
    assert s % (BLK * DIL_BRANCHES[-1][1]) == 0 and d == D_MODEL
    h = x.reshape(n, d)
    tq = 256

    dil_slopes = _alibi_slopes(DIL_HEADS)
    dil_bias = jnp.stack([_band_bias(dil_slopes, w // dl, dl) for (w, dl) in DIL_BRANCHES])
    swa_bias = _band_bias(_alibi_slopes(SWA_Q_HEADS), SWA_WINDOW - 1, 1)
    swa_bias = swa_bias.reshape(SWA_KV_HEADS, (SWA_Q_HEADS // SWA_KV_HEADS) * BLK, 2 * BLK)
    rope_a, rope_bm, rope_bp = _rope_tables(s)

    for layer in range(depth):
        i = layer // 2
        g_attn = attn_norm[layer][None, :]
        if layer % 2 == 0:
            w_in = e_w_in[i]
            hq = FOX_HEADS * HEAD_DIM
            scale = HEAD_DIM ** -0.5
            cols = [w_in[:, 0:hq] * (scale * LOG2E), w_in[:, hq:2 * hq], w_in[:, 2 * hq:3 * hq]]
            o = 3 * hq + FOX_HEADS
            cols += [w_in[:, o:o + hq] * scale, w_in[:, o + hq:o + 2 * hq], w_in[:, o + 2 * hq:o + 3 * hq]]
            w_main = jnp.concatenate(cols, axis=1).astype(BF16)
            wf_hi, wf_lo = _hi_lo(_pad_cols(w_in[:, 3 * hq:o], LANES))
            b_f = _pad_cols(e_b_f[i][None, :], LANES)
            proj, c = _proj_even(h, g_attn, w_main, wf_hi, wf_lo, b_f, s)
            proj = proj.reshape(b, s, -1)
            o_a = _causal_attention(proj, proj, proj, c.reshape(b, s, LANES), q_blk0=0, k_blk0=4, v_blk0=8,
                                    n_pairs=FOX_HEADS // 2)
            o_b = _dilated_attention(proj, dil_bias, q_blk0=12, k_blk0=16, v_blk0=20)
            w_out = e_w_out[i].astype(BF16)
        else:
            w_in = o_w_in[i]
            o1 = MLA_Q_RANK + MLA_KV_RANK
            o2 = o1 + MLA_ROPE
            sq = SWA_Q_HEADS * HEAD_DIM
            kpe_cols = jnp.pad(w_in[:, o1:o2], ((0, 0), (MLA_NOPE, LANES - MLA_NOPE - MLA_ROPE)))
            w_main = jnp.concatenate(
                [w_in[:, :o1], kpe_cols, w_in[:, o2:o2 + sq] * (HEAD_DIM ** -0.5), w_in[:, o2 + sq:]],
                axis=1).astype(BF16)
            dq = MLA_NOPE + MLA_ROPE
            wuq = o_w_uq[i].reshape(MLA_Q_RANK, MLA_HEADS, dq) * (dq ** -0.5 * LOG2E)
            wuq = jnp.pad(wuq, ((0, 0), (0, 0), (0, LANES - dq))).reshape(MLA_Q_RANK, MLA_HEADS * LANES)
            wukv = o_w_ukv[i].reshape(MLA_KV_RANK, MLA_HEADS, MLA_NOPE + MLA_V)
            wuk = jnp.pad(wukv[:, :, :MLA_NOPE], ((0, 0), (0, 0), (0, LANES - MLA_NOPE)))
            wuk = wuk.reshape(MLA_KV_RANK, MLA_HEADS * LANES)
            wuv = wukv[:, :, MLA_NOPE:].reshape(MLA_KV_RANK, MLA_HEADS * MLA_V)
            q_full, k_full, v_mla, swa = _proj_odd(
                h, g_attn, w_main, o_q_norm[i][None, :], o_kv_norm[i][None, :], wuq.astype(BF16),
                wuk.astype(BF16), wuv.astype(BF16), rope_a, rope_bm, rope_bp, s)
            o_a = _causal_attention(q_full.reshape(b, s, -1), k_full.reshape(b, s, -1), v_mla.reshape(b, s, -1),
                                    None, q_blk0=0, k_blk0=0, v_blk0=0, n_pairs=MLA_HEADS // 2)
            grp = SWA_Q_HEADS // SWA_KV_HEADS
            sink_col = jnp.repeat(o_sinks[i].reshape(SWA_KV_HEADS, grp), BLK, axis=1)[:, :, None]
            o_b = _swa_attention(swa.reshape(b, s, -1), swa_bias, sink_col, q_blk0=0, k_blk0=4, v_blk0=5)
            w_out = o_w_out[i].astype(BF16)

        wr_hi, wr_lo, b_r = _router_weights(moe_w_group[layer], moe_b_group[layer], moe_w_router[layer],
                                            moe_b_router[layer])
        h, hn, gates = _out_router(h, o_a.reshape(n, -1), o_b.reshape(n, -1), w_out, ffn_norm[layer][None, :],
                                   wr_hi, wr_lo, b_r)
        wgu, wd = _moe_weights(moe_w_gate[layer], moe_w_up[layer], moe_w_down[layer])
        h = _moe(hn, gates, h, wgu, wd, final_norm[None, :], final_norm=layer == depth - 1)
    return h.reshape(b, s, d)
```

```python
import functools

import numpy as np
import jax
import jax.numpy as jnp
from jax import lax
from jax.experimental import pallas as pl
from jax.experimental.pallas import tpu as pltpu

F32 = jnp.float32
BF16 = jnp.bfloat16

D_MODEL = 1024
HEAD_DIM = 64
BLK = 128
NEG = -1e30
RMS_EPS = 1e-6
FOX_HEADS = 8
DIL_HEADS = 8
DIL_BRANCHES = ((128, 1), (512, 4), (2048, 16))
MLA_HEADS = 8
MLA_Q_RANK = 384
MLA_KV_RANK = 256
MLA_NOPE = 64
MLA_ROPE = 32
MLA_V = 64
ROPE_THETA = 10000.0
SWA_Q_HEADS = 8
SWA_KV_HEADS = 2
SWA_WINDOW = 128
N_GROUPS = 4
EXPERTS_PER_GROUP = 8
N_EXPERTS = N_GROUPS * EXPERTS_PER_GROUP
D_EXPERT = 256

LANES = 128
VMEM_LIMIT = 56 * 1024 * 1024

NT_DIMS = (((1,), (1,)), ((), ()))
LOG2E = 1.4426950408889634


def _params(*sem):
    return pltpu.CompilerParams(dimension_semantics=sem, vmem_limit_bytes=VMEM_LIMIT)


def _dot(a, b):
    return jnp.dot(a, b, preferred_element_type=F32)


def _rms(x, g):
    return x * lax.rsqrt(jnp.mean(x * x, axis=-1, keepdims=True) + RMS_EPS) * g


def _hi_lo(w):
    hi = w.astype(BF16)
    return hi, (w - hi.astype(F32)).astype(BF16)


def _dot_hi(x, w_hi, w_lo):
    x_hi = x.astype(BF16)
    x_lo = (x - x_hi.astype(F32)).astype(BF16)
    return _dot(x_hi, w_hi) + (_dot(x_hi, w_lo) + _dot(x_lo, w_hi))


def _split3(x):
    x1 = x.astype(BF16)
    r1 = x - x1.astype(F32)
    x2 = r1.astype(BF16)
    x3 = (r1 - x2.astype(F32)).astype(BF16)
    return x1, x2, x3


def _proj_even_kernel(x_ref, g_ref, w_ref, wf_hi_ref, wf_lo_ref, bf_ref, out_ref, c_ref, carry_ref, *, tiles_per_seq):
    xn = _rms(x_ref[...], g_ref[...])
    xb = xn.astype(BF16)
    n_out = out_ref.shape[1]
    for c in range(0, n_out, 512):
        out_ref[:, c:c + 512] = _dot(xb, w_ref[:, c:c + 512]).astype(out_ref.dtype)
    z = _dot_hi(xn, wf_hi_ref[...], wf_lo_ref[...]) + bf_ref[...]
    logf = jnp.minimum(z, 0.0) - jnp.log1p(jnp.exp(-jnp.abs(z)))

    @pl.when(pl.program_id(0) % tiles_per_seq == 0)
    def _():
        carry_ref[...] = jnp.zeros_like(carry_ref)

    tm = logf.shape[0]
    ri = lax.broadcasted_iota(jnp.int32, (tm, tm), 0)
    ci = lax.broadcasted_iota(jnp.int32, (tm, tm), 1)
    lower = jnp.where(ci <= ri, 1.0, 0.0).astype(BF16)
    l1, l2, l3 = _split3(logf)
    c = (_dot(lower, l1) + (_dot(lower, l2) + _dot(lower, l3))) + carry_ref[...]
    c_ref[...] = c
    carry_ref[...] = c[tm - 1:tm, :]


def _proj_even(h, g, w, wf_hi, wf_lo, bf, seq, tm=512):
    n, d = h.shape
    n_out = w.shape[1]
    return pl.pallas_call(
        functools.partial(_proj_even_kernel, tiles_per_seq=seq // tm),
        grid=(n // tm,),
        in_specs=[
            pl.BlockSpec((tm, d), lambda i: (i, 0)),
            pl.BlockSpec((1, d), lambda i: (0, 0)),
            pl.BlockSpec((d, n_out), lambda i: (0, 0)),
            pl.BlockSpec((d, LANES), lambda i: (0, 0)),
            pl.BlockSpec((d, LANES), lambda i: (0, 0)),
            pl.BlockSpec((1, LANES), lambda i: (0, 0)),
        ],
        out_specs=[
            pl.BlockSpec((tm, n_out), lambda i: (i, 0)),
            pl.BlockSpec((tm, LANES), lambda i: (i, 0)),
        ],
        out_shape=[
            jax.ShapeDtypeStruct((n, n_out), BF16),
            jax.ShapeDtypeStruct((n, LANES), F32),
        ],
        scratch_shapes=[pltpu.VMEM((1, LANES), F32)],
        compiler_params=_params("arbitrary"),
        name="proj_even",
    )(h, g, w, wf_hi, wf_lo, bf)


def _own_lanes(lane, h):
    return lane < HEAD_DIM if h == 0 else lane >= HEAD_DIM


def _causal_kernel(*refs, fox, tq, tk, chunk):
    if fox:
        q_ref, k_ref, v_ref, c_ref, o_ref, vaug, kaug = refs
    else:
        q_ref, k_ref, v_ref, o_ref, vaug = refs
    pair = pl.program_id(1)
    qi = pl.program_id(2)
    seq = v_ref.shape[1]

    @pl.when(qi == 0)
    def _():
        lane = lax.broadcasted_iota(jnp.int32, (chunk, LANES), 1)

        def fill(t, carry):
            rows = pl.ds(pl.multiple_of(t * chunk, chunk), chunk)
            vp = v_ref[0, rows, :].astype(F32)
            if fox:
                kp = k_ref[0, rows, :].astype(F32)
                c = c_ref[0, rows, :]
            for h in range(2):
                own = _own_lanes(lane, h)
                vaug[h, rows, :] = jnp.where(own, vp, 1.0).astype(BF16)
                if fox:
                    ch = jnp.sum(jnp.where(lane == 2 * pair + h, c, 0.0), axis=-1, keepdims=True)
                    c1, c2, c3 = _split3(ch * (-LOG2E))
                    base = HEAD_DIM if h == 0 else 0
                    extra = jnp.where(lane == base, c1.astype(F32),
                                      jnp.where(lane == base + 1, c2.astype(F32),
                                                jnp.where(lane == base + 2, c3.astype(F32), 0.0)))
                    kaug[h, rows, :] = jnp.where(own, kp, extra).astype(BF16)
            return carry

        lax.fori_loop(0, seq // chunk, fill, 0)

    lane_q = lax.broadcasted_iota(jnp.int32, (tq, LANES), 1)
    qs = []
    for h in range(2):
        if fox:
            base = HEAD_DIM if h == 0 else 0
            ones = jnp.where((lane_q >= base) & (lane_q < base + 3), 1.0, 0.0)
            qs.append(jnp.where(_own_lanes(lane_q, h), q_ref[0].astype(F32), ones).astype(BF16))
        else:
            qs.append(q_ref[0, :, h * LANES:(h + 1) * LANES])

    row = lax.broadcasted_iota(jnp.int32, (tq, tk), 0)
    col = lax.broadcasted_iota(jnp.int32, (tq, tk), 1)

    def block(j, carry, masked):
        start = pl.multiple_of(j * tk, tk)
        new = []
        for h in range(2):
            m, acc = carry[h]
            if fox:
                kj = kaug[h, pl.ds(start, tk), :]
            else:
                kj = k_ref[0, pl.ds(start, tk), h * LANES:(h + 1) * LANES]
            s = lax.dot_general(qs[h], kj, NT_DIMS, preferred_element_type=F32)
            if masked:
                s = jnp.where(col <= row, s, NEG)
            m_new = jnp.maximum(m, jnp.max(s, axis=-1, keepdims=True))
            p = jnp.exp2(s - m_new)
            acc = jnp.exp2(m - m_new) * acc + _dot(p.astype(BF16), vaug[h, pl.ds(start, tk), :])
            new.append((m_new, acc))
        return tuple(new)

    init = tuple((jnp.full((tq, 1), NEG, F32), jnp.zeros((tq, LANES), F32)) for _ in range(2))
    carry = lax.fori_loop(0, qi, lambda j, c: block(j, c, False), init)
    (_, acc0), (_, acc1) = block(qi, carry, True)
    out = jnp.where(lane_q < HEAD_DIM, acc0 / acc0[:, HEAD_DIM:HEAD_DIM + 1], acc1 / acc1[:, 0:1])
    o_ref[0] = out.astype(o_ref.dtype)


def _causal_attention(q_arr, k_arr, v_arr, c_arr, *, q_blk0, k_blk0, v_blk0, n_pairs, tq=512):
    b, s, _ = q_arr.shape
    fox = c_arr is not None
    qk_w = LANES if fox else 2 * LANES
    in_specs = [
        pl.BlockSpec((1, tq, qk_w), lambda bi, p, qi: (bi, qi, q_blk0 + p)),
        pl.BlockSpec((1, s, qk_w), lambda bi, p, qi: (bi, 0, k_blk0 + p)),
        pl.BlockSpec((1, s, LANES), lambda bi, p, qi: (bi, 0, v_blk0 + p)),
    ]
    args = [q_arr, k_arr, v_arr]
    scratch = [pltpu.VMEM((2, s, LANES), BF16)]
    if fox:
        in_specs.append(pl.BlockSpec((1, s, LANES), lambda bi, p, qi: (bi, 0, 0)))
        args.append(c_arr)
        scratch.append(pltpu.VMEM((2, s, LANES), BF16))
    return pl.pallas_call(
        functools.partial(_causal_kernel, fox=fox, tq=tq, tk=tq, chunk=512),
        grid=(b, n_pairs, s // tq),
        in_specs=in_specs,
        out_specs=pl.BlockSpec((1, tq, LANES), lambda bi, p, qi: (bi, qi, p)),
        out_shape=jax.ShapeDtypeStruct((b, s, n_pairs * LANES), BF16),
        scratch_shapes=scratch,
        compiler_params=_params("parallel", "parallel", "arbitrary"),
        name="causal_attention",
    )(*args)


DIL_PAD = BLK * max(d for _, d in DIL_BRANCHES)
DIL_UNROLL = 4


def _dilated_kernel(q_ref, k_ref, v_ref, bias_ref, o_ref, qf, kf, vf, acc_s, m_s, l_s, *, seq):
    qf[...] = q_ref[0].astype(F32)
    kf[0:DIL_PAD, :] = jnp.zeros((DIL_PAD, LANES), F32)
    vf[0:DIL_PAD, :] = jnp.zeros((DIL_PAD, LANES), F32)
    kf[DIL_PAD:, :] = k_ref[0].astype(F32)
    vf[DIL_PAD:, :] = v_ref[0].astype(F32)
    lane = lax.broadcasted_iota(jnp.int32, (BLK, LANES), 1)
    head0 = lane < HEAD_DIM
    ones = jnp.ones((2 * BLK, LANES), BF16)

    def unit(br, dil, u, first):
        bi = u // dil
        q_start = (u % dil) + (dil * BLK) * bi
        k_start = q_start + (DIL_PAD - dil * BLK)
        no_prev = jnp.where(bi == 0, 1, 0)
        qb = qf[pl.ds(q_start, BLK, stride=dil), :].astype(BF16)
        kb = kf[pl.ds(k_start, 2 * BLK, stride=dil), :].astype(BF16)
        vb = vf[pl.ds(k_start, 2 * BLK, stride=dil), :].astype(BF16)
        v_aug = jnp.concatenate([vb, ones], axis=1)
        ms, accs = [], []
        for h in range(2):
            lo, hi = h * HEAD_DIM, (h + 1) * HEAD_DIM
            s = lax.dot_general(qb[:, lo:hi], kb[:, lo:hi], NT_DIMS, preferred_element_type=F32)
            s = s + bias_ref[br, h, no_prev]
            m = jnp.max(s, axis=-1, keepdims=True)
            ms.append(jnp.broadcast_to(m, (BLK, LANES)))
            accs.append(_dot(jnp.exp2(s - m).astype(BF16), v_aug))
        m_b = jnp.where(head0, ms[0], ms[1])
        acc_b = jnp.where(head0, accs[0][:, :LANES], accs[1][:, :LANES])
        l_b = jnp.where(head0, accs[0][:, LANES:], accs[1][:, LANES:])
        idx = pl.ds(q_start, BLK, stride=dil)
        if first:
            m_s[idx, :] = m_b
            l_s[idx, :] = l_b
            acc_s[idx, :] = acc_b
            return
        m_old = m_s[idx, :]
        m_new = jnp.maximum(m_old, m_b)
        a_old = jnp.exp2(m_old - m_new)
        a_b = jnp.exp2(m_b - m_new)
        m_s[idx, :] = m_new
        l_s[idx, :] = a_old * l_s[idx, :] + a_b * l_b
        acc_s[idx, :] = a_old * acc_s[idx, :] + a_b * acc_b

    order = sorted(range(len(DIL_BRANCHES)), key=lambda i: -DIL_BRANCHES[i][1])
    for pos, br in enumerate(order):
        dil = DIL_BRANCHES[br][1]

        def group(t, carry, br=br, dil=dil, first=pos == 0):
            for i in range(DIL_UNROLL):
                unit(br, dil, t * DIL_UNROLL + i, first)
            return carry

        lax.fori_loop(0, seq // (BLK * DIL_UNROLL), group, 0)

    o_ref[0] = (acc_s[...] / l_s[...]).astype(o_ref.dtype)


def _dilated_attention(proj, bias, *, q_blk0, k_blk0, v_blk0):
    b, s, _ = proj.shape
    n_pairs = DIL_HEADS // 2
    blk = lambda off: pl.BlockSpec((1, s, LANES), lambda bi, p: (bi, 0, off + p))
    state = [pltpu.VMEM((s, LANES), F32) for _ in range(4)]
    padded = [pltpu.VMEM((DIL_PAD + s, LANES), F32) for _ in range(2)]
    return pl.pallas_call(
        functools.partial(_dilated_kernel, seq=s),
        grid=(b, n_pairs),
        in_specs=[blk(q_blk0), blk(k_blk0), blk(v_blk0),
                  pl.BlockSpec((len(DIL_BRANCHES), 2, 2, BLK, 2 * BLK), lambda bi, p: (0, p, 0, 0, 0))],
        out_specs=pl.BlockSpec((1, s, LANES), lambda bi, p: (bi, 0, p)),
        out_shape=jax.ShapeDtypeStruct((b, s, n_pairs * LANES), BF16),
        scratch_shapes=[state[0], padded[0], padded[1], state[1], state[2], state[3]],
        compiler_params=_params("parallel", "parallel"),
        name="dilated_attention",
    )(proj, proj, proj, bias)


def _swa_kernel(q_ref, k_ref, v_ref, bias_ref, sink_ref, o_ref, *, tc):
    ci = pl.program_id(1)
    grp = SWA_Q_HEADS // SWA_KV_HEADS
    col = lax.broadcasted_iota(jnp.int32, (grp * BLK, 2 * BLK), 1)
    for bb in range(tc // BLK):
        own = pl.multiple_of(ci * tc + bb * BLK, BLK)
        prev = pl.multiple_of(jnp.maximum(own - BLK, 0), BLK)
        no_prev = jnp.where(own == 0, NEG, 0.0).astype(F32)
        pieces = []
        for kv in range(SWA_KV_HEADS):
            lo, hi = kv * HEAD_DIM, (kv + 1) * HEAD_DIM
            k2 = jnp.concatenate([k_ref[0, pl.ds(prev, BLK), lo:hi], k_ref[0, pl.ds(own, BLK), lo:hi]], axis=0)
            v2 = jnp.concatenate([v_ref[0, pl.ds(prev, BLK), lo:hi], v_ref[0, pl.ds(own, BLK), lo:hi]], axis=0)
            q4 = jnp.concatenate(
                [q_ref[0, bb * BLK:(bb + 1) * BLK, (kv * grp + g) * HEAD_DIM:(kv * grp + g + 1) * HEAD_DIM]
                 for g in range(grp)], axis=0)
            s = lax.dot_general(q4, k2, NT_DIMS, preferred_element_type=F32)
            s = s + bias_ref[kv] + jnp.where(col < BLK, no_prev, 0.0)
            sink = sink_ref[kv]
            m = jnp.maximum(jnp.max(s, axis=-1, keepdims=True), sink)
            p = jnp.exp(s - m)
            den = jnp.sum(p, axis=-1, keepdims=True) + jnp.exp(sink - m)
            o = _dot(p.astype(BF16), v2) / den
            pieces += [o[g * BLK:(g + 1) * BLK] for g in range(grp)]
        o_ref[0, bb * BLK:(bb + 1) * BLK, :] = jnp.concatenate(pieces, axis=-1).astype(o_ref.dtype)


def _swa_attention(proj, bias, sink_col, *, q_blk0, k_blk0, v_blk0, tc=512):
    b, s, _ = proj.shape
    qw = SWA_Q_HEADS * HEAD_DIM
    return pl.pallas_call(
        functools.partial(_swa_kernel, tc=tc),
        grid=(b, s // tc),
        in_specs=[
            pl.BlockSpec((1, tc, qw), lambda bi, ci: (bi, ci, q_blk0 * LANES // qw)),
            pl.BlockSpec((1, s, LANES), lambda bi, ci: (bi, 0, k_blk0)),
            pl.BlockSpec((1, s, LANES), lambda bi, ci: (bi, 0, v_blk0)),
            pl.BlockSpec(bias.shape, lambda bi, ci: (0, 0, 0)),
            pl.BlockSpec(sink_col.shape, lambda bi, ci: (0, 0, 0)),
        ],
        out_specs=pl.BlockSpec((1, tc, qw), lambda bi, ci: (bi, ci, 0)),
        out_shape=jax.ShapeDtypeStruct((b, s, qw), BF16),
        compiler_params=_params("parallel", "parallel"),
        name="swa_attention",
    )(proj, proj, proj, bias, sink_col)


def _rope128(x, a, bm, bp):
    return x * a + pltpu.roll(x, LANES - 16, axis=1) * bm + pltpu.roll(x, 16, axis=1) * bp


def _proj_odd_kernel(x_ref, g_ref, w_ref, qn_ref, kvn_ref, wuq_ref, wuk_ref, wuv_ref, a_ref, bm_ref, bp_ref,
                     q_out, k_out, v_out, swa_out):
    xb = _rms(x_ref[...], g_ref[...]).astype(BF16)
    w = w_ref
    c_q = _dot(xb, w[:, 0:MLA_Q_RANK])
    c_kv = _dot(xb, w[:, MLA_Q_RANK:MLA_Q_RANK + MLA_KV_RANK])
    o1 = MLA_Q_RANK + MLA_KV_RANK
    kpe = _dot(xb, w[:, o1:o1 + LANES])
    o2 = o1 + LANES
    swa_out[...] = _dot(xb, w[:, o2:]).astype(swa_out.dtype)
    a, bm, bp = a_ref[...], bm_ref[...], bp_ref[...]
    kpe = _rope128(kpe, a, bm, bp)
    cqn = _rms(c_q, qn_ref[...]).astype(BF16)
    ckvn = _rms(c_kv, kvn_ref[...]).astype(BF16)
    v_out[...] = _dot(ckvn, wuv_ref[...]).astype(v_out.dtype)
    for h in range(MLA_HEADS):
        sl = slice(h * LANES, (h + 1) * LANES)
        q_out[:, sl] = _rope128(_dot(cqn, wuq_ref[:, sl]), a, bm, bp).astype(q_out.dtype)
        k_out[:, sl] = (_dot(ckvn, wuk_ref[:, sl]) + kpe).astype(k_out.dtype)


def _proj_odd(h, g, w, qn, kvn, wuq, wuk, wuv, rope_a, rope_bm, rope_bp, seq, tm=512):
    n, d = h.shape
    n_in = w.shape[1]
    n_swa = n_in - (MLA_Q_RANK + MLA_KV_RANK + LANES)
    tiles_per_seq = seq // tm
    full = lambda shape: pl.BlockSpec(shape, lambda i: (0, 0))
    rope = pl.BlockSpec((tm, LANES), lambda i: (i % tiles_per_seq, 0))
    return pl.pallas_call(
        _proj_odd_kernel,
        grid=(n // tm,),
        in_specs=[
            pl.BlockSpec((tm, d), lambda i: (i, 0)), full((1, d)), full(w.shape),
            full(qn.shape), full(kvn.shape), full(wuq.shape), full(wuk.shape), full(wuv.shape),
            rope, rope, rope,
        ],
        out_specs=[
            pl.BlockSpec((tm, MLA_HEADS * LANES), lambda i: (i, 0)),
            pl.BlockSpec((tm, MLA_HEADS * LANES), lambda i: (i, 0)),
            pl.BlockSpec((tm, MLA_HEADS * MLA_V), lambda i: (i, 0)),
            pl.BlockSpec((tm, n_swa), lambda i: (i, 0)),
        ],
        out_shape=[
            jax.ShapeDtypeStruct((n, MLA_HEADS * LANES), BF16),
            jax.ShapeDtypeStruct((n, MLA_HEADS * LANES), BF16),
            jax.ShapeDtypeStruct((n, MLA_HEADS * MLA_V), BF16),
            jax.ShapeDtypeStruct((n, n_swa), BF16),
        ],
        compiler_params=_params("parallel"),
        name="proj_odd",
    )(h, g, w, qn, kvn, wuq, wuk, wuv, rope_a, rope_bm, rope_bp)


def _out_router_kernel(h_ref, oa_ref, ob_ref, wo_ref, g_ref, wr_hi_ref, wr_lo_ref, br_ref,
                       h_out, hn_out, gates_out):
    half = oa_ref.shape[1]
    h = h_ref[...] + _dot(oa_ref[...], wo_ref[0:half, :]) + _dot(ob_ref[...], wo_ref[half:, :])
    h_out[...] = h
    hn = _rms(h, g_ref[...])
    hn_out[...] = hn.astype(hn_out.dtype)
    z = _dot_hi(hn, wr_hi_ref[...], wr_lo_ref[...]) + br_ref[...]
    lane = lax.broadcasted_iota(jnp.int32, z.shape, 1)
    big = jnp.int32(LANES)
    is_grp = (lane >= N_EXPERTS) & (lane < N_EXPERTS + N_GROUPS)
    zg = jnp.where(is_grp, z, -jnp.inf)
    g_max = jnp.max(zg, axis=-1, keepdims=True)
    g_w = 1.0 / jnp.sum(jnp.exp(zg - g_max), axis=-1, keepdims=True)
    g_idx = jnp.min(jnp.where(zg == g_max, lane - N_EXPERTS, big), axis=-1, keepdims=True)
    in_grp = (lane < N_EXPERTS) & ((lane // EXPERTS_PER_GROUP) == g_idx)
    ze = jnp.where(in_grp, z, -jnp.inf)
    v1 = jnp.max(ze, axis=-1, keepdims=True)
    i1 = jnp.min(jnp.where(ze == v1, lane, big), axis=-1, keepdims=True)
    ze2 = jnp.where(lane == i1, -jnp.inf, ze)
    v2 = jnp.max(ze2, axis=-1, keepdims=True)
    i2 = jnp.min(jnp.where(ze2 == v2, lane, big), axis=-1, keepdims=True)
    e2 = jnp.exp(v2 - v1)
    w1 = g_w / (1.0 + e2)
    w2 = g_w * e2 / (1.0 + e2)
    gates_out[...] = jnp.where(lane == i1, w1, 0.0) + jnp.where(lane == i2, w2, 0.0)


def _out_router(h, oa, ob, wo, g, wr_hi, wr_lo, br, tm=512):
    n, d = h.shape
    half = oa.shape[1]
    full = lambda shape: pl.BlockSpec(shape, lambda i: (0, 0))
    tile = lambda w: pl.BlockSpec((tm, w), lambda i: (i, 0))
    return pl.pallas_call(
        _out_router_kernel,
        grid=(n // tm,),
        in_specs=[tile(d), tile(half), tile(half), full(wo.shape), full((1, d)),
                  full(wr_hi.shape), full(wr_lo.shape), full((1, LANES))],
        out_specs=[tile(d), tile(d), tile(LANES)],
        out_shape=[
            jax.ShapeDtypeStruct((n, d), F32),
            jax.ShapeDtypeStruct((n, d), BF16),
            jax.ShapeDtypeStruct((n, LANES), F32),
        ],
        compiler_params=_params("parallel"),
        name="out_router",
    )(h, oa, ob, wo, g, wr_hi, wr_lo, br)


def _moe_kernel(hn_ref, gates_ref, h_ref, wgu_ref, wd_ref, fg_ref, o_ref, acc_ref, *, final_norm):
    e = pl.program_id(1)

    @pl.when(e == 0)
    def _():
        acc_ref[...] = jnp.zeros_like(acc_ref)

    gates = gates_ref[...]
    lane = lax.broadcasted_iota(jnp.int32, gates.shape, 1)
    gate = jnp.sum(jnp.where(lane == e, gates, 0.0), axis=-1, keepdims=True)
    au = _dot(hn_ref[...], wgu_ref[0])
    a, u = au[:, :D_EXPERT], au[:, D_EXPERT:]
    act = (a * jax.nn.sigmoid(a)) * u * gate
    acc_ref[...] += _dot(act.astype(BF16), wd_ref[0])

    @pl.when(e == pl.num_programs(1) - 1)
    def _():
        h = h_ref[...] + acc_ref[...]
        o_ref[...] = _rms(h, fg_ref[...]) if final_norm else h


def _moe(hn, gates, h, wgu, wd, fg, final_norm, tm=1024):
    n, d = h.shape
    ne = wgu.shape[0]
    return pl.pallas_call(
        functools.partial(_moe_kernel, final_norm=final_norm),
        grid=(n // tm, ne),
        in_specs=[
            pl.BlockSpec((tm, d), lambda i, e: (i, 0)),
            pl.BlockSpec((tm, LANES), lambda i, e: (i, 0)),
            pl.BlockSpec((tm, d), lambda i, e: (i, 0)),
            pl.BlockSpec((1, d, 2 * D_EXPERT), lambda i, e: (e, 0, 0)),
            pl.BlockSpec((1, D_EXPERT, d), lambda i, e: (e, 0, 0)),
            pl.BlockSpec((1, d), lambda i, e: (0, 0)),
        ],
        out_specs=pl.BlockSpec((tm, d), lambda i, e: (i, 0)),
        out_shape=jax.ShapeDtypeStruct((n, d), F32),
        scratch_shapes=[pltpu.VMEM((tm, d), F32)],
        compiler_params=_params("parallel", "arbitrary"),
        name="moe_experts",
    )(hn, gates, h, wgu, wd, fg)


def _alibi_slopes(n):
    return jnp.exp2(-8.0 * jnp.arange(1, n + 1, dtype=F32) / n)


def _band_bias(slopes, max_steps, step_dist):
    steps = (jnp.arange(BLK)[:, None] + BLK) - jnp.arange(2 * BLK)[None, :]
    in_band = (steps >= 0) & (steps <= max_steps)
    dist = (steps * step_dist).astype(F32)
    return jnp.where(in_band[None], -slopes.astype(F32)[:, None, None] * dist[None], NEG)


def _rope_tables(s):
    inv = ROPE_THETA ** (-jnp.arange(0, MLA_ROPE, 2, dtype=F32) / MLA_ROPE)
    ang = jnp.arange(s, dtype=F32)[:, None] * inv[None, :]
    cos, sin = jnp.cos(ang), jnp.sin(ang)
    half = MLA_ROPE // 2
    zeros_tail = jnp.zeros((s, LANES - MLA_NOPE - MLA_ROPE), F32)
    a = jnp.concatenate([jnp.ones((s, MLA_NOPE), F32), cos, cos, zeros_tail], axis=1)
    zeros_nope = jnp.zeros((s, MLA_NOPE), F32)
    zeros_half = jnp.zeros((s, half), F32)
    bm = jnp.concatenate([zeros_nope, -sin, zeros_half, zeros_tail], axis=1)
    bp = jnp.concatenate([zeros_nope, zeros_half, sin, zeros_tail], axis=1)
    return a, bm, bp


def _pad_cols(w, width):
    return jnp.pad(w, ((0, 0), (0, width - w.shape[1])))


def _router_weights(w_group, b_group, w_router, b_router):
    w = _pad_cols(jnp.concatenate([w_router, w_group], axis=1), LANES)
    b = _pad_cols(jnp.concatenate([b_router, b_group])[None, :], LANES)
    hi, lo = _hi_lo(w)
    return hi, lo, b


def _moe_weights(w_gate, w_up, w_down):
    d = w_gate.shape[-2]
    wgu = jnp.concatenate([w_gate.reshape(N_EXPERTS, d, D_EXPERT), w_up.reshape(N_EXPERTS, d, D_EXPERT)], axis=-1)
    return wgu.astype(BF16), w_down.reshape(N_EXPERTS, D_EXPERT, d).astype(BF16)


def kernel(x, attn_norm, ffn_norm, final_norm, e_w_in, e_b_f, e_w_out, o_w_in, o_q_norm, o_kv_norm, o_w_uq,
           o_w_ukv, o_sinks, o_w_out, moe_w_group, moe_b_group, moe_w_router, moe_b_router, moe_w_gate,
           moe_w_up, moe_w_down):
    b, s, d = x.shape
    n = b * s
    depth = attn_norm.shape[0]
    assert s % (BLK * DIL_BRANCHES[-1][1]) == 0 and d == D_MODEL
    h = x.reshape(n, d)
    tq = 256

    dil_slopes = _alibi_slopes(DIL_HEADS)
    dil_bias = jnp.stack([_band_bias(dil_slopes, w // dl, dl) for (w, dl) in DIL_BRANCHES]) * LOG2E
    own_half = jnp.arange(2 * BLK) >= BLK
    dil_bias = jnp.stack([dil_bias, jnp.where(own_half, dil_bias, NEG)], axis=2)
    swa_bias = _band_bias(_alibi_slopes(SWA_Q_HEADS), SWA_WINDOW - 1, 1)
    swa_bias = swa_bias.reshape(SWA_KV_HEADS, (SWA_Q_HEADS // SWA_KV_HEADS) * BLK, 2 * BLK)
    rope_a, rope_bm, rope_bp = _rope_tables(s)

    for layer in range(depth):
        i = layer // 2
        g_attn = attn_norm[layer][None, :]
        if layer % 2 == 0:
            w_in = e_w_in[i]
            hq = FOX_HEADS * HEAD_DIM
            scale = HEAD_DIM ** -0.5
            cols = [w_in[:, 0:hq] * (scale * LOG2E), w_in[:, hq:2 * hq], w_in[:, 2 * hq:3 * hq]]
            o = 3 * hq + FOX_HEADS
            cols += [w_in[:, o:o + hq] * (scale * LOG2E), w_in[:, o + hq:o + 2 * hq], w_in[:, o + 2 * hq:o + 3 * hq]]
            w_main = jnp.concatenate(cols, axis=1).astype(BF16)
            wf_hi, wf_lo = _hi_lo(_pad_cols(w_in[:, 3 * hq:o], LANES))
            b_f = _pad_cols(e_b_f[i][None, :], LANES)
            proj, c = _proj_even(h, g_attn, w_main, wf_hi, wf_lo, b_f, s)
            proj = proj.reshape(b, s, -1)
            o_a = _causal_attention(proj, proj, proj, c.reshape(b, s, LANES), q_blk0=0, k_blk0=4, v_blk0=8,
                                    n_pairs=FOX_HEADS // 2)
            o_b = _dilated_attention(proj, dil_bias, q_blk0=12, k_blk0=16, v_blk0=20)
            w_out = e_w_out[i].astype(BF16)
        else:
            w_in = o_w_in[i]
            o1 = MLA_Q_RANK + MLA_KV_RANK
            o2 = o1 + MLA_ROPE
            sq = SWA_Q_HEADS * HEAD_DIM
            kpe_cols = jnp.pad(w_in[:, o1:o2], ((0, 0), (MLA_NOPE, LANES - MLA_NOPE - MLA_ROPE)))
            w_main = jnp.concatenate(
                [w_in[:, :o1], kpe_cols, w_in[:, o2:o2 + sq] * (HEAD_DIM ** -0.5), w_in[:, o2 + sq:]],
                axis=1).astype(BF16)
            dq = MLA_NOPE + MLA_ROPE
            wuq = o_w_uq[i].reshape(MLA_Q_RANK, MLA_HEADS, dq) * (dq ** -0.5 * LOG2E)
            wuq = jnp.pad(wuq, ((0, 0), (0, 0), (0, LANES - dq))).reshape(MLA_Q_RANK, MLA_HEADS * LANES)
            wukv = o_w_ukv[i].reshape(MLA_KV_RANK, MLA_HEADS, MLA_NOPE + MLA_V)
            wuk = jnp.pad(wukv[:, :, :MLA_NOPE], ((0, 0), (0, 0), (0, LANES - MLA_NOPE)))
            wuk = wuk.reshape(MLA_KV_RANK, MLA_HEADS * LANES)
            wuv = wukv[:, :, MLA_NOPE:].reshape(MLA_KV_RANK, MLA_HEADS * MLA_V)
            q_full, k_full, v_mla, swa = _proj_odd(
                h, g_attn, w_main, o_q_norm[i][None, :], o_kv_norm[i][None, :], wuq.astype(BF16),
                wuk.astype(BF16), wuv.astype(BF16), rope_a, rope_bm, rope_bp, s)
            o_a = _causal_attention(q_full.reshape(b, s, -1), k_full.reshape(b, s, -1), v_mla.reshape(b, s, -1),
                                    None, q_blk0=0, k_blk0=0, v_blk0=0, n_pairs=MLA_HEADS // 2)
            grp = SWA_Q_HEADS // SWA_KV_HEADS
            sink_col = jnp.repeat(o_sinks[i].reshape(SWA_KV_HEADS, grp), BLK, axis=1)[:, :, None]
            o_b = _swa_attention(swa.reshape(b, s, -1), swa_bias, sink_col, q_blk0=0, k_blk0=4, v_blk0=5)
            w_out = o_w_out[i].astype(BF16)

        wr_hi, wr_lo, b_r = _router_weights(moe_w_group[layer], moe_b_group[layer], moe_w_router[layer],
                                            moe_b_router[layer])
        h, hn, gates = _out_router(h, o_a.reshape(n, -1), o_b.reshape(n, -1), w_out, ffn_norm[layer][None, :],
                                   wr_hi, wr_lo, b_r)
        wgu, wd = _moe_weights(moe_w_gate[layer], moe_w_up[layer], moe_w_down[layer])
        h = _moe(hn, gates, h, wgu, wd, final_norm[None, :], final_norm=layer == depth - 1)
    return h.reshape(b, s, d)
```

```python
import functools

import numpy as np
import jax
import jax.numpy as jnp
from jax import lax
from jax.experimental import pallas as pl
from jax.experimental.pallas import tpu as pltpu

F32 = jnp.float32
BF16 = jnp.bfloat16

D_MODEL = 1024
HEAD_DIM = 64
BLK = 128
NEG = -1e30
RMS_EPS = 1e-6
FOX_HEADS = 8
DIL_HEADS = 8
DIL_BRANCHES = ((128, 1), (512, 4), (2048, 16))
MLA_HEADS = 8
MLA_Q_RANK = 384
MLA_KV_RANK = 256
MLA_NOPE = 64
MLA_ROPE = 32
MLA_V = 64
ROPE_THETA = 10000.0
SWA_Q_HEADS = 8
SWA_KV_HEADS = 2
SWA_WINDOW = 128
N_GROUPS = 4
EXPERTS_PER_GROUP = 8
N_EXPERTS = N_GROUPS * EXPERTS_PER_GROUP
D_EXPERT = 256

LANES = 128
VMEM_LIMIT = 56 * 1024 * 1024

NT_DIMS = (((1,), (1,)), ((), ()))
LOG2E = 1.4426950408889634


def _params(*sem):
    return pltpu.CompilerParams(dimension_semantics=sem, vmem_limit_bytes=VMEM_LIMIT)


def _dot(a, b):
    return jnp.dot(a, b, preferred_element_type=F32)


def _rms(x, g):
    return x * lax.rsqrt(jnp.mean(x * x, axis=-1, keepdims=True) + RMS_EPS) * g


def _hi_lo(w):
    hi = w.astype(BF16)
    return hi, (w - hi.astype(F32)).astype(BF16)


def _dot_hi(x, w_hi, w_lo):
    x_hi = x.astype(BF16)
    x_lo = (x - x_hi.astype(F32)).astype(BF16)
    return _dot(x_hi, w_hi) + (_dot(x_hi, w_lo) + _dot(x_lo, w_hi))


def _split3(x):
    x1 = x.astype(BF16)
    r1 = x - x1.astype(F32)
    x2 = r1.astype(BF16)
    x3 = (r1 - x2.astype(F32)).astype(BF16)
    return x1, x2, x3


def _proj_even_kernel(x_ref, g_ref, w_ref, wf_hi_ref, wf_lo_ref, bf_ref, out_ref, c_ref, carry_ref, *, tiles_per_seq):
    xn = _rms(x_ref[...], g_ref[...])
    xb = xn.astype(BF16)
    n_out = out_ref.shape[1]
    for c in range(0, n_out, 512):
        out_ref[:, c:c + 512] = _dot(xb, w_ref[:, c:c + 512]).astype(out_ref.dtype)
    z = _dot_hi(xn, wf_hi_ref[...], wf_lo_ref[...]) + bf_ref[...]
    logf = jnp.minimum(z, 0.0) - jnp.log1p(jnp.exp(-jnp.abs(z)))

    @pl.when(pl.program_id(0) % tiles_per_seq == 0)
    def _():
        carry_ref[...] = jnp.zeros_like(carry_ref)

    tm = logf.shape[0]
    ri = lax.broadcasted_iota(jnp.int32, (tm, tm), 0)
    ci = lax.broadcasted_iota(jnp.int32, (tm, tm), 1)
    lower = jnp.where(ci <= ri, 1.0, 0.0).astype(BF16)
    l1, l2, l3 = _split3(logf)
    c = (_dot(lower, l1) + (_dot(lower, l2) + _dot(lower, l3))) + carry_ref[...]
    c_ref[...] = c
    carry_ref[...] = c[tm - 1:tm, :]


def _proj_even(h, g, w, wf_hi, wf_lo, bf, seq, tm=512):
    n, d = h.shape
    n_out = w.shape[1]
    return pl.pallas_call(
        functools.partial(_proj_even_kernel, tiles_per_seq=seq // tm),
        grid=(n // tm,),
        in_specs=[
            pl.BlockSpec((tm, d), lambda i: (i, 0)),
            pl.BlockSpec((1, d), lambda i: (0, 0)),
            pl.BlockSpec((d, n_out), lambda i: (0, 0)),
            pl.BlockSpec((d, LANES), lambda i: (0, 0)),
            pl.BlockSpec((d, LANES), lambda i: (0, 0)),
            pl.BlockSpec((1, LANES), lambda i: (0, 0)),
        ],
        out_specs=[
            pl.BlockSpec((tm, n_out), lambda i: (i, 0)),
            pl.BlockSpec((tm, LANES), lambda i: (i, 0)),
        ],
        out_shape=[
            jax.ShapeDtypeStruct((n, n_out), BF16),
            jax.ShapeDtypeStruct((n, LANES), F32),
        ],
        scratch_shapes=[pltpu.VMEM((1, LANES), F32)],
        compiler_params=_params("arbitrary"),
        name="proj_even",
    )(h, g, w, wf_hi, wf_lo, bf)


def _own_lanes(lane, h):
    return lane < HEAD_DIM if h == 0 else lane >= HEAD_DIM


def _causal_kernel(*refs, fox, tq, tk, chunk):
    if fox:
        q_ref, k_ref, v_ref, c_ref, o_ref, vaug, kaug = refs
    else:
        q_ref, k_ref, v_ref, o_ref, vaug = refs
    pair = pl.program_id(1)
    qi = pl.program_id(2)
    seq = v_ref.shape[1]

    @pl.when(qi == 0)
    def _():
        lane = lax.broadcasted_iota(jnp.int32, (chunk, LANES), 1)

        def fill(t, carry):
            rows = pl.ds(pl.multiple_of(t * chunk, chunk), chunk)
            vp = v_ref[0, rows, :].astype(F32)
            if fox:
                kp = k_ref[0, rows, :].astype(F32)
                c = c_ref[0, rows, :]
            for h in range(2):
                own = _own_lanes(lane, h)
                vaug[h, rows, :] = jnp.where(own, vp, 1.0).astype(BF16)
                if fox:
                    ch = jnp.sum(jnp.where(lane == 2 * pair + h, c, 0.0), axis=-1, keepdims=True)
                    c1, c2, c3 = _split3(ch * (-LOG2E))
                    base = HEAD_DIM if h == 0 else 0
                    extra = jnp.where(lane == base, c1.astype(F32),
                                      jnp.where(lane == base + 1, c2.astype(F32),
                                                jnp.where(lane == base + 2, c3.astype(F32), 0.0)))
                    kaug[h, rows, :] = jnp.where(own, kp, extra).astype(BF16)
            return carry

        lax.fori_loop(0, seq // chunk, fill, 0)

    lane_q = lax.broadcasted_iota(jnp.int32, (tq, LANES), 1)
    qs = []
    for h in range(2):
        if fox:
            base = HEAD_DIM if h == 0 else 0
            ones = jnp.where((lane_q >= base) & (lane_q < base + 3), 1.0, 0.0)
            qs.append(jnp.where(_own_lanes(lane_q, h), q_ref[0].astype(F32), ones).astype(BF16))
        else:
            qs.append(q_ref[0, :, h * LANES:(h + 1) * LANES])

    row = lax.broadcasted_iota(jnp.int32, (tq, tk), 0)
    col = lax.broadcasted_iota(jnp.int32, (tq, tk), 1)

    def block(j, carry, masked):
        start = pl.multiple_of(j * tk, tk)
        new = []
        for h in range(2):
            m, acc = carry[h]
            if fox:
                kj = kaug[h, pl.ds(start, tk), :]
            else:
                kj = k_ref[0, pl.ds(start, tk), h * LANES:(h + 1) * LANES]
            s = lax.dot_general(qs[h], kj, NT_DIMS, preferred_element_type=F32)
            if masked:
                s = jnp.where(col <= row, s, NEG)
            m_new = jnp.maximum(m, jnp.max(s, axis=-1, keepdims=True))
            p = jnp.exp2(s - m_new)
            acc = jnp.exp2(m - m_new) * acc + _dot(p.astype(BF16), vaug[h, pl.ds(start, tk), :])
            new.append((m_new, acc))
        return tuple(new)

    init = tuple((jnp.full((tq, 1), NEG, F32), jnp.zeros((tq, LANES), F32)) for _ in range(2))
    carry = lax.fori_loop(0, qi, lambda j, c: block(j, c, False), init)
    (_, acc0), (_, acc1) = block(qi, carry, True)
    out = jnp.where(lane_q < HEAD_DIM, acc0 / acc0[:, HEAD_DIM:HEAD_DIM + 1], acc1 / acc1[:, 0:1])
    o_ref[0] = out.astype(o_ref.dtype)


def _causal_attention(q_arr, k_arr, v_arr, c_arr, *, q_blk0, k_blk0, v_blk0, n_pairs, tq=512):
    b, s, _ = q_arr.shape
    fox = c_arr is not None
    qk_w = LANES if fox else 2 * LANES
    in_specs = [
        pl.BlockSpec((1, tq, qk_w), lambda bi, p, qi: (bi, qi, q_blk0 + p)),
        pl.BlockSpec((1, s, qk_w), lambda bi, p, qi: (bi, 0, k_blk0 + p)),
        pl.BlockSpec((1, s, LANES), lambda bi, p, qi: (bi, 0, v_blk0 + p)),
    ]
    args = [q_arr, k_arr, v_arr]
    scratch = [pltpu.VMEM((2, s, LANES), BF16)]
    if fox:
        in_specs.append(pl.BlockSpec((1, s, LANES), lambda bi, p, qi: (bi, 0, 0)))
        args.append(c_arr)
        scratch.append(pltpu.VMEM((2, s, LANES), BF16))
    return pl.pallas_call(
        functools.partial(_causal_kernel, fox=fox, tq=tq, tk=tq, chunk=512),
        grid=(b, n_pairs, s // tq),
        in_specs=in_specs,
        out_specs=pl.BlockSpec((1, tq, LANES), lambda bi, p, qi: (bi, qi, p)),
        out_shape=jax.ShapeDtypeStruct((b, s, n_pairs * LANES), BF16),
        scratch_shapes=scratch,
        compiler_params=_params("parallel", "parallel", "arbitrary"),
        name="causal_attention",
    )(*args)


DIL_PAD = BLK * max(d for _, d in DIL_BRANCHES)
DIL_UNROLL = 4


def _dilated_kernel(q_ref, k_ref, v_ref, bias_ref, o_ref, qf, kf, vf, acc_s, m_s, l_s, *, seq):
    qf[...] = q_ref[0].astype(F32)
    kf[0:DIL_PAD, :] = jnp.zeros((DIL_PAD, LANES), F32)
    vf[0:DIL_PAD, :] = jnp.zeros((DIL_PAD, LANES), F32)
    kf[DIL_PAD:, :] = k_ref[0].astype(F32)
    vf[DIL_PAD:, :] = v_ref[0].astype(F32)
    lane = lax.broadcasted_iota(jnp.int32, (BLK, LANES), 1)
    head0 = lane < HEAD_DIM
    ones = jnp.ones((2 * BLK, LANES), BF16)

    def unit(br, dil, u, first):
        bi = u // dil
        q_start = (u % dil) + (dil * BLK) * bi
        k_start = q_start + (DIL_PAD - dil * BLK)
        no_prev = jnp.where(bi == 0, 1, 0)
        qb = qf[pl.ds(q_start, BLK, stride=dil), :].astype(BF16)
        kb = kf[pl.ds(k_start, 2 * BLK, stride=dil), :].astype(BF16)
        vb = vf[pl.ds(k_start, 2 * BLK, stride=dil), :].astype(BF16)
        v_aug = jnp.concatenate([vb, ones], axis=1)
        ms, accs = [], []
        for h in range(2):
            lo, hi = h * HEAD_DIM, (h + 1) * HEAD_DIM
            s = lax.dot_general(qb[:, lo:hi], kb[:, lo:hi], NT_DIMS, preferred_element_type=F32)
            s = s + bias_ref[br, h, no_prev]
            m = jnp.max(s, axis=-1, keepdims=True)
            ms.append(jnp.broadcast_to(m, (BLK, LANES)))
            accs.append(_dot(jnp.exp2(s - m).astype(BF16), v_aug))
        m_b = jnp.where(head0, ms[0], ms[1])
        acc_b = jnp.where(head0, accs[0][:, :LANES], accs[1][:, :LANES])
        l_b = jnp.where(head0, accs[0][:, LANES:], accs[1][:, LANES:])
        idx = pl.ds(q_start, BLK, stride=dil)
        if first:
            m_s[idx, :] = m_b
            l_s[idx, :] = l_b
            acc_s[idx, :] = acc_b
            return
        m_old = m_s[idx, :]
        m_new = jnp.maximum(m_old, m_b)
        a_old = jnp.exp2(m_old - m_new)
        a_b = jnp.exp2(m_b - m_new)
        m_s[idx, :] = m_new
        l_s[idx, :] = a_old * l_s[idx, :] + a_b * l_b
        acc_s[idx, :] = a_old * acc_s[idx, :] + a_b * acc_b

    order = sorted(range(len(DIL_BRANCHES)), key=lambda i: -DIL_BRANCHES[i][1])
    for pos, br in enumerate(order):
        dil = DIL_BRANCHES[br][1]

        def group(t, carry, br=br, dil=dil, first=pos == 0):
            for i in range(DIL_UNROLL):
                unit(br, dil, t * DIL_UNROLL + i, first)
            return carry

        lax.fori_loop(0, seq // (BLK * DIL_UNROLL), group, 0)

    o_ref[0] = (acc_s[...] / l_s[...]).astype(o_ref.dtype)


def _dilated_attention(proj, bias, *, q_blk0, k_blk0, v_blk0):
    b, s, _ = proj.shape
    n_pairs = DIL_HEADS // 2
    blk = lambda off: pl.BlockSpec((1, s, LANES), lambda bi, p: (bi, 0, off + p))
    state = [pltpu.VMEM((s, LANES), F32) for _ in range(4)]
    padded = [pltpu.VMEM((DIL_PAD + s, LANES), F32) for _ in range(2)]
    return pl.pallas_call(
        functools.partial(_dilated_kernel, seq=s),
        grid=(b, n_pairs),
        in_specs=[blk(q_blk0), blk(k_blk0), blk(v_blk0),
                  pl.BlockSpec((len(DIL_BRANCHES), 2, 2, BLK, 2 * BLK), lambda bi, p: (0, p, 0, 0, 0))],
        out_specs=pl.BlockSpec((1, s, LANES), lambda bi, p: (bi, 0, p)),
        out_shape=jax.ShapeDtypeStruct((b, s, n_pairs * LANES), BF16),
        scratch_shapes=[state[0], padded[0], padded[1], state[1], state[2], state[3]],
        compiler_params=_params("parallel", "parallel"),
        name="dilated_attention",
    )(proj, proj, proj, bias)


def _swa_kernel(q_ref, k_ref, v_ref, bias_ref, sink_ref, o_ref, *, tc):
    ci = pl.program_id(1)
    grp = SWA_Q_HEADS // SWA_KV_HEADS
    col = lax.broadcasted_iota(jnp.int32, (grp * BLK, 2 * BLK), 1)
    for bb in range(tc // BLK):
        own = pl.multiple_of(ci * tc + bb * BLK, BLK)
        prev = pl.multiple_of(jnp.maximum(own - BLK, 0), BLK)
        no_prev = jnp.where(own == 0, NEG, 0.0).astype(F32)
        pieces = []
        for kv in range(SWA_KV_HEADS):
            lo, hi = kv * HEAD_DIM, (kv + 1) * HEAD_DIM
            k2 = jnp.concatenate([k_ref[0, pl.ds(prev, BLK), lo:hi], k_ref[0, pl.ds(own, BLK), lo:hi]], axis=0)
            v2 = jnp.concatenate([v_ref[0, pl.ds(prev, BLK), lo:hi], v_ref[0, pl.ds(own, BLK), lo:hi]], axis=0)
            q4 = jnp.concatenate(
                [q_ref[0, bb * BLK:(bb + 1) * BLK, (kv * grp + g) * HEAD_DIM:(kv * grp + g + 1) * HEAD_DIM]
                 for g in range(grp)], axis=0)
            s = lax.dot_general(q4, k2, NT_DIMS, preferred_element_type=F32)
            s = s + bias_ref[kv] + jnp.where(col < BLK, no_prev, 0.0)
            sink = sink_ref[kv]
            m = jnp.maximum(jnp.max(s, axis=-1, keepdims=True), sink)
            p = jnp.exp(s - m)
            den = jnp.sum(p, axis=-1, keepdims=True) + jnp.exp(sink - m)
            o = _dot(p.astype(BF16), v2) / den
            pieces += [o[g * BLK:(g + 1) * BLK] for g in range(grp)]
        o_ref[0, bb * BLK:(bb + 1) * BLK, :] = jnp.concatenate(pieces, axis=-1).astype(o_ref.dtype)


def _swa_attention(proj, bias, sink_col, *, q_blk0, k_blk0, v_blk0, tc=512):
    b, s, _ = proj.shape
    qw = SWA_Q_HEADS * HEAD_DIM
    return pl.pallas_call(
        functools.partial(_swa_kernel, tc=tc),
        grid=(b, s // tc),
        in_specs=[
            pl.BlockSpec((1, tc, qw), lambda bi, ci: (bi, ci, q_blk0 * LANES // qw)),
            pl.BlockSpec((1, s, LANES), lambda bi, ci: (bi, 0, k_blk0)),
            pl.BlockSpec((1, s, LANES), lambda bi, ci: (bi, 0, v_blk0)),
            pl.BlockSpec(bias.shape, lambda bi, ci: (0, 0, 0)),
            pl.BlockSpec(sink_col.shape, lambda bi, ci: (0, 0, 0)),
        ],
        out_specs=pl.BlockSpec((1, tc, qw), lambda bi, ci: (bi, ci, 0)),
        out_shape=jax.ShapeDtypeStruct((b, s, qw), BF16),
        compiler_params=_params("parallel", "parallel"),
        name="swa_attention",
    )(proj, proj, proj, bias, sink_col)


def _rope128(x, a, bm, bp):
    return x * a + pltpu.roll(x, LANES - 16, axis=1) * bm + pltpu.roll(x, 16, axis=1) * bp


def _proj_odd_kernel(x_ref, g_ref, w_ref, qn_ref, kvn_ref, wuq_ref, wuk_ref, wuv_ref, a_ref, bm_ref, bp_ref,
                     q_out, k_out, v_out, swa_out):
    xb = _rms(x_ref[...], g_ref[...]).astype(BF16)
    w = w_ref
    c_q = _dot(xb, w[:, 0:MLA_Q_RANK])
    c_kv = _dot(xb, w[:, MLA_Q_RANK:MLA_Q_RANK + MLA_KV_RANK])
    o1 = MLA_Q_RANK + MLA_KV_RANK
    kpe = _dot(xb, w[:, o1:o1 + LANES])
    o2 = o1 + LANES
    swa_out[...] = _dot(xb, w[:, o2:]).astype(swa_out.dtype)
    a, bm, bp = a_ref[...], bm_ref[...], bp_ref[...]
    kpe = _rope128(kpe, a, bm, bp)
    cqn = _rms(c_q, qn_ref[...]).astype(BF16)
    ckvn = _rms(c_kv, kvn_ref[...]).astype(BF16)
    v_out[...] = _dot(ckvn, wuv_ref[...]).astype(v_out.dtype)
    for h in range(MLA_HEADS):
        sl = slice(h * LANES, (h + 1) * LANES)
        q_out[:, sl] = _rope128(_dot(cqn, wuq_ref[:, sl]), a, bm, bp).astype(q_out.dtype)
        k_out[:, sl] = (_dot(ckvn, wuk_ref[:, sl]) + kpe).astype(k_out.dtype)


def _proj_odd(h, g, w, qn, kvn, wuq, wuk, wuv, rope_a, rope_bm, rope_bp, seq, tm=512):
    n, d = h.shape
    n_in = w.shape[1]
    n_swa = n_in - (MLA_Q_RANK + MLA_KV_RANK + LANES)
    tiles_per_seq = seq // tm
    full = lambda shape: pl.BlockSpec(shape, lambda i: (0, 0))
    rope = pl.BlockSpec((tm, LANES), lambda i: (i % tiles_per_seq, 0))
    return pl.pallas_call(
        _proj_odd_kernel,
        grid=(n // tm,),
        in_specs=[
            pl.BlockSpec((tm, d), lambda i: (i, 0)), full((1, d)), full(w.shape),
            full(qn.shape), full(kvn.shape), full(wuq.shape), full(wuk.shape), full(wuv.shape),
            rope, rope, rope,
        ],
        out_specs=[
            pl.BlockSpec((tm, MLA_HEADS * LANES), lambda i: (i, 0)),
            pl.BlockSpec((tm, MLA_HEADS * LANES), lambda i: (i, 0)),
            pl.BlockSpec((tm, MLA_HEADS * MLA_V), lambda i: (i, 0)),
            pl.BlockSpec((tm, n_swa), lambda i: (i, 0)),
        ],
        out_shape=[
            jax.ShapeDtypeStruct((n, MLA_HEADS * LANES), BF16),
            jax.ShapeDtypeStruct((n, MLA_HEADS * LANES), BF16),
            jax.ShapeDtypeStruct((n, MLA_HEADS * MLA_V), BF16),
            jax.ShapeDtypeStruct((n, n_swa), BF16),
        ],
        compiler_params=_params("parallel"),
        name="proj_odd",
    )(h, g, w, qn, kvn, wuq, wuk, wuv, rope_a, rope_bm, rope_bp)


def _out_router_kernel(h_ref, oa_ref, ob_ref, wo_ref, g_ref, wr_hi_ref, wr_lo_ref, br_ref,
                       h_out, hn_out, gates_out, counts_out):
    half = oa_ref.shape[1]
    h = h_ref[...] + _dot(oa_ref[...], wo_ref[0:half, :]) + _dot(ob_ref[...], wo_ref[half:, :])
    h_out[...] = h
    hn = _rms(h, g_ref[...])
    hn_out[...] = hn.astype(hn_out.dtype)
    z = _dot_hi(hn, wr_hi_ref[...], wr_lo_ref[...]) + br_ref[...]
    lane = lax.broadcasted_iota(jnp.int32, z.shape, 1)
    big = jnp.int32(LANES)
    is_grp = (lane >= N_EXPERTS) & (lane < N_EXPERTS + N_GROUPS)
    zg = jnp.where(is_grp, z, -jnp.inf)
    g_max = jnp.max(zg, axis=-1, keepdims=True)
    g_w = 1.0 / jnp.sum(jnp.exp(zg - g_max), axis=-1, keepdims=True)
    g_idx = jnp.min(jnp.where(zg == g_max, lane - N_EXPERTS, big), axis=-1, keepdims=True)
    in_grp = (lane < N_EXPERTS) & ((lane // EXPERTS_PER_GROUP) == g_idx)
    ze = jnp.where(in_grp, z, -jnp.inf)
    v1 = jnp.max(ze, axis=-1, keepdims=True)
    i1 = jnp.min(jnp.where(ze == v1, lane, big), axis=-1, keepdims=True)
    ze2 = jnp.where(lane == i1, -jnp.inf, ze)
    v2 = jnp.max(ze2, axis=-1, keepdims=True)
    i2 = jnp.min(jnp.where(ze2 == v2, lane, big), axis=-1, keepdims=True)
    e2 = jnp.exp(v2 - v1)
    w1 = g_w / (1.0 + e2)
    w2 = g_w * e2 / (1.0 + e2)
    for grp in range(N_GROUPS):
        first = grp * EXPERTS_PER_GROUP
        slot = jnp.where(lane == i1 - first, w1, 0.0) + jnp.where(lane == i2 - first, w2, 0.0)
        slot = jnp.where(lane < EXPERTS_PER_GROUP, slot, 0.0)
        slot = jnp.where((lane == EXPERTS_PER_GROUP) & (g_idx == grp), 1.0, slot)
        gates_out[:, grp * LANES:(grp + 1) * LANES] = slot
    routed = jnp.where((lane == g_idx) & (lane < N_GROUPS), 1.0, 0.0)
    counts_out[0] = jnp.broadcast_to(jnp.sum(routed, axis=0, keepdims=True), counts_out.shape[1:])


def _out_router(h, oa, ob, wo, g, wr_hi, wr_lo, br, tm=512):
    n, d = h.shape
    half = oa.shape[1]
    full = lambda shape: pl.BlockSpec(shape, lambda i: (0, 0))
    tile = lambda w: pl.BlockSpec((tm, w), lambda i: (i, 0))
    return pl.pallas_call(
        _out_router_kernel,
        grid=(n // tm,),
        in_specs=[tile(d), tile(half), tile(half), full(wo.shape), full((1, d)),
                  full(wr_hi.shape), full(wr_lo.shape), full((1, LANES))],
        out_specs=[tile(d), tile(d), tile(N_GROUPS * LANES), pl.BlockSpec((1, 8, LANES), lambda i: (i, 0, 0))],
        out_shape=[
            jax.ShapeDtypeStruct((n, d), F32),
            jax.ShapeDtypeStruct((n, d), BF16),
            jax.ShapeDtypeStruct((n, N_GROUPS * LANES), F32),
            jax.ShapeDtypeStruct((n // tm, 8, LANES), F32),
        ],
        compiler_params=_params("parallel"),
        name="out_router",
    )(h, oa, ob, wo, g, wr_hi, wr_lo, br)


MOE_CHUNK = 256
MOE_TILE = 1024
ROUTER_TILE = 512


def _moe_kernel(counts_ref, hn_ref, gates_ref, h_ref, wgu_ref, wd_ref, fg_ref, o_ref, acc_ref, tri_ref, *, final_norm):
    i = pl.program_id(0)
    grp = pl.program_id(1)
    tm = hn_ref.shape[0]

    @pl.when((i == 0) & (grp == 0))
    def _():
        ri = lax.broadcasted_iota(jnp.int32, (tm, tm), 0)
        ci = lax.broadcasted_iota(jnp.int32, (tm, tm), 1)
        tri_ref[...] = jnp.where(ci < ri, 1.0, 0.0).astype(BF16)

    @pl.when(grp == 0)
    def _():
        acc_ref[...] = jnp.zeros_like(acc_ref)

    gates = gates_ref[...]
    lane = lax.broadcasted_iota(jnp.int32, gates.shape, 1)
    flag = jnp.where(lane == EXPERTS_PER_GROUP, gates, 0.0)
    rank = _dot(tri_ref[...], flag.astype(BF16))
    pos = jnp.where(flag > 0.0, rank, -1.0)
    pos_col = jnp.sum(jnp.where(lane == EXPERTS_PER_GROUP, pos, 0.0), axis=-1, keepdims=True)
    pos_row = jnp.transpose(jnp.broadcast_to(pos_col, (tm, LANES)))[0:1, :]
    g_hi = gates.astype(BF16)
    g_lo = (gates - g_hi.astype(F32)).astype(BF16)
    slot_col = lax.broadcasted_iota(jnp.int32, (MOE_CHUNK, 1), 0).astype(F32)
    slot_row = lax.broadcasted_iota(jnp.int32, (1, MOE_CHUNK), 1).astype(F32)

    def chunk(k, carry):
        base = (k * MOE_CHUNK).astype(F32)
        gather = jnp.where(pos_row == slot_col + base, 1.0, 0.0).astype(BF16)
        scatter = jnp.where(pos_col == slot_row + base, 1.0, 0.0).astype(BF16)
        xc = _dot(gather, hn_ref[...]).astype(BF16)
        gc = _dot(gather, g_hi) + _dot(gather, g_lo)
        y = jnp.zeros((MOE_CHUNK, acc_ref.shape[1]), F32)
        for e in range(EXPERTS_PER_GROUP):
            au = _dot(xc, wgu_ref[0, e])
            a, u = au[:, :D_EXPERT], au[:, D_EXPERT:]
            act = (a * jax.nn.sigmoid(a)) * u * gc[:, e:e + 1]
            y = y + _dot(act.astype(BF16), wd_ref[0, e])
        acc_ref[...] += _dot(scatter, y.astype(BF16))
        return carry

    n_chunks = (counts_ref[i * N_GROUPS + grp] + (MOE_CHUNK - 1)) // MOE_CHUNK
    lax.fori_loop(0, n_chunks, chunk, 0)

    @pl.when(grp == N_GROUPS - 1)
    def _():
        h = h_ref[...] + acc_ref[...]
        o_ref[...] = _rms(h, fg_ref[...]) if final_norm else h


def _moe(counts, hn, gates, h, wgu, wd, fg, final_norm, tm=1024):
    n, d = h.shape
    once = pl.Buffered(1)
    grid_spec = pltpu.PrefetchScalarGridSpec(
        num_scalar_prefetch=1,
        grid=(n // tm, N_GROUPS),
        in_specs=[
            pl.BlockSpec((tm, d), lambda i, g, c: (i, 0), pipeline_mode=once),
            pl.BlockSpec((tm, LANES), lambda i, g, c: (i, g)),
            pl.BlockSpec((tm, d), lambda i, g, c: (i, 0), pipeline_mode=once),
            pl.BlockSpec((1, EXPERTS_PER_GROUP, d, 2 * D_EXPERT), lambda i, g, c: (g, 0, 0, 0)),
            pl.BlockSpec((1, EXPERTS_PER_GROUP, D_EXPERT, d), lambda i, g, c: (g, 0, 0, 0)),
            pl.BlockSpec((1, d), lambda i, g, c: (0, 0)),
        ],
        out_specs=pl.BlockSpec((tm, d), lambda i, g, c: (i, 0), pipeline_mode=once),
        scratch_shapes=[pltpu.VMEM((tm, d), F32), pltpu.VMEM((tm, tm), BF16)],
    )
    return pl.pallas_call(
        functools.partial(_moe_kernel, final_norm=final_norm),
        grid_spec=grid_spec,
        out_shape=jax.ShapeDtypeStruct((n, d), F32),
        compiler_params=_params("arbitrary", "arbitrary"),
        name="moe_experts",
    )(counts, hn, gates, h, wgu, wd, fg)


def _alibi_slopes(n):
    return jnp.exp2(-8.0 * jnp.arange(1, n + 1, dtype=F32) / n)


def _band_bias(slopes, max_steps, step_dist):
    steps = (jnp.arange(BLK)[:, None] + BLK) - jnp.arange(2 * BLK)[None, :]
    in_band = (steps >= 0) & (steps <= max_steps)
    dist = (steps * step_dist).astype(F32)
    return jnp.where(in_band[None], -slopes.astype(F32)[:, None, None] * dist[None], NEG)


def _rope_tables(s):
    inv = ROPE_THETA ** (-jnp.arange(0, MLA_ROPE, 2, dtype=F32) / MLA_ROPE)
    ang = jnp.arange(s, dtype=F32)[:, None] * inv[None, :]
    cos, sin = jnp.cos(ang), jnp.sin(ang)
    half = MLA_ROPE // 2
    zeros_tail = jnp.zeros((s, LANES - MLA_NOPE - MLA_ROPE), F32)
    a = jnp.concatenate([jnp.ones((s, MLA_NOPE), F32), cos, cos, zeros_tail], axis=1)
    zeros_nope = jnp.zeros((s, MLA_NOPE), F32)
    zeros_half = jnp.zeros((s, half), F32)
    bm = jnp.concatenate([zeros_nope, -sin, zeros_half, zeros_tail], axis=1)
    bp = jnp.concatenate([zeros_nope, zeros_half, sin, zeros_tail], axis=1)
    return a, bm, bp


def _pad_cols(w, width):
    return jnp.pad(w, ((0, 0), (0, width - w.shape[1])))


def _router_weights(w_group, b_group, w_router, b_router):
    w = _pad_cols(jnp.concatenate([w_router, w_group], axis=1), LANES)
    b = _pad_cols(jnp.concatenate([b_router, b_group])[None, :], LANES)
    hi, lo = _hi_lo(w)
    return hi, lo, b


def _moe_weights(w_gate, w_up, w_down):
    d = w_gate.shape[-2]
    return jnp.concatenate([w_gate, w_up], axis=-1).astype(BF16), w_down.astype(BF16)


def kernel(x, attn_norm, ffn_norm, final_norm, e_w_in, e_b_f, e_w_out, o_w_in, o_q_norm, o_kv_norm, o_w_uq,
           o_w_ukv, o_sinks, o_w_out, moe_w_group, moe_b_group, moe_w_router, moe_b_router, moe_w_gate,
           moe_w_up, moe_w_down):
    b, s, d = x.shape
    n = b * s
    depth = attn_norm.shape[0]
    assert s % (BLK * DIL_BRANCHES[-1][1]) == 0 and d == D_MODEL
    h = x.reshape(n, d)
    tq = 256

    dil_slopes = _alibi_slopes(DIL_HEADS)
    dil_bias = jnp.stack([_band_bias(dil_slopes, w // dl, dl) for (w, dl) in DIL_BRANCHES]) * LOG2E
    own_half = jnp.arange(2 * BLK) >= BLK
    dil_bias = jnp.stack([dil_bias, jnp.where(own_half, dil_bias, NEG)], axis=2)
    swa_bias = _band_bias(_alibi_slopes(SWA_Q_HEADS), SWA_WINDOW - 1, 1)
    swa_bias = swa_bias.reshape(SWA_KV_HEADS, (SWA_Q_HEADS // SWA_KV_HEADS) * BLK, 2 * BLK)
    rope_a, rope_bm, rope_bp = _rope_tables(s)

    for layer in range(depth):
        i = layer // 2
        g_attn = attn_norm[layer][None, :]
        if layer % 2 == 0:
            w_in = e_w_in[i]
            hq = FOX_HEADS * HEAD_DIM
            scale = HEAD_DIM ** -0.5
            cols = [w_in[:, 0:hq] * (scale * LOG2E), w_in[:, hq:2 * hq], w_in[:, 2 * hq:3 * hq]]
            o = 3 * hq + FOX_HEADS
            cols += [w_in[:, o:o + hq] * (scale * LOG2E), w_in[:, o + hq:o + 2 * hq], w_in[:, o + 2 * hq:o + 3 * hq]]
            w_main = jnp.concatenate(cols, axis=1).astype(BF16)
            wf_hi, wf_lo = _hi_lo(_pad_cols(w_in[:, 3 * hq:o], LANES))
            b_f = _pad_cols(e_b_f[i][None, :], LANES)
            proj, c = _proj_even(h, g_attn, w_main, wf_hi, wf_lo, b_f, s)
            proj = proj.reshape(b, s, -1)
            o_a = _causal_attention(proj, proj, proj, c.reshape(b, s, LANES), q_blk0=0, k_blk0=4, v_blk0=8,
                                    n_pairs=FOX_HEADS // 2)
            o_b = _dilated_attention(proj, dil_bias, q_blk0=12, k_blk0=16, v_blk0=20)
            w_out = e_w_out[i].astype(BF16)
        else:
            w_in = o_w_in[i]
            o1 = MLA_Q_RANK + MLA_KV_RANK
            o2 = o1 + MLA_ROPE
            sq = SWA_Q_HEADS * HEAD_DIM
            kpe_cols = jnp.pad(w_in[:, o1:o2], ((0, 0), (MLA_NOPE, LANES - MLA_NOPE - MLA_ROPE)))
            w_main = jnp.concatenate(
                [w_in[:, :o1], kpe_cols, w_in[:, o2:o2 + sq] * (HEAD_DIM ** -0.5), w_in[:, o2 + sq:]],
                axis=1).astype(BF16)
            dq = MLA_NOPE + MLA_ROPE
            wuq = o_w_uq[i].reshape(MLA_Q_RANK, MLA_HEADS, dq) * (dq ** -0.5 * LOG2E)
            wuq = jnp.pad(wuq, ((0, 0), (0, 0), (0, LANES - dq))).reshape(MLA_Q_RANK, MLA_HEADS * LANES)
            wukv = o_w_ukv[i].reshape(MLA_KV_RANK, MLA_HEADS, MLA_NOPE + MLA_V)
            wuk = jnp.pad(wukv[:, :, :MLA_NOPE], ((0, 0), (0, 0), (0, LANES - MLA_NOPE)))
            wuk = wuk.reshape(MLA_KV_RANK, MLA_HEADS * LANES)
            wuv = wukv[:, :, MLA_NOPE:].reshape(MLA_KV_RANK, MLA_HEADS * MLA_V)
            q_full, k_full, v_mla, swa = _proj_odd(
                h, g_attn, w_main, o_q_norm[i][None, :], o_kv_norm[i][None, :], wuq.astype(BF16),
                wuk.astype(BF16), wuv.astype(BF16), rope_a, rope_bm, rope_bp, s)
            o_a = _causal_attention(q_full.reshape(b, s, -1), k_full.reshape(b, s, -1), v_mla.reshape(b, s, -1),
                                    None, q_blk0=0, k_blk0=0, v_blk0=0, n_pairs=MLA_HEADS // 2)
            grp = SWA_Q_HEADS // SWA_KV_HEADS
            sink_col = jnp.repeat(o_sinks[i].reshape(SWA_KV_HEADS, grp), BLK, axis=1)[:, :, None]
            o_b = _swa_attention(swa.reshape(b, s, -1), swa_bias, sink_col, q_blk0=0, k_blk0=4, v_blk0=5)
            w_out = o_w_out[i].astype(BF16)

        wr_hi, wr_lo, b_r = _router_weights(moe_w_group[layer], moe_b_group[layer], moe_w_router[layer],
                                            moe_b_router[layer])
        h, hn, gates, counts = _out_router(h, o_a.reshape(n, -1), o_b.reshape(n, -1), w_out,
                                           ffn_norm[layer][None, :], wr_hi, wr_lo, b_r, tm=ROUTER_TILE)
        counts = counts[:, 0, :N_GROUPS].reshape(n // MOE_TILE, MOE_TILE // ROUTER_TILE, N_GROUPS).sum(axis=1)
        counts = counts.astype(jnp.int32).reshape(-1)
        wgu, wd = _moe_weights(moe_w_gate[layer], moe_w_up[layer], moe_w_down[layer])
        h = _moe(counts, hn, gates, h, wgu, wd, final_norm[None, :], final_norm=layer == depth - 1, tm=MOE_TILE)
    return h.reshape(b, s, d)
```

```python
import functools

import numpy as np
import jax
import jax.numpy as jnp
from jax import lax
from jax.experimental import pallas as pl
from jax.experimental.pallas import tpu as pltpu

F32 = jnp.float32
BF16 = jnp.bfloat16

D_MODEL = 1024
HEAD_DIM = 64
BLK = 128
NEG = -1e30
RMS_EPS = 1e-6
FOX_HEADS = 8
DIL_HEADS = 8
DIL_BRANCHES = ((128, 1), (512, 4), (2048, 16))
MLA_HEADS = 8
MLA_Q_RANK = 384
MLA_KV_RANK = 256
MLA_NOPE = 64
MLA_ROPE = 32
MLA_V = 64
ROPE_THETA = 10000.0
SWA_Q_HEADS = 8
SWA_KV_HEADS = 2
SWA_WINDOW = 128
N_GROUPS = 4
EXPERTS_PER_GROUP = 8
N_EXPERTS = N_GROUPS * EXPERTS_PER_GROUP
D_EXPERT = 256

LANES = 128
VMEM_LIMIT = 56 * 1024 * 1024

NT_DIMS = (((1,), (1,)), ((), ()))
LOG2E = 1.4426950408889634


def _params(*sem):
    return pltpu.CompilerParams(dimension_semantics=sem, vmem_limit_bytes=VMEM_LIMIT)


def _dot(a, b):
    return jnp.dot(a, b, preferred_element_type=F32)


def _rms(x, g):
    return x * lax.rsqrt(jnp.mean(x * x, axis=-1, keepdims=True) + RMS_EPS) * g


def _hi_lo(w):
    hi = w.astype(BF16)
    return hi, (w - hi.astype(F32)).astype(BF16)


def _dot_hi(x, w_hi, w_lo):
    x_hi = x.astype(BF16)
    x_lo = (x - x_hi.astype(F32)).astype(BF16)
    return _dot(x_hi, w_hi) + (_dot(x_hi, w_lo) + _dot(x_lo, w_hi))


def _split3(x):
    x1 = x.astype(BF16)
    r1 = x - x1.astype(F32)
    x2 = r1.astype(BF16)
    x3 = (r1 - x2.astype(F32)).astype(BF16)
    return x1, x2, x3


def _proj_even_kernel(x_ref, g_ref, w_ref, wf_hi_ref, wf_lo_ref, bf_ref, out_ref, c_ref, carry_ref, *, tiles_per_seq):
    xn = _rms(x_ref[...], g_ref[...])
    xb = xn.astype(BF16)
    n_out = out_ref.shape[1]
    for c in range(0, n_out, 512):
        out_ref[:, c:c + 512] = _dot(xb, w_ref[:, c:c + 512]).astype(out_ref.dtype)
    z = _dot_hi(xn, wf_hi_ref[...], wf_lo_ref[...]) + bf_ref[...]
    logf = jnp.minimum(z, 0.0) - jnp.log1p(jnp.exp(-jnp.abs(z)))

    @pl.when(pl.program_id(0) % tiles_per_seq == 0)
    def _():
        carry_ref[...] = jnp.zeros_like(carry_ref)

    tm = logf.shape[0]
    ri = lax.broadcasted_iota(jnp.int32, (tm, tm), 0)
    ci = lax.broadcasted_iota(jnp.int32, (tm, tm), 1)
    lower = jnp.where(ci <= ri, 1.0, 0.0).astype(BF16)
    l1, l2, l3 = _split3(logf)
    c = (_dot(lower, l1) + (_dot(lower, l2) + _dot(lower, l3))) + carry_ref[...]
    c_ref[...] = c
    carry_ref[...] = c[tm - 1:tm, :]


def _proj_even(h, g, w, wf_hi, wf_lo, bf, seq, tm=512):
    n, d = h.shape
    n_out = w.shape[1]
    return pl.pallas_call(
        functools.partial(_proj_even_kernel, tiles_per_seq=seq // tm),
        grid=(n // tm,),
        in_specs=[
            pl.BlockSpec((tm, d), lambda i: (i, 0)),
            pl.BlockSpec((1, d), lambda i: (0, 0)),
            pl.BlockSpec((d, n_out), lambda i: (0, 0)),
            pl.BlockSpec((d, LANES), lambda i: (0, 0)),
            pl.BlockSpec((d, LANES), lambda i: (0, 0)),
            pl.BlockSpec((1, LANES), lambda i: (0, 0)),
        ],
        out_specs=[
            pl.BlockSpec((tm, n_out), lambda i: (i, 0)),
            pl.BlockSpec((tm, LANES), lambda i: (i, 0)),
        ],
        out_shape=[
            jax.ShapeDtypeStruct((n, n_out), BF16),
            jax.ShapeDtypeStruct((n, LANES), F32),
        ],
        scratch_shapes=[pltpu.VMEM((1, LANES), F32)],
        compiler_params=_params("arbitrary"),
        name="proj_even",
    )(h, g, w, wf_hi, wf_lo, bf)


def _own_lanes(lane, h):
    return lane < HEAD_DIM if h == 0 else lane >= HEAD_DIM


def _causal_kernel(*refs, fox, tq, tk):
    if fox:
        q_ref, k_ref, v_ref, c_ref, o_ref, vaug_t, s_even, s_odd, kaug = refs
    else:
        q_ref, k_ref, v_ref, o_ref, vaug_t, s_even, s_odd = refs
    pair = pl.program_id(1)
    qi = pl.program_id(2)
    seq = v_ref.shape[1]

    @pl.when(qi == 0)
    def _():
        lane = lax.broadcasted_iota(jnp.int32, (tk, LANES), 1)
        feat = lax.broadcasted_iota(jnp.int32, (LANES, tk), 0)

        def fill(t, carry):
            rows = pl.ds(pl.multiple_of(t * tk, tk), tk)
            vt = jnp.transpose(v_ref[0, rows, :].astype(F32))
            if fox:
                kp = k_ref[0, rows, :].astype(F32)
                c = c_ref[0, rows, :]
            for h in range(2):
                vaug_t[h, t] = jnp.where(_own_lanes(feat, h), vt, 1.0).astype(BF16)
                if fox:
                    ch = jnp.sum(jnp.where(lane == 2 * pair + h, c, 0.0), axis=-1, keepdims=True)
                    c1, c2, c3 = _split3(ch * (-LOG2E))
                    base = HEAD_DIM if h == 0 else 0
                    extra = jnp.where(lane == base, c1.astype(F32),
                                      jnp.where(lane == base + 1, c2.astype(F32),
                                                jnp.where(lane == base + 2, c3.astype(F32), 0.0)))
                    kaug[h, rows, :] = jnp.where(_own_lanes(lane, h), kp, extra).astype(BF16)
            return carry

        lax.fori_loop(0, seq // tk, fill, 0)

    lane_q = lax.broadcasted_iota(jnp.int32, (tq, LANES), 1)
    qs = []
    for h in range(2):
        if fox:
            base = HEAD_DIM if h == 0 else 0
            ones = jnp.where((lane_q >= base) & (lane_q < base + 3), 1.0, 0.0)
            qs.append(jnp.where(_own_lanes(lane_q, h), q_ref[0].astype(F32), ones).astype(BF16))
        else:
            qs.append(q_ref[0, :, h * LANES:(h + 1) * LANES])

    key = lax.broadcasted_iota(jnp.int32, (tk, tq), 0)
    qry = lax.broadcasted_iota(jnp.int32, (tk, tq), 1)

    def scores(j, buf):
        start = pl.multiple_of(j * tk, tk)
        for h in range(2):
            if fox:
                kj = kaug[h, pl.ds(start, tk), :]
            else:
                kj = k_ref[0, pl.ds(start, tk), h * LANES:(h + 1) * LANES]
            buf[h] = lax.dot_general(kj, qs[h], NT_DIMS, preferred_element_type=F32)

    def absorb(j, buf, carry, masked):
        new = []
        for h in range(2):
            m, acc = carry[h]
            s = buf[h]
            if masked:
                s = jnp.where(key <= qry, s, NEG)
            m_new = jnp.maximum(m, jnp.max(s, axis=0, keepdims=True))
            p = jnp.exp2(s - m_new)
            acc = jnp.exp2(m - m_new) * acc + _dot(vaug_t[h, j], p.astype(BF16))
            new.append((m_new, acc))
        return tuple(new)

    def finish(carry):
        (_, acc0), (_, acc1) = carry
        feat_q = lax.broadcasted_iota(jnp.int32, (LANES, tq), 0)
        out_t = jnp.where(feat_q < HEAD_DIM, acc0 / acc0[HEAD_DIM:HEAD_DIM + 1, :], acc1 / acc1[0:1, :])
        o_ref[0] = jnp.transpose(out_t).astype(o_ref.dtype)

    def pair_step(t, carry):
        scores(2 * t + 1, s_odd)
        carry = absorb(2 * t, s_even, carry, False)
        scores(2 * t + 2, s_even)
        return absorb(2 * t + 1, s_odd, carry, False)

    init = tuple((jnp.full((1, tq), NEG, F32), jnp.zeros((LANES, tq), F32)) for _ in range(2))
    scores(0, s_even)
    carry = lax.fori_loop(0, qi // 2, pair_step, init)

    @pl.when(qi % 2 == 0)
    def _():
        finish(absorb(qi, s_even, carry, True))

    @pl.when(qi % 2 == 1)
    def _():
        scores(qi, s_odd)
        finish(absorb(qi, s_odd, absorb(qi - 1, s_even, carry, False), True))


def _causal_attention(q_arr, k_arr, v_arr, c_arr, *, q_blk0, k_blk0, v_blk0, n_pairs, tq=512):
    b, s, _ = q_arr.shape
    fox = c_arr is not None
    qk_w = LANES if fox else 2 * LANES
    in_specs = [
        pl.BlockSpec((1, tq, qk_w), lambda bi, p, qi: (bi, qi, q_blk0 + p)),
        pl.BlockSpec((1, s, qk_w), lambda bi, p, qi: (bi, 0, k_blk0 + p)),
        pl.BlockSpec((1, s, LANES), lambda bi, p, qi: (bi, 0, v_blk0 + p)),
    ]
    args = [q_arr, k_arr, v_arr]
    scratch = [pltpu.VMEM((2, s // tq, LANES, tq), BF16), pltpu.VMEM((2, tq, tq), F32), pltpu.VMEM((2, tq, tq), F32)]
    if fox:
        in_specs.append(pl.BlockSpec((1, s, LANES), lambda bi, p, qi: (bi, 0, 0)))
        args.append(c_arr)
        scratch.append(pltpu.VMEM((2, s, LANES), BF16))
    return pl.pallas_call(
        functools.partial(_causal_kernel, fox=fox, tq=tq, tk=tq),
        grid=(b, n_pairs, s // tq),
        in_specs=in_specs,
        out_specs=pl.BlockSpec((1, tq, LANES), lambda bi, p, qi: (bi, qi, p)),
        out_shape=jax.ShapeDtypeStruct((b, s, n_pairs * LANES), BF16),
        scratch_shapes=scratch,
        compiler_params=_params("parallel", "parallel", "arbitrary"),
        name="causal_attention",
    )(*args)


DIL_PAD = BLK * max(d for _, d in DIL_BRANCHES)
DIL_UNROLL = 4


def _dilated_kernel(q_ref, k_ref, v_ref, bias_ref, o_ref, qf, kf, vf, acc_s, m_s, l_s, *, seq):
    qf[...] = q_ref[0].astype(F32)
    kf[0:DIL_PAD, :] = jnp.zeros((DIL_PAD, LANES), F32)
    vf[0:DIL_PAD, :] = jnp.zeros((DIL_PAD, LANES), F32)
    kf[DIL_PAD:, :] = k_ref[0].astype(F32)
    vf[DIL_PAD:, :] = v_ref[0].astype(F32)
    lane = lax.broadcasted_iota(jnp.int32, (BLK, LANES), 1)
    head0 = lane < HEAD_DIM
    ones = jnp.ones((2 * BLK, LANES), BF16)

    def unit(br, dil, u, first):
        bi = u // dil
        q_start = (u % dil) + (dil * BLK) * bi
        k_start = q_start + (DIL_PAD - dil * BLK)
        no_prev = jnp.where(bi == 0, 1, 0)
        qb = qf[pl.ds(q_start, BLK, stride=dil), :].astype(BF16)
        kb = kf[pl.ds(k_start, 2 * BLK, stride=dil), :].astype(BF16)
        vb = vf[pl.ds(k_start, 2 * BLK, stride=dil), :].astype(BF16)
        v_aug = jnp.concatenate([vb, ones], axis=1)
        ms, accs = [], []
        for h in range(2):
            lo, hi = h * HEAD_DIM, (h + 1) * HEAD_DIM
            s = lax.dot_general(qb[:, lo:hi], kb[:, lo:hi], NT_DIMS, preferred_element_type=F32)
            s = s + bias_ref[br, h, no_prev]
            m = jnp.max(s, axis=-1, keepdims=True)
            ms.append(jnp.broadcast_to(m, (BLK, LANES)))
            accs.append(_dot(jnp.exp2(s - m).astype(BF16), v_aug))
        m_b = jnp.where(head0, ms[0], ms[1])
        acc_b = jnp.where(head0, accs[0][:, :LANES], accs[1][:, :LANES])
        l_b = jnp.where(head0, accs[0][:, LANES:], accs[1][:, LANES:])
        idx = pl.ds(q_start, BLK, stride=dil)
        if first:
            m_s[idx, :] = m_b
            l_s[idx, :] = l_b
            acc_s[idx, :] = acc_b
            return
        m_old = m_s[idx, :]
        m_new = jnp.maximum(m_old, m_b)
        a_old = jnp.exp2(m_old - m_new)
        a_b = jnp.exp2(m_b - m_new)
        m_s[idx, :] = m_new
        l_s[idx, :] = a_old * l_s[idx, :] + a_b * l_b
        acc_s[idx, :] = a_old * acc_s[idx, :] + a_b * acc_b

    order = sorted(range(len(DIL_BRANCHES)), key=lambda i: -DIL_BRANCHES[i][1])
    for pos, br in enumerate(order):
        dil = DIL_BRANCHES[br][1]

        def group(t, carry, br=br, dil=dil, first=pos == 0):
            for i in range(DIL_UNROLL):
                unit(br, dil, t * DIL_UNROLL + i, first)
            return carry

        lax.fori_loop(0, seq // (BLK * DIL_UNROLL), group, 0)

    o_ref[0] = (acc_s[...] / l_s[...]).astype(o_ref.dtype)


def _dilated_attention(proj, bias, *, q_blk0, k_blk0, v_blk0):
    b, s, _ = proj.shape
    n_pairs = DIL_HEADS // 2
    blk = lambda off: pl.BlockSpec((1, s, LANES), lambda bi, p: (bi, 0, off + p))
    state = [pltpu.VMEM((s, LANES), F32) for _ in range(4)]
    padded = [pltpu.VMEM((DIL_PAD + s, LANES), F32) for _ in range(2)]
    return pl.pallas_call(
        functools.partial(_dilated_kernel, seq=s),
        grid=(b, n_pairs),
        in_specs=[blk(q_blk0), blk(k_blk0), blk(v_blk0),
                  pl.BlockSpec((len(DIL_BRANCHES), 2, 2, BLK, 2 * BLK), lambda bi, p: (0, p, 0, 0, 0))],
        out_specs=pl.BlockSpec((1, s, LANES), lambda bi, p: (bi, 0, p)),
        out_shape=jax.ShapeDtypeStruct((b, s, n_pairs * LANES), BF16),
        scratch_shapes=[state[0], padded[0], padded[1], state[1], state[2], state[3]],
        compiler_params=_params("parallel", "parallel"),
        name="dilated_attention",
    )(proj, proj, proj, bias)


def _swa_kernel(q_ref, k_ref, v_ref, bias_ref, sink_ref, o_ref, *, tc):
    ci = pl.program_id(1)
    grp = SWA_Q_HEADS // SWA_KV_HEADS
    col = lax.broadcasted_iota(jnp.int32, (grp * BLK, 2 * BLK), 1)
    for bb in range(tc // BLK):
        own = pl.multiple_of(ci * tc + bb * BLK, BLK)
        prev = pl.multiple_of(jnp.maximum(own - BLK, 0), BLK)
        no_prev = jnp.where(own == 0, NEG, 0.0).astype(F32)
        pieces = []
        for kv in range(SWA_KV_HEADS):
            lo, hi = kv * HEAD_DIM, (kv + 1) * HEAD_DIM
            k2 = jnp.concatenate([k_ref[0, pl.ds(prev, BLK), lo:hi], k_ref[0, pl.ds(own, BLK), lo:hi]], axis=0)
            v2 = jnp.concatenate([v_ref[0, pl.ds(prev, BLK), lo:hi], v_ref[0, pl.ds(own, BLK), lo:hi]], axis=0)
            q4 = jnp.concatenate(
                [q_ref[0, bb * BLK:(bb + 1) * BLK, (kv * grp + g) * HEAD_DIM:(kv * grp + g + 1) * HEAD_DIM]
                 for g in range(grp)], axis=0)
            s = lax.dot_general(q4, k2, NT_DIMS, preferred_element_type=F32)
            s = s + bias_ref[kv] + jnp.where(col < BLK, no_prev, 0.0)
            sink = sink_ref[kv]
            m = jnp.maximum(jnp.max(s, axis=-1, keepdims=True), sink)
            p = jnp.exp(s - m)
            den = jnp.sum(p, axis=-1, keepdims=True) + jnp.exp(sink - m)
            o = _dot(p.astype(BF16), v2) / den
            pieces += [o[g * BLK:(g + 1) * BLK] for g in range(grp)]
        o_ref[0, bb * BLK:(bb + 1) * BLK, :] = jnp.concatenate(pieces, axis=-1).astype(o_ref.dtype)


def _swa_attention(proj, bias, sink_col, *, q_blk0, k_blk0, v_blk0, tc=512):
    b, s, _ = proj.shape
    qw = SWA_Q_HEADS * HEAD_DIM
    return pl.pallas_call(
        functools.partial(_swa_kernel, tc=tc),
        grid=(b, s // tc),
        in_specs=[
            pl.BlockSpec((1, tc, qw), lambda bi, ci: (bi, ci, q_blk0 * LANES // qw)),
            pl.BlockSpec((1, s, LANES), lambda bi, ci: (bi, 0, k_blk0)),
            pl.BlockSpec((1, s, LANES), lambda bi, ci: (bi, 0, v_blk0)),
            pl.BlockSpec(bias.shape, lambda bi, ci: (0, 0, 0)),
            pl.BlockSpec(sink_col.shape, lambda bi, ci: (0, 0, 0)),
        ],
        out_specs=pl.BlockSpec((1, tc, qw), lambda bi, ci: (bi, ci, 0)),
        out_shape=jax.ShapeDtypeStruct((b, s, qw), BF16),
        compiler_params=_params("parallel", "parallel"),
        name="swa_attention",
    )(proj, proj, proj, bias, sink_col)


def _rope128(x, a, bm, bp):
    return x * a + pltpu.roll(x, LANES - 16, axis=1) * bm + pltpu.roll(x, 16, axis=1) * bp


def _proj_odd_kernel(x_ref, g_ref, w_ref, qn_ref, kvn_ref, wuq_ref, wuk_ref, wuv_ref, a_ref, bm_ref, bp_ref,
                     q_out, k_out, v_out, swa_out):
    xb = _rms(x_ref[...], g_ref[...]).astype(BF16)
    w = w_ref
    c_q = _dot(xb, w[:, 0:MLA_Q_RANK])
    c_kv = _dot(xb, w[:, MLA_Q_RANK:MLA_Q_RANK + MLA_KV_RANK])
    o1 = MLA_Q_RANK + MLA_KV_RANK
    kpe = _dot(xb, w[:, o1:o1 + LANES])
    o2 = o1 + LANES
    swa_out[...] = _dot(xb, w[:, o2:]).astype(swa_out.dtype)
    a, bm, bp = a_ref[...], bm_ref[...], bp_ref[...]
    kpe = _rope128(kpe, a, bm, bp)
    cqn = _rms(c_q, qn_ref[...]).astype(BF16)
    ckvn = _rms(c_kv, kvn_ref[...]).astype(BF16)
    v_out[...] = _dot(ckvn, wuv_ref[...]).astype(v_out.dtype)
    for h in range(MLA_HEADS):
        sl = slice(h * LANES, (h + 1) * LANES)
        q_out[:, sl] = _rope128(_dot(cqn, wuq_ref[:, sl]), a, bm, bp).astype(q_out.dtype)
        k_out[:, sl] = (_dot(ckvn, wuk_ref[:, sl]) + kpe).astype(k_out.dtype)


def _proj_odd(h, g, w, qn, kvn, wuq, wuk, wuv, rope_a, rope_bm, rope_bp, seq, tm=512):
    n, d = h.shape
    n_in = w.shape[1]
    n_swa = n_in - (MLA_Q_RANK + MLA_KV_RANK + LANES)
    tiles_per_seq = seq // tm
    full = lambda shape: pl.BlockSpec(shape, lambda i: (0, 0))
    rope = pl.BlockSpec((tm, LANES), lambda i: (i % tiles_per_seq, 0))
    return pl.pallas_call(
        _proj_odd_kernel,
        grid=(n // tm,),
        in_specs=[
            pl.BlockSpec((tm, d), lambda i: (i, 0)), full((1, d)), full(w.shape),
            full(qn.shape), full(kvn.shape), full(wuq.shape), full(wuk.shape), full(wuv.shape),
            rope, rope, rope,
        ],
        out_specs=[
            pl.BlockSpec((tm, MLA_HEADS * LANES), lambda i: (i, 0)),
            pl.BlockSpec((tm, MLA_HEADS * LANES), lambda i: (i, 0)),
            pl.BlockSpec((tm, MLA_HEADS * MLA_V), lambda i: (i, 0)),
            pl.BlockSpec((tm, n_swa), lambda i: (i, 0)),
        ],
        out_shape=[
            jax.ShapeDtypeStruct((n, MLA_HEADS * LANES), BF16),
            jax.ShapeDtypeStruct((n, MLA_HEADS * LANES), BF16),
            jax.ShapeDtypeStruct((n, MLA_HEADS * MLA_V), BF16),
            jax.ShapeDtypeStruct((n, n_swa), BF16),
        ],
        compiler_params=_params("parallel"),
        name="proj_odd",
    )(h, g, w, qn, kvn, wuq, wuk, wuv, rope_a, rope_bm, rope_bp)


def _out_router_kernel(h_ref, oa_ref, ob_ref, wo_ref, g_ref, wr_hi_ref, wr_lo_ref, br_ref,
                       h_out, hn_out, gates_out, counts_out):
    half = oa_ref.shape[1]
    h = h_ref[...] + _dot(oa_ref[...], wo_ref[0:half, :]) + _dot(ob_ref[...], wo_ref[half:, :])
    h_out[...] = h
    hn = _rms(h, g_ref[...])
    hn_out[...] = hn.astype(hn_out.dtype)
    z = _dot_hi(hn, wr_hi_ref[...], wr_lo_ref[...]) + br_ref[...]
    lane = lax.broadcasted_iota(jnp.int32, z.shape, 1)
    big = jnp.int32(LANES)
    is_grp = (lane >= N_EXPERTS) & (lane < N_EXPERTS + N_GROUPS)
    zg = jnp.where(is_grp, z, -jnp.inf)
    g_max = jnp.max(zg, axis=-1, keepdims=True)
    g_w = 1.0 / jnp.sum(jnp.exp(zg - g_max), axis=-1, keepdims=True)
    g_idx = jnp.min(jnp.where(zg == g_max, lane - N_EXPERTS, big), axis=-1, keepdims=True)
    in_grp = (lane < N_EXPERTS) & ((lane // EXPERTS_PER_GROUP) == g_idx)
    ze = jnp.where(in_grp, z, -jnp.inf)
    v1 = jnp.max(ze, axis=-1, keepdims=True)
    i1 = jnp.min(jnp.where(ze == v1, lane, big), axis=-1, keepdims=True)
    ze2 = jnp.where(lane == i1, -jnp.inf, ze)
    v2 = jnp.max(ze2, axis=-1, keepdims=True)
    i2 = jnp.min(jnp.where(ze2 == v2, lane, big), axis=-1, keepdims=True)
    e2 = jnp.exp(v2 - v1)
    w1 = g_w / (1.0 + e2)
    w2 = g_w * e2 / (1.0 + e2)
    for grp in range(N_GROUPS):
        first = grp * EXPERTS_PER_GROUP
        slot = jnp.where(lane == i1 - first, w1, 0.0) + jnp.where(lane == i2 - first, w2, 0.0)
        slot = jnp.where(lane < EXPERTS_PER_GROUP, slot, 0.0)
        slot = jnp.where((lane == EXPERTS_PER_GROUP) & (g_idx == grp), 1.0, slot)
        gates_out[:, grp * LANES:(grp + 1) * LANES] = slot
    routed = jnp.where((lane == g_idx) & (lane < N_GROUPS), 1.0, 0.0)
    counts_out[0] = jnp.broadcast_to(jnp.sum(routed, axis=0, keepdims=True), counts_out.shape[1:])


def _out_router(h, oa, ob, wo, g, wr_hi, wr_lo, br, tm=512):
    n, d = h.shape
    half = oa.shape[1]
    full = lambda shape: pl.BlockSpec(shape, lambda i: (0, 0))
    tile = lambda w: pl.BlockSpec((tm, w), lambda i: (i, 0))
    return pl.pallas_call(
        _out_router_kernel,
        grid=(n // tm,),
        in_specs=[tile(d), tile(half), tile(half), full(wo.shape), full((1, d)),
                  full(wr_hi.shape), full(wr_lo.shape), full((1, LANES))],
        out_specs=[tile(d), tile(d), tile(N_GROUPS * LANES), pl.BlockSpec((1, 8, LANES), lambda i: (i, 0, 0))],
        out_shape=[
            jax.ShapeDtypeStruct((n, d), F32),
            jax.ShapeDtypeStruct((n, d), BF16),
            jax.ShapeDtypeStruct((n, N_GROUPS * LANES), F32),
            jax.ShapeDtypeStruct((n // tm, 8, LANES), F32),
        ],
        compiler_params=_params("parallel"),
        name="out_router",
    )(h, oa, ob, wo, g, wr_hi, wr_lo, br)


MOE_CHUNK = 256
MOE_TILE = 1024
ROUTER_TILE = 512


def _moe_kernel(counts_ref, hn_ref, gates_ref, h_ref, wgu_ref, wd_ref, fg_ref, o_ref, acc_ref, tri_ref, *, final_norm):
    i = pl.program_id(0)
    grp = pl.program_id(1)
    tm = hn_ref.shape[0]

    @pl.when((i == 0) & (grp == 0))
    def _():
        ri = lax.broadcasted_iota(jnp.int32, (tm, tm), 0)
        ci = lax.broadcasted_iota(jnp.int32, (tm, tm), 1)
        tri_ref[...] = jnp.where(ci < ri, 1.0, 0.0).astype(BF16)

    @pl.when(grp == 0)
    def _():
        acc_ref[...] = jnp.zeros_like(acc_ref)

    gates = gates_ref[...]
    lane = lax.broadcasted_iota(jnp.int32, gates.shape, 1)
    flag = jnp.where(lane == EXPERTS_PER_GROUP, gates, 0.0)
    rank = _dot(tri_ref[...], flag.astype(BF16))
    pos = jnp.where(flag > 0.0, rank, -1.0)
    pos_col = jnp.sum(jnp.where(lane == EXPERTS_PER_GROUP, pos, 0.0), axis=-1, keepdims=True)
    pos_row = jnp.transpose(jnp.broadcast_to(pos_col, (tm, LANES)))[0:1, :]
    g_hi = gates.astype(BF16)
    g_lo = (gates - g_hi.astype(F32)).astype(BF16)
    slot_col = lax.broadcasted_iota(jnp.int32, (MOE_CHUNK, 1), 0).astype(F32)
    slot_row = lax.broadcasted_iota(jnp.int32, (1, MOE_CHUNK), 1).astype(F32)

    def chunk(k, carry):
        base = (k * MOE_CHUNK).astype(F32)
        gather = jnp.where(pos_row == slot_col + base, 1.0, 0.0).astype(BF16)
        scatter = jnp.where(pos_col == slot_row + base, 1.0, 0.0).astype(BF16)
        xc = _dot(gather, hn_ref[...]).astype(BF16)
        gc = _dot(gather, g_hi) + _dot(gather, g_lo)
        y = jnp.zeros((MOE_CHUNK, acc_ref.shape[1]), F32)
        for e in range(EXPERTS_PER_GROUP):
            au = _dot(xc, wgu_ref[0, e])
            a, u = au[:, :D_EXPERT], au[:, D_EXPERT:]
            act = (a * jax.nn.sigmoid(a)) * u * gc[:, e:e + 1]
            y = y + _dot(act.astype(BF16), wd_ref[0, e])
        acc_ref[...] += _dot(scatter, y.astype(BF16))
        return carry

    n_chunks = (counts_ref[i * N_GROUPS + grp] + (MOE_CHUNK - 1)) // MOE_CHUNK
    lax.fori_loop(0, n_chunks, chunk, 0)

    @pl.when(grp == N_GROUPS - 1)
    def _():
        h = h_ref[...] + acc_ref[...]
        o_ref[...] = _rms(h, fg_ref[...]) if final_norm else h


def _moe(counts, hn, gates, h, wgu, wd, fg, final_norm, tm=1024):
    n, d = h.shape
    once = pl.Buffered(1)
    grid_spec = pltpu.PrefetchScalarGridSpec(
        num_scalar_prefetch=1,
        grid=(n // tm, N_GROUPS),
        in_specs=[
            pl.BlockSpec((tm, d), lambda i, g, c: (i, 0), pipeline_mode=once),
            pl.BlockSpec((tm, LANES), lambda i, g, c: (i, g)),
            pl.BlockSpec((tm, d), lambda i, g, c: (i, 0), pipeline_mode=once),
            pl.BlockSpec((1, EXPERTS_PER_GROUP, d, 2 * D_EXPERT), lambda i, g, c: (g, 0, 0, 0)),
            pl.BlockSpec((1, EXPERTS_PER_GROUP, D_EXPERT, d), lambda i, g, c: (g, 0, 0, 0)),
            pl.BlockSpec((1, d), lambda i, g, c: (0, 0)),
        ],
        out_specs=pl.BlockSpec((tm, d), lambda i, g, c: (i, 0), pipeline_mode=once),
        scratch_shapes=[pltpu.VMEM((tm, d), F32), pltpu.VMEM((tm, tm), BF16)],
    )
    return pl.pallas_call(
        functools.partial(_moe_kernel, final_norm=final_norm),
        grid_spec=grid_spec,
        out_shape=jax.ShapeDtypeStruct((n, d), F32),
        compiler_params=_params("arbitrary", "arbitrary"),
        name="moe_experts",
    )(counts, hn, gates, h, wgu, wd, fg)


def _alibi_slopes(n):
    return jnp.exp2(-8.0 * jnp.arange(1, n + 1, dtype=F32) / n)


def _band_bias(slopes, max_steps, step_dist):
    steps = (jnp.arange(BLK)[:, None] + BLK) - jnp.arange(2 * BLK)[None, :]
    in_band = (steps >= 0) & (steps <= max_steps)
    dist = (steps * step_dist).astype(F32)
    return jnp.where(in_band[None], -slopes.astype(F32)[:, None, None] * dist[None], NEG)


def _rope_tables(s):
    inv = ROPE_THETA ** (-jnp.arange(0, MLA_ROPE, 2, dtype=F32) / MLA_ROPE)
    ang = jnp.arange(s, dtype=F32)[:, None] * inv[None, :]
    cos, sin = jnp.cos(ang), jnp.sin(ang)
    half = MLA_ROPE // 2
    zeros_tail = jnp.zeros((s, LANES - MLA_NOPE - MLA_ROPE), F32)
    a = jnp.concatenate([jnp.ones((s, MLA_NOPE), F32), cos, cos, zeros_tail], axis=1)
    zeros_nope = jnp.zeros((s, MLA_NOPE), F32)
    zeros_half = jnp.zeros((s, half), F32)
    bm = jnp.concatenate([zeros_nope, -sin, zeros_half, zeros_tail], axis=1)
    bp = jnp.concatenate([zeros_nope, zeros_half, sin, zeros_tail], axis=1)
    return a, bm, bp


def _pad_cols(w, width):
    return jnp.pad(w, ((0, 0), (0, width - w.shape[1])))


def _router_weights(w_group, b_group, w_router, b_router):
    w = _pad_cols(jnp.concatenate([w_router, w_group], axis=1), LANES)
    b = _pad_cols(jnp.concatenate([b_router, b_group])[None, :], LANES)
    hi, lo = _hi_lo(w)
    return hi, lo, b


def _moe_weights(w_gate, w_up, w_down):
    d = w_gate.shape[-2]
    return jnp.concatenate([w_gate, w_up], axis=-1).astype(BF16), w_down.astype(BF16)


def kernel(x, attn_norm, ffn_norm, final_norm, e_w_in, e_b_f, e_w_out, o_w_in, o_q_norm, o_kv_norm, o_w_uq,
           o_w_ukv, o_sinks, o_w_out, moe_w_group, moe_b_group, moe_w_router, moe_b_router, moe_w_gate,
           moe_w_up, moe_w_down):
    b, s, d = x.shape
    n = b * s
    depth = attn_norm.shape[0]
    assert s % (BLK * DIL_BRANCHES[-1][1]) == 0 and d == D_MODEL
    h = x.reshape(n, d)
    tq = 256

    dil_slopes = _alibi_slopes(DIL_HEADS)
    dil_bias = jnp.stack([_band_bias(dil_slopes, w // dl, dl) for (w, dl) in DIL_BRANCHES]) * LOG2E
    own_half = jnp.arange(2 * BLK) >= BLK
    dil_bias = jnp.stack([dil_bias, jnp.where(own_half, dil_bias, NEG)], axis=2)
    swa_bias = _band_bias(_alibi_slopes(SWA_Q_HEADS), SWA_WINDOW - 1, 1)
    swa_bias = swa_bias.reshape(SWA_KV_HEADS, (SWA_Q_HEADS // SWA_KV_HEADS) * BLK, 2 * BLK)
    rope_a, rope_bm, rope_bp = _rope_tables(s)

    for layer in range(depth):
        i = layer // 2
        g_attn = attn_norm[layer][None, :]
        if layer % 2 == 0:
            w_in = e_w_in[i]
            hq = FOX_HEADS * HEAD_DIM
            scale = HEAD_DIM ** -0.5
            cols = [w_in[:, 0:hq] * (scale * LOG2E), w_in[:, hq:2 * hq], w_in[:, 2 * hq:3 * hq]]
            o = 3 * hq + FOX_HEADS
            cols += [w_in[:, o:o + hq] * (scale * LOG2E), w_in[:, o + hq:o + 2 * hq], w_in[:, o + 2 * hq:o + 3 * hq]]
            w_main = jnp.concatenate(cols, axis=1).astype(BF16)
            wf_hi, wf_lo = _hi_lo(_pad_cols(w_in[:, 3 * hq:o], LANES))
            b_f = _pad_cols(e_b_f[i][None, :], LANES)
            proj, c = _proj_even(h, g_attn, w_main, wf_hi, wf_lo, b_f, s)
            proj = proj.reshape(b, s, -1)
            o_a = _causal_attention(proj, proj, proj, c.reshape(b, s, LANES), q_blk0=0, k_blk0=4, v_blk0=8,
                                    n_pairs=FOX_HEADS // 2)
            o_b = _dilated_attention(proj, dil_bias, q_blk0=12, k_blk0=16, v_blk0=20)
            w_out = e_w_out[i].astype(BF16)
        else:
            w_in = o_w_in[i]
            o1 = MLA_Q_RANK + MLA_KV_RANK
            o2 = o1 + MLA_ROPE
            sq = SWA_Q_HEADS * HEAD_DIM
            kpe_cols = jnp.pad(w_in[:, o1:o2], ((0, 0), (MLA_NOPE, LANES - MLA_NOPE - MLA_ROPE)))
            w_main = jnp.concatenate(
                [w_in[:, :o1], kpe_cols, w_in[:, o2:o2 + sq] * (HEAD_DIM ** -0.5), w_in[:, o2 + sq:]],
                axis=1).astype(BF16)
            dq = MLA_NOPE + MLA_ROPE
            wuq = o_w_uq[i].reshape(MLA_Q_RANK, MLA_HEADS, dq) * (dq ** -0.5 * LOG2E)
            wuq = jnp.pad(wuq, ((0, 0), (0, 0), (0, LANES - dq))).reshape(MLA_Q_RANK, MLA_HEADS * LANES)
            wukv = o_w_ukv[i].reshape(MLA_KV_RANK, MLA_HEADS, MLA_NOPE + MLA_V)
            wuk = jnp.pad(wukv[:, :, :MLA_NOPE], ((0, 0), (0, 0), (0, LANES - MLA_NOPE)))
            wuk = wuk.reshape(MLA_KV_RANK, MLA_HEADS * LANES)
            wuv = wukv[:, :, MLA_NOPE:].reshape(MLA_KV_RANK, MLA_HEADS * MLA_V)
            q_full, k_full, v_mla, swa = _proj_odd(
                h, g_attn, w_main, o_q_norm[i][None, :], o_kv_norm[i][None, :], wuq.astype(BF16),
                wuk.astype(BF16), wuv.astype(BF16), rope_a, rope_bm, rope_bp, s)
            o_a = _causal_attention(q_full.reshape(b, s, -1), k_full.reshape(b, s, -1), v_mla.reshape(b, s, -1),
                                    None, q_blk0=0, k_blk0=0, v_blk0=0, n_pairs=MLA_HEADS // 2)
            grp = SWA_Q_HEADS // SWA_KV_HEADS
            sink_col = jnp.repeat(o_sinks[i].reshape(SWA_KV_HEADS, grp), BLK, axis=1)[:, :, None]
            o_b = _swa_attention(swa.reshape(b, s, -1), swa_bias, sink_col, q_blk0=0, k_blk0=4, v_blk0=5)
            w_out = o_w_out[i].astype(BF16)

        wr_hi, wr_lo, b_r = _router_weights(moe_w_group[layer], moe_b_group[layer], moe_w_router[layer],
                                            moe_b_router[layer])
        h, hn, gates, counts = _out_router(h, o_a.reshape(n, -1), o_b.reshape(n, -1), w_out,
                                           ffn_norm[layer][None, :], wr_hi, wr_lo, b_r, tm=ROUTER_TILE)
        counts = counts[:, 0, :N_GROUPS].reshape(n // MOE_TILE, MOE_TILE // ROUTER_TILE, N_GROUPS).sum(axis=1)
        counts = counts.astype(jnp.int32).reshape(-1)
        wgu, wd = _moe_weights(moe_w_gate[layer], moe_w_up[layer], moe_w_down[layer])
        h = _moe(counts, hn, gates, h, wgu, wd, final_norm[None, :], final_norm=layer == depth - 1, tm=MOE_TILE)
    return h.reshape(b, s, d)
```

```python
import functools

import numpy as np
import jax
import jax.numpy as jnp
from jax import lax
from jax.experimental import pallas as pl
from jax.experimental.pallas import tpu as pltpu

F32 = jnp.float32
BF16 = jnp.bfloat16

D_MODEL = 1024
HEAD_DIM = 64
BLK = 128
NEG = -1e30
RMS_EPS = 1e-6
FOX_HEADS = 8
DIL_HEADS = 8
DIL_BRANCHES = ((128, 1), (512, 4), (2048, 16))
MLA_HEADS = 8
MLA_Q_RANK = 384
MLA_KV_RANK = 256
MLA_NOPE = 64
MLA_ROPE = 32
MLA_V = 64
ROPE_THETA = 10000.0
SWA_Q_HEADS = 8
SWA_KV_HEADS = 2
SWA_WINDOW = 128
N_GROUPS = 4
EXPERTS_PER_GROUP = 8
N_EXPERTS = N_GROUPS * EXPERTS_PER_GROUP
D_EXPERT = 256

LANES = 128
VMEM_LIMIT = 56 * 1024 * 1024

NT_DIMS = (((1,), (1,)), ((), ()))
LOG2E = 1.4426950408889634


def _params(*sem):
    return pltpu.CompilerParams(dimension_semantics=sem, vmem_limit_bytes=VMEM_LIMIT)


def _dot(a, b):
    return jnp.dot(a, b, preferred_element_type=F32)


def _rms(x, g):
    return x * lax.rsqrt(jnp.mean(x * x, axis=-1, keepdims=True) + RMS_EPS) * g


def _hi_lo(w):
    hi = w.astype(BF16)
    return hi, (w - hi.astype(F32)).astype(BF16)


def _dot_hi(x, w_hi, w_lo):
    x_hi = x.astype(BF16)
    x_lo = (x - x_hi.astype(F32)).astype(BF16)
    return _dot(x_hi, w_hi) + (_dot(x_hi, w_lo) + _dot(x_lo, w_hi))


def _split3(x):
    x1 = x.astype(BF16)
    r1 = x - x1.astype(F32)
    x2 = r1.astype(BF16)
    x3 = (r1 - x2.astype(F32)).astype(BF16)
    return x1, x2, x3


def _proj_even_kernel(x_ref, g_ref, w_ref, wf_hi_ref, wf_lo_ref, bf_ref, out_ref, c_ref, carry_ref, *, tiles_per_seq):
    xn = _rms(x_ref[...], g_ref[...])
    xb = xn.astype(BF16)
    n_out = out_ref.shape[1]
    for c in range(0, n_out, 512):
        out_ref[:, c:c + 512] = _dot(xb, w_ref[:, c:c + 512]).astype(out_ref.dtype)
    z = _dot_hi(xn, wf_hi_ref[...], wf_lo_ref[...]) + bf_ref[...]
    logf = jnp.minimum(z, 0.0) - jnp.log1p(jnp.exp(-jnp.abs(z)))

    @pl.when(pl.program_id(0) % tiles_per_seq == 0)
    def _():
        carry_ref[...] = jnp.zeros_like(carry_ref)

    tm = logf.shape[0]
    ri = lax.broadcasted_iota(jnp.int32, (tm, tm), 0)
    ci = lax.broadcasted_iota(jnp.int32, (tm, tm), 1)
    lower = jnp.where(ci <= ri, 1.0, 0.0).astype(BF16)
    l1, l2, l3 = _split3(logf)
    c = (_dot(lower, l1) + (_dot(lower, l2) + _dot(lower, l3))) + carry_ref[...]
    c_ref[...] = c
    carry_ref[...] = c[tm - 1:tm, :]


def _proj_even(h, g, w, wf_hi, wf_lo, bf, seq, tm=512):
    n, d = h.shape
    n_out = w.shape[1]
    return pl.pallas_call(
        functools.partial(_proj_even_kernel, tiles_per_seq=seq // tm),
        grid=(n // tm,),
        in_specs=[
            pl.BlockSpec((tm, d), lambda i: (i, 0)),
            pl.BlockSpec((1, d), lambda i: (0, 0)),
            pl.BlockSpec((d, n_out), lambda i: (0, 0)),
            pl.BlockSpec((d, LANES), lambda i: (0, 0)),
            pl.BlockSpec((d, LANES), lambda i: (0, 0)),
            pl.BlockSpec((1, LANES), lambda i: (0, 0)),
        ],
        out_specs=[
            pl.BlockSpec((tm, n_out), lambda i: (i, 0)),
            pl.BlockSpec((tm, LANES), lambda i: (i, 0)),
        ],
        out_shape=[
            jax.ShapeDtypeStruct((n, n_out), BF16),
            jax.ShapeDtypeStruct((n, LANES), F32),
        ],
        scratch_shapes=[pltpu.VMEM((1, LANES), F32)],
        compiler_params=_params("arbitrary"),
        name="proj_even",
    )(h, g, w, wf_hi, wf_lo, bf)


def _own_lanes(lane, h):
    return lane < HEAD_DIM if h == 0 else lane >= HEAD_DIM


def _causal_kernel(*refs, fox, tq, tk):
    if fox:
        q_ref, k_ref, v_ref, c_ref, o_ref, vaug_t, s_even, s_odd, kaug = refs
    else:
        q_ref, k_ref, v_ref, o_ref, vaug_t, s_even, s_odd = refs
    pair = pl.program_id(1)
    qi = pl.program_id(2)
    seq = v_ref.shape[1]

    @pl.when(qi == 0)
    def _():
        lane = lax.broadcasted_iota(jnp.int32, (tk, LANES), 1)
        feat = lax.broadcasted_iota(jnp.int32, (LANES, tk), 0)

        def fill(t, carry):
            rows = pl.ds(pl.multiple_of(t * tk, tk), tk)
            vt = jnp.transpose(v_ref[0, rows, :].astype(F32))
            if fox:
                kp = k_ref[0, rows, :].astype(F32)
                c = c_ref[0, rows, :]
            for h in range(2):
                vaug_t[h, t] = jnp.where(_own_lanes(feat, h), vt, 1.0).astype(BF16)
                if fox:
                    ch = jnp.sum(jnp.where(lane == 2 * pair + h, c, 0.0), axis=-1, keepdims=True)
                    c1, c2, c3 = _split3(ch * (-LOG2E))
                    base = HEAD_DIM if h == 0 else 0
                    extra = jnp.where(lane == base, c1.astype(F32),
                                      jnp.where(lane == base + 1, c2.astype(F32),
                                                jnp.where(lane == base + 2, c3.astype(F32), 0.0)))
                    kaug[h, rows, :] = jnp.where(_own_lanes(lane, h), kp, extra).astype(BF16)
            return carry

        lax.fori_loop(0, seq // tk, fill, 0)

    lane_q = lax.broadcasted_iota(jnp.int32, (tq, LANES), 1)
    qs = []
    for h in range(2):
        if fox:
            base = HEAD_DIM if h == 0 else 0
            ones = jnp.where((lane_q >= base) & (lane_q < base + 3), 1.0, 0.0)
            qs.append(jnp.where(_own_lanes(lane_q, h), q_ref[0].astype(F32), ones).astype(BF16))
        else:
            qs.append(q_ref[0, :, h * LANES:(h + 1) * LANES])

    key = lax.broadcasted_iota(jnp.int32, (tk, tq), 0)
    qry = lax.broadcasted_iota(jnp.int32, (tk, tq), 1)

    def scores(j, buf):
        start = pl.multiple_of(j * tk, tk)
        for h in range(2):
            if fox:
                kj = kaug[h, pl.ds(start, tk), :]
            else:
                kj = k_ref[0, pl.ds(start, tk), h * LANES:(h + 1) * LANES]
            buf[h] = lax.dot_general(kj, qs[h], NT_DIMS, preferred_element_type=F32)

    def absorb(j, buf, carry, masked):
        new = []
        for h in range(2):
            m, acc = carry[h]
            s = buf[h]
            if masked:
                s = jnp.where(key <= qry, s, NEG)
            m_new = jnp.maximum(m, jnp.max(s, axis=0, keepdims=True))
            p = jnp.exp2(s - m_new)
            acc = jnp.exp2(m - m_new) * acc + _dot(vaug_t[h, j], p.astype(BF16))
            new.append((m_new, acc))
        return tuple(new)

    def finish(carry):
        (_, acc0), (_, acc1) = carry
        feat_q = lax.broadcasted_iota(jnp.int32, (LANES, tq), 0)
        out_t = jnp.where(feat_q < HEAD_DIM, acc0 / acc0[HEAD_DIM:HEAD_DIM + 1, :], acc1 / acc1[0:1, :])
        o_ref[0] = jnp.transpose(out_t).astype(o_ref.dtype)

    def pair_step(t, carry):
        scores(2 * t + 1, s_odd)
        carry = absorb(2 * t, s_even, carry, False)
        scores(2 * t + 2, s_even)
        return absorb(2 * t + 1, s_odd, carry, False)

    init = tuple((jnp.full((1, tq), NEG, F32), jnp.zeros((LANES, tq), F32)) for _ in range(2))
    scores(0, s_even)
    carry = lax.fori_loop(0, qi // 2, pair_step, init)

    @pl.when(qi % 2 == 0)
    def _():
        finish(absorb(qi, s_even, carry, True))

    @pl.when(qi % 2 == 1)
    def _():
        scores(qi, s_odd)
        finish(absorb(qi, s_odd, absorb(qi - 1, s_even, carry, False), True))


def _causal_attention(q_arr, k_arr, v_arr, c_arr, *, q_blk0, k_blk0, v_blk0, n_pairs, tq=512):
    b, s, _ = q_arr.shape
    fox = c_arr is not None
    qk_w = LANES if fox else 2 * LANES
    in_specs = [
        pl.BlockSpec((1, tq, qk_w), lambda bi, p, qi: (bi, qi, q_blk0 + p)),
        pl.BlockSpec((1, s, qk_w), lambda bi, p, qi: (bi, 0, k_blk0 + p)),
        pl.BlockSpec((1, s, LANES), lambda bi, p, qi: (bi, 0, v_blk0 + p)),
    ]
    args = [q_arr, k_arr, v_arr]
    scratch = [pltpu.VMEM((2, s // tq, LANES, tq), BF16), pltpu.VMEM((2, tq, tq), F32), pltpu.VMEM((2, tq, tq), F32)]
    if fox:
        in_specs.append(pl.BlockSpec((1, s, LANES), lambda bi, p, qi: (bi, 0, 0)))
        args.append(c_arr)
        scratch.append(pltpu.VMEM((2, s, LANES), BF16))
    return pl.pallas_call(
        functools.partial(_causal_kernel, fox=fox, tq=tq, tk=tq),
        grid=(b, n_pairs, s // tq),
        in_specs=in_specs,
        out_specs=pl.BlockSpec((1, tq, LANES), lambda bi, p, qi: (bi, qi, p)),
        out_shape=jax.ShapeDtypeStruct((b, s, n_pairs * LANES), BF16),
        scratch_shapes=scratch,
        compiler_params=_params("parallel", "parallel", "arbitrary"),
        name="causal_attention",
    )(*args)


DIL_PAD = BLK * max(d for _, d in DIL_BRANCHES)
DIL_UNROLL = 4


def _dilated_kernel(q_ref, k_ref, v_ref, bias_ref, o_ref, qf, kf, vf, acc_s, m_s, l_s, s_even, s_odd, *, seq):
    qf[...] = q_ref[0].astype(F32)
    kf[0:DIL_PAD, :] = jnp.zeros((DIL_PAD, LANES), F32)
    vf[0:DIL_PAD, :] = jnp.zeros((DIL_PAD, LANES), F32)
    kf[DIL_PAD:, :] = k_ref[0].astype(F32)
    vf[DIL_PAD:, :] = v_ref[0].astype(F32)
    head0 = lax.broadcasted_iota(jnp.int32, (BLK, LANES), 1) < HEAD_DIM
    head0_k = lax.broadcasted_iota(jnp.int32, (2 * BLK, LANES), 1) < HEAD_DIM
    ones0 = jnp.where(head0_k, 1.0, 0.0).astype(BF16)
    ones1 = jnp.where(head0_k, 0.0, 1.0).astype(BF16)

    def place(dil, u):
        bi = u // dil
        q_start = (u % dil) + (dil * BLK) * bi
        return bi, q_start, q_start + (DIL_PAD - dil * BLK)

    def scores(br, dil, g, buf):
        for i in range(DIL_UNROLL):
            bi, q_start, k_start = place(dil, g * DIL_UNROLL + i)
            q2 = qf[pl.ds(q_start, BLK, stride=dil), :]
            q_st = jnp.concatenate([jnp.where(head0, q2, 0.0), jnp.where(head0, 0.0, q2)], axis=0).astype(BF16)
            kb = kf[pl.ds(k_start, 2 * BLK, stride=dil), :].astype(BF16)
            s = lax.dot_general(q_st, kb, NT_DIMS, preferred_element_type=F32)
            buf[i] = s + bias_ref[br, 0, jnp.where(bi == 0, 1, 0)]

    def absorb(br, dil, g, buf, first):
        for i in range(DIL_UNROLL):
            _, q_start, k_start = place(dil, g * DIL_UNROLL + i)
            s = buf[i]
            m = jnp.max(s, axis=-1, keepdims=True)
            p = jnp.exp2(s - m).astype(BF16)
            p_cat = jnp.concatenate([p[:BLK], p[BLK:]], axis=1)
            v2 = vf[pl.ds(k_start, 2 * BLK, stride=dil), :]
            rhs = jnp.concatenate([
                jnp.concatenate([jnp.where(head0_k, v2, 0.0).astype(BF16), ones0], axis=1),
                jnp.concatenate([jnp.where(head0_k, 0.0, v2).astype(BF16), ones1], axis=1)], axis=0)
            acc2 = _dot(p_cat, rhs)
            acc_b, l_b = acc2[:, :LANES], acc2[:, LANES:]
            m_b = jnp.where(head0, jnp.broadcast_to(m[:BLK], (BLK, LANES)), jnp.broadcast_to(m[BLK:], (BLK, LANES)))
            idx = pl.ds(q_start, BLK, stride=dil)
            if first:
                m_s[idx, :] = m_b
                l_s[idx, :] = l_b
                acc_s[idx, :] = acc_b
                continue
            m_old = m_s[idx, :]
            m_new = jnp.maximum(m_old, m_b)
            a_old = jnp.exp2(m_old - m_new)
            a_b = jnp.exp2(m_b - m_new)
            m_s[idx, :] = m_new
            l_s[idx, :] = a_old * l_s[idx, :] + a_b * l_b
            acc_s[idx, :] = a_old * acc_s[idx, :] + a_b * acc_b

    n_groups = seq // (BLK * DIL_UNROLL)
    order = sorted(range(len(DIL_BRANCHES)), key=lambda i: -DIL_BRANCHES[i][1])
    for pos, br in enumerate(order):
        dil = DIL_BRANCHES[br][1]
        first = pos == 0

        def trip(t, carry, br=br, dil=dil, first=first):
            scores(br, dil, 2 * t + 1, s_odd)
            absorb(br, dil, 2 * t, s_even, first)
            scores(br, dil, 2 * t + 2, s_even)
            absorb(br, dil, 2 * t + 1, s_odd, first)
            return carry

        scores(br, dil, 0, s_even)
        lax.fori_loop(0, n_groups // 2 - 1, trip, 0)
        scores(br, dil, n_groups - 1, s_odd)
        absorb(br, dil, n_groups - 2, s_even, first)
        absorb(br, dil, n_groups - 1, s_odd, first)

    o_ref[0] = (acc_s[...] / l_s[...]).astype(o_ref.dtype)


def _dilated_attention(proj, bias, *, q_blk0, k_blk0, v_blk0):
    b, s, _ = proj.shape
    n_pairs = DIL_HEADS // 2
    assert (s // (BLK * DIL_UNROLL)) % 2 == 0
    blk = lambda off: pl.BlockSpec((1, s, LANES), lambda bi, p: (bi, 0, off + p))
    state = [pltpu.VMEM((s, LANES), F32) for _ in range(4)]
    padded = [pltpu.VMEM((DIL_PAD + s, LANES), F32) for _ in range(2)]
    score_bufs = [pltpu.VMEM((DIL_UNROLL, 2 * BLK, 2 * BLK), F32) for _ in range(2)]
    return pl.pallas_call(
        functools.partial(_dilated_kernel, seq=s),
        grid=(b, n_pairs),
        in_specs=[blk(q_blk0), blk(k_blk0), blk(v_blk0),
                  pl.BlockSpec((len(DIL_BRANCHES), 1, 2, 2 * BLK, 2 * BLK), lambda bi, p: (0, p, 0, 0, 0))],
        out_specs=pl.BlockSpec((1, s, LANES), lambda bi, p: (bi, 0, p)),
        out_shape=jax.ShapeDtypeStruct((b, s, n_pairs * LANES), BF16),
        scratch_shapes=[state[0], padded[0], padded[1], state[1], state[2], state[3]] + score_bufs,
        compiler_params=_params("parallel", "parallel"),
        name="dilated_attention",
    )(proj, proj, proj, bias)


def _swa_kernel(q_ref, k_ref, v_ref, bias_ref, sink_ref, o_ref, *, tc):
    ci = pl.program_id(1)
    grp = SWA_Q_HEADS // SWA_KV_HEADS
    col = lax.broadcasted_iota(jnp.int32, (grp * BLK, 2 * BLK), 1)
    for bb in range(tc // BLK):
        own = pl.multiple_of(ci * tc + bb * BLK, BLK)
        prev = pl.multiple_of(jnp.maximum(own - BLK, 0), BLK)
        no_prev = jnp.where(own == 0, NEG, 0.0).astype(F32)
        pieces = []
        for kv in range(SWA_KV_HEADS):
            lo, hi = kv * HEAD_DIM, (kv + 1) * HEAD_DIM
            k2 = jnp.concatenate([k_ref[0, pl.ds(prev, BLK), lo:hi], k_ref[0, pl.ds(own, BLK), lo:hi]], axis=0)
            v2 = jnp.concatenate([v_ref[0, pl.ds(prev, BLK), lo:hi], v_ref[0, pl.ds(own, BLK), lo:hi]], axis=0)
            q4 = jnp.concatenate(
                [q_ref[0, bb * BLK:(bb + 1) * BLK, (kv * grp + g) * HEAD_DIM:(kv * grp + g + 1) * HEAD_DIM]
                 for g in range(grp)], axis=0)
            s = lax.dot_general(q4, k2, NT_DIMS, preferred_element_type=F32)
            s = s + bias_ref[kv] + jnp.where(col < BLK, no_prev, 0.0)
            sink = sink_ref[kv]
            m = jnp.maximum(jnp.max(s, axis=-1, keepdims=True), sink)
            p = jnp.exp(s - m)
            den = jnp.sum(p, axis=-1, keepdims=True) + jnp.exp(sink - m)
            o = _dot(p.astype(BF16), v2) / den
            pieces += [o[g * BLK:(g + 1) * BLK] for g in range(grp)]
        o_ref[0, bb * BLK:(bb + 1) * BLK, :] = jnp.concatenate(pieces, axis=-1).astype(o_ref.dtype)


def _swa_attention(proj, bias, sink_col, *, q_blk0, k_blk0, v_blk0, tc=512):
    b, s, _ = proj.shape
    qw = SWA_Q_HEADS * HEAD_DIM
    return pl.pallas_call(
        functools.partial(_swa_kernel, tc=tc),
        grid=(b, s // tc),
        in_specs=[
            pl.BlockSpec((1, tc, qw), lambda bi, ci: (bi, ci, q_blk0 * LANES // qw)),
            pl.BlockSpec((1, s, LANES), lambda bi, ci: (bi, 0, k_blk0)),
            pl.BlockSpec((1, s, LANES), lambda bi, ci: (bi, 0, v_blk0)),
            pl.BlockSpec(bias.shape, lambda bi, ci: (0, 0, 0)),
            pl.BlockSpec(sink_col.shape, lambda bi, ci: (0, 0, 0)),
        ],
        out_specs=pl.BlockSpec((1, tc, qw), lambda bi, ci: (bi, ci, 0)),
        out_shape=jax.ShapeDtypeStruct((b, s, qw), BF16),
        compiler_params=_params("parallel", "parallel"),
        name="swa_attention",
    )(proj, proj, proj, bias, sink_col)


def _rope128(x, a, bm, bp):
    return x * a + pltpu.roll(x, LANES - 16, axis=1) * bm + pltpu.roll(x, 16, axis=1) * bp


def _proj_odd_kernel(x_ref, g_ref, w_ref, qn_ref, kvn_ref, wuq_ref, wuk_ref, wuv_ref, a_ref, bm_ref, bp_ref,
                     q_out, k_out, v_out, swa_out):
    xb = _rms(x_ref[...], g_ref[...]).astype(BF16)
    w = w_ref
    c_q = _dot(xb, w[:, 0:MLA_Q_RANK])
    c_kv = _dot(xb, w[:, MLA_Q_RANK:MLA_Q_RANK + MLA_KV_RANK])
    o1 = MLA_Q_RANK + MLA_KV_RANK
    kpe = _dot(xb, w[:, o1:o1 + LANES])
    o2 = o1 + LANES
    swa_out[...] = _dot(xb, w[:, o2:]).astype(swa_out.dtype)
    a, bm, bp = a_ref[...], bm_ref[...], bp_ref[...]
    kpe = _rope128(kpe, a, bm, bp)
    cqn = _rms(c_q, qn_ref[...]).astype(BF16)
    ckvn = _rms(c_kv, kvn_ref[...]).astype(BF16)
    v_out[...] = _dot(ckvn, wuv_ref[...]).astype(v_out.dtype)
    for h in range(MLA_HEADS):
        sl = slice(h * LANES, (h + 1) * LANES)
        q_out[:, sl] = _rope128(_dot(cqn, wuq_ref[:, sl]), a, bm, bp).astype(q_out.dtype)
        k_out[:, sl] = (_dot(ckvn, wuk_ref[:, sl]) + kpe).astype(k_out.dtype)


def _proj_odd(h, g, w, qn, kvn, wuq, wuk, wuv, rope_a, rope_bm, rope_bp, seq, tm=512):
    n, d = h.shape
    n_in = w.shape[1]
    n_swa = n_in - (MLA_Q_RANK + MLA_KV_RANK + LANES)
    tiles_per_seq = seq // tm
    full = lambda shape: pl.BlockSpec(shape, lambda i: (0, 0))
    rope = pl.BlockSpec((tm, LANES), lambda i: (i % tiles_per_seq, 0))
    return pl.pallas_call(
        _proj_odd_kernel,
        grid=(n // tm,),
        in_specs=[
            pl.BlockSpec((tm, d), lambda i: (i, 0)), full((1, d)), full(w.shape),
            full(qn.shape), full(kvn.shape), full(wuq.shape), full(wuk.shape), full(wuv.shape),
            rope, rope, rope,
        ],
        out_specs=[
            pl.BlockSpec((tm, MLA_HEADS * LANES), lambda i: (i, 0)),
            pl.BlockSpec((tm, MLA_HEADS * LANES), lambda i: (i, 0)),
            pl.BlockSpec((tm, MLA_HEADS * MLA_V), lambda i: (i, 0)),
            pl.BlockSpec((tm, n_swa), lambda i: (i, 0)),
        ],
        out_shape=[
            jax.ShapeDtypeStruct((n, MLA_HEADS * LANES), BF16),
            jax.ShapeDtypeStruct((n, MLA_HEADS * LANES), BF16),
            jax.ShapeDtypeStruct((n, MLA_HEADS * MLA_V), BF16),
            jax.ShapeDtypeStruct((n, n_swa), BF16),
        ],
        compiler_params=_params("parallel"),
        name="proj_odd",
    )(h, g, w, qn, kvn, wuq, wuk, wuv, rope_a, rope_bm, rope_bp)


def _out_router_kernel(h_ref, oa_ref, ob_ref, wo_ref, g_ref, wr_hi_ref, wr_lo_ref, br_ref,
                       h_out, hn_out, gates_out, counts_out):
    half = oa_ref.shape[1]
    h = h_ref[...] + _dot(oa_ref[...], wo_ref[0:half, :]) + _dot(ob_ref[...], wo_ref[half:, :])
    h_out[...] = h
    hn = _rms(h, g_ref[...])
    hn_out[...] = hn.astype(hn_out.dtype)
    z = _dot_hi(hn, wr_hi_ref[...], wr_lo_ref[...]) + br_ref[...]
    lane = lax.broadcasted_iota(jnp.int32, z.shape, 1)
    big = jnp.int32(LANES)
    is_grp = (lane >= N_EXPERTS) & (lane < N_EXPERTS + N_GROUPS)
    zg = jnp.where(is_grp, z, -jnp.inf)
    g_max = jnp.max(zg, axis=-1, keepdims=True)
    g_w = 1.0 / jnp.sum(jnp.exp(zg - g_max), axis=-1, keepdims=True)
    g_idx = jnp.min(jnp.where(zg == g_max, lane - N_EXPERTS, big), axis=-1, keepdims=True)
    in_grp = (lane < N_EXPERTS) & ((lane // EXPERTS_PER_GROUP) == g_idx)
    ze = jnp.where(in_grp, z, -jnp.inf)
    v1 = jnp.max(ze, axis=-1, keepdims=True)
    i1 = jnp.min(jnp.where(ze == v1, lane, big), axis=-1, keepdims=True)
    ze2 = jnp.where(lane == i1, -jnp.inf, ze)
    v2 = jnp.max(ze2, axis=-1, keepdims=True)
    i2 = jnp.min(jnp.where(ze2 == v2, lane, big), axis=-1, keepdims=True)
    e2 = jnp.exp(v2 - v1)
    w1 = g_w / (1.0 + e2)
    w2 = g_w * e2 / (1.0 + e2)
    for grp in range(N_GROUPS):
        first = grp * EXPERTS_PER_GROUP
        slot = jnp.where(lane == i1 - first, w1, 0.0) + jnp.where(lane == i2 - first, w2, 0.0)
        slot = jnp.where(lane < EXPERTS_PER_GROUP, slot, 0.0)
        slot = jnp.where((lane == EXPERTS_PER_GROUP) & (g_idx == grp), 1.0, slot)
        gates_out[:, grp * LANES:(grp + 1) * LANES] = slot
    routed = jnp.where((lane == g_idx) & (lane < N_GROUPS), 1.0, 0.0)
    counts_out[0] = jnp.broadcast_to(jnp.sum(routed, axis=0, keepdims=True), counts_out.shape[1:])


def _out_router(h, oa, ob, wo, g, wr_hi, wr_lo, br, tm=512):
    n, d = h.shape
    half = oa.shape[1]
    full = lambda shape: pl.BlockSpec(shape, lambda i: (0, 0))
    tile = lambda w: pl.BlockSpec((tm, w), lambda i: (i, 0))
    return pl.pallas_call(
        _out_router_kernel,
        grid=(n // tm,),
        in_specs=[tile(d), tile(half), tile(half), full(wo.shape), full((1, d)),
                  full(wr_hi.shape), full(wr_lo.shape), full((1, LANES))],
        out_specs=[tile(d), tile(d), tile(N_GROUPS * LANES), pl.BlockSpec((1, 8, LANES), lambda i: (i, 0, 0))],
        out_shape=[
            jax.ShapeDtypeStruct((n, d), F32),
            jax.ShapeDtypeStruct((n, d), BF16),
            jax.ShapeDtypeStruct((n, N_GROUPS * LANES), F32),
            jax.ShapeDtypeStruct((n // tm, 8, LANES), F32),
        ],
        compiler_params=_params("parallel"),
        name="out_router",
    )(h, oa, ob, wo, g, wr_hi, wr_lo, br)


MOE_CHUNK = 256
MOE_TILE = 1024
ROUTER_TILE = 512


def _moe_kernel(counts_ref, hn_ref, gates_ref, h_ref, wgu_ref, wd_ref, fg_ref, o_ref, acc_ref, tri_ref, *, final_norm):
    i = pl.program_id(0)
    grp = pl.program_id(1)
    tm = hn_ref.shape[0]

    @pl.when((i == 0) & (grp == 0))
    def _():
        ri = lax.broadcasted_iota(jnp.int32, (tm, tm), 0)
        ci = lax.broadcasted_iota(jnp.int32, (tm, tm), 1)
        tri_ref[...] = jnp.where(ci < ri, 1.0, 0.0).astype(BF16)

    @pl.when(grp == 0)
    def _():
        acc_ref[...] = jnp.zeros_like(acc_ref)

    gates = gates_ref[...]
    lane = lax.broadcasted_iota(jnp.int32, gates.shape, 1)
    flag = jnp.where(lane == EXPERTS_PER_GROUP, gates, 0.0)
    rank = _dot(tri_ref[...], flag.astype(BF16))
    pos = jnp.where(flag > 0.0, rank, -1.0)
    pos_col = jnp.sum(jnp.where(lane == EXPERTS_PER_GROUP, pos, 0.0), axis=-1, keepdims=True)
    pos_row = jnp.transpose(jnp.broadcast_to(pos_col, (tm, LANES)))[0:1, :]
    g_hi = gates.astype(BF16)
    g_lo = (gates - g_hi.astype(F32)).astype(BF16)

    def chunk(first_pos, rows):
        base = first_pos.astype(F32)
        slot_col = lax.broadcasted_iota(jnp.int32, (rows, 1), 0).astype(F32) + base
        slot_row = lax.broadcasted_iota(jnp.int32, (1, rows), 1).astype(F32) + base
        gather = jnp.where(pos_row == slot_col, 1.0, 0.0).astype(BF16)
        scatter = jnp.where(pos_col == slot_row, 1.0, 0.0).astype(BF16)
        xc = _dot(gather, hn_ref[...]).astype(BF16)
        gc = _dot(gather, g_hi) + _dot(gather, g_lo)
        y = jnp.zeros((rows, acc_ref.shape[1]), F32)
        for e in range(EXPERTS_PER_GROUP):
            au = _dot(xc, wgu_ref[0, e])
            a, u = au[:, :D_EXPERT], au[:, D_EXPERT:]
            act = (a * jax.nn.sigmoid(a)) * u * gc[:, e:e + 1]
            y = y + _dot(act.astype(BF16), wd_ref[0, e])
        acc_ref[...] += _dot(scatter, y.astype(BF16))

    count = counts_ref[i * N_GROUPS + grp]
    n_full = count // MOE_CHUNK
    rest = count - n_full * MOE_CHUNK

    def full_chunk(k, carry):
        chunk(k * MOE_CHUNK, MOE_CHUNK)
        return carry

    lax.fori_loop(0, n_full, full_chunk, 0)

    @pl.when(rest > MOE_CHUNK // 2)
    def _():
        chunk(n_full * MOE_CHUNK, MOE_CHUNK)

    @pl.when((rest > 0) & (rest <= MOE_CHUNK // 2))
    def _():
        chunk(n_full * MOE_CHUNK, MOE_CHUNK // 2)

    @pl.when(grp == N_GROUPS - 1)
    def _():
        h = h_ref[...] + acc_ref[...]
        o_ref[...] = _rms(h, fg_ref[...]) if final_norm else h


def _moe(counts, hn, gates, h, wgu, wd, fg, final_norm, tm=1024):
    n, d = h.shape
    once = pl.Buffered(1)
    grid_spec = pltpu.PrefetchScalarGridSpec(
        num_scalar_prefetch=1,
        grid=(n // tm, N_GROUPS),
        in_specs=[
            pl.BlockSpec((tm, d), lambda i, g, c: (i, 0), pipeline_mode=once),
            pl.BlockSpec((tm, LANES), lambda i, g, c: (i, g)),
            pl.BlockSpec((tm, d), lambda i, g, c: (i, 0), pipeline_mode=once),
            pl.BlockSpec((1, EXPERTS_PER_GROUP, d, 2 * D_EXPERT), lambda i, g, c: (g, 0, 0, 0)),
            pl.BlockSpec((1, EXPERTS_PER_GROUP, D_EXPERT, d), lambda i, g, c: (g, 0, 0, 0)),
            pl.BlockSpec((1, d), lambda i, g, c: (0, 0)),
        ],
        out_specs=pl.BlockSpec((tm, d), lambda i, g, c: (i, 0), pipeline_mode=once),
        scratch_shapes=[pltpu.VMEM((tm, d), F32), pltpu.VMEM((tm, tm), BF16)],
    )
    return pl.pallas_call(
        functools.partial(_moe_kernel, final_norm=final_norm),
        grid_spec=grid_spec,
        out_shape=jax.ShapeDtypeStruct((n, d), F32),
        compiler_params=_params("arbitrary", "arbitrary"),
        name="moe_experts",
    )(counts, hn, gates, h, wgu, wd, fg)


def _alibi_slopes(n):
    return jnp.exp2(-8.0 * jnp.arange(1, n + 1, dtype=F32) / n)


def _band_bias(slopes, max_steps, step_dist):
    steps = (jnp.arange(BLK)[:, None] + BLK) - jnp.arange(2 * BLK)[None, :]
    in_band = (steps >= 0) & (steps <= max_steps)
    dist = (steps * step_dist).astype(F32)
    return jnp.where(in_band[None], -slopes.astype(F32)[:, None, None] * dist[None], NEG)


def _rope_tables(s):
    inv = ROPE_THETA ** (-jnp.arange(0, MLA_ROPE, 2, dtype=F32) / MLA_ROPE)
    ang = jnp.arange(s, dtype=F32)[:, None] * inv[None, :]
    cos, sin = jnp.cos(ang), jnp.sin(ang)
    half = MLA_ROPE // 2
    zeros_tail = jnp.zeros((s, LANES - MLA_NOPE - MLA_ROPE), F32)
    a = jnp.concatenate([jnp.ones((s, MLA_NOPE), F32), cos, cos, zeros_tail], axis=1)
    zeros_nope = jnp.zeros((s, MLA_NOPE), F32)
    zeros_half = jnp.zeros((s, half), F32)
    bm = jnp.concatenate([zeros_nope, -sin, zeros_half, zeros_tail], axis=1)
    bp = jnp.concatenate([zeros_nope, zeros_half, sin, zeros_tail], axis=1)
    return a, bm, bp


def _pad_cols(w, width):
    return jnp.pad(w, ((0, 0), (0, width - w.shape[1])))


def _router_weights(w_group, b_group, w_router, b_router):
    w = _pad_cols(jnp.concatenate([w_router, w_group], axis=1), LANES)
    b = _pad_cols(jnp.concatenate([b_router, b_group])[None, :], LANES)
    hi, lo = _hi_lo(w)
    return hi, lo, b


def _moe_weights(w_gate, w_up, w_down):
    d = w_gate.shape[-2]
    return jnp.concatenate([w_gate, w_up], axis=-1).astype(BF16), w_down.astype(BF16)


def kernel(x, attn_norm, ffn_norm, final_norm, e_w_in, e_b_f, e_w_out, o_w_in, o_q_norm, o_kv_norm, o_w_uq,
           o_w_ukv, o_sinks, o_w_out, moe_w_group, moe_b_group, moe_w_router, moe_b_router, moe_w_gate,
           moe_w_up, moe_w_down):
    b, s, d = x.shape
    n = b * s
    depth = attn_norm.shape[0]
    assert s % (BLK * DIL_BRANCHES[-1][1]) == 0 and d == D_MODEL
    h = x.reshape(n, d)
    tq = 256

    dil_slopes = _alibi_slopes(DIL_HEADS)
    dil_bias = jnp.stack([_band_bias(dil_slopes, w // dl, dl) for (w, dl) in DIL_BRANCHES]) * LOG2E
    own_half = jnp.arange(2 * BLK) >= BLK
    dil_bias = jnp.stack([dil_bias, jnp.where(own_half, dil_bias, NEG)], axis=2)
    dil_bias = dil_bias.reshape(len(DIL_BRANCHES), DIL_HEADS // 2, 2, 2, BLK, 2 * BLK)
    dil_bias = dil_bias.transpose(0, 1, 3, 2, 4, 5).reshape(len(DIL_BRANCHES), DIL_HEADS // 2, 2, 2 * BLK, 2 * BLK)
    swa_bias = _band_bias(_alibi_slopes(SWA_Q_HEADS), SWA_WINDOW - 1, 1)
    swa_bias = swa_bias.reshape(SWA_KV_HEADS, (SWA_Q_HEADS // SWA_KV_HEADS) * BLK, 2 * BLK)
    rope_a, rope_bm, rope_bp = _rope_tables(s)

    for layer in range(depth):
        i = layer // 2
        g_attn = attn_norm[layer][None, :]
        if layer % 2 == 0:
            w_in = e_w_in[i]
            hq = FOX_HEADS * HEAD_DIM
            scale = HEAD_DIM ** -0.5
            cols = [w_in[:, 0:hq] * (scale * LOG2E), w_in[:, hq:2 * hq], w_in[:, 2 * hq:3 * hq]]
            o = 3 * hq + FOX_HEADS
            cols += [w_in[:, o:o + hq] * (scale * LOG2E), w_in[:, o + hq:o + 2 * hq], w_in[:, o + 2 * hq:o + 3 * hq]]
            w_main = jnp.concatenate(cols, axis=1).astype(BF16)
            wf_hi, wf_lo = _hi_lo(_pad_cols(w_in[:, 3 * hq:o], LANES))
            b_f = _pad_cols(e_b_f[i][None, :], LANES)
            proj, c = _proj_even(h, g_attn, w_main, wf_hi, wf_lo, b_f, s)
            proj = proj.reshape(b, s, -1)
            o_a = _causal_attention(proj, proj, proj, c.reshape(b, s, LANES), q_blk0=0, k_blk0=4, v_blk0=8,
                                    n_pairs=FOX_HEADS // 2)
            o_b = _dilated_attention(proj, dil_bias, q_blk0=12, k_blk0=16, v_blk0=20)
            w_out = e_w_out[i].astype(BF16)
        else:
            w_in = o_w_in[i]
            o1 = MLA_Q_RANK + MLA_KV_RANK
            o2 = o1 + MLA_ROPE
            sq = SWA_Q_HEADS * HEAD_DIM
            kpe_cols = jnp.pad(w_in[:, o1:o2], ((0, 0), (MLA_NOPE, LANES - MLA_NOPE - MLA_ROPE)))
            w_main = jnp.concatenate(
                [w_in[:, :o1], kpe_cols, w_in[:, o2:o2 + sq] * (HEAD_DIM ** -0.5), w_in[:, o2 + sq:]],
                axis=1).astype(BF16)
            dq = MLA_NOPE + MLA_ROPE
            wuq = o_w_uq[i].reshape(MLA_Q_RANK, MLA_HEADS, dq) * (dq ** -0.5 * LOG2E)
            wuq = jnp.pad(wuq, ((0, 0), (0, 0), (0, LANES - dq))).reshape(MLA_Q_RANK, MLA_HEADS * LANES)
            wukv = o_w_ukv[i].reshape(MLA_KV_RANK, MLA_HEADS, MLA_NOPE + MLA_V)
            wuk = jnp.pad(wukv[:, :, :MLA_NOPE], ((0, 0), (0, 0), (0, LANES - MLA_NOPE)))
            wuk = wuk.reshape(MLA_KV_RANK, MLA_HEADS * LANES)
            wuv = wukv[:, :, MLA_NOPE:].reshape(MLA_KV_RANK, MLA_HEADS * MLA_V)
            q_full, k_full, v_mla, swa = _proj_odd(
                h, g_attn, w_main, o_q_norm[i][None, :], o_kv_norm[i][None, :], wuq.astype(BF16),
                wuk.astype(BF16), wuv.astype(BF16), rope_a, rope_bm, rope_bp, s)
            o_a = _causal_attention(q_full.reshape(b, s, -1), k_full.reshape(b, s, -1), v_mla.reshape(b, s, -1),
                                    None, q_blk0=0, k_blk0=0, v_blk0=0, n_pairs=MLA_HEADS // 2)
            grp = SWA_Q_HEADS // SWA_KV_HEADS
            sink_col = jnp.repeat(o_sinks[i].reshape(SWA_KV_HEADS, grp), BLK, axis=1)[:, :, None]
            o_b = _swa_attention(swa.reshape(b, s, -1), swa_bias, sink_col, q_blk0=0, k_blk0=4, v_blk0=5)
            w_out = o_w_out[i].astype(BF16)

        wr_hi, wr_lo, b_r = _router_weights(moe_w_group[layer], moe_b_group[layer], moe_w_router[layer],
                                            moe_b_router[layer])
        h, hn, gates, counts = _out_router(h, o_a.reshape(n, -1), o_b.reshape(n, -1), w_out,
                                           ffn_norm[layer][None, :], wr_hi, wr_lo, b_r, tm=ROUTER_TILE)
        counts = counts[:, 0, :N_GROUPS].reshape(n // MOE_TILE, MOE_TILE // ROUTER_TILE, N_GROUPS).sum(axis=1)
        counts = counts.astype(jnp.int32).reshape(-1)
        wgu, wd = _moe_weights(moe_w_gate[layer], moe_w_up[layer], moe_w_down[layer])
        h = _moe(counts, hn, gates, h, wgu, wd, final_norm[None, :], final_norm=layer == depth - 1, tm=MOE_TILE)
    return h.reshape(b, s, d)
```

```python
import functools

import jax
import jax.numpy as jnp
from jax import lax
from jax.experimental import pallas as pl
from jax.experimental.pallas import tpu as pltpu

F32 = jnp.float32
BF16 = jnp.bfloat16

D_MODEL = 1024
HEAD_DIM = 64
BLK = 128
NEG = -1e30
RMS_EPS = 1e-6
FOX_HEADS = 8
DIL_HEADS = 8
DIL_BRANCHES = ((128, 1), (512, 4), (2048, 16))
MLA_HEADS = 8
MLA_Q_RANK = 384
MLA_KV_RANK = 256
MLA_NOPE = 64
MLA_ROPE = 32
MLA_V = 64
ROPE_THETA = 10000.0
SWA_Q_HEADS = 8
SWA_KV_HEADS = 2
SWA_WINDOW = 128
N_GROUPS = 4
EXPERTS_PER_GROUP = 8
N_EXPERTS = N_GROUPS * EXPERTS_PER_GROUP
D_EXPERT = 256

LANES = 128
VMEM_LIMIT = 56 * 1024 * 1024

NT_DIMS = (((1,), (1,)), ((), ()))
LOG2E = 1.4426950408889634


def _params(*sem):
    return pltpu.CompilerParams(dimension_semantics=sem, vmem_limit_bytes=VMEM_LIMIT)


def _dot(a, b):
    return jnp.dot(a, b, preferred_element_type=F32)


def _rms(x, g):
    return x * lax.rsqrt(jnp.mean(x * x, axis=-1, keepdims=True) + RMS_EPS) * g


def _hi_lo(w):
    hi = w.astype(BF16)
    return hi, (w - hi.astype(F32)).astype(BF16)


def _dot_hi(x, w_hi, w_lo):
    x_hi = x.astype(BF16)
    x_lo = (x - x_hi.astype(F32)).astype(BF16)
    return _dot(x_hi, w_hi) + (_dot(x_hi, w_lo) + _dot(x_lo, w_hi))


def _split3(x):
    x1 = x.astype(BF16)
    r1 = x - x1.astype(F32)
    x2 = r1.astype(BF16)
    x3 = (r1 - x2.astype(F32)).astype(BF16)
    return x1, x2, x3


def _proj_even_kernel(x_ref, g_ref, w_ref, wf_hi_ref, wf_lo_ref, bf_ref, out_ref, c_ref, carry_ref, *, tiles_per_seq):
    xn = _rms(x_ref[...], g_ref[...])
    xb = xn.astype(BF16)
    n_out = out_ref.shape[1]
    for c in range(0, n_out, 512):
        out_ref[:, c:c + 512] = _dot(xb, w_ref[:, c:c + 512]).astype(out_ref.dtype)
    z = _dot_hi(xn, wf_hi_ref[...], wf_lo_ref[...]) + bf_ref[...]
    logf = jnp.minimum(z, 0.0) - jnp.log1p(jnp.exp(-jnp.abs(z)))

    @pl.when(pl.program_id(0) % tiles_per_seq == 0)
    def _():
        carry_ref[...] = jnp.zeros_like(carry_ref)

    tm = logf.shape[0]
    ri = lax.broadcasted_iota(jnp.int32, (tm, tm), 0)
    ci = lax.broadcasted_iota(jnp.int32, (tm, tm), 1)
    lower = jnp.where(ci <= ri, 1.0, 0.0).astype(BF16)
    l1, l2, l3 = _split3(logf)
    c = (_dot(lower, l1) + (_dot(lower, l2) + _dot(lower, l3))) + carry_ref[...]
    c_ref[...] = c
    carry_ref[...] = c[tm - 1:tm, :]


def _proj_even(h, g, w, wf_hi, wf_lo, bf, seq, tm=512):
    n, d = h.shape
    n_out = w.shape[1]
    return pl.pallas_call(
        functools.partial(_proj_even_kernel, tiles_per_seq=seq // tm),
        grid=(n // tm,),
        in_specs=[
            pl.BlockSpec((tm, d), lambda i: (i, 0)),
            pl.BlockSpec((1, d), lambda i: (0, 0)),
            pl.BlockSpec((d, n_out), lambda i: (0, 0)),
            pl.BlockSpec((d, LANES), lambda i: (0, 0)),
            pl.BlockSpec((d, LANES), lambda i: (0, 0)),
            pl.BlockSpec((1, LANES), lambda i: (0, 0)),
        ],
        out_specs=[
            pl.BlockSpec((tm, n_out), lambda i: (i, 0)),
            pl.BlockSpec((tm, LANES), lambda i: (i, 0)),
        ],
        out_shape=[
            jax.ShapeDtypeStruct((n, n_out), BF16),
            jax.ShapeDtypeStruct((n, LANES), F32),
        ],
        scratch_shapes=[pltpu.VMEM((1, LANES), F32)],
        compiler_params=_params("arbitrary"),
        name="proj_even",
    )(h, g, w, wf_hi, wf_lo, bf)


def _own_lanes(lane, h):
    return lane < HEAD_DIM if h == 0 else lane >= HEAD_DIM


def _causal_kernel(*refs, fox, tq, tk):
    if fox:
        q_ref, k_ref, v_ref, c_ref, o_ref, vaug_t, s_even, s_odd, kaug = refs
    else:
        q_ref, k_ref, v_ref, o_ref, vaug_t, s_even, s_odd = refs
    pair = pl.program_id(1)
    qi = pl.program_id(2)
    seq = v_ref.shape[1]

    @pl.when(qi == 0)
    def _():
        lane = lax.broadcasted_iota(jnp.int32, (tk, LANES), 1)
        feat = lax.broadcasted_iota(jnp.int32, (LANES, tk), 0)

        def fill(t, carry):
            rows = pl.ds(pl.multiple_of(t * tk, tk), tk)
            vt = jnp.transpose(v_ref[0, rows, :].astype(F32))
            if fox:
                kp = k_ref[0, rows, :].astype(F32)
                c = c_ref[0, rows, :]
            for h in range(2):
                vaug_t[h, t] = jnp.where(_own_lanes(feat, h), vt, 1.0).astype(BF16)
                if fox:
                    ch = jnp.sum(jnp.where(lane == 2 * pair + h, c, 0.0), axis=-1, keepdims=True)
                    c1, c2, c3 = _split3(ch * (-LOG2E))
                    base = HEAD_DIM if h == 0 else 0
                    extra = jnp.where(lane == base, c1.astype(F32),
                                      jnp.where(lane == base + 1, c2.astype(F32),
                                                jnp.where(lane == base + 2, c3.astype(F32), 0.0)))
                    kaug[h, rows, :] = jnp.where(_own_lanes(lane, h), kp, extra).astype(BF16)
            return carry

        lax.fori_loop(0, seq // tk, fill, 0)

    lane_q = lax.broadcasted_iota(jnp.int32, (tq, LANES), 1)
    qs = []
    for h in range(2):
        if fox:
            base = HEAD_DIM if h == 0 else 0
            ones = jnp.where((lane_q >= base) & (lane_q < base + 3), 1.0, 0.0)
            qs.append(jnp.where(_own_lanes(lane_q, h), q_ref[0].astype(F32), ones).astype(BF16))
        else:
            qs.append(q_ref[0, :, h * LANES:(h + 1) * LANES])

    key = lax.broadcasted_iota(jnp.int32, (tk, tq), 0)
    qry = lax.broadcasted_iota(jnp.int32, (tk, tq), 1)

    def scores(j, buf):
        start = pl.multiple_of(j * tk, tk)
        for h in range(2):
            if fox:
                kj = kaug[h, pl.ds(start, tk), :]
            else:
                kj = k_ref[0, pl.ds(start, tk), h * LANES:(h + 1) * LANES]
            buf[h] = lax.dot_general(kj, qs[h], NT_DIMS, preferred_element_type=F32)

    def absorb(j, buf, carry, masked):
        new = []
        for h in range(2):
            m, acc = carry[h]
            s = buf[h]
            if masked:
                s = jnp.where(key <= qry, s, NEG)
            m_new = jnp.maximum(m, jnp.max(s, axis=0, keepdims=True))
            p = jnp.exp2(s - m_new)
            acc = jnp.exp2(m - m_new) * acc + _dot(vaug_t[h, j], p.astype(BF16))
            new.append((m_new, acc))
        return tuple(new)

    def finish(carry):
        (_, acc0), (_, acc1) = carry
        feat_q = lax.broadcasted_iota(jnp.int32, (LANES, tq), 0)
        out_t = jnp.where(feat_q < HEAD_DIM, acc0 / acc0[HEAD_DIM:HEAD_DIM + 1, :], acc1 / acc1[0:1, :])
        o_ref[0] = jnp.transpose(out_t).astype(o_ref.dtype)

    def pair_step(t, carry):
        scores(2 * t + 1, s_odd)
        carry = absorb(2 * t, s_even, carry, False)
        scores(2 * t + 2, s_even)
        return absorb(2 * t + 1, s_odd, carry, False)

    init = tuple((jnp.full((1, tq), NEG, F32), jnp.zeros((LANES, tq), F32)) for _ in range(2))
    scores(0, s_even)
    carry = lax.fori_loop(0, qi // 2, pair_step, init)

    @pl.when(qi % 2 == 0)
    def _():
        finish(absorb(qi, s_even, carry, True))

    @pl.when(qi % 2 == 1)
    def _():
        scores(qi, s_odd)
        finish(absorb(qi, s_odd, absorb(qi - 1, s_even, carry, False), True))


def _causal_attention(q_arr, k_arr, v_arr, c_arr, *, q_blk0, k_blk0, v_blk0, n_pairs, tq=512):
    b, s, _ = q_arr.shape
    fox = c_arr is not None
    qk_w = LANES if fox else 2 * LANES
    in_specs = [
        pl.BlockSpec((1, tq, qk_w), lambda bi, p, qi: (bi, qi, q_blk0 + p)),
        pl.BlockSpec((1, s, qk_w), lambda bi, p, qi: (bi, 0, k_blk0 + p)),
        pl.BlockSpec((1, s, LANES), lambda bi, p, qi: (bi, 0, v_blk0 + p)),
    ]
    args = [q_arr, k_arr, v_arr]
    scratch = [pltpu.VMEM((2, s // tq, LANES, tq), BF16), pltpu.VMEM((2, tq, tq), F32), pltpu.VMEM((2, tq, tq), F32)]
    if fox:
        in_specs.append(pl.BlockSpec((1, s, LANES), lambda bi, p, qi: (bi, 0, 0)))
        args.append(c_arr)
        scratch.append(pltpu.VMEM((2, s, LANES), BF16))
    return pl.pallas_call(
        functools.partial(_causal_kernel, fox=fox, tq=tq, tk=tq),
        grid=(b, n_pairs, s // tq),
        in_specs=in_specs,
        out_specs=pl.BlockSpec((1, tq, LANES), lambda bi, p, qi: (bi, qi, p)),
        out_shape=jax.ShapeDtypeStruct((b, s, n_pairs * LANES), BF16),
        scratch_shapes=scratch,
        compiler_params=_params("parallel", "parallel", "arbitrary"),
        name="causal_attention",
    )(*args)


DIL_PAD = BLK * max(d for _, d in DIL_BRANCHES)
DIL_UNROLL = 4


def _dilated_kernel(q_ref, k_ref, v_ref, bias_ref, o_ref, qf, kf, vf, acc_s, m_s, l_s, s_even, s_odd, *, seq):
    qf[...] = q_ref[0].astype(F32)
    kf[0:DIL_PAD, :] = jnp.zeros((DIL_PAD, LANES), F32)
    vf[0:DIL_PAD, :] = jnp.zeros((DIL_PAD, LANES), F32)
    kf[DIL_PAD:, :] = k_ref[0].astype(F32)
    vf[DIL_PAD:, :] = v_ref[0].astype(F32)
    head0 = lax.broadcasted_iota(jnp.int32, (BLK, LANES), 1) < HEAD_DIM
    head0_k = lax.broadcasted_iota(jnp.int32, (2 * BLK, LANES), 1) < HEAD_DIM
    ones0 = jnp.where(head0_k, 1.0, 0.0).astype(BF16)
    ones1 = jnp.where(head0_k, 0.0, 1.0).astype(BF16)

    def place(dil, u):
        bi = u // dil
        q_start = (u % dil) + (dil * BLK) * bi
        return bi, q_start, q_start + (DIL_PAD - dil * BLK)

    def scores(br, dil, g, buf):
        for i in range(DIL_UNROLL):
            bi, q_start, k_start = place(dil, g * DIL_UNROLL + i)
            q2 = qf[pl.ds(q_start, BLK, stride=dil), :]
            q_st = jnp.concatenate([jnp.where(head0, q2, 0.0), jnp.where(head0, 0.0, q2)], axis=0).astype(BF16)
            kb = kf[pl.ds(k_start, 2 * BLK, stride=dil), :].astype(BF16)
            s = lax.dot_general(q_st, kb, NT_DIMS, preferred_element_type=F32)
            buf[i] = s + bias_ref[br, 0, jnp.where(bi == 0, 1, 0)]

    def absorb(br, dil, g, buf, first):
        for i in range(DIL_UNROLL):
            _, q_start, k_start = place(dil, g * DIL_UNROLL + i)
            s = buf[i]
            m = jnp.max(s, axis=-1, keepdims=True)
            p = jnp.exp2(s - m).astype(BF16)
            p_cat = jnp.concatenate([p[:BLK], p[BLK:]], axis=1)
            v2 = vf[pl.ds(k_start, 2 * BLK, stride=dil), :]
            rhs = jnp.concatenate([
                jnp.concatenate([jnp.where(head0_k, v2, 0.0).astype(BF16), ones0], axis=1),
                jnp.concatenate([jnp.where(head0_k, 0.0, v2).astype(BF16), ones1], axis=1)], axis=0)
            acc2 = _dot(p_cat, rhs)
            acc_b, l_b = acc2[:, :LANES], acc2[:, LANES:]
            m_b = jnp.where(head0, jnp.broadcast_to(m[:BLK], (BLK, LANES)), jnp.broadcast_to(m[BLK:], (BLK, LANES)))
            idx = pl.ds(q_start, BLK, stride=dil)
            if first:
                m_s[idx, :] = m_b
                l_s[idx, :] = l_b
                acc_s[idx, :] = acc_b
                continue
            m_old = m_s[idx, :]
            m_new = jnp.maximum(m_old, m_b)
            a_old = jnp.exp2(m_old - m_new)
            a_b = jnp.exp2(m_b - m_new)
            m_s[idx, :] = m_new
            l_s[idx, :] = a_old * l_s[idx, :] + a_b * l_b
            acc_s[idx, :] = a_old * acc_s[idx, :] + a_b * acc_b

    n_groups = seq // (BLK * DIL_UNROLL)
    order = sorted(range(len(DIL_BRANCHES)), key=lambda i: -DIL_BRANCHES[i][1])
    for pos, br in enumerate(order):
        dil = DIL_BRANCHES[br][1]
        first = pos == 0

        def trip(t, carry, br=br, dil=dil, first=first):
            scores(br, dil, 2 * t + 1, s_odd)
            absorb(br, dil, 2 * t, s_even, first)
            scores(br, dil, 2 * t + 2, s_even)
            absorb(br, dil, 2 * t + 1, s_odd, first)
            return carry

        scores(br, dil, 0, s_even)
        lax.fori_loop(0, n_groups // 2 - 1, trip, 0)
        scores(br, dil, n_groups - 1, s_odd)
        absorb(br, dil, n_groups - 2, s_even, first)
        absorb(br, dil, n_groups - 1, s_odd, first)

    o_ref[0] = (acc_s[...] / l_s[...]).astype(o_ref.dtype)


def _dilated_attention(proj, bias, *, q_blk0, k_blk0, v_blk0):
    b, s, _ = proj.shape
    n_pairs = DIL_HEADS // 2
    assert (s // (BLK * DIL_UNROLL)) % 2 == 0
    blk = lambda off: pl.BlockSpec((1, s, LANES), lambda bi, p: (bi, 0, off + p))
    state = [pltpu.VMEM((s, LANES), F32) for _ in range(4)]
    padded = [pltpu.VMEM((DIL_PAD + s, LANES), F32) for _ in range(2)]
    score_bufs = [pltpu.VMEM((DIL_UNROLL, 2 * BLK, 2 * BLK), F32) for _ in range(2)]
    return pl.pallas_call(
        functools.partial(_dilated_kernel, seq=s),
        grid=(b, n_pairs),
        in_specs=[blk(q_blk0), blk(k_blk0), blk(v_blk0),
                  pl.BlockSpec((len(DIL_BRANCHES), 1, 2, 2 * BLK, 2 * BLK), lambda bi, p: (0, p, 0, 0, 0))],
        out_specs=pl.BlockSpec((1, s, LANES), lambda bi, p: (bi, 0, p)),
        out_shape=jax.ShapeDtypeStruct((b, s, n_pairs * LANES), BF16),
        scratch_shapes=[state[0], padded[0], padded[1], state[1], state[2], state[3]] + score_bufs,
        compiler_params=_params("parallel", "parallel"),
        name="dilated_attention",
    )(proj, proj, proj, bias)


def _swa_kernel(q_ref, k_ref, v_ref, bias_ref, sink_ref, o_ref, *, tc):
    ci = pl.program_id(1)
    grp = SWA_Q_HEADS // SWA_KV_HEADS
    col = lax.broadcasted_iota(jnp.int32, (grp * BLK, 2 * BLK), 1)
    for bb in range(tc // BLK):
        own = pl.multiple_of(ci * tc + bb * BLK, BLK)
        prev = pl.multiple_of(jnp.maximum(own - BLK, 0), BLK)
        no_prev = jnp.where(own == 0, NEG, 0.0).astype(F32)
        pieces = []
        for kv in range(SWA_KV_HEADS):
            lo, hi = kv * HEAD_DIM, (kv + 1) * HEAD_DIM
            k2 = jnp.concatenate([k_ref[0, pl.ds(prev, BLK), lo:hi], k_ref[0, pl.ds(own, BLK), lo:hi]], axis=0)
            v2 = jnp.concatenate([v_ref[0, pl.ds(prev, BLK), lo:hi], v_ref[0, pl.ds(own, BLK), lo:hi]], axis=0)
            q4 = jnp.concatenate(
                [q_ref[0, bb * BLK:(bb + 1) * BLK, (kv * grp + g) * HEAD_DIM:(kv * grp + g + 1) * HEAD_DIM]
                 for g in range(grp)], axis=0)
            s = lax.dot_general(q4, k2, NT_DIMS, preferred_element_type=F32)
            s = s + bias_ref[kv] + jnp.where(col < BLK, no_prev, 0.0)
            sink = sink_ref[kv]
            m = jnp.maximum(jnp.max(s, axis=-1, keepdims=True), sink)
            p = jnp.exp(s - m)
            den = jnp.sum(p, axis=-1, keepdims=True) + jnp.exp(sink - m)
            o = _dot(p.astype(BF16), v2) / den
            pieces += [o[g * BLK:(g + 1) * BLK] for g in range(grp)]
        o_ref[0, bb * BLK:(bb + 1) * BLK, :] = jnp.concatenate(pieces, axis=-1).astype(o_ref.dtype)


def _swa_attention(proj, bias, sink_col, *, q_blk0, k_blk0, v_blk0, tc=512):
    b, s, _ = proj.shape
    qw = SWA_Q_HEADS * HEAD_DIM
    return pl.pallas_call(
        functools.partial(_swa_kernel, tc=tc),
        grid=(b, s // tc),
        in_specs=[
            pl.BlockSpec((1, tc, qw), lambda bi, ci: (bi, ci, q_blk0 * LANES // qw)),
            pl.BlockSpec((1, s, LANES), lambda bi, ci: (bi, 0, k_blk0)),
            pl.BlockSpec((1, s, LANES), lambda bi, ci: (bi, 0, v_blk0)),
            pl.BlockSpec(bias.shape, lambda bi, ci: (0, 0, 0)),
            pl.BlockSpec(sink_col.shape, lambda bi, ci: (0, 0, 0)),
        ],
        out_specs=pl.BlockSpec((1, tc, qw), lambda bi, ci: (bi, ci, 0)),
        out_shape=jax.ShapeDtypeStruct((b, s, qw), BF16),
        compiler_params=_params("parallel", "parallel"),
        name="swa_attention",
    )(proj, proj, proj, bias, sink_col)


def _rope128(x, a, bm, bp):
    return x * a + pltpu.roll(x, LANES - 16, axis=1) * bm + pltpu.roll(x, 16, axis=1) * bp


def _proj_odd_kernel(x_ref, g_ref, w_ref, qn_ref, kvn_ref, wuq_ref, wuk_ref, wuv_ref, a_ref, bm_ref, bp_ref,
                     q_out, k_out, v_out, swa_out):
    xb = _rms(x_ref[...], g_ref[...]).astype(BF16)
    w = w_ref
    c_q = _dot(xb, w[:, 0:MLA_Q_RANK])
    c_kv = _dot(xb, w[:, MLA_Q_RANK:MLA_Q_RANK + MLA_KV_RANK])
    o1 = MLA_Q_RANK + MLA_KV_RANK
    kpe = _dot(xb, w[:, o1:o1 + LANES])
    o2 = o1 + LANES
    swa_out[...] = _dot(xb, w[:, o2:]).astype(swa_out.dtype)
    a, bm, bp = a_ref[...], bm_ref[...], bp_ref[...]
    kpe = _rope128(kpe, a, bm, bp)
    cqn = _rms(c_q, qn_ref[...]).astype(BF16)
    ckvn = _rms(c_kv, kvn_ref[...]).astype(BF16)
    v_out[...] = _dot(ckvn, wuv_ref[...]).astype(v_out.dtype)
    for h in range(MLA_HEADS):
        sl = slice(h * LANES, (h + 1) * LANES)
        q_out[:, sl] = _rope128(_dot(cqn, wuq_ref[:, sl]), a, bm, bp).astype(q_out.dtype)
        k_out[:, sl] = (_dot(ckvn, wuk_ref[:, sl]) + kpe).astype(k_out.dtype)


def _proj_odd(h, g, w, qn, kvn, wuq, wuk, wuv, rope_a, rope_bm, rope_bp, seq, tm=512):
    n, d = h.shape
    n_in = w.shape[1]
    n_swa = n_in - (MLA_Q_RANK + MLA_KV_RANK + LANES)
    tiles_per_seq = seq // tm
    full = lambda shape: pl.BlockSpec(shape, lambda i: (0, 0))
    rope = pl.BlockSpec((tm, LANES), lambda i: (i % tiles_per_seq, 0))
    return pl.pallas_call(
        _proj_odd_kernel,
        grid=(n // tm,),
        in_specs=[
            pl.BlockSpec((tm, d), lambda i: (i, 0)), full((1, d)), full(w.shape),
            full(qn.shape), full(kvn.shape), full(wuq.shape), full(wuk.shape), full(wuv.shape),
            rope, rope, rope,
        ],
        out_specs=[
            pl.BlockSpec((tm, MLA_HEADS * LANES), lambda i: (i, 0)),
            pl.BlockSpec((tm, MLA_HEADS * LANES), lambda i: (i, 0)),
            pl.BlockSpec((tm, MLA_HEADS * MLA_V), lambda i: (i, 0)),
            pl.BlockSpec((tm, n_swa), lambda i: (i, 0)),
        ],
        out_shape=[
            jax.ShapeDtypeStruct((n, MLA_HEADS * LANES), BF16),
            jax.ShapeDtypeStruct((n, MLA_HEADS * LANES), BF16),
            jax.ShapeDtypeStruct((n, MLA_HEADS * MLA_V), BF16),
            jax.ShapeDtypeStruct((n, n_swa), BF16),
        ],
        compiler_params=_params("parallel"),
        name="proj_odd",
    )(h, g, w, qn, kvn, wuq, wuk, wuv, rope_a, rope_bm, rope_bp)


def _out_router_kernel(h_ref, oa_ref, ob_ref, wo_ref, g_ref, wr_hi_ref, wr_lo_ref, br_ref,
                       h_out, hn_out, gates_out, counts_out):
    half = oa_ref.shape[1]
    h = h_ref[...] + _dot(oa_ref[...], wo_ref[0:half, :]) + _dot(ob_ref[...], wo_ref[half:, :])
    h_out[...] = h
    hn = _rms(h, g_ref[...])
    hn_out[...] = hn.astype(hn_out.dtype)
    z = _dot_hi(hn, wr_hi_ref[...], wr_lo_ref[...]) + br_ref[...]
    lane = lax.broadcasted_iota(jnp.int32, z.shape, 1)
    big = jnp.int32(LANES)
    is_grp = (lane >= N_EXPERTS) & (lane < N_EXPERTS + N_GROUPS)
    zg = jnp.where(is_grp, z, -jnp.inf)
    g_max = jnp.max(zg, axis=-1, keepdims=True)
    g_w = 1.0 / jnp.sum(jnp.exp(zg - g_max), axis=-1, keepdims=True)
    g_idx = jnp.min(jnp.where(zg == g_max, lane - N_EXPERTS, big), axis=-1, keepdims=True)
    in_grp = (lane < N_EXPERTS) & ((lane // EXPERTS_PER_GROUP) == g_idx)
    ze = jnp.where(in_grp, z, -jnp.inf)
    v1 = jnp.max(ze, axis=-1, keepdims=True)
    i1 = jnp.min(jnp.where(ze == v1, lane, big), axis=-1, keepdims=True)
    ze2 = jnp.where(lane == i1, -jnp.inf, ze)
    v2 = jnp.max(ze2, axis=-1, keepdims=True)
    i2 = jnp.min(jnp.where(ze2 == v2, lane, big), axis=-1, keepdims=True)
    e2 = jnp.exp(v2 - v1)
    w1 = g_w / (1.0 + e2)
    w2 = g_w * e2 / (1.0 + e2)
    for grp in range(N_GROUPS):
        first = grp * EXPERTS_PER_GROUP
        slot = jnp.where(lane == i1 - first, w1, 0.0) + jnp.where(lane == i2 - first, w2, 0.0)
        slot = jnp.where(lane < EXPERTS_PER_GROUP, slot, 0.0)
        slot = jnp.where((lane == EXPERTS_PER_GROUP) & (g_idx == grp), 1.0, slot)
        gates_out[:, grp * LANES:(grp + 1) * LANES] = slot
    routed = jnp.where((lane == g_idx) & (lane < N_GROUPS), 1.0, 0.0)
    counts_out[0] = jnp.broadcast_to(jnp.sum(routed, axis=0, keepdims=True), counts_out.shape[1:])


def _out_router(h, oa, ob, wo, g, wr_hi, wr_lo, br, tm=512):
    n, d = h.shape
    half = oa.shape[1]
    full = lambda shape: pl.BlockSpec(shape, lambda i: (0, 0))
    tile = lambda w: pl.BlockSpec((tm, w), lambda i: (i, 0))
    return pl.pallas_call(
        _out_router_kernel,
        grid=(n // tm,),
        in_specs=[tile(d), tile(half), tile(half), full(wo.shape), full((1, d)),
                  full(wr_hi.shape), full(wr_lo.shape), full((1, LANES))],
        out_specs=[tile(d), tile(d), tile(N_GROUPS * LANES), pl.BlockSpec((1, 8, LANES), lambda i: (i, 0, 0))],
        out_shape=[
            jax.ShapeDtypeStruct((n, d), F32),
            jax.ShapeDtypeStruct((n, d), BF16),
            jax.ShapeDtypeStruct((n, N_GROUPS * LANES), F32),
            jax.ShapeDtypeStruct((n // tm, 8, LANES), F32),
        ],
        compiler_params=_params("parallel"),
        name="out_router",
    )(h, oa, ob, wo, g, wr_hi, wr_lo, br)


MOE_CHUNK = 256
MOE_TILE = 1024
MOE_ROWS = 1024
MOE_ALIGN = 16
MOE_EXPERTS_PER_STEP = 2
ROUTER_TILE = 512


def _moe_sorted_tiles(n):
    n_seg = (n // MOE_TILE) * N_GROUPS
    rows = n + n_seg * (MOE_ALIGN - 1) + N_GROUPS * (MOE_CHUNK + MOE_ROWS - 1)
    return -(-rows // MOE_ROWS)


def _moe_tables(counts, n):
    n_rt = _moe_sorted_tiles(n)
    seg_len = (counts + (MOE_ALIGN - 1)) // MOE_ALIGN * MOE_ALIGN
    group_len = seg_len.sum(axis=0)
    group_span = (group_len + MOE_CHUNK + MOE_ROWS - 1) // MOE_ROWS * MOE_ROWS
    group_end = jnp.cumsum(group_span)
    group_start = group_end - group_span
    seg_off = group_start[None, :] + jnp.cumsum(seg_len, axis=0) - seg_len
    rt_start = jnp.arange(n_rt, dtype=jnp.int32) * MOE_ROWS
    rt_group = jnp.minimum(jnp.sum(rt_start[:, None] >= group_end[None, :], axis=1), N_GROUPS - 1)
    rt_valid = jnp.clip(group_start[rt_group] + group_len[rt_group] - rt_start, 0, MOE_ROWS)
    n_used = (group_end[-1] // MOE_ROWS).reshape(1)
    i32 = lambda a: a.astype(jnp.int32)
    return i32(seg_off.reshape(-1)), i32(rt_group), i32(rt_valid), i32(n_used)


def _moe_positions(gates_ref, tri_ref, first_step):
    tm = gates_ref.shape[0]

    @pl.when(first_step)
    def _():
        ri = lax.broadcasted_iota(jnp.int32, (tm, tm), 0)
        ci = lax.broadcasted_iota(jnp.int32, (tm, tm), 1)
        tri_ref[...] = jnp.where(ci < ri, 1.0, 0.0).astype(BF16)

    gates = gates_ref[...]
    lane = lax.broadcasted_iota(jnp.int32, gates.shape, 1)
    flag = jnp.where(lane == EXPERTS_PER_GROUP, gates, 0.0)
    rank = _dot(tri_ref[...], flag.astype(BF16))
    pos = jnp.where(flag > 0.0, rank, -1.0)
    pos_col = jnp.sum(jnp.where(lane == EXPERTS_PER_GROUP, pos, 0.0), axis=-1, keepdims=True)
    pos_row = jnp.transpose(jnp.broadcast_to(pos_col, (tm, LANES)))[0:1, :]
    return gates, pos_col, pos_row


def _moe_chunks(count, body):
    n_full = count // MOE_CHUNK
    rest = count - n_full * MOE_CHUNK

    def full_chunk(k, carry):
        body(k * MOE_CHUNK, MOE_CHUNK)
        return carry

    lax.fori_loop(0, n_full, full_chunk, 0)

    @pl.when(rest > MOE_CHUNK // 2)
    def _():
        body(n_full * MOE_CHUNK, MOE_CHUNK)

    @pl.when((rest > 0) & (rest <= MOE_CHUNK // 2))
    def _():
        body(n_full * MOE_CHUNK, MOE_CHUNK // 2)


def _moe_pack_kernel(counts_ref, off_ref, hn_ref, gates_ref, xs_in, gs_in, xs_ref, gs_ref, tri_ref, xbuf, gbuf):
    del xs_in, gs_in
    i = pl.program_id(0)
    grp = pl.program_id(1)
    seg = i * N_GROUPS + grp
    gates, _, pos_row = _moe_positions(gates_ref, tri_ref, (i == 0) & (grp == 0))
    g_hi = gates.astype(BF16)
    g_lo = (gates - g_hi.astype(F32)).astype(BF16)

    def chunk(first_pos, rows):
        slot = lax.broadcasted_iota(jnp.int32, (rows, 1), 0).astype(F32) + first_pos.astype(F32)
        gather = jnp.where(pos_row == slot, 1.0, 0.0).astype(BF16)
        xbuf[0:rows, :] = _dot(gather, hn_ref[...]).astype(BF16)
        gbuf[0:rows, :] = _dot(gather, g_hi) + _dot(gather, g_lo)
        dst = pl.multiple_of(off_ref[seg] + first_pos, MOE_ALIGN)
        pltpu.sync_copy(xbuf.at[pl.ds(0, rows)], xs_ref.at[pl.ds(dst, rows)])
        pltpu.sync_copy(gbuf.at[pl.ds(0, rows)], gs_ref.at[pl.ds(dst, rows)])

    _moe_chunks(counts_ref[seg], chunk)


def _moe_pack(counts, seg_off, hn, gates, n_rt):
    n, d = hn.shape
    rows = n_rt * MOE_ROWS
    grid_spec = pltpu.PrefetchScalarGridSpec(
        num_scalar_prefetch=2,
        grid=(n // MOE_TILE, N_GROUPS),
        in_specs=[
            pl.BlockSpec((MOE_TILE, d), lambda i, g, c, o: (i, 0)),
            pl.BlockSpec((MOE_TILE, LANES), lambda i, g, c, o: (i, g)),
            pl.BlockSpec(memory_space=pl.ANY),
            pl.BlockSpec(memory_space=pl.ANY),
        ],
        out_specs=[pl.BlockSpec(memory_space=pl.ANY), pl.BlockSpec(memory_space=pl.ANY)],
        scratch_shapes=[pltpu.VMEM((MOE_TILE, MOE_TILE), BF16), pltpu.VMEM((MOE_CHUNK, d), BF16),
                        pltpu.VMEM((MOE_CHUNK, LANES), F32)],
    )
    return pl.pallas_call(
        _moe_pack_kernel,
        grid_spec=grid_spec,
        out_shape=[jax.ShapeDtypeStruct((rows, d), BF16), jax.ShapeDtypeStruct((rows, LANES), F32)],
        input_output_aliases={4: 0, 5: 1},
        compiler_params=_params("arbitrary", "arbitrary"),
        name="moe_pack",
    )(counts, seg_off, hn, gates, jnp.zeros((rows, d), BF16), jnp.zeros((rows, LANES), F32))


def _moe_experts_kernel(grp_ref, valid_ref, used_ref, xs_ref, gs_ref, wg_ref, wu_ref, wd_ref, ys_ref, acc_ref):
    rt = pl.program_id(0)
    j = pl.program_id(1)

    @pl.when(rt < used_ref[0])
    def _():
        @pl.when(j == 0)
        def _():
            acc_ref[...] = jnp.zeros_like(acc_ref)

        x = xs_ref[...]
        gates = gs_ref[...]
        lane = lax.broadcasted_iota(jnp.int32, gates.shape, 1)
        for e in range(MOE_EXPERTS_PER_STEP):
            gate = jnp.sum(jnp.where(lane == j * MOE_EXPERTS_PER_STEP + e, gates, 0.0), axis=-1, keepdims=True)
            a = _dot(x, wg_ref[0, 0, e].astype(BF16))
            u = _dot(x, wu_ref[0, 0, e].astype(BF16))
            act = (a * jax.nn.sigmoid(a)) * u * gate
            acc_ref[...] += _dot(act.astype(BF16), wd_ref[0, 0, e].astype(BF16))

        @pl.when(j == pl.num_programs(1) - 1)
        def _():
            row = lax.broadcasted_iota(jnp.int32, (acc_ref.shape[0], 1), 0)
            ys_ref[...] = jnp.where(row < valid_ref[rt], acc_ref[...], 0.0).astype(ys_ref.dtype)

    @pl.when((rt >= used_ref[0]) & (j == pl.num_programs(1) - 1))
    def _():
        ys_ref[...] = jnp.zeros_like(ys_ref)


def _moe_experts(rt_group, rt_valid, n_used, xs, gs, w_gate, w_up, w_down, layer):
    rows, d = xs.shape
    n_rt = rows // MOE_ROWS
    eps = MOE_EXPERTS_PER_STEP
    n_steps = EXPERTS_PER_GROUP // eps

    def tile_idx(rt, j, grp, valid, used):
        return jnp.minimum(rt, used[0] - 1)

    def w_idx(rt, j, grp, valid, used):
        live = rt < used[0]
        return (layer, grp[tile_idx(rt, j, grp, valid, used)], jnp.where(live, j, n_steps - 1), 0, 0)

    row_map = lambda rt, j, grp, valid, used: (tile_idx(rt, j, grp, valid, used), 0)
    grid_spec = pltpu.PrefetchScalarGridSpec(
        num_scalar_prefetch=3,
        grid=(n_rt, n_steps),
        in_specs=[
            pl.BlockSpec((MOE_ROWS, d), row_map),
            pl.BlockSpec((MOE_ROWS, LANES), row_map),
            pl.BlockSpec((1, 1, eps, d, D_EXPERT), w_idx),
            pl.BlockSpec((1, 1, eps, d, D_EXPERT), w_idx),
            pl.BlockSpec((1, 1, eps, D_EXPERT, d), w_idx),
        ],
        out_specs=pl.BlockSpec((MOE_ROWS, d), lambda rt, j, grp, valid, used: (rt, 0)),
        scratch_shapes=[pltpu.VMEM((MOE_ROWS, d), F32)],
    )
    return pl.pallas_call(
        _moe_experts_kernel,
        grid_spec=grid_spec,
        out_shape=jax.ShapeDtypeStruct((rows, d), BF16),
        compiler_params=_params("arbitrary", "arbitrary"),
        name="moe_experts",
    )(rt_group, rt_valid, n_used, xs, gs, w_gate, w_up, w_down)


def _moe_combine_kernel(counts_ref, off_ref, gates_ref, h_ref, fg_ref, ys_ref, o_ref, acc_ref, tri_ref, ybuf,
                        *, final_norm):
    i = pl.program_id(0)
    grp = pl.program_id(1)
    seg = i * N_GROUPS + grp

    @pl.when(grp == 0)
    def _():
        acc_ref[...] = jnp.zeros_like(acc_ref)

    _, pos_col, _ = _moe_positions(gates_ref, tri_ref, (i == 0) & (grp == 0))

    def chunk(first_pos, rows):
        src = pl.multiple_of(off_ref[seg] + first_pos, MOE_ALIGN)
        pltpu.sync_copy(ys_ref.at[pl.ds(src, rows)], ybuf.at[pl.ds(0, rows)])
        slot = lax.broadcasted_iota(jnp.int32, (1, rows), 1).astype(F32) + first_pos.astype(F32)
        scatter = jnp.where(pos_col == slot, 1.0, 0.0).astype(BF16)
        acc_ref[...] += _dot(scatter, ybuf[0:rows, :])

    _moe_chunks(counts_ref[seg], chunk)

    @pl.when(grp == N_GROUPS - 1)
    def _():
        h = h_ref[...] + acc_ref[...]
        o_ref[...] = _rms(h, fg_ref[...]) if final_norm else h


def _moe_combine(counts, seg_off, gates, h, fg, ys, final_norm):
    n, d = h.shape
    grid_spec = pltpu.PrefetchScalarGridSpec(
        num_scalar_prefetch=2,
        grid=(n // MOE_TILE, N_GROUPS),
        in_specs=[
            pl.BlockSpec((MOE_TILE, LANES), lambda i, g, c, o: (i, g)),
            pl.BlockSpec((MOE_TILE, d), lambda i, g, c, o: (i, 0)),
            pl.BlockSpec((1, d), lambda i, g, c, o: (0, 0)),
            pl.BlockSpec(memory_space=pl.ANY),
        ],
        out_specs=pl.BlockSpec((MOE_TILE, d), lambda i, g, c, o: (i, 0)),
        scratch_shapes=[pltpu.VMEM((MOE_TILE, d), F32), pltpu.VMEM((MOE_TILE, MOE_TILE), BF16),
                        pltpu.VMEM((MOE_CHUNK, d), BF16)],
    )
    return pl.pallas_call(
        functools.partial(_moe_combine_kernel, final_norm=final_norm),
        grid_spec=grid_spec,
        out_shape=jax.ShapeDtypeStruct((n, d), F32),
        compiler_params=_params("arbitrary", "arbitrary"),
        name="moe_combine",
    )(counts, seg_off, gates, h, fg, ys)


def _alibi_slopes(n):
    return jnp.exp2(-8.0 * jnp.arange(1, n + 1, dtype=F32) / n)


def _band_bias(slopes, max_steps, step_dist):
    steps = (jnp.arange(BLK)[:, None] + BLK) - jnp.arange(2 * BLK)[None, :]
    in_band = (steps >= 0) & (steps <= max_steps)
    dist = (steps * step_dist).astype(F32)
    return jnp.where(in_band[None], -slopes.astype(F32)[:, None, None] * dist[None], NEG)


def _rope_tables(s):
    inv = ROPE_THETA ** (-jnp.arange(0, MLA_ROPE, 2, dtype=F32) / MLA_ROPE)
    ang = jnp.arange(s, dtype=F32)[:, None] * inv[None, :]
    cos, sin = jnp.cos(ang), jnp.sin(ang)
    half = MLA_ROPE // 2
    zeros_tail = jnp.zeros((s, LANES - MLA_NOPE - MLA_ROPE), F32)
    a = jnp.concatenate([jnp.ones((s, MLA_NOPE), F32), cos, cos, zeros_tail], axis=1)
    zeros_nope = jnp.zeros((s, MLA_NOPE), F32)
    zeros_half = jnp.zeros((s, half), F32)
    bm = jnp.concatenate([zeros_nope, -sin, zeros_half, zeros_tail], axis=1)
    bp = jnp.concatenate([zeros_nope, zeros_half, sin, zeros_tail], axis=1)
    return a, bm, bp


def _pad_cols(w, width):
    return jnp.pad(w, ((0, 0), (0, width - w.shape[1])))


def _router_weights(w_group, b_group, w_router, b_router):
    w = _pad_cols(jnp.concatenate([w_router, w_group], axis=1), LANES)
    b = _pad_cols(jnp.concatenate([b_router, b_group])[None, :], LANES)
    hi, lo = _hi_lo(w)
    return hi, lo, b


def kernel(x, attn_norm, ffn_norm, final_norm, e_w_in, e_b_f, e_w_out, o_w_in, o_q_norm, o_kv_norm, o_w_uq,
           o_w_ukv, o_sinks, o_w_out, moe_w_group, moe_b_group, moe_w_router, moe_b_router, moe_w_gate,
           moe_w_up, moe_w_down):
    b, s, d = x.shape
    n = b * s
    depth = attn_norm.shape[0]
    assert s % (BLK * DIL_BRANCHES[-1][1]) == 0 and d == D_MODEL
    h = x.reshape(n, d)

    dil_slopes = _alibi_slopes(DIL_HEADS)
    dil_bias = jnp.stack([_band_bias(dil_slopes, w // dl, dl) for (w, dl) in DIL_BRANCHES]) * LOG2E
    own_half = jnp.arange(2 * BLK) >= BLK
    dil_bias = jnp.stack([dil_bias, jnp.where(own_half, dil_bias, NEG)], axis=2)
    dil_bias = dil_bias.reshape(len(DIL_BRANCHES), DIL_HEADS // 2, 2, 2, BLK, 2 * BLK)
    dil_bias = dil_bias.transpose(0, 1, 3, 2, 4, 5).reshape(len(DIL_BRANCHES), DIL_HEADS // 2, 2, 2 * BLK, 2 * BLK)
    swa_bias = _band_bias(_alibi_slopes(SWA_Q_HEADS), SWA_WINDOW - 1, 1)
    swa_bias = swa_bias.reshape(SWA_KV_HEADS, (SWA_Q_HEADS // SWA_KV_HEADS) * BLK, 2 * BLK)
    rope_a, rope_bm, rope_bp = _rope_tables(s)

    for layer in range(depth):
        i = layer // 2
        g_attn = attn_norm[layer][None, :]
        if layer % 2 == 0:
            w_in = e_w_in[i]
            hq = FOX_HEADS * HEAD_DIM
            scale = HEAD_DIM ** -0.5
            cols = [w_in[:, 0:hq] * (scale * LOG2E), w_in[:, hq:2 * hq], w_in[:, 2 * hq:3 * hq]]
            o = 3 * hq + FOX_HEADS
            cols += [w_in[:, o:o + hq] * (scale * LOG2E), w_in[:, o + hq:o + 2 * hq], w_in[:, o + 2 * hq:o + 3 * hq]]
            w_main = jnp.concatenate(cols, axis=1).astype(BF16)
            wf_hi, wf_lo = _hi_lo(_pad_cols(w_in[:, 3 * hq:o], LANES))
            b_f = _pad_cols(e_b_f[i][None, :], LANES)
            proj, c = _proj_even(h, g_attn, w_main, wf_hi, wf_lo, b_f, s)
            proj = proj.reshape(b, s, -1)
            o_a = _causal_attention(proj, proj, proj, c.reshape(b, s, LANES), q_blk0=0, k_blk0=4, v_blk0=8,
                                    n_pairs=FOX_HEADS // 2)
            o_b = _dilated_attention(proj, dil_bias, q_blk0=12, k_blk0=16, v_blk0=20)
            w_out = e_w_out[i].astype(BF16)
        else:
            w_in = o_w_in[i]
            o1 = MLA_Q_RANK + MLA_KV_RANK
            o2 = o1 + MLA_ROPE
            sq = SWA_Q_HEADS * HEAD_DIM
            kpe_cols = jnp.pad(w_in[:, o1:o2], ((0, 0), (MLA_NOPE, LANES - MLA_NOPE - MLA_ROPE)))
            w_main = jnp.concatenate(
                [w_in[:, :o1], kpe_cols, w_in[:, o2:o2 + sq] * (HEAD_DIM ** -0.5), w_in[:, o2 + sq:]],
                axis=1).astype(BF16)
            dq = MLA_NOPE + MLA_ROPE
            wuq = o_w_uq[i].reshape(MLA_Q_RANK, MLA_HEADS, dq) * (dq ** -0.5 * LOG2E)
            wuq = jnp.pad(wuq, ((0, 0), (0, 0), (0, LANES - dq))).reshape(MLA_Q_RANK, MLA_HEADS * LANES)
            wukv = o_w_ukv[i].reshape(MLA_KV_RANK, MLA_HEADS, MLA_NOPE + MLA_V)
            wuk = jnp.pad(wukv[:, :, :MLA_NOPE], ((0, 0), (0, 0), (0, LANES - MLA_NOPE)))
            wuk = wuk.reshape(MLA_KV_RANK, MLA_HEADS * LANES)
            wuv = wukv[:, :, MLA_NOPE:].reshape(MLA_KV_RANK, MLA_HEADS * MLA_V)
            q_full, k_full, v_mla, swa = _proj_odd(
                h, g_attn, w_main, o_q_norm[i][None, :], o_kv_norm[i][None, :], wuq.astype(BF16),
                wuk.astype(BF16), wuv.astype(BF16), rope_a, rope_bm, rope_bp, s)
            o_a = _causal_attention(q_full.reshape(b, s, -1), k_full.reshape(b, s, -1), v_mla.reshape(b, s, -1),
                                    None, q_blk0=0, k_blk0=0, v_blk0=0, n_pairs=MLA_HEADS // 2)
            grp = SWA_Q_HEADS // SWA_KV_HEADS
            sink_col = jnp.repeat(o_sinks[i].reshape(SWA_KV_HEADS, grp), BLK, axis=1)[:, :, None]
            o_b = _swa_attention(swa.reshape(b, s, -1), swa_bias, sink_col, q_blk0=0, k_blk0=4, v_blk0=5)
            w_out = o_w_out[i].astype(BF16)

        wr_hi, wr_lo, b_r = _router_weights(moe_w_group[layer], moe_b_group[layer], moe_w_router[layer],
                                            moe_b_router[layer])
        h, hn, gates, counts = _out_router(h, o_a.reshape(n, -1), o_b.reshape(n, -1), w_out,
                                           ffn_norm[layer][None, :], wr_hi, wr_lo, b_r, tm=ROUTER_TILE)
        counts = counts[:, 0, :N_GROUPS].reshape(n // MOE_TILE, MOE_TILE // ROUTER_TILE, N_GROUPS).sum(axis=1)
        counts = counts.astype(jnp.int32)
        seg_off, rt_group, rt_valid, n_used = _moe_tables(counts, n)
        counts = counts.reshape(-1)
        xs, gs = _moe_pack(counts, seg_off, hn, gates, _moe_sorted_tiles(n))
        ys = _moe_experts(rt_group, rt_valid, n_used, xs, gs, moe_w_gate, moe_w_up, moe_w_down, layer)
        h = _moe_combine(counts, seg_off, gates, h, final_norm[None, :], ys, final_norm=layer == depth - 1)
    return h.reshape(b, s, d)
```

```python
import functools

import jax
import jax.numpy as jnp
from jax import lax
from jax.experimental import pallas as pl
from jax.experimental.pallas import tpu as pltpu

F32 = jnp.float32
BF16 = jnp.bfloat16

D_MODEL = 1024
HEAD_DIM = 64
BLK = 128
NEG = -1e30
RMS_EPS = 1e-6
FOX_HEADS = 8
DIL_HEADS = 8
DIL_BRANCHES = ((128, 1), (512, 4), (2048, 16))
MLA_HEADS = 8
MLA_Q_RANK = 384
MLA_KV_RANK = 256
MLA_NOPE = 64
MLA_ROPE = 32
MLA_V = 64
ROPE_THETA = 10000.0
SWA_Q_HEADS = 8
SWA_KV_HEADS = 2
SWA_WINDOW = 128
N_GROUPS = 4
EXPERTS_PER_GROUP = 8
N_EXPERTS = N_GROUPS * EXPERTS_PER_GROUP
D_EXPERT = 256

LANES = 128
VMEM_LIMIT = 56 * 1024 * 1024

NT_DIMS = (((1,), (1,)), ((), ()))
LOG2E = 1.4426950408889634


def _params(*sem):
    return pltpu.CompilerParams(dimension_semantics=sem, vmem_limit_bytes=VMEM_LIMIT)


def _dot(a, b):
    return jnp.dot(a, b, preferred_element_type=F32)


def _rms(x, g):
    return x * lax.rsqrt(jnp.mean(x * x, axis=-1, keepdims=True) + RMS_EPS) * g


def _hi_lo(w):
    hi = w.astype(BF16)
    return hi, (w - hi.astype(F32)).astype(BF16)


def _dot_hi(x, w_hi, w_lo):
    x_hi = x.astype(BF16)
    x_lo = (x - x_hi.astype(F32)).astype(BF16)
    return _dot(x_hi, w_hi) + (_dot(x_hi, w_lo) + _dot(x_lo, w_hi))


def _split3(x):
    x1 = x.astype(BF16)
    r1 = x - x1.astype(F32)
    x2 = r1.astype(BF16)
    x3 = (r1 - x2.astype(F32)).astype(BF16)
    return x1, x2, x3


def _proj_even_kernel(x_ref, g_ref, w_ref, wf_hi_ref, wf_lo_ref, bf_ref, out_ref, c_ref, carry_ref, *, tiles_per_seq):
    xn = _rms(x_ref[...], g_ref[...])
    xb = xn.astype(BF16)
    n_out = out_ref.shape[1]
    for c in range(0, n_out, 512):
        out_ref[:, c:c + 512] = _dot(xb, w_ref[:, c:c + 512]).astype(out_ref.dtype)
    z = _dot_hi(xn, wf_hi_ref[...], wf_lo_ref[...]) + bf_ref[...]
    logf = jnp.minimum(z, 0.0) - jnp.log1p(jnp.exp(-jnp.abs(z)))

    @pl.when(pl.program_id(0) % tiles_per_seq == 0)
    def _():
        carry_ref[...] = jnp.zeros_like(carry_ref)

    tm = logf.shape[0]
    ri = lax.broadcasted_iota(jnp.int32, (tm, tm), 0)
    ci = lax.broadcasted_iota(jnp.int32, (tm, tm), 1)
    lower = jnp.where(ci <= ri, 1.0, 0.0).astype(BF16)
    l1, l2, l3 = _split3(logf)
    c = (_dot(lower, l1) + (_dot(lower, l2) + _dot(lower, l3))) + carry_ref[...]
    c_ref[...] = c
    carry_ref[...] = c[tm - 1:tm, :]


def _proj_even(h, g, w, wf_hi, wf_lo, bf, seq, tm=512):
    n, d = h.shape
    n_out = w.shape[1]
    return pl.pallas_call(
        functools.partial(_proj_even_kernel, tiles_per_seq=seq // tm),
        grid=(n // tm,),
        in_specs=[
            pl.BlockSpec((tm, d), lambda i: (i, 0)),
            pl.BlockSpec((1, d), lambda i: (0, 0)),
            pl.BlockSpec((d, n_out), lambda i: (0, 0)),
            pl.BlockSpec((d, LANES), lambda i: (0, 0)),
            pl.BlockSpec((d, LANES), lambda i: (0, 0)),
            pl.BlockSpec((1, LANES), lambda i: (0, 0)),
        ],
        out_specs=[
            pl.BlockSpec((tm, n_out), lambda i: (i, 0)),
            pl.BlockSpec((tm, LANES), lambda i: (i, 0)),
        ],
        out_shape=[
            jax.ShapeDtypeStruct((n, n_out), BF16),
            jax.ShapeDtypeStruct((n, LANES), F32),
        ],
        scratch_shapes=[pltpu.VMEM((1, LANES), F32)],
        compiler_params=_params("arbitrary"),
        name="proj_even",
    )(h, g, w, wf_hi, wf_lo, bf)


def _own_lanes(lane, h):
    return lane < HEAD_DIM if h == 0 else lane >= HEAD_DIM


def _causal_kernel(*refs, fox, tq, tk):
    if fox:
        q_ref, k_ref, v_ref, c_ref, o_ref, vaug_t, s_even, s_odd, kaug = refs
    else:
        q_ref, k_ref, v_ref, o_ref, vaug_t, s_even, s_odd = refs
    pair = pl.program_id(1)
    qi = pl.program_id(2)
    seq = v_ref.shape[1]

    @pl.when(qi == 0)
    def _():
        lane = lax.broadcasted_iota(jnp.int32, (tk, LANES), 1)
        feat = lax.broadcasted_iota(jnp.int32, (LANES, tk), 0)

        def fill(t, carry):
            rows = pl.ds(pl.multiple_of(t * tk, tk), tk)
            vt = jnp.transpose(v_ref[0, rows, :].astype(F32))
            if fox:
                kp = k_ref[0, rows, :].astype(F32)
                c = c_ref[0, rows, :]
            for h in range(2):
                vaug_t[h, t] = jnp.where(_own_lanes(feat, h), vt, 1.0).astype(BF16)
                if fox:
                    ch = jnp.sum(jnp.where(lane == 2 * pair + h, c, 0.0), axis=-1, keepdims=True)
                    c1, c2, c3 = _split3(ch * (-LOG2E))
                    base = HEAD_DIM if h == 0 else 0
                    extra = jnp.where(lane == base, c1.astype(F32),
                                      jnp.where(lane == base + 1, c2.astype(F32),
                                                jnp.where(lane == base + 2, c3.astype(F32), 0.0)))
                    kaug[h, rows, :] = jnp.where(_own_lanes(lane, h), kp, extra).astype(BF16)
            return carry

        lax.fori_loop(0, seq // tk, fill, 0)

    lane_q = lax.broadcasted_iota(jnp.int32, (tq, LANES), 1)
    qs = []
    for h in range(2):
        if fox:
            base = HEAD_DIM if h == 0 else 0
            ones = jnp.where((lane_q >= base) & (lane_q < base + 3), 1.0, 0.0)
            qs.append(jnp.where(_own_lanes(lane_q, h), q_ref[0].astype(F32), ones).astype(BF16))
        else:
            qs.append(q_ref[0, :, h * LANES:(h + 1) * LANES])

    key = lax.broadcasted_iota(jnp.int32, (tk, tq), 0)
    qry = lax.broadcasted_iota(jnp.int32, (tk, tq), 1)

    def scores(j, buf):
        start = pl.multiple_of(j * tk, tk)
        for h in range(2):
            if fox:
                kj = kaug[h, pl.ds(start, tk), :]
            else:
                kj = k_ref[0, pl.ds(start, tk), h * LANES:(h + 1) * LANES]
            buf[h] = lax.dot_general(kj, qs[h], NT_DIMS, preferred_element_type=F32)

    def absorb(j, buf, carry, masked):
        new = []
        for h in range(2):
            m, acc = carry[h]
            s = buf[h]
            if masked:
                s = jnp.where(key <= qry, s, NEG)
            m_new = jnp.maximum(m, jnp.max(s, axis=0, keepdims=True))
            p = jnp.exp2(s - m_new)
            acc = jnp.exp2(m - m_new) * acc + _dot(vaug_t[h, j], p.astype(BF16))
            new.append((m_new, acc))
        return tuple(new)

    def finish(carry):
        (_, acc0), (_, acc1) = carry
        feat_q = lax.broadcasted_iota(jnp.int32, (LANES, tq), 0)
        out_t = jnp.where(feat_q < HEAD_DIM, acc0 / acc0[HEAD_DIM:HEAD_DIM + 1, :], acc1 / acc1[0:1, :])
        o_ref[0] = jnp.transpose(out_t).astype(o_ref.dtype)

    def pair_step(t, carry):
        scores(2 * t + 1, s_odd)
        carry = absorb(2 * t, s_even, carry, False)
        scores(2 * t + 2, s_even)
        return absorb(2 * t + 1, s_odd, carry, False)

    init = tuple((jnp.full((1, tq), NEG, F32), jnp.zeros((LANES, tq), F32)) for _ in range(2))
    scores(0, s_even)
    carry = lax.fori_loop(0, qi // 2, pair_step, init)

    @pl.when(qi % 2 == 0)
    def _():
        finish(absorb(qi, s_even, carry, True))

    @pl.when(qi % 2 == 1)
    def _():
        scores(qi, s_odd)
        finish(absorb(qi, s_odd, absorb(qi - 1, s_even, carry, False), True))


def _causal_attention(q_arr, k_arr, v_arr, c_arr, *, q_blk0, k_blk0, v_blk0, n_pairs, tq=512):
    b, s, _ = q_arr.shape
    fox = c_arr is not None
    qk_w = LANES if fox else 2 * LANES
    in_specs = [
        pl.BlockSpec((1, tq, qk_w), lambda bi, p, qi: (bi, qi, q_blk0 + p)),
        pl.BlockSpec((1, s, qk_w), lambda bi, p, qi: (bi, 0, k_blk0 + p)),
        pl.BlockSpec((1, s, LANES), lambda bi, p, qi: (bi, 0, v_blk0 + p)),
    ]
    args = [q_arr, k_arr, v_arr]
    scratch = [pltpu.VMEM((2, s // tq, LANES, tq), BF16), pltpu.VMEM((2, tq, tq), F32), pltpu.VMEM((2, tq, tq), F32)]
    if fox:
        in_specs.append(pl.BlockSpec((1, s, LANES), lambda bi, p, qi: (bi, 0, 0)))
        args.append(c_arr)
        scratch.append(pltpu.VMEM((2, s, LANES), BF16))
    return pl.pallas_call(
        functools.partial(_causal_kernel, fox=fox, tq=tq, tk=tq),
        grid=(b, n_pairs, s // tq),
        in_specs=in_specs,
        out_specs=pl.BlockSpec((1, tq, LANES), lambda bi, p, qi: (bi, qi, p)),
        out_shape=jax.ShapeDtypeStruct((b, s, n_pairs * LANES), BF16),
        scratch_shapes=scratch,
        compiler_params=_params("parallel", "parallel", "arbitrary"),
        name="causal_attention",
    )(*args)


DIL_PAD = BLK * max(d for _, d in DIL_BRANCHES)
DIL_UNROLL = 4


def _dilated_kernel(q_ref, k_ref, v_ref, bias_ref, o_ref, qf, kf, vf, acc_s, m_s, l_s, s_even, s_odd, *, seq):
    qf[...] = q_ref[0].astype(F32)
    kf[0:DIL_PAD, :] = jnp.zeros((DIL_PAD, LANES), F32)
    vf[0:DIL_PAD, :] = jnp.zeros((DIL_PAD, LANES), F32)
    kf[DIL_PAD:, :] = k_ref[0].astype(F32)
    vf[DIL_PAD:, :] = v_ref[0].astype(F32)
    head0 = lax.broadcasted_iota(jnp.int32, (BLK, LANES), 1) < HEAD_DIM
    head0_k = lax.broadcasted_iota(jnp.int32, (2 * BLK, LANES), 1) < HEAD_DIM
    ones0 = jnp.where(head0_k, 1.0, 0.0).astype(BF16)
    ones1 = jnp.where(head0_k, 0.0, 1.0).astype(BF16)

    def place(dil, u):
        bi = u // dil
        q_start = (u % dil) + (dil * BLK) * bi
        return bi, q_start, q_start + (DIL_PAD - dil * BLK)

    def scores(br, dil, g, buf):
        for i in range(DIL_UNROLL):
            bi, q_start, k_start = place(dil, g * DIL_UNROLL + i)
            q2 = qf[pl.ds(q_start, BLK, stride=dil), :]
            q_st = jnp.concatenate([jnp.where(head0, q2, 0.0), jnp.where(head0, 0.0, q2)], axis=0).astype(BF16)
            kb = kf[pl.ds(k_start, 2 * BLK, stride=dil), :].astype(BF16)
            s = lax.dot_general(q_st, kb, NT_DIMS, preferred_element_type=F32)
            buf[i] = s + bias_ref[br, 0, jnp.where(bi == 0, 1, 0)]

    def absorb(br, dil, g, buf, first):
        for i in range(DIL_UNROLL):
            _, q_start, k_start = place(dil, g * DIL_UNROLL + i)
            s = buf[i]
            m = jnp.max(s, axis=-1, keepdims=True)
            p = jnp.exp2(s - m).astype(BF16)
            p_cat = jnp.concatenate([p[:BLK], p[BLK:]], axis=1)
            v2 = vf[pl.ds(k_start, 2 * BLK, stride=dil), :]
            rhs = jnp.concatenate([
                jnp.concatenate([jnp.where(head0_k, v2, 0.0).astype(BF16), ones0], axis=1),
                jnp.concatenate([jnp.where(head0_k, 0.0, v2).astype(BF16), ones1], axis=1)], axis=0)
            acc2 = _dot(p_cat, rhs)
            acc_b, l_b = acc2[:, :LANES], acc2[:, LANES:]
            m_b = jnp.where(head0, jnp.broadcast_to(m[:BLK], (BLK, LANES)), jnp.broadcast_to(m[BLK:], (BLK, LANES)))
            idx = pl.ds(q_start, BLK, stride=dil)
            if first:
                m_s[idx, :] = m_b
                l_s[idx, :] = l_b
                acc_s[idx, :] = acc_b
                continue
            m_old = m_s[idx, :]
            m_new = jnp.maximum(m_old, m_b)
            a_old = jnp.exp2(m_old - m_new)
            a_b = jnp.exp2(m_b - m_new)
            m_s[idx, :] = m_new
            l_s[idx, :] = a_old * l_s[idx, :] + a_b * l_b
            acc_s[idx, :] = a_old * acc_s[idx, :] + a_b * acc_b

    n_groups = seq // (BLK * DIL_UNROLL)
    order = sorted(range(len(DIL_BRANCHES)), key=lambda i: -DIL_BRANCHES[i][1])
    for pos, br in enumerate(order):
        dil = DIL_BRANCHES[br][1]
        first = pos == 0

        def trip(t, carry, br=br, dil=dil, first=first):
            scores(br, dil, 2 * t + 1, s_odd)
            absorb(br, dil, 2 * t, s_even, first)
            scores(br, dil, 2 * t + 2, s_even)
            absorb(br, dil, 2 * t + 1, s_odd, first)
            return carry

        scores(br, dil, 0, s_even)
        lax.fori_loop(0, n_groups // 2 - 1, trip, 0)
        scores(br, dil, n_groups - 1, s_odd)
        absorb(br, dil, n_groups - 2, s_even, first)
        absorb(br, dil, n_groups - 1, s_odd, first)

    o_ref[0] = (acc_s[...] / l_s[...]).astype(o_ref.dtype)


def _dilated_attention(proj, bias, *, q_blk0, k_blk0, v_blk0):
    b, s, _ = proj.shape
    n_pairs = DIL_HEADS // 2
    assert (s // (BLK * DIL_UNROLL)) % 2 == 0
    blk = lambda off: pl.BlockSpec((1, s, LANES), lambda bi, p: (bi, 0, off + p))
    state = [pltpu.VMEM((s, LANES), F32) for _ in range(4)]
    padded = [pltpu.VMEM((DIL_PAD + s, LANES), F32) for _ in range(2)]
    score_bufs = [pltpu.VMEM((DIL_UNROLL, 2 * BLK, 2 * BLK), F32) for _ in range(2)]
    return pl.pallas_call(
        functools.partial(_dilated_kernel, seq=s),
        grid=(b, n_pairs),
        in_specs=[blk(q_blk0), blk(k_blk0), blk(v_blk0),
                  pl.BlockSpec((len(DIL_BRANCHES), 1, 2, 2 * BLK, 2 * BLK), lambda bi, p: (0, p, 0, 0, 0))],
        out_specs=pl.BlockSpec((1, s, LANES), lambda bi, p: (bi, 0, p)),
        out_shape=jax.ShapeDtypeStruct((b, s, n_pairs * LANES), BF16),
        scratch_shapes=[state[0], padded[0], padded[1], state[1], state[2], state[3]] + score_bufs,
        compiler_params=_params("parallel", "parallel"),
        name="dilated_attention",
    )(proj, proj, proj, bias)


def _swa_kernel(q_ref, k_ref, v_ref, bias_ref, sink_ref, o_ref, *, tc):
    ci = pl.program_id(1)
    grp = SWA_Q_HEADS // SWA_KV_HEADS
    col = lax.broadcasted_iota(jnp.int32, (grp * BLK, 2 * BLK), 1)
    for bb in range(tc // BLK):
        own = pl.multiple_of(ci * tc + bb * BLK, BLK)
        prev = pl.multiple_of(jnp.maximum(own - BLK, 0), BLK)
        no_prev = jnp.where(own == 0, NEG, 0.0).astype(F32)
        pieces = []
        for kv in range(SWA_KV_HEADS):
            lo, hi = kv * HEAD_DIM, (kv + 1) * HEAD_DIM
            k2 = jnp.concatenate([k_ref[0, pl.ds(prev, BLK), lo:hi], k_ref[0, pl.ds(own, BLK), lo:hi]], axis=0)
            v2 = jnp.concatenate([v_ref[0, pl.ds(prev, BLK), lo:hi], v_ref[0, pl.ds(own, BLK), lo:hi]], axis=0)
            q4 = jnp.concatenate(
                [q_ref[0, bb * BLK:(bb + 1) * BLK, (kv * grp + g) * HEAD_DIM:(kv * grp + g + 1) * HEAD_DIM]
                 for g in range(grp)], axis=0)
            s = lax.dot_general(q4, k2, NT_DIMS, preferred_element_type=F32)
            s = s + bias_ref[kv] + jnp.where(col < BLK, no_prev, 0.0)
            sink = sink_ref[kv]
            m = jnp.maximum(jnp.max(s, axis=-1, keepdims=True), sink)
            p = jnp.exp(s - m)
            den = jnp.sum(p, axis=-1, keepdims=True) + jnp.exp(sink - m)
            o = _dot(p.astype(BF16), v2) / den
            pieces += [o[g * BLK:(g + 1) * BLK] for g in range(grp)]
        o_ref[0, bb * BLK:(bb + 1) * BLK, :] = jnp.concatenate(pieces, axis=-1).astype(o_ref.dtype)


def _swa_attention(proj, bias, sink_col, *, q_blk0, k_blk0, v_blk0, tc=512):
    b, s, _ = proj.shape
    qw = SWA_Q_HEADS * HEAD_DIM
    return pl.pallas_call(
        functools.partial(_swa_kernel, tc=tc),
        grid=(b, s // tc),
        in_specs=[
            pl.BlockSpec((1, tc, qw), lambda bi, ci: (bi, ci, q_blk0 * LANES // qw)),
            pl.BlockSpec((1, s, LANES), lambda bi, ci: (bi, 0, k_blk0)),
            pl.BlockSpec((1, s, LANES), lambda bi, ci: (bi, 0, v_blk0)),
            pl.BlockSpec(bias.shape, lambda bi, ci: (0, 0, 0)),
            pl.BlockSpec(sink_col.shape, lambda bi, ci: (0, 0, 0)),
        ],
        out_specs=pl.BlockSpec((1, tc, qw), lambda bi, ci: (bi, ci, 0)),
        out_shape=jax.ShapeDtypeStruct((b, s, qw), BF16),
        compiler_params=_params("parallel", "parallel"),
        name="swa_attention",
    )(proj, proj, proj, bias, sink_col)


def _rope128(x, a, bm, bp):
    return x * a + pltpu.roll(x, LANES - 16, axis=1) * bm + pltpu.roll(x, 16, axis=1) * bp


def _proj_odd_kernel(x_ref, g_ref, w_ref, qn_ref, kvn_ref, wuq_ref, wuk_ref, wuv_ref, a_ref, bm_ref, bp_ref,
                     q_out, k_out, v_out, swa_out):
    xb = _rms(x_ref[...], g_ref[...]).astype(BF16)
    w = w_ref
    c_q = _dot(xb, w[:, 0:MLA_Q_RANK])
    c_kv = _dot(xb, w[:, MLA_Q_RANK:MLA_Q_RANK + MLA_KV_RANK])
    o1 = MLA_Q_RANK + MLA_KV_RANK
    kpe = _dot(xb, w[:, o1:o1 + LANES])
    o2 = o1 + LANES
    swa_out[...] = _dot(xb, w[:, o2:]).astype(swa_out.dtype)
    a, bm, bp = a_ref[...], bm_ref[...], bp_ref[...]
    kpe = _rope128(kpe, a, bm, bp)
    cqn = _rms(c_q, qn_ref[...]).astype(BF16)
    ckvn = _rms(c_kv, kvn_ref[...]).astype(BF16)
    v_out[...] = _dot(ckvn, wuv_ref[...]).astype(v_out.dtype)
    for h in range(MLA_HEADS):
        sl = slice(h * LANES, (h + 1) * LANES)
        q_out[:, sl] = _rope128(_dot(cqn, wuq_ref[:, sl]), a, bm, bp).astype(q_out.dtype)
        k_out[:, sl] = (_dot(ckvn, wuk_ref[:, sl]) + kpe).astype(k_out.dtype)


def _proj_odd(h, g, w, qn, kvn, wuq, wuk, wuv, rope_a, rope_bm, rope_bp, seq, tm=512):
    n, d = h.shape
    n_in = w.shape[1]
    n_swa = n_in - (MLA_Q_RANK + MLA_KV_RANK + LANES)
    tiles_per_seq = seq // tm
    full = lambda shape: pl.BlockSpec(shape, lambda i: (0, 0))
    rope = pl.BlockSpec((tm, LANES), lambda i: (i % tiles_per_seq, 0))
    return pl.pallas_call(
        _proj_odd_kernel,
        grid=(n // tm,),
        in_specs=[
            pl.BlockSpec((tm, d), lambda i: (i, 0)), full((1, d)), full(w.shape),
            full(qn.shape), full(kvn.shape), full(wuq.shape), full(wuk.shape), full(wuv.shape),
            rope, rope, rope,
        ],
        out_specs=[
            pl.BlockSpec((tm, MLA_HEADS * LANES), lambda i: (i, 0)),
            pl.BlockSpec((tm, MLA_HEADS * LANES), lambda i: (i, 0)),
            pl.BlockSpec((tm, MLA_HEADS * MLA_V), lambda i: (i, 0)),
            pl.BlockSpec((tm, n_swa), lambda i: (i, 0)),
        ],
        out_shape=[
            jax.ShapeDtypeStruct((n, MLA_HEADS * LANES), BF16),
            jax.ShapeDtypeStruct((n, MLA_HEADS * LANES), BF16),
            jax.ShapeDtypeStruct((n, MLA_HEADS * MLA_V), BF16),
            jax.ShapeDtypeStruct((n, n_swa), BF16),
        ],
        compiler_params=_params("parallel"),
        name="proj_odd",
    )(h, g, w, qn, kvn, wuq, wuk, wuv, rope_a, rope_bm, rope_bp)


MOE_TILE = 1024
ROUTER_TILE = 512


def _out_router_kernel(h_ref, oa_ref, ob_ref, wo_ref, g_ref, wr_hi_ref, wr_lo_ref, br_ref,
                       h_out, hn_out, gates_out, counts_out, pos_col_out, pos_row_out, tri_ref, seen_ref):
    i = pl.program_id(0)
    tm = h_ref.shape[0]
    half = oa_ref.shape[1]
    h = h_ref[...] + _dot(oa_ref[...], wo_ref[0:half, :]) + _dot(ob_ref[...], wo_ref[half:, :])
    h_out[...] = h
    hn = _rms(h, g_ref[...])
    hn_out[...] = hn.astype(hn_out.dtype)
    z = _dot_hi(hn, wr_hi_ref[...], wr_lo_ref[...]) + br_ref[...]
    lane = lax.broadcasted_iota(jnp.int32, z.shape, 1)
    big = jnp.int32(LANES)
    is_grp = (lane >= N_EXPERTS) & (lane < N_EXPERTS + N_GROUPS)
    zg = jnp.where(is_grp, z, -jnp.inf)
    g_max = jnp.max(zg, axis=-1, keepdims=True)
    g_w = 1.0 / jnp.sum(jnp.exp(zg - g_max), axis=-1, keepdims=True)
    g_idx = jnp.min(jnp.where(zg == g_max, lane - N_EXPERTS, big), axis=-1, keepdims=True)
    in_grp = (lane < N_EXPERTS) & ((lane // EXPERTS_PER_GROUP) == g_idx)
    ze = jnp.where(in_grp, z, -jnp.inf)
    v1 = jnp.max(ze, axis=-1, keepdims=True)
    i1 = jnp.min(jnp.where(ze == v1, lane, big), axis=-1, keepdims=True)
    ze2 = jnp.where(lane == i1, -jnp.inf, ze)
    v2 = jnp.max(ze2, axis=-1, keepdims=True)
    i2 = jnp.min(jnp.where(ze2 == v2, lane, big), axis=-1, keepdims=True)
    e2 = jnp.exp(v2 - v1)
    w1 = g_w / (1.0 + e2)
    w2 = g_w * e2 / (1.0 + e2)
    for grp in range(N_GROUPS):
        first = grp * EXPERTS_PER_GROUP
        slot = jnp.where(lane == i1 - first, w1, 0.0) + jnp.where(lane == i2 - first, w2, 0.0)
        gates_out[:, grp * LANES:(grp + 1) * LANES] = jnp.where(lane < EXPERTS_PER_GROUP, slot, 0.0)

    @pl.when(i == 0)
    def _():
        ri = lax.broadcasted_iota(jnp.int32, (tm, tm), 0)
        ci = lax.broadcasted_iota(jnp.int32, (tm, tm), 1)
        tri_ref[...] = jnp.where(ci < ri, 1.0, 0.0).astype(BF16)

    @pl.when(i % (MOE_TILE // tm) == 0)
    def _():
        seen_ref[...] = jnp.zeros_like(seen_ref)

    routed = jnp.where((lane == g_idx) & (lane < N_GROUPS), 1.0, 0.0)
    pos = jnp.where(routed > 0.0, _dot(tri_ref[...], routed.astype(BF16)) + seen_ref[...], -1.0)
    pos_col_out[...] = pos
    pos_row_out[0] = jnp.transpose(pos)[0:8, :]
    here = jnp.sum(routed, axis=0, keepdims=True)
    seen_ref[...] += here
    counts_out[0] = jnp.broadcast_to(here, counts_out.shape[1:])


def _out_router(h, oa, ob, wo, g, wr_hi, wr_lo, br):
    n, d = h.shape
    tm = ROUTER_TILE
    per_tile = MOE_TILE // tm
    half = oa.shape[1]
    full = lambda shape: pl.BlockSpec(shape, lambda i: (0, 0))
    tile = lambda w: pl.BlockSpec((tm, w), lambda i: (i, 0))
    return pl.pallas_call(
        _out_router_kernel,
        grid=(n // tm,),
        in_specs=[tile(d), tile(half), tile(half), full(wo.shape), full((1, d)),
                  full(wr_hi.shape), full(wr_lo.shape), full((1, LANES))],
        out_specs=[tile(d), tile(d), tile(N_GROUPS * LANES), pl.BlockSpec((1, 8, LANES), lambda i: (i, 0, 0)),
                   tile(LANES), pl.BlockSpec((1, 8, tm), lambda i: (i // per_tile, 0, i % per_tile))],
        out_shape=[
            jax.ShapeDtypeStruct((n, d), F32),
            jax.ShapeDtypeStruct((n, d), BF16),
            jax.ShapeDtypeStruct((n, N_GROUPS * LANES), F32),
            jax.ShapeDtypeStruct((n // tm, 8, LANES), F32),
            jax.ShapeDtypeStruct((n, LANES), F32),
            jax.ShapeDtypeStruct((n // MOE_TILE, 8, MOE_TILE), F32),
        ],
        scratch_shapes=[pltpu.VMEM((tm, tm), BF16), pltpu.VMEM((1, LANES), F32)],
        compiler_params=_params("arbitrary"),
        name="out_router",
    )(h, oa, ob, wo, g, wr_hi, wr_lo, br)


MOE_CHUNK = 256
MOE_ROWS = 1024
MOE_ALIGN = 16
MOE_EXPERTS_PER_STEP = 2


def _moe_sorted_tiles(n):
    n_seg = (n // MOE_TILE) * N_GROUPS
    rows = n + n_seg * (MOE_ALIGN - 1) + N_GROUPS * (MOE_CHUNK + MOE_ROWS - 1)
    return -(-rows // MOE_ROWS)


def _moe_tables(counts, n):
    n_rt = _moe_sorted_tiles(n)
    seg_len = (counts + (MOE_ALIGN - 1)) // MOE_ALIGN * MOE_ALIGN
    group_len = seg_len.sum(axis=0)
    group_span = (group_len + MOE_CHUNK + MOE_ROWS - 1) // MOE_ROWS * MOE_ROWS
    group_end = jnp.cumsum(group_span)
    group_start = group_end - group_span
    seg_off = group_start[None, :] + jnp.cumsum(seg_len, axis=0) - seg_len
    rt_start = jnp.arange(n_rt, dtype=jnp.int32) * MOE_ROWS
    rt_group = jnp.minimum(jnp.sum(rt_start[:, None] >= group_end[None, :], axis=1), N_GROUPS - 1)
    rt_valid = jnp.clip(group_start[rt_group] + group_len[rt_group] - rt_start, 0, MOE_ROWS)
    n_used = (group_end[-1] // MOE_ROWS).reshape(1)
    i32 = lambda a: a.astype(jnp.int32)
    return i32(seg_off.reshape(-1)), i32(rt_group), i32(rt_valid), i32(n_used)


def _moe_chunks(count, body):
    n_full = count // MOE_CHUNK
    rest = count - n_full * MOE_CHUNK

    def full_chunk(k, carry):
        body(k * MOE_CHUNK, MOE_CHUNK)
        return carry

    lax.fori_loop(0, n_full, full_chunk, 0)

    @pl.when(rest > MOE_CHUNK // 2)
    def _():
        body(n_full * MOE_CHUNK, MOE_CHUNK)

    @pl.when((rest > 0) & (rest <= MOE_CHUNK // 2))
    def _():
        body(n_full * MOE_CHUNK, MOE_CHUNK // 2)


def _moe_pack_kernel(counts_ref, off_ref, hn_ref, gates_ref, pos_ref, xs_in, xs_ref, xbuf, sem, issued_ref):
    del xs_in
    i = pl.program_id(0)
    d = hn_ref.shape[1]

    @pl.when(i == 0)
    def _():
        issued_ref[0] = 0

    def write(slot):
        return pltpu.make_async_copy(xbuf.at[slot], xs_ref.at[pl.ds(0, MOE_CHUNK)], sem.at[slot])

    lane = lax.broadcasted_iota(jnp.int32, (hn_ref.shape[0], LANES), 1)
    for grp in range(N_GROUPS):
        seg = i * N_GROUPS + grp
        gates = gates_ref[:, grp * LANES:(grp + 1) * LANES]
        g_hi = gates.astype(BF16).astype(F32)
        g_lo = pltpu.roll(gates - g_hi, EXPERTS_PER_GROUP, axis=1)
        gate_cols = jnp.where(lane < EXPERTS_PER_GROUP, g_hi, jnp.where(lane < 2 * EXPERTS_PER_GROUP, g_lo, 0.0))
        gate_cols = gate_cols.astype(BF16)
        pos_row = pos_ref[0, grp:grp + 1, :]

        def chunk(first_pos, rows, seg=seg, gate_cols=gate_cols, pos_row=pos_row):
            n_done = issued_ref[0]
            slot = n_done % 2
            slot_pos = lax.broadcasted_iota(jnp.int32, (rows, 1), 0).astype(F32) + first_pos.astype(F32)
            gather = jnp.where(pos_row == slot_pos, 1.0, 0.0).astype(BF16)
            xbuf[slot, 0:rows, 0:d] = _dot(gather, hn_ref[...]).astype(BF16)
            xbuf[slot, 0:rows, d:] = _dot(gather, gate_cols).astype(BF16)
            if rows < MOE_CHUNK:
                xbuf[slot, rows:, :] = jnp.zeros((MOE_CHUNK - rows, d + LANES), BF16)

            @pl.when(n_done > 0)
            def _():
                write(1 - slot).wait()

            dst = pl.multiple_of(off_ref[seg] + first_pos, MOE_ALIGN)
            pltpu.make_async_copy(xbuf.at[slot], xs_ref.at[pl.ds(dst, MOE_CHUNK)], sem.at[slot]).start()
            issued_ref[0] = n_done + 1

        _moe_chunks(counts_ref[seg], chunk)

    @pl.when((i == pl.num_programs(0) - 1) & (issued_ref[0] > 0))
    def _():
        write((issued_ref[0] - 1) % 2).wait()


def _moe_pack(counts, seg_off, hn, gates, pos_row, n_rt):
    n, d = hn.shape
    rows = n_rt * MOE_ROWS
    grid_spec = pltpu.PrefetchScalarGridSpec(
        num_scalar_prefetch=2,
        grid=(n // MOE_TILE,),
        in_specs=[
            pl.BlockSpec((MOE_TILE, d), lambda i, c, o: (i, 0)),
            pl.BlockSpec((MOE_TILE, N_GROUPS * LANES), lambda i, c, o: (i, 0)),
            pl.BlockSpec((1, 8, MOE_TILE), lambda i, c, o: (i, 0, 0)),
            pl.BlockSpec(memory_space=pl.ANY),
        ],
        out_specs=pl.BlockSpec(memory_space=pl.ANY),
        scratch_shapes=[pltpu.VMEM((2, MOE_CHUNK, d + LANES), BF16), pltpu.SemaphoreType.DMA((2,)),
                        pltpu.SMEM((1,), jnp.int32)],
    )
    return pl.pallas_call(
        _moe_pack_kernel,
        grid_spec=grid_spec,
        out_shape=jax.ShapeDtypeStruct((rows, d + LANES), BF16),
        input_output_aliases={5: 0},
        compiler_params=_params("arbitrary"),
        name="moe_pack",
    )(counts, seg_off, hn, gates, pos_row, jnp.zeros((rows, d + LANES), BF16))


def _moe_experts_kernel(grp_ref, valid_ref, used_ref, xs_ref, wg_ref, wu_ref, wd_ref, ys_ref, acc_ref):
    rt = pl.program_id(0)
    j = pl.program_id(1)
    d = ys_ref.shape[1]

    @pl.when(rt < used_ref[0])
    def _():
        @pl.when(j == 0)
        def _():
            acc_ref[...] = jnp.zeros_like(acc_ref)

        x = xs_ref[:, 0:d]
        gate_cols = xs_ref[:, d:].astype(F32)
        lane = lax.broadcasted_iota(jnp.int32, gate_cols.shape, 1)
        for e in range(MOE_EXPERTS_PER_STEP):
            idx = j * MOE_EXPERTS_PER_STEP + e
            mine = (lane == idx) | (lane == idx + EXPERTS_PER_GROUP)
            gate = jnp.sum(jnp.where(mine, gate_cols, 0.0), axis=-1, keepdims=True)
            a = _dot(x, wg_ref[0, 0, e].astype(BF16))
            u = _dot(x, wu_ref[0, 0, e].astype(BF16))
            act = (a * jax.nn.sigmoid(a)) * u * gate
            acc_ref[...] += _dot(act.astype(BF16), wd_ref[0, 0, e].astype(BF16))

        @pl.when(j == pl.num_programs(1) - 1)
        def _():
            row = lax.broadcasted_iota(jnp.int32, (acc_ref.shape[0], 1), 0)
            ys_ref[...] = jnp.where(row < valid_ref[rt], acc_ref[...], 0.0).astype(ys_ref.dtype)

    @pl.when((rt >= used_ref[0]) & (j == pl.num_programs(1) - 1))
    def _():
        ys_ref[...] = jnp.zeros_like(ys_ref)


def _moe_experts(rt_group, rt_valid, n_used, xs, w_gate, w_up, w_down, layer):
    rows = xs.shape[0]
    d = w_gate.shape[-2]
    n_rt = rows // MOE_ROWS
    eps = MOE_EXPERTS_PER_STEP
    n_steps = EXPERTS_PER_GROUP // eps

    def tile_idx(rt, j, grp, valid, used):
        return jnp.minimum(rt, used[0] - 1)

    def w_idx(rt, j, grp, valid, used):
        live = rt < used[0]
        return (layer, grp[tile_idx(rt, j, grp, valid, used)], jnp.where(live, j, n_steps - 1), 0, 0)

    grid_spec = pltpu.PrefetchScalarGridSpec(
        num_scalar_prefetch=3,
        grid=(n_rt, n_steps),
        in_specs=[
            pl.BlockSpec((MOE_ROWS, d + LANES), lambda rt, j, grp, valid, used: (tile_idx(rt, j, grp, valid, used), 0)),
            pl.BlockSpec((1, 1, eps, d, D_EXPERT), w_idx),
            pl.BlockSpec((1, 1, eps, d, D_EXPERT), w_idx),
            pl.BlockSpec((1, 1, eps, D_EXPERT, d), w_idx),
        ],
        out_specs=pl.BlockSpec((MOE_ROWS, d), lambda rt, j, grp, valid, used: (rt, 0)),
        scratch_shapes=[pltpu.VMEM((MOE_ROWS, d), F32)],
    )
    return pl.pallas_call(
        _moe_experts_kernel,
        grid_spec=grid_spec,
        out_shape=jax.ShapeDtypeStruct((rows, d), BF16),
        compiler_params=_params("arbitrary", "arbitrary"),
        name="moe_experts",
    )(rt_group, rt_valid, n_used, xs, w_gate, w_up, w_down)


def _moe_combine_kernel(counts_ref, off_ref, pos_ref, h_ref, fg_ref, ys_ref, o_ref, acc_ref, ybuf, ymore, sem,
                        *, final_norm):
    i = pl.program_id(0)

    def read(grp, first_pos):
        src = pl.multiple_of(off_ref[i * N_GROUPS + grp] + first_pos, MOE_ALIGN)
        return pltpu.make_async_copy(ys_ref.at[pl.ds(src, MOE_CHUNK)], ybuf.at[grp], sem.at[grp])

    for grp in range(N_GROUPS):
        @pl.when(counts_ref[i * N_GROUPS + grp] > 0)
        def _(grp=grp):
            read(grp, 0).start()

    acc_ref[...] = h_ref[...]
    slot_pos = lax.broadcasted_iota(jnp.int32, (1, MOE_CHUNK), 1).astype(F32)
    for grp in range(N_GROUPS):
        count = counts_ref[i * N_GROUPS + grp]
        pos_col = pos_ref[:, grp:grp + 1]

        @pl.when(count > 0)
        def _(grp=grp, pos_col=pos_col):
            read(grp, 0).wait()
            scatter = jnp.where(pos_col == slot_pos, 1.0, 0.0).astype(BF16)
            acc_ref[...] += _dot(scatter, ybuf[grp])

        def more(k, carry, grp=grp, pos_col=pos_col):
            src = pl.multiple_of(off_ref[i * N_GROUPS + grp] + k * MOE_CHUNK, MOE_ALIGN)
            pltpu.sync_copy(ys_ref.at[pl.ds(src, MOE_CHUNK)], ymore)
            scatter = jnp.where(pos_col == slot_pos + (k * MOE_CHUNK).astype(F32), 1.0, 0.0).astype(BF16)
            acc_ref[...] += _dot(scatter, ymore[...])
            return carry

        lax.fori_loop(1, (count + (MOE_CHUNK - 1)) // MOE_CHUNK, more, 0)

    h = acc_ref[...]
    o_ref[...] = _rms(h, fg_ref[...]) if final_norm else h


def _moe_combine(counts, seg_off, pos_col, h, fg, ys, final_norm):
    n, d = h.shape
    grid_spec = pltpu.PrefetchScalarGridSpec(
        num_scalar_prefetch=2,
        grid=(n // MOE_TILE,),
        in_specs=[
            pl.BlockSpec((MOE_TILE, LANES), lambda i, c, o: (i, 0)),
            pl.BlockSpec((MOE_TILE, d), lambda i, c, o: (i, 0)),
            pl.BlockSpec((1, d), lambda i, c, o: (0, 0)),
            pl.BlockSpec(memory_space=pl.ANY),
        ],
        out_specs=pl.BlockSpec((MOE_TILE, d), lambda i, c, o: (i, 0)),
        scratch_shapes=[pltpu.VMEM((MOE_TILE, d), F32), pltpu.VMEM((N_GROUPS, MOE_CHUNK, d), BF16),
                        pltpu.VMEM((MOE_CHUNK, d), BF16), pltpu.SemaphoreType.DMA((N_GROUPS,))],
    )
    return pl.pallas_call(
        functools.partial(_moe_combine_kernel, final_norm=final_norm),
        grid_spec=grid_spec,
        out_shape=jax.ShapeDtypeStruct((n, d), F32),
        compiler_params=_params("arbitrary"),
        name="moe_combine",
    )(counts, seg_off, pos_col, h, fg, ys)


def _alibi_slopes(n):
    return jnp.exp2(-8.0 * jnp.arange(1, n + 1, dtype=F32) / n)


def _band_bias(slopes, max_steps, step_dist):
    steps = (jnp.arange(BLK)[:, None] + BLK) - jnp.arange(2 * BLK)[None, :]
    in_band = (steps >= 0) & (steps <= max_steps)
    dist = (steps * step_dist).astype(F32)
    return jnp.where(in_band[None], -slopes.astype(F32)[:, None, None] * dist[None], NEG)


def _rope_tables(s):
    inv = ROPE_THETA ** (-jnp.arange(0, MLA_ROPE, 2, dtype=F32) / MLA_ROPE)
    ang = jnp.arange(s, dtype=F32)[:, None] * inv[None, :]
    cos, sin = jnp.cos(ang), jnp.sin(ang)
    half = MLA_ROPE // 2
    zeros_tail = jnp.zeros((s, LANES - MLA_NOPE - MLA_ROPE), F32)
    a = jnp.concatenate([jnp.ones((s, MLA_NOPE), F32), cos, cos, zeros_tail], axis=1)
    zeros_nope = jnp.zeros((s, MLA_NOPE), F32)
    zeros_half = jnp.zeros((s, half), F32)
    bm = jnp.concatenate([zeros_nope, -sin, zeros_half, zeros_tail], axis=1)
    bp = jnp.concatenate([zeros_nope, zeros_half, sin, zeros_tail], axis=1)
    return a, bm, bp


def _pad_cols(w, width):
    return jnp.pad(w, ((0, 0), (0, width - w.shape[1])))


def _router_weights(w_group, b_group, w_router, b_router):
    w = _pad_cols(jnp.concatenate([w_router, w_group], axis=1), LANES)
    b = _pad_cols(jnp.concatenate([b_router, b_group])[None, :], LANES)
    hi, lo = _hi_lo(w)
    return hi, lo, b


def kernel(x, attn_norm, ffn_norm, final_norm, e_w_in, e_b_f, e_w_out, o_w_in, o_q_norm, o_kv_norm, o_w_uq,
           o_w_ukv, o_sinks, o_w_out, moe_w_group, moe_b_group, moe_w_router, moe_b_router, moe_w_gate,
           moe_w_up, moe_w_down):
    b, s, d = x.shape
    n = b * s
    depth = attn_norm.shape[0]
    assert s % (BLK * DIL_BRANCHES[-1][1]) == 0 and d == D_MODEL
    h = x.reshape(n, d)

    dil_slopes = _alibi_slopes(DIL_HEADS)
    dil_bias = jnp.stack([_band_bias(dil_slopes, w // dl, dl) for (w, dl) in DIL_BRANCHES]) * LOG2E
    own_half = jnp.arange(2 * BLK) >= BLK
    dil_bias = jnp.stack([dil_bias, jnp.where(own_half, dil_bias, NEG)], axis=2)
    dil_bias = dil_bias.reshape(len(DIL_BRANCHES), DIL_HEADS // 2, 2, 2, BLK, 2 * BLK)
    dil_bias = dil_bias.transpose(0, 1, 3, 2, 4, 5).reshape(len(DIL_BRANCHES), DIL_HEADS // 2, 2, 2 * BLK, 2 * BLK)
    swa_bias = _band_bias(_alibi_slopes(SWA_Q_HEADS), SWA_WINDOW - 1, 1)
    swa_bias = swa_bias.reshape(SWA_KV_HEADS, (SWA_Q_HEADS // SWA_KV_HEADS) * BLK, 2 * BLK)
    rope_a, rope_bm, rope_bp = _rope_tables(s)

    for layer in range(depth):
        i = layer // 2
        g_attn = attn_norm[layer][None, :]
        if layer % 2 == 0:
            w_in = e_w_in[i]
            hq = FOX_HEADS * HEAD_DIM
            scale = HEAD_DIM ** -0.5
            cols = [w_in[:, 0:hq] * (scale * LOG2E), w_in[:, hq:2 * hq], w_in[:, 2 * hq:3 * hq]]
            o = 3 * hq + FOX_HEADS
            cols += [w_in[:, o:o + hq] * (scale * LOG2E), w_in[:, o + hq:o + 2 * hq], w_in[:, o + 2 * hq:o + 3 * hq]]
            w_main = jnp.concatenate(cols, axis=1).astype(BF16)
            wf_hi, wf_lo = _hi_lo(_pad_cols(w_in[:, 3 * hq:o], LANES))
            b_f = _pad_cols(e_b_f[i][None, :], LANES)
            proj, c = _proj_even(h, g_attn, w_main, wf_hi, wf_lo, b_f, s)
            proj = proj.reshape(b, s, -1)
            o_a = _causal_attention(proj, proj, proj, c.reshape(b, s, LANES), q_blk0=0, k_blk0=4, v_blk0=8,
                                    n_pairs=FOX_HEADS // 2)
            o_b = _dilated_attention(proj, dil_bias, q_blk0=12, k_blk0=16, v_blk0=20)
            w_out = e_w_out[i].astype(BF16)
        else:
            w_in = o_w_in[i]
            o1 = MLA_Q_RANK + MLA_KV_RANK
            o2 = o1 + MLA_ROPE
            sq = SWA_Q_HEADS * HEAD_DIM
            kpe_cols = jnp.pad(w_in[:, o1:o2], ((0, 0), (MLA_NOPE, LANES - MLA_NOPE - MLA_ROPE)))
            w_main = jnp.concatenate(
                [w_in[:, :o1], kpe_cols, w_in[:, o2:o2 + sq] * (HEAD_DIM ** -0.5), w_in[:, o2 + sq:]],
                axis=1).astype(BF16)
            dq = MLA_NOPE + MLA_ROPE
            wuq = o_w_uq[i].reshape(MLA_Q_RANK, MLA_HEADS, dq) * (dq ** -0.5 * LOG2E)
            wuq = jnp.pad(wuq, ((0, 0), (0, 0), (0, LANES - dq))).reshape(MLA_Q_RANK, MLA_HEADS * LANES)
            wukv = o_w_ukv[i].reshape(MLA_KV_RANK, MLA_HEADS, MLA_NOPE + MLA_V)
            wuk = jnp.pad(wukv[:, :, :MLA_NOPE], ((0, 0), (0, 0), (0, LANES - MLA_NOPE)))
            wuk = wuk.reshape(MLA_KV_RANK, MLA_HEADS * LANES)
            wuv = wukv[:, :, MLA_NOPE:].reshape(MLA_KV_RANK, MLA_HEADS * MLA_V)
            q_full, k_full, v_mla, swa = _proj_odd(
                h, g_attn, w_main, o_q_norm[i][None, :], o_kv_norm[i][None, :], wuq.astype(BF16),
                wuk.astype(BF16), wuv.astype(BF16), rope_a, rope_bm, rope_bp, s)
            o_a = _causal_attention(q_full.reshape(b, s, -1), k_full.reshape(b, s, -1), v_mla.reshape(b, s, -1),
                                    None, q_blk0=0, k_blk0=0, v_blk0=0, n_pairs=MLA_HEADS // 2)
            grp = SWA_Q_HEADS // SWA_KV_HEADS
            sink_col = jnp.repeat(o_sinks[i].reshape(SWA_KV_HEADS, grp), BLK, axis=1)[:, :, None]
            o_b = _swa_attention(swa.reshape(b, s, -1), swa_bias, sink_col, q_blk0=0, k_blk0=4, v_blk0=5)
            w_out = o_w_out[i].astype(BF16)

        wr_hi, wr_lo, b_r = _router_weights(moe_w_group[layer], moe_b_group[layer], moe_w_router[layer],
                                            moe_b_router[layer])
        h, hn, gates, counts, pos_col, pos_row = _out_router(
            h, o_a.reshape(n, -1), o_b.reshape(n, -1), w_out, ffn_norm[layer][None, :], wr_hi, wr_lo, b_r)
        counts = counts[:, 0, :N_GROUPS].reshape(n // MOE_TILE, MOE_TILE // ROUTER_TILE, N_GROUPS).sum(axis=1)
        counts = counts.astype(jnp.int32)
        seg_off, rt_group, rt_valid, n_used = _moe_tables(counts, n)
        counts = counts.reshape(-1)
        xs = _moe_pack(counts, seg_off, hn, gates, pos_row, _moe_sorted_tiles(n))
        ys = _moe_experts(rt_group, rt_valid, n_used, xs, moe_w_gate, moe_w_up, moe_w_down, layer)
        h = _moe_combine(counts, seg_off, pos_col, h, final_norm[None, :], ys, final_norm=layer == depth - 1)
    return h.reshape(b, s, d)
```

```python
import functools

import jax
import jax.numpy as jnp
from jax import lax
from jax.experimental import pallas as pl
from jax.experimental.pallas import tpu as pltpu

F32 = jnp.float32
BF16 = jnp.bfloat16

D_MODEL = 1024
HEAD_DIM = 64
BLK = 128
NEG = -1e30
RMS_EPS = 1e-6
FOX_HEADS = 8
DIL_HEADS = 8
DIL_BRANCHES = ((128, 1), (512, 4), (2048, 16))
MLA_HEADS = 8
MLA_Q_RANK = 384
MLA_KV_RANK = 256
MLA_NOPE = 64
MLA_ROPE = 32
MLA_V = 64
ROPE_THETA = 10000.0
SWA_Q_HEADS = 8
SWA_KV_HEADS = 2
SWA_WINDOW = 128
N_GROUPS = 4
EXPERTS_PER_GROUP = 8
N_EXPERTS = N_GROUPS * EXPERTS_PER_GROUP
D_EXPERT = 256

LANES = 128
VMEM_LIMIT = 56 * 1024 * 1024

NT_DIMS = (((1,), (1,)), ((), ()))
LOG2E = 1.4426950408889634


def _params(*sem):
    return pltpu.CompilerParams(dimension_semantics=sem, vmem_limit_bytes=VMEM_LIMIT)


def _dot(a, b):
    return jnp.dot(a, b, preferred_element_type=F32)


def _rms(x, g):
    return x * lax.rsqrt(jnp.mean(x * x, axis=-1, keepdims=True) + RMS_EPS) * g


def _hi_lo(w):
    hi = w.astype(BF16)
    return hi, (w - hi.astype(F32)).astype(BF16)


def _dot_hi(x, w_hi, w_lo):
    x_hi = x.astype(BF16)
    x_lo = (x - x_hi.astype(F32)).astype(BF16)
    return _dot(x_hi, w_hi) + (_dot(x_hi, w_lo) + _dot(x_lo, w_hi))


def _split3(x):
    x1 = x.astype(BF16)
    r1 = x - x1.astype(F32)
    x2 = r1.astype(BF16)
    x3 = (r1 - x2.astype(F32)).astype(BF16)
    return x1, x2, x3


def _proj_even_kernel(x_ref, g_ref, w_ref, wf_hi_ref, wf_lo_ref, bf_ref, out_ref, c_ref, carry_ref, *, tiles_per_seq):
    xn = _rms(x_ref[...], g_ref[...])
    xb = xn.astype(BF16)
    n_out = out_ref.shape[1]
    for c in range(0, n_out, 512):
        out_ref[:, c:c + 512] = _dot(xb, w_ref[:, c:c + 512]).astype(out_ref.dtype)
    z = _dot_hi(xn, wf_hi_ref[...], wf_lo_ref[...]) + bf_ref[...]
    logf = jnp.minimum(z, 0.0) - jnp.log1p(jnp.exp(-jnp.abs(z)))

    @pl.when(pl.program_id(0) % tiles_per_seq == 0)
    def _():
        carry_ref[...] = jnp.zeros_like(carry_ref)

    tm = logf.shape[0]
    ri = lax.broadcasted_iota(jnp.int32, (tm, tm), 0)
    ci = lax.broadcasted_iota(jnp.int32, (tm, tm), 1)
    lower = jnp.where(ci <= ri, 1.0, 0.0).astype(BF16)
    l1, l2, l3 = _split3(logf)
    c = (_dot(lower, l1) + (_dot(lower, l2) + _dot(lower, l3))) + carry_ref[...]
    c_ref[...] = c
    carry_ref[...] = c[tm - 1:tm, :]


def _proj_even(h, g, w, wf_hi, wf_lo, bf, seq, tm=512):
    n, d = h.shape
    n_out = w.shape[1]
    return pl.pallas_call(
        functools.partial(_proj_even_kernel, tiles_per_seq=seq // tm),
        grid=(n // tm,),
        in_specs=[
            pl.BlockSpec((tm, d), lambda i: (i, 0)),
            pl.BlockSpec((1, d), lambda i: (0, 0)),
            pl.BlockSpec((d, n_out), lambda i: (0, 0)),
            pl.BlockSpec((d, LANES), lambda i: (0, 0)),
            pl.BlockSpec((d, LANES), lambda i: (0, 0)),
            pl.BlockSpec((1, LANES), lambda i: (0, 0)),
        ],
        out_specs=[
            pl.BlockSpec((tm, n_out), lambda i: (i, 0)),
            pl.BlockSpec((tm, LANES), lambda i: (i, 0)),
        ],
        out_shape=[
            jax.ShapeDtypeStruct((n, n_out), BF16),
            jax.ShapeDtypeStruct((n, LANES), F32),
        ],
        scratch_shapes=[pltpu.VMEM((1, LANES), F32)],
        compiler_params=_params("arbitrary"),
        name="proj_even",
    )(h, g, w, wf_hi, wf_lo, bf)


def _own_lanes(lane, h):
    return lane < HEAD_DIM if h == 0 else lane >= HEAD_DIM


def _causal_kernel(*refs, fox, tq, tk):
    if fox:
        q_ref, k_ref, v_ref, c_ref, o_ref, vaug_t, s_even, s_odd, kaug = refs
    else:
        q_ref, k_ref, v_ref, o_ref, vaug_t, s_even, s_odd = refs
    pair = pl.program_id(1)
    qi = pl.program_id(2)
    seq = v_ref.shape[1]

    @pl.when(qi == 0)
    def _():
        lane = lax.broadcasted_iota(jnp.int32, (tk, LANES), 1)
        feat = lax.broadcasted_iota(jnp.int32, (LANES, tk), 0)

        def fill(t, carry):
            rows = pl.ds(pl.multiple_of(t * tk, tk), tk)
            vt = jnp.transpose(v_ref[0, rows, :].astype(F32))
            if fox:
                kp = k_ref[0, rows, :].astype(F32)
                c = c_ref[0, rows, :]
            for h in range(2):
                vaug_t[h, t] = jnp.where(_own_lanes(feat, h), vt, 1.0).astype(BF16)
                if fox:
                    ch = jnp.sum(jnp.where(lane == 2 * pair + h, c, 0.0), axis=-1, keepdims=True)
                    c1, c2, c3 = _split3(ch * (-LOG2E))
                    base = HEAD_DIM if h == 0 else 0
                    extra = jnp.where(lane == base, c1.astype(F32),
                                      jnp.where(lane == base + 1, c2.astype(F32),
                                                jnp.where(lane == base + 2, c3.astype(F32), 0.0)))
                    kaug[h, rows, :] = jnp.where(_own_lanes(lane, h), kp, extra).astype(BF16)
            return carry

        lax.fori_loop(0, seq // tk, fill, 0)

    lane_q = lax.broadcasted_iota(jnp.int32, (tq, LANES), 1)
    qs = []
    for h in range(2):
        if fox:
            base = HEAD_DIM if h == 0 else 0
            ones = jnp.where((lane_q >= base) & (lane_q < base + 3), 1.0, 0.0)
            qs.append(jnp.where(_own_lanes(lane_q, h), q_ref[0].astype(F32), ones).astype(BF16))
        else:
            qs.append(q_ref[0, :, h * LANES:(h + 1) * LANES])

    key = lax.broadcasted_iota(jnp.int32, (tk, tq), 0)
    qry = lax.broadcasted_iota(jnp.int32, (tk, tq), 1)

    def scores(j, buf):
        start = pl.multiple_of(j * tk, tk)
        for h in range(2):
            if fox:
                kj = kaug[h, pl.ds(start, tk), :]
            else:
                kj = k_ref[0, pl.ds(start, tk), h * LANES:(h + 1) * LANES]
            buf[h] = lax.dot_general(kj, qs[h], NT_DIMS, preferred_element_type=F32)

    def absorb(j, buf, carry, masked):
        new = []
        for h in range(2):
            m, acc = carry[h]
            s = buf[h]
            if masked:
                s = jnp.where(key <= qry, s, NEG)
            m_new = jnp.maximum(m, jnp.max(s, axis=0, keepdims=True))
            p = jnp.exp2(s - m_new)
            acc = jnp.exp2(m - m_new) * acc + _dot(vaug_t[h, j], p.astype(BF16))
            new.append((m_new, acc))
        return tuple(new)

    def finish(carry):
        (_, acc0), (_, acc1) = carry
        feat_q = lax.broadcasted_iota(jnp.int32, (LANES, tq), 0)
        out_t = jnp.where(feat_q < HEAD_DIM, acc0 / acc0[HEAD_DIM:HEAD_DIM + 1, :], acc1 / acc1[0:1, :])
        o_ref[0] = jnp.transpose(out_t).astype(o_ref.dtype)

    def pair_step(t, carry):
        scores(2 * t + 1, s_odd)
        carry = absorb(2 * t, s_even, carry, False)
        scores(2 * t + 2, s_even)
        return absorb(2 * t + 1, s_odd, carry, False)

    init = tuple((jnp.full((1, tq), NEG, F32), jnp.zeros((LANES, tq), F32)) for _ in range(2))
    scores(0, s_even)
    carry = lax.fori_loop(0, qi // 2, pair_step, init)

    @pl.when(qi % 2 == 0)
    def _():
        finish(absorb(qi, s_even, carry, True))

    @pl.when(qi % 2 == 1)
    def _():
        scores(qi, s_odd)
        finish(absorb(qi, s_odd, absorb(qi - 1, s_even, carry, False), True))


def _causal_attention(q_arr, k_arr, v_arr, c_arr, *, q_blk0, k_blk0, v_blk0, n_pairs, tq=512):
    b, s, _ = q_arr.shape
    fox = c_arr is not None
    qk_w = LANES if fox else 2 * LANES
    in_specs = [
        pl.BlockSpec((1, tq, qk_w), lambda bi, p, qi: (bi, qi, q_blk0 + p)),
        pl.BlockSpec((1, s, qk_w), lambda bi, p, qi: (bi, 0, k_blk0 + p)),
        pl.BlockSpec((1, s, LANES), lambda bi, p, qi: (bi, 0, v_blk0 + p)),
    ]
    args = [q_arr, k_arr, v_arr]
    scratch = [pltpu.VMEM((2, s // tq, LANES, tq), BF16), pltpu.VMEM((2, tq, tq), F32), pltpu.VMEM((2, tq, tq), F32)]
    if fox:
        in_specs.append(pl.BlockSpec((1, s, LANES), lambda bi, p, qi: (bi, 0, 0)))
        args.append(c_arr)
        scratch.append(pltpu.VMEM((2, s, LANES), BF16))
    return pl.pallas_call(
        functools.partial(_causal_kernel, fox=fox, tq=tq, tk=tq),
        grid=(b, n_pairs, s // tq),
        in_specs=in_specs,
        out_specs=pl.BlockSpec((1, tq, LANES), lambda bi, p, qi: (bi, qi, p)),
        out_shape=jax.ShapeDtypeStruct((b, s, n_pairs * LANES), BF16),
        scratch_shapes=scratch,
        compiler_params=_params("parallel", "parallel", "arbitrary"),
        name="causal_attention",
    )(*args)


DIL_PAD = BLK * max(d for _, d in DIL_BRANCHES)
DIL_UNROLL = 4


def _dilated_kernel(q_ref, k_ref, v_ref, bias_ref, o_ref, qf, kf, vf, acc_s, m_s, l_s, s_even, s_odd, *, seq):
    qf[...] = q_ref[0].astype(F32)
    kf[0:DIL_PAD, :] = jnp.zeros((DIL_PAD, LANES), F32)
    vf[0:DIL_PAD, :] = jnp.zeros((DIL_PAD, LANES), F32)
    kf[DIL_PAD:, :] = k_ref[0].astype(F32)
    vf[DIL_PAD:, :] = v_ref[0].astype(F32)
    head0 = lax.broadcasted_iota(jnp.int32, (BLK, LANES), 1) < HEAD_DIM
    head0_k = lax.broadcasted_iota(jnp.int32, (2 * BLK, LANES), 1) < HEAD_DIM
    ones0 = jnp.where(head0_k, 1.0, 0.0).astype(BF16)
    ones1 = jnp.where(head0_k, 0.0, 1.0).astype(BF16)

    def place(dil, u):
        bi = u // dil
        q_start = (u % dil) + (dil * BLK) * bi
        return bi, q_start, q_start + (DIL_PAD - dil * BLK)

    def scores(br, dil, g, buf):
        for i in range(DIL_UNROLL):
            bi, q_start, k_start = place(dil, g * DIL_UNROLL + i)
            q2 = qf[pl.ds(q_start, BLK, stride=dil), :]
            q_st = jnp.concatenate([jnp.where(head0, q2, 0.0), jnp.where(head0, 0.0, q2)], axis=0).astype(BF16)
            kb = kf[pl.ds(k_start, 2 * BLK, stride=dil), :].astype(BF16)
            s = lax.dot_general(q_st, kb, NT_DIMS, preferred_element_type=F32)
            buf[i] = s + bias_ref[br, 0, jnp.where(bi == 0, 1, 0)]

    def absorb(br, dil, g, buf, first):
        for i in range(DIL_UNROLL):
            _, q_start, k_start = place(dil, g * DIL_UNROLL + i)
            s = buf[i]
            m = jnp.max(s, axis=-1, keepdims=True)
            p = jnp.exp2(s - m).astype(BF16)
            p_cat = jnp.concatenate([p[:BLK], p[BLK:]], axis=1)
            v2 = vf[pl.ds(k_start, 2 * BLK, stride=dil), :]
            rhs = jnp.concatenate([
                jnp.concatenate([jnp.where(head0_k, v2, 0.0).astype(BF16), ones0], axis=1),
                jnp.concatenate([jnp.where(head0_k, 0.0, v2).astype(BF16), ones1], axis=1)], axis=0)
            acc2 = _dot(p_cat, rhs)
            acc_b, l_b = acc2[:, :LANES], acc2[:, LANES:]
            m_b = jnp.where(head0, jnp.broadcast_to(m[:BLK], (BLK, LANES)), jnp.broadcast_to(m[BLK:], (BLK, LANES)))
            idx = pl.ds(q_start, BLK, stride=dil)
            if first:
                m_s[idx, :] = m_b
                l_s[idx, :] = l_b
                acc_s[idx, :] = acc_b
                continue
            m_old = m_s[idx, :]
            m_new = jnp.maximum(m_old, m_b)
            a_old = jnp.exp2(m_old - m_new)
            a_b = jnp.exp2(m_b - m_new)
            m_s[idx, :] = m_new
            l_s[idx, :] = a_old * l_s[idx, :] + a_b * l_b
            acc_s[idx, :] = a_old * acc_s[idx, :] + a_b * acc_b

    n_groups = seq // (BLK * DIL_UNROLL)
    order = sorted(range(len(DIL_BRANCHES)), key=lambda i: -DIL_BRANCHES[i][1])
    for pos, br in enumerate(order):
        dil = DIL_BRANCHES[br][1]
        first = pos == 0

        def trip(t, carry, br=br, dil=dil, first=first):
            scores(br, dil, 2 * t + 1, s_odd)
            absorb(br, dil, 2 * t, s_even, first)
            scores(br, dil, 2 * t + 2, s_even)
            absorb(br, dil, 2 * t + 1, s_odd, first)
            return carry

        scores(br, dil, 0, s_even)
        lax.fori_loop(0, n_groups // 2 - 1, trip, 0)
        scores(br, dil, n_groups - 1, s_odd)
        absorb(br, dil, n_groups - 2, s_even, first)
        absorb(br, dil, n_groups - 1, s_odd, first)

    o_ref[0] = (acc_s[...] / l_s[...]).astype(o_ref.dtype)


def _dilated_attention(proj, bias, *, q_blk0, k_blk0, v_blk0):
    b, s, _ = proj.shape
    n_pairs = DIL_HEADS // 2
    assert (s // (BLK * DIL_UNROLL)) % 2 == 0
    blk = lambda off: pl.BlockSpec((1, s, LANES), lambda bi, p: (bi, 0, off + p))
    state = [pltpu.VMEM((s, LANES), F32) for _ in range(4)]
    padded = [pltpu.VMEM((DIL_PAD + s, LANES), F32) for _ in range(2)]
    score_bufs = [pltpu.VMEM((DIL_UNROLL, 2 * BLK, 2 * BLK), F32) for _ in range(2)]
    return pl.pallas_call(
        functools.partial(_dilated_kernel, seq=s),
        grid=(b, n_pairs),
        in_specs=[blk(q_blk0), blk(k_blk0), blk(v_blk0),
                  pl.BlockSpec((len(DIL_BRANCHES), 1, 2, 2 * BLK, 2 * BLK), lambda bi, p: (0, p, 0, 0, 0))],
        out_specs=pl.BlockSpec((1, s, LANES), lambda bi, p: (bi, 0, p)),
        out_shape=jax.ShapeDtypeStruct((b, s, n_pairs * LANES), BF16),
        scratch_shapes=[state[0], padded[0], padded[1], state[1], state[2], state[3]] + score_bufs,
        compiler_params=_params("parallel", "parallel"),
        name="dilated_attention",
    )(proj, proj, proj, bias)


def _swa_kernel(q_ref, k_ref, v_ref, bias_ref, sink_ref, o_ref, vaug_t, s_even, s_odd, *, tc):
    ci = pl.program_id(1)
    seq = k_ref.shape[1]
    grp = SWA_Q_HEADS // SWA_KV_HEADS
    n_blk = tc // BLK

    @pl.when(ci == 0)
    def _():
        feat = lax.broadcasted_iota(jnp.int32, (LANES, BLK), 0)

        def fill(t, carry):
            vt = jnp.transpose(v_ref[0, pl.ds(pl.multiple_of(t * BLK, BLK), BLK), :].astype(F32))
            for kv in range(SWA_KV_HEADS):
                vaug_t[kv, t] = jnp.where(_own_lanes(feat, kv), vt, 1.0).astype(BF16)
            return carry

        lax.fori_loop(0, seq // BLK, fill, 0)

    lane_q = lax.broadcasted_iota(jnp.int32, (BLK, LANES), 1)
    feat_o = lax.broadcasted_iota(jnp.int32, (LANES, grp * BLK), 0)

    def blocks(bb):
        own = ci * n_blk + bb
        return jnp.maximum(own - 1, 0), own

    def scores(bb, kv, buf):
        prev, own = blocks(bb)
        k2 = jnp.concatenate([k_ref[0, pl.ds(pl.multiple_of(prev * BLK, BLK), BLK), :],
                              k_ref[0, pl.ds(pl.multiple_of(own * BLK, BLK), BLK), :]], axis=0)
        q4 = jnp.concatenate(
            [jnp.where(_own_lanes(lane_q, kv), q_ref[0, bb * BLK:(bb + 1) * BLK, g * LANES:(g + 1) * LANES].astype(F32),
                       0.0) for g in range(grp)], axis=0).astype(BF16)
        s = lax.dot_general(k2, q4, NT_DIMS, preferred_element_type=F32)
        buf[...] = s + bias_ref[kv, jnp.where(own == 0, 1, 0)]

    def absorb(bb, kv, buf):
        prev, own = blocks(bb)
        s = buf[...]
        sink = sink_ref[kv]
        m = jnp.maximum(jnp.max(s, axis=0, keepdims=True), sink)
        p = jnp.exp2(s - m).astype(BF16)
        v2 = jnp.concatenate([vaug_t[kv, prev], vaug_t[kv, own]], axis=1)
        acc = _dot(v2, p)
        den = (acc[HEAD_DIM:HEAD_DIM + 1, :] if kv == 0 else acc[0:1, :]) + jnp.exp2(sink - m)
        return acc / den

    units = [(bb, kv) for bb in range(n_blk) for kv in range(SWA_KV_HEADS)]
    bufs = (s_even, s_odd)
    scores(*units[0], bufs[0])
    outs = {}
    for n, unit in enumerate(units):
        if n + 1 < len(units):
            scores(*units[n + 1], bufs[(n + 1) % 2])
        outs[unit] = absorb(*unit, bufs[n % 2])
    for bb in range(n_blk):
        out_t = jnp.where(feat_o < HEAD_DIM, outs[(bb, 0)], outs[(bb, 1)])
        for g in range(grp):
            o_ref[0, bb * BLK:(bb + 1) * BLK, g * LANES:(g + 1) * LANES] = jnp.transpose(
                out_t[:, g * BLK:(g + 1) * BLK]).astype(o_ref.dtype)


def _swa_attention(proj, bias, sink_row, *, q_blk0, k_blk0, v_blk0, tc=512):
    b, s, _ = proj.shape
    qw = SWA_Q_HEADS * HEAD_DIM
    grp = SWA_Q_HEADS // SWA_KV_HEADS
    return pl.pallas_call(
        functools.partial(_swa_kernel, tc=tc),
        grid=(b, s // tc),
        in_specs=[
            pl.BlockSpec((1, tc, qw), lambda bi, ci: (bi, ci, q_blk0 * LANES // qw)),
            pl.BlockSpec((1, s, LANES), lambda bi, ci: (bi, 0, k_blk0)),
            pl.BlockSpec((1, s, LANES), lambda bi, ci: (bi, 0, v_blk0)),
            pl.BlockSpec(bias.shape, lambda bi, ci: (0, 0, 0, 0)),
            pl.BlockSpec(sink_row.shape, lambda bi, ci: (0, 0, 0)),
        ],
        out_specs=pl.BlockSpec((1, tc, qw), lambda bi, ci: (bi, ci, 0)),
        out_shape=jax.ShapeDtypeStruct((b, s, qw), BF16),
        scratch_shapes=[pltpu.VMEM((SWA_KV_HEADS, s // BLK, LANES, BLK), BF16),
                        pltpu.VMEM((2 * BLK, grp * BLK), F32), pltpu.VMEM((2 * BLK, grp * BLK), F32)],
        compiler_params=_params("parallel", "arbitrary"),
        name="swa_attention",
    )(proj, proj, proj, bias, sink_row)


def _rope128(x, a, bm, bp):
    return x * a + pltpu.roll(x, LANES - 16, axis=1) * bm + pltpu.roll(x, 16, axis=1) * bp


def _proj_odd_kernel(x_ref, g_ref, w_ref, qn_ref, kvn_ref, wuq_ref, wuk_ref, wuv_ref, a_ref, bm_ref, bp_ref,
                     q_out, k_out, v_out, swa_out):
    xb = _rms(x_ref[...], g_ref[...]).astype(BF16)
    w = w_ref
    c_q = _dot(xb, w[:, 0:MLA_Q_RANK])
    c_kv = _dot(xb, w[:, MLA_Q_RANK:MLA_Q_RANK + MLA_KV_RANK])
    o1 = MLA_Q_RANK + MLA_KV_RANK
    kpe = _dot(xb, w[:, o1:o1 + LANES])
    o2 = o1 + LANES
    swa_out[...] = _dot(xb, w[:, o2:]).astype(swa_out.dtype)
    a, bm, bp = a_ref[...], bm_ref[...], bp_ref[...]
    kpe = _rope128(kpe, a, bm, bp)
    cqn = _rms(c_q, qn_ref[...]).astype(BF16)
    ckvn = _rms(c_kv, kvn_ref[...]).astype(BF16)
    v_out[...] = _dot(ckvn, wuv_ref[...]).astype(v_out.dtype)
    q_raw = _dot(cqn, wuq_ref[...])
    k_raw = _dot(ckvn, wuk_ref[...])
    for h in range(MLA_HEADS):
        sl = slice(h * LANES, (h + 1) * LANES)
        q_out[:, sl] = _rope128(q_raw[:, sl], a, bm, bp).astype(q_out.dtype)
        k_out[:, sl] = (k_raw[:, sl] + kpe).astype(k_out.dtype)


def _proj_odd(h, g, w, qn, kvn, wuq, wuk, wuv, rope_a, rope_bm, rope_bp, seq, tm=512):
    n, d = h.shape
    n_in = w.shape[1]
    n_swa = n_in - (MLA_Q_RANK + MLA_KV_RANK + LANES)
    tiles_per_seq = seq // tm
    full = lambda shape: pl.BlockSpec(shape, lambda i: (0, 0))
    rope = pl.BlockSpec((tm, LANES), lambda i: (i % tiles_per_seq, 0))
    return pl.pallas_call(
        _proj_odd_kernel,
        grid=(n // tm,),
        in_specs=[
            pl.BlockSpec((tm, d), lambda i: (i, 0)), full((1, d)), full(w.shape),
            full(qn.shape), full(kvn.shape), full(wuq.shape), full(wuk.shape), full(wuv.shape),
            rope, rope, rope,
        ],
        out_specs=[
            pl.BlockSpec((tm, MLA_HEADS * LANES), lambda i: (i, 0)),
            pl.BlockSpec((tm, MLA_HEADS * LANES), lambda i: (i, 0)),
            pl.BlockSpec((tm, MLA_HEADS * MLA_V), lambda i: (i, 0)),
            pl.BlockSpec((tm, n_swa), lambda i: (i, 0)),
        ],
        out_shape=[
            jax.ShapeDtypeStruct((n, MLA_HEADS * LANES), BF16),
            jax.ShapeDtypeStruct((n, MLA_HEADS * LANES), BF16),
            jax.ShapeDtypeStruct((n, MLA_HEADS * MLA_V), BF16),
            jax.ShapeDtypeStruct((n, n_swa), BF16),
        ],
        compiler_params=_params("parallel"),
        name="proj_odd",
    )(h, g, w, qn, kvn, wuq, wuk, wuv, rope_a, rope_bm, rope_bp)


MOE_TILE = 1024
ROUTER_TILE = 512


def _out_router_kernel(h_ref, oa_ref, ob_ref, wo_ref, g_ref, wr_hi_ref, wr_lo_ref, br_ref,
                       h_out, hn_out, gates_out, counts_out, pos_col_out, pos_row_out, tri_ref, seen_ref):
    i = pl.program_id(0)
    tm = h_ref.shape[0]
    half = oa_ref.shape[1]
    h = h_ref[...] + _dot(oa_ref[...], wo_ref[0:half, :]) + _dot(ob_ref[...], wo_ref[half:, :])
    h_out[...] = h
    hn = _rms(h, g_ref[...])
    hn_out[...] = hn.astype(hn_out.dtype)
    z = _dot_hi(hn, wr_hi_ref[...], wr_lo_ref[...]) + br_ref[...]
    lane = lax.broadcasted_iota(jnp.int32, z.shape, 1)
    big = jnp.int32(LANES)
    is_grp = (lane >= N_EXPERTS) & (lane < N_EXPERTS + N_GROUPS)
    zg = jnp.where(is_grp, z, -jnp.inf)
    g_max = jnp.max(zg, axis=-1, keepdims=True)
    g_w = 1.0 / jnp.sum(jnp.exp(zg - g_max), axis=-1, keepdims=True)
    g_idx = jnp.min(jnp.where(zg == g_max, lane - N_EXPERTS, big), axis=-1, keepdims=True)
    in_grp = (lane < N_EXPERTS) & ((lane // EXPERTS_PER_GROUP) == g_idx)
    ze = jnp.where(in_grp, z, -jnp.inf)
    v1 = jnp.max(ze, axis=-1, keepdims=True)
    i1 = jnp.min(jnp.where(ze == v1, lane, big), axis=-1, keepdims=True)
    ze2 = jnp.where(lane == i1, -jnp.inf, ze)
    v2 = jnp.max(ze2, axis=-1, keepdims=True)
    i2 = jnp.min(jnp.where(ze2 == v2, lane, big), axis=-1, keepdims=True)
    e2 = jnp.exp(v2 - v1)
    w1 = g_w / (1.0 + e2)
    w2 = g_w * e2 / (1.0 + e2)
    for grp in range(N_GROUPS):
        first = grp * EXPERTS_PER_GROUP
        slot = jnp.where(lane == i1 - first, w1, 0.0) + jnp.where(lane == i2 - first, w2, 0.0)
        gates_out[:, grp * LANES:(grp + 1) * LANES] = jnp.where(lane < EXPERTS_PER_GROUP, slot, 0.0)

    @pl.when(i == 0)
    def _():
        ri = lax.broadcasted_iota(jnp.int32, (tm, tm), 0)
        ci = lax.broadcasted_iota(jnp.int32, (tm, tm), 1)
        tri_ref[...] = jnp.where(ci < ri, 1.0, 0.0).astype(BF16)

    @pl.when(i % (MOE_TILE // tm) == 0)
    def _():
        seen_ref[...] = jnp.zeros_like(seen_ref)

    routed = jnp.where((lane == g_idx) & (lane < N_GROUPS), 1.0, 0.0)
    pos = jnp.where(routed > 0.0, _dot(tri_ref[...], routed.astype(BF16)) + seen_ref[...], -1.0)
    pos_col_out[...] = pos
    pos_row_out[0] = jnp.transpose(pos)[0:8, :]
    here = jnp.sum(routed, axis=0, keepdims=True)
    seen_ref[...] += here
    counts_out[0] = jnp.broadcast_to(here, counts_out.shape[1:])


def _out_router(h, oa, ob, wo, g, wr_hi, wr_lo, br):
    n, d = h.shape
    tm = ROUTER_TILE
    per_tile = MOE_TILE // tm
    half = oa.shape[1]
    full = lambda shape: pl.BlockSpec(shape, lambda i: (0, 0))
    tile = lambda w: pl.BlockSpec((tm, w), lambda i: (i, 0))
    return pl.pallas_call(
        _out_router_kernel,
        grid=(n // tm,),
        in_specs=[tile(d), tile(half), tile(half), full(wo.shape), full((1, d)),
                  full(wr_hi.shape), full(wr_lo.shape), full((1, LANES))],
        out_specs=[tile(d), tile(d), tile(N_GROUPS * LANES), pl.BlockSpec((1, 8, LANES), lambda i: (i, 0, 0)),
                   tile(LANES), pl.BlockSpec((1, 8, tm), lambda i: (i // per_tile, 0, i % per_tile))],
        out_shape=[
            jax.ShapeDtypeStruct((n, d), F32),
            jax.ShapeDtypeStruct((n, d), BF16),
            jax.ShapeDtypeStruct((n, N_GROUPS * LANES), F32),
            jax.ShapeDtypeStruct((n // tm, 8, LANES), F32),
            jax.ShapeDtypeStruct((n, LANES), F32),
            jax.ShapeDtypeStruct((n // MOE_TILE, 8, MOE_TILE), F32),
        ],
        scratch_shapes=[pltpu.VMEM((tm, tm), BF16), pltpu.VMEM((1, LANES), F32)],
        compiler_params=_params("arbitrary"),
        name="out_router",
    )(h, oa, ob, wo, g, wr_hi, wr_lo, br)


MOE_CHUNK = 256
MOE_ROWS = 1024
MOE_ALIGN = 16
MOE_EXPERTS_PER_STEP = 4


def _moe_sorted_tiles(n):
    n_seg = (n // MOE_TILE) * N_GROUPS
    rows = n + n_seg * (MOE_ALIGN - 1) + N_GROUPS * (MOE_CHUNK + MOE_ROWS - 1)
    return -(-rows // MOE_ROWS)


def _moe_tables(counts, n):
    n_rt = _moe_sorted_tiles(n)
    seg_len = (counts + (MOE_ALIGN - 1)) // MOE_ALIGN * MOE_ALIGN
    group_len = seg_len.sum(axis=0)
    group_span = (group_len + MOE_CHUNK + MOE_ROWS - 1) // MOE_ROWS * MOE_ROWS
    group_end = jnp.cumsum(group_span)
    group_start = group_end - group_span
    seg_off = group_start[None, :] + jnp.cumsum(seg_len, axis=0) - seg_len
    rt_start = jnp.arange(n_rt, dtype=jnp.int32) * MOE_ROWS
    rt_group = jnp.minimum(jnp.sum(rt_start[:, None] >= group_end[None, :], axis=1), N_GROUPS - 1)
    rt_valid = jnp.clip(group_start[rt_group] + group_len[rt_group] - rt_start, 0, MOE_ROWS)
    n_used = (group_end[-1] // MOE_ROWS).reshape(1)
    i32 = lambda a: a.astype(jnp.int32)
    return i32(seg_off.reshape(-1)), i32(rt_group), i32(rt_valid), i32(n_used)


def _moe_chunks(count, body):
    n_full = count // MOE_CHUNK
    rest = count - n_full * MOE_CHUNK

    def full_chunk(k, carry):
        body(k * MOE_CHUNK, MOE_CHUNK)
        return carry

    lax.fori_loop(0, n_full, full_chunk, 0)

    @pl.when(rest > MOE_CHUNK // 2)
    def _():
        body(n_full * MOE_CHUNK, MOE_CHUNK)

    @pl.when((rest > 0) & (rest <= MOE_CHUNK // 2))
    def _():
        body(n_full * MOE_CHUNK, MOE_CHUNK // 2)


def _moe_pack_kernel(counts_ref, off_ref, hn_ref, gates_ref, pos_ref, xs_in, xs_ref, xbuf, sem, issued_ref):
    del xs_in
    i = pl.program_id(0)
    d = hn_ref.shape[1]

    @pl.when(i == 0)
    def _():
        issued_ref[0] = 0

    def write(slot):
        return pltpu.make_async_copy(xbuf.at[slot], xs_ref.at[pl.ds(0, MOE_CHUNK)], sem.at[slot])

    lane = lax.broadcasted_iota(jnp.int32, (hn_ref.shape[0], LANES), 1)
    for grp in range(N_GROUPS):
        seg = i * N_GROUPS + grp
        gates = gates_ref[:, grp * LANES:(grp + 1) * LANES]
        g_hi = gates.astype(BF16).astype(F32)
        g_lo = pltpu.roll(gates - g_hi, EXPERTS_PER_GROUP, axis=1)
        gate_cols = jnp.where(lane < EXPERTS_PER_GROUP, g_hi, jnp.where(lane < 2 * EXPERTS_PER_GROUP, g_lo, 0.0))
        gate_cols = gate_cols.astype(BF16)
        pos_row = pos_ref[0, grp:grp + 1, :]

        def chunk(first_pos, rows, seg=seg, gate_cols=gate_cols, pos_row=pos_row):
            n_done = issued_ref[0]
            slot = n_done % 2
            slot_pos = lax.broadcasted_iota(jnp.int32, (rows, 1), 0).astype(F32) + first_pos.astype(F32)
            gather = jnp.where(pos_row == slot_pos, 1.0, 0.0).astype(BF16)
            xbuf[slot, 0:rows, 0:d] = _dot(gather, hn_ref[...]).astype(BF16)
            xbuf[slot, 0:rows, d:] = _dot(gather, gate_cols).astype(BF16)
            if rows < MOE_CHUNK:
                xbuf[slot, rows:, :] = jnp.zeros((MOE_CHUNK - rows, d + LANES), BF16)

            @pl.when(n_done > 0)
            def _():
                write(1 - slot).wait()

            dst = pl.multiple_of(off_ref[seg] + first_pos, MOE_ALIGN)
            pltpu.make_async_copy(xbuf.at[slot], xs_ref.at[pl.ds(dst, MOE_CHUNK)], sem.at[slot]).start()
            issued_ref[0] = n_done + 1

        _moe_chunks(counts_ref[seg], chunk)

    @pl.when((i == pl.num_programs(0) - 1) & (issued_ref[0] > 0))
    def _():
        write((issued_ref[0] - 1) % 2).wait()


def _moe_pack(counts, seg_off, hn, gates, pos_row, n_rt):
    n, d = hn.shape
    rows = n_rt * MOE_ROWS
    grid_spec = pltpu.PrefetchScalarGridSpec(
        num_scalar_prefetch=2,
        grid=(n // MOE_TILE,),
        in_specs=[
            pl.BlockSpec((MOE_TILE, d), lambda i, c, o: (i, 0)),
            pl.BlockSpec((MOE_TILE, N_GROUPS * LANES), lambda i, c, o: (i, 0)),
            pl.BlockSpec((1, 8, MOE_TILE), lambda i, c, o: (i, 0, 0)),
            pl.BlockSpec(memory_space=pl.ANY),
        ],
        out_specs=pl.BlockSpec(memory_space=pl.ANY),
        scratch_shapes=[pltpu.VMEM((2, MOE_CHUNK, d + LANES), BF16), pltpu.SemaphoreType.DMA((2,)),
                        pltpu.SMEM((1,), jnp.int32)],
    )
    return pl.pallas_call(
        _moe_pack_kernel,
        grid_spec=grid_spec,
        out_shape=jax.ShapeDtypeStruct((rows, d + LANES), BF16),
        input_output_aliases={5: 0},
        compiler_params=_params("arbitrary"),
        name="moe_pack",
    )(counts, seg_off, hn, gates, pos_row, jnp.zeros((rows, d + LANES), BF16))


def _moe_experts_kernel(grp_ref, valid_ref, used_ref, xs_ref, wg_ref, wu_ref, wd_ref, ys_ref, acc_ref):
    rt = pl.program_id(0)
    j = pl.program_id(1)
    d = ys_ref.shape[1]

    @pl.when(rt < used_ref[0])
    def _():
        @pl.when(j == 0)
        def _():
            acc_ref[...] = jnp.zeros_like(acc_ref)

        x = xs_ref[:, 0:d]
        gate_cols = xs_ref[:, d:].astype(F32)
        lane = lax.broadcasted_iota(jnp.int32, gate_cols.shape, 1)
        acts = []
        for e in range(MOE_EXPERTS_PER_STEP):
            idx = j * MOE_EXPERTS_PER_STEP + e
            mine = (lane == idx) | (lane == idx + EXPERTS_PER_GROUP)
            gate = jnp.sum(jnp.where(mine, gate_cols, 0.0), axis=-1, keepdims=True)
            a = _dot(x, wg_ref[0, 0, e].astype(BF16))
            u = _dot(x, wu_ref[0, 0, e].astype(BF16))
            acts.append(((a * jax.nn.sigmoid(a)) * u * gate).astype(BF16))
        w_down = wd_ref[0, 0].astype(BF16).reshape(MOE_EXPERTS_PER_STEP * D_EXPERT, d)
        acc_ref[...] += _dot(jnp.concatenate(acts, axis=1), w_down)

        @pl.when(j == pl.num_programs(1) - 1)
        def _():
            row = lax.broadcasted_iota(jnp.int32, (acc_ref.shape[0], 1), 0)
            ys_ref[...] = jnp.where(row < valid_ref[rt], acc_ref[...], 0.0).astype(ys_ref.dtype)

    @pl.when((rt >= used_ref[0]) & (j == pl.num_programs(1) - 1))
    def _():
        ys_ref[...] = jnp.zeros_like(ys_ref)


def _moe_experts(rt_group, rt_valid, n_used, xs, w_gate, w_up, w_down, layer):
    rows = xs.shape[0]
    d = w_gate.shape[-2]
    n_rt = rows // MOE_ROWS
    eps = MOE_EXPERTS_PER_STEP
    n_steps = EXPERTS_PER_GROUP // eps

    def tile_idx(rt, j, grp, valid, used):
        return jnp.minimum(rt, used[0] - 1)

    def w_idx(rt, j, grp, valid, used):
        live = rt < used[0]
        return (layer, grp[tile_idx(rt, j, grp, valid, used)], jnp.where(live, j, n_steps - 1), 0, 0)

    grid_spec = pltpu.PrefetchScalarGridSpec(
        num_scalar_prefetch=3,
        grid=(n_rt, n_steps),
        in_specs=[
            pl.BlockSpec((MOE_ROWS, d + LANES), lambda rt, j, grp, valid, used: (tile_idx(rt, j, grp, valid, used), 0)),
            pl.BlockSpec((1, 1, eps, d, D_EXPERT), w_idx),
            pl.BlockSpec((1, 1, eps, d, D_EXPERT), w_idx),
            pl.BlockSpec((1, 1, eps, D_EXPERT, d), w_idx),
        ],
        out_specs=pl.BlockSpec((MOE_ROWS, d), lambda rt, j, grp, valid, used: (rt, 0)),
        scratch_shapes=[pltpu.VMEM((MOE_ROWS, d), F32)],
    )
    return pl.pallas_call(
        _moe_experts_kernel,
        grid_spec=grid_spec,
        out_shape=jax.ShapeDtypeStruct((rows, d), BF16),
        compiler_params=_params("arbitrary", "arbitrary"),
        name="moe_experts",
    )(rt_group, rt_valid, n_used, xs, w_gate, w_up, w_down)


def _moe_combine_kernel(counts_ref, off_ref, pos_ref, h_ref, fg_ref, ys_ref, o_ref, acc_ref, ybuf, ymore, sem,
                        *, final_norm):
    i = pl.program_id(0)

    def read(grp):
        src = pl.multiple_of(off_ref[i * N_GROUPS + grp], MOE_ALIGN)
        return pltpu.make_async_copy(ys_ref.at[pl.ds(src, MOE_CHUNK)], ybuf.at[pl.ds(grp * MOE_CHUNK, MOE_CHUNK)],
                                     sem.at[grp])

    @pl.when(i == 0)
    def _():
        ybuf[...] = jnp.zeros_like(ybuf)

    for grp in range(N_GROUPS):
        @pl.when(counts_ref[i * N_GROUPS + grp] > 0)
        def _(grp=grp):
            read(grp).start()

    slot_pos = lax.broadcasted_iota(jnp.int32, (1, MOE_CHUNK), 1).astype(F32)
    scatter = jnp.concatenate(
        [jnp.where(pos_ref[:, grp:grp + 1] == slot_pos, 1.0, 0.0).astype(BF16) for grp in range(N_GROUPS)], axis=1)
    for grp in range(N_GROUPS):
        @pl.when(counts_ref[i * N_GROUPS + grp] > 0)
        def _(grp=grp):
            read(grp).wait()

    acc_ref[...] = h_ref[...] + _dot(scatter, ybuf[...])
    for grp in range(N_GROUPS):
        count = counts_ref[i * N_GROUPS + grp]
        pos_col = pos_ref[:, grp:grp + 1]

        def more(k, carry, grp=grp, pos_col=pos_col):
            src = pl.multiple_of(off_ref[i * N_GROUPS + grp] + k * MOE_CHUNK, MOE_ALIGN)
            pltpu.sync_copy(ys_ref.at[pl.ds(src, MOE_CHUNK)], ymore)
            scatter = jnp.where(pos_col == slot_pos + (k * MOE_CHUNK).astype(F32), 1.0, 0.0).astype(BF16)
            acc_ref[...] += _dot(scatter, ymore[...])
            return carry

        lax.fori_loop(1, (count + (MOE_CHUNK - 1)) // MOE_CHUNK, more, 0)

    h = acc_ref[...]
    o_ref[...] = _rms(h, fg_ref[...]) if final_norm else h


def _moe_combine(counts, seg_off, pos_col, h, fg, ys, final_norm):
    n, d = h.shape
    grid_spec = pltpu.PrefetchScalarGridSpec(
        num_scalar_prefetch=2,
        grid=(n // MOE_TILE,),
        in_specs=[
            pl.BlockSpec((MOE_TILE, LANES), lambda i, c, o: (i, 0)),
            pl.BlockSpec((MOE_TILE, d), lambda i, c, o: (i, 0)),
            pl.BlockSpec((1, d), lambda i, c, o: (0, 0)),
            pl.BlockSpec(memory_space=pl.ANY),
        ],
        out_specs=pl.BlockSpec((MOE_TILE, d), lambda i, c, o: (i, 0)),
        scratch_shapes=[pltpu.VMEM((MOE_TILE, d), F32), pltpu.VMEM((N_GROUPS * MOE_CHUNK, d), BF16),
                        pltpu.VMEM((MOE_CHUNK, d), BF16), pltpu.SemaphoreType.DMA((N_GROUPS,))],
    )
    return pl.pallas_call(
        functools.partial(_moe_combine_kernel, final_norm=final_norm),
        grid_spec=grid_spec,
        out_shape=jax.ShapeDtypeStruct((n, d), F32),
        compiler_params=_params("arbitrary"),
        name="moe_combine",
    )(counts, seg_off, pos_col, h, fg, ys)


def _alibi_slopes(n):
    return jnp.exp2(-8.0 * jnp.arange(1, n + 1, dtype=F32) / n)


def _band_bias(slopes, max_steps, step_dist):
    steps = (jnp.arange(BLK)[:, None] + BLK) - jnp.arange(2 * BLK)[None, :]
    in_band = (steps >= 0) & (steps <= max_steps)
    dist = (steps * step_dist).astype(F32)
    return jnp.where(in_band[None], -slopes.astype(F32)[:, None, None] * dist[None], NEG)


def _rope_tables(s):
    inv = ROPE_THETA ** (-jnp.arange(0, MLA_ROPE, 2, dtype=F32) / MLA_ROPE)
    ang = jnp.arange(s, dtype=F32)[:, None] * inv[None, :]
    cos, sin = jnp.cos(ang), jnp.sin(ang)
    half = MLA_ROPE // 2
    zeros_tail = jnp.zeros((s, LANES - MLA_NOPE - MLA_ROPE), F32)
    a = jnp.concatenate([jnp.ones((s, MLA_NOPE), F32), cos, cos, zeros_tail], axis=1)
    zeros_nope = jnp.zeros((s, MLA_NOPE), F32)
    zeros_half = jnp.zeros((s, half), F32)
    bm = jnp.concatenate([zeros_nope, -sin, zeros_half, zeros_tail], axis=1)
    bp = jnp.concatenate([zeros_nope, zeros_half, sin, zeros_tail], axis=1)
    return a, bm, bp


def _swa_head_order(w, axis):
    grp = SWA_Q_HEADS // SWA_KV_HEADS
    shape = w.shape
    w = w.reshape(shape[:axis] + (SWA_KV_HEADS, grp, HEAD_DIM) + shape[axis + 1:])
    return jnp.swapaxes(w, axis, axis + 1).reshape(shape)


def _pad_cols(w, width):
    return jnp.pad(w, ((0, 0), (0, width - w.shape[1])))


def _router_weights(w_group, b_group, w_router, b_router):
    w = _pad_cols(jnp.concatenate([w_router, w_group], axis=1), LANES)
    b = _pad_cols(jnp.concatenate([b_router, b_group])[None, :], LANES)
    hi, lo = _hi_lo(w)
    return hi, lo, b


def kernel(x, attn_norm, ffn_norm, final_norm, e_w_in, e_b_f, e_w_out, o_w_in, o_q_norm, o_kv_norm, o_w_uq,
           o_w_ukv, o_sinks, o_w_out, moe_w_group, moe_b_group, moe_w_router, moe_b_router, moe_w_gate,
           moe_w_up, moe_w_down):
    b, s, d = x.shape
    n = b * s
    depth = attn_norm.shape[0]
    assert s % (BLK * DIL_BRANCHES[-1][1]) == 0 and d == D_MODEL
    h = x.reshape(n, d)

    dil_slopes = _alibi_slopes(DIL_HEADS)
    dil_bias = jnp.stack([_band_bias(dil_slopes, w // dl, dl) for (w, dl) in DIL_BRANCHES]) * LOG2E
    own_half = jnp.arange(2 * BLK) >= BLK
    dil_bias = jnp.stack([dil_bias, jnp.where(own_half, dil_bias, NEG)], axis=2)
    dil_bias = dil_bias.reshape(len(DIL_BRANCHES), DIL_HEADS // 2, 2, 2, BLK, 2 * BLK)
    dil_bias = dil_bias.transpose(0, 1, 3, 2, 4, 5).reshape(len(DIL_BRANCHES), DIL_HEADS // 2, 2, 2 * BLK, 2 * BLK)
    swa_grp = SWA_Q_HEADS // SWA_KV_HEADS
    swa_bias = _band_bias(_alibi_slopes(SWA_Q_HEADS), SWA_WINDOW - 1, 1) * LOG2E
    swa_bias = swa_bias.reshape(SWA_KV_HEADS, swa_grp, BLK, 2 * BLK).transpose(0, 3, 1, 2)
    swa_bias = swa_bias.reshape(SWA_KV_HEADS, 2 * BLK, swa_grp * BLK)
    swa_bias = jnp.stack([swa_bias, jnp.where(own_half[None, :, None], swa_bias, NEG)], axis=1)
    rope_a, rope_bm, rope_bp = _rope_tables(s)

    for layer in range(depth):
        i = layer // 2
        g_attn = attn_norm[layer][None, :]
        if layer % 2 == 0:
            w_in = e_w_in[i]
            hq = FOX_HEADS * HEAD_DIM
            scale = HEAD_DIM ** -0.5
            cols = [w_in[:, 0:hq] * (scale * LOG2E), w_in[:, hq:2 * hq], w_in[:, 2 * hq:3 * hq]]
            o = 3 * hq + FOX_HEADS
            cols += [w_in[:, o:o + hq] * (scale * LOG2E), w_in[:, o + hq:o + 2 * hq], w_in[:, o + 2 * hq:o + 3 * hq]]
            w_main = jnp.concatenate(cols, axis=1).astype(BF16)
            wf_hi, wf_lo = _hi_lo(_pad_cols(w_in[:, 3 * hq:o], LANES))
            b_f = _pad_cols(e_b_f[i][None, :], LANES)
            proj, c = _proj_even(h, g_attn, w_main, wf_hi, wf_lo, b_f, s)
            proj = proj.reshape(b, s, -1)
            o_a = _causal_attention(proj, proj, proj, c.reshape(b, s, LANES), q_blk0=0, k_blk0=4, v_blk0=8,
                                    n_pairs=FOX_HEADS // 2)
            o_b = _dilated_attention(proj, dil_bias, q_blk0=12, k_blk0=16, v_blk0=20)
            w_out = e_w_out[i].astype(BF16)
        else:
            w_in = o_w_in[i]
            o1 = MLA_Q_RANK + MLA_KV_RANK
            o2 = o1 + MLA_ROPE
            sq = SWA_Q_HEADS * HEAD_DIM
            kpe_cols = jnp.pad(w_in[:, o1:o2], ((0, 0), (MLA_NOPE, LANES - MLA_NOPE - MLA_ROPE)))
            w_main = jnp.concatenate(
                [w_in[:, :o1], kpe_cols, _swa_head_order(w_in[:, o2:o2 + sq] * (HEAD_DIM ** -0.5 * LOG2E), axis=1),
                 w_in[:, o2 + sq:]],
                axis=1).astype(BF16)
            dq = MLA_NOPE + MLA_ROPE
            wuq = o_w_uq[i].reshape(MLA_Q_RANK, MLA_HEADS, dq) * (dq ** -0.5 * LOG2E)
            wuq = jnp.pad(wuq, ((0, 0), (0, 0), (0, LANES - dq))).reshape(MLA_Q_RANK, MLA_HEADS * LANES)
            wukv = o_w_ukv[i].reshape(MLA_KV_RANK, MLA_HEADS, MLA_NOPE + MLA_V)
            wuk = jnp.pad(wukv[:, :, :MLA_NOPE], ((0, 0), (0, 0), (0, LANES - MLA_NOPE)))
            wuk = wuk.reshape(MLA_KV_RANK, MLA_HEADS * LANES)
            wuv = wukv[:, :, MLA_NOPE:].reshape(MLA_KV_RANK, MLA_HEADS * MLA_V)
            q_full, k_full, v_mla, swa = _proj_odd(
                h, g_attn, w_main, o_q_norm[i][None, :], o_kv_norm[i][None, :], wuq.astype(BF16),
                wuk.astype(BF16), wuv.astype(BF16), rope_a, rope_bm, rope_bp, s)
            o_a = _causal_attention(q_full.reshape(b, s, -1), k_full.reshape(b, s, -1), v_mla.reshape(b, s, -1),
                                    None, q_blk0=0, k_blk0=0, v_blk0=0, n_pairs=MLA_HEADS // 2)
            grp = SWA_Q_HEADS // SWA_KV_HEADS
            sink_row = jnp.repeat(o_sinks[i].reshape(SWA_KV_HEADS, grp) * LOG2E, BLK, axis=1)[:, None, :]
            o_b = _swa_attention(swa.reshape(b, s, -1), swa_bias, sink_row, q_blk0=0, k_blk0=4, v_blk0=5)
            half = MLA_HEADS * MLA_V
            w_out = jnp.concatenate([o_w_out[i][:half], _swa_head_order(o_w_out[i][half:], axis=0)], axis=0)
            w_out = w_out.astype(BF16)

        wr_hi, wr_lo, b_r = _router_weights(moe_w_group[layer], moe_b_group[layer], moe_w_router[layer],
                                            moe_b_router[layer])
        h, hn, gates, counts, pos_col, pos_row = _out_router(
            h, o_a.reshape(n, -1), o_b.reshape(n, -1), w_out, ffn_norm[layer][None, :], wr_hi, wr_lo, b_r)
        counts = counts[:, 0, :N_GROUPS].reshape(n // MOE_TILE, MOE_TILE // ROUTER_TILE, N_GROUPS).sum(axis=1)
        counts = counts.astype(jnp.int32)
        seg_off, rt_group, rt_valid, n_used = _moe_tables(counts, n)
        counts = counts.reshape(-1)
        xs = _moe_pack(counts, seg_off, hn, gates, pos_row, _moe_sorted_tiles(n))
        ys = _moe_experts(rt_group, rt_valid, n_used, xs, moe_w_gate, moe_w_up, moe_w_down, layer)
        h = _moe_combine(counts, seg_off, pos_col, h, final_norm[None, :], ys, final_norm=layer == depth - 1)
    return h.reshape(b, s, d)
```

```python
import functools

import jax
import jax.numpy as jnp
from jax import lax
from jax.experimental import pallas as pl
from jax.experimental.pallas import tpu as pltpu

F32 = jnp.float32
BF16 = jnp.bfloat16

D_MODEL = 1024
HEAD_DIM = 64
BLK = 128
NEG = -1e30
RMS_EPS = 1e-6
FOX_HEADS = 8
DIL_HEADS = 8
DIL_BRANCHES = ((128, 1), (512, 4), (2048, 16))
MLA_HEADS = 8
MLA_Q_RANK = 384
MLA_KV_RANK = 256
MLA_NOPE = 64
MLA_ROPE = 32
MLA_V = 64
ROPE_THETA = 10000.0
SWA_Q_HEADS = 8
SWA_KV_HEADS = 2
SWA_WINDOW = 128
N_GROUPS = 4
EXPERTS_PER_GROUP = 8
N_EXPERTS = N_GROUPS * EXPERTS_PER_GROUP
D_EXPERT = 256

LANES = 128
VMEM_LIMIT = 56 * 1024 * 1024

NT_DIMS = (((1,), (1,)), ((), ()))
LOG2E = 1.4426950408889634


def _params(*sem):
    return pltpu.CompilerParams(dimension_semantics=sem, vmem_limit_bytes=VMEM_LIMIT)


def _dot(a, b):
    return jnp.dot(a, b, preferred_element_type=F32)


def _rms(x, g):
    return x * lax.rsqrt(jnp.mean(x * x, axis=-1, keepdims=True) + RMS_EPS) * g


def _hi_lo(w):
    hi = w.astype(BF16)
    return hi, (w - hi.astype(F32)).astype(BF16)


def _dot_hi(x, w_hi, w_lo):
    x_hi = x.astype(BF16)
    x_lo = (x - x_hi.astype(F32)).astype(BF16)
    return _dot(x_hi, w_hi) + (_dot(x_hi, w_lo) + _dot(x_lo, w_hi))


def _split3(x):
    x1 = x.astype(BF16)
    r1 = x - x1.astype(F32)
    x2 = r1.astype(BF16)
    x3 = (r1 - x2.astype(F32)).astype(BF16)
    return x1, x2, x3


def _proj_even_kernel(x_ref, g_ref, w_ref, wf_hi_ref, wf_lo_ref, bf_ref, out_ref, c_ref, carry_ref, tri_ref, *,
                      tiles_per_seq):
    xn = _rms(x_ref[...], g_ref[...])
    xb = xn.astype(BF16)
    n_out = out_ref.shape[1]
    for c in range(0, n_out, 512):
        out_ref[:, c:c + 512] = _dot(xb, w_ref[:, c:c + 512]).astype(out_ref.dtype)
    z = _dot_hi(xn, wf_hi_ref[...], wf_lo_ref[...]) + bf_ref[...]
    logf = jnp.minimum(z, 0.0) - jnp.log1p(jnp.exp(-jnp.abs(z)))

    @pl.when(pl.program_id(0) % tiles_per_seq == 0)
    def _():
        carry_ref[...] = jnp.zeros_like(carry_ref)

    tm = logf.shape[0]

    @pl.when(pl.program_id(0) == 0)
    def _():
        ri = lax.broadcasted_iota(jnp.int32, (tm, tm), 0)
        ci = lax.broadcasted_iota(jnp.int32, (tm, tm), 1)
        tri_ref[...] = jnp.where(ci <= ri, 1.0, 0.0).astype(BF16)

    lower = tri_ref[...]
    l1, l2, l3 = _split3(logf)
    c = (_dot(lower, l1) + (_dot(lower, l2) + _dot(lower, l3))) + carry_ref[...]
    c_ref[...] = c
    carry_ref[...] = c[tm - 1:tm, :]


def _proj_even(h, g, w, wf_hi, wf_lo, bf, seq, tm=512):
    n, d = h.shape
    n_out = w.shape[1]
    return pl.pallas_call(
        functools.partial(_proj_even_kernel, tiles_per_seq=seq // tm),
        grid=(n // tm,),
        in_specs=[
            pl.BlockSpec((tm, d), lambda i: (i, 0)),
            pl.BlockSpec((1, d), lambda i: (0, 0)),
            pl.BlockSpec((d, n_out), lambda i: (0, 0)),
            pl.BlockSpec((d, LANES), lambda i: (0, 0)),
            pl.BlockSpec((d, LANES), lambda i: (0, 0)),
            pl.BlockSpec((1, LANES), lambda i: (0, 0)),
        ],
        out_specs=[
            pl.BlockSpec((tm, n_out), lambda i: (i, 0)),
            pl.BlockSpec((tm, LANES), lambda i: (i, 0)),
        ],
        out_shape=[
            jax.ShapeDtypeStruct((n, n_out), BF16),
            jax.ShapeDtypeStruct((n, LANES), F32),
        ],
        scratch_shapes=[pltpu.VMEM((1, LANES), F32), pltpu.VMEM((tm, tm), BF16)],
        compiler_params=_params("arbitrary"),
        name="proj_even",
    )(h, g, w, wf_hi, wf_lo, bf)


def _own_lanes(lane, h):
    return lane < HEAD_DIM if h == 0 else lane >= HEAD_DIM


def _causal_kernel(*refs, fox, tq, tk):
    if fox:
        q_ref, k_ref, v_ref, c_ref, o_ref, vaug_t, s_even, s_odd, kaug = refs
    else:
        q_ref, k_ref, v_ref, o_ref, vaug_t, s_even, s_odd = refs
    pair = pl.program_id(1)
    qi = pl.program_id(2)
    seq = v_ref.shape[1]

    @pl.when(qi == 0)
    def _():
        lane = lax.broadcasted_iota(jnp.int32, (tk, LANES), 1)
        feat = lax.broadcasted_iota(jnp.int32, (LANES, tk), 0)

        def fill(t, carry):
            rows = pl.ds(pl.multiple_of(t * tk, tk), tk)
            vt = jnp.transpose(v_ref[0, rows, :].astype(F32))
            if fox:
                kp = k_ref[0, rows, :].astype(F32)
                c = c_ref[0, rows, :]
            for h in range(2):
                vaug_t[h, t] = jnp.where(_own_lanes(feat, h), vt, 1.0).astype(BF16)
                if fox:
                    ch = jnp.sum(jnp.where(lane == 2 * pair + h, c, 0.0), axis=-1, keepdims=True)
                    c1, c2, c3 = _split3(ch * (-LOG2E))
                    base = HEAD_DIM if h == 0 else 0
                    extra = jnp.where(lane == base, c1.astype(F32),
                                      jnp.where(lane == base + 1, c2.astype(F32),
                                                jnp.where(lane == base + 2, c3.astype(F32), 0.0)))
                    kaug[h, rows, :] = jnp.where(_own_lanes(lane, h), kp, extra).astype(BF16)
            return carry

        lax.fori_loop(0, seq // tk, fill, 0)

    lane_q = lax.broadcasted_iota(jnp.int32, (tq, LANES), 1)
    qs = []
    for h in range(2):
        if fox:
            base = HEAD_DIM if h == 0 else 0
            ones = jnp.where((lane_q >= base) & (lane_q < base + 3), 1.0, 0.0)
            qs.append(jnp.where(_own_lanes(lane_q, h), q_ref[0].astype(F32), ones).astype(BF16))
        else:
            qs.append(q_ref[0, :, h * LANES:(h + 1) * LANES])

    key = lax.broadcasted_iota(jnp.int32, (tk, tq), 0)
    qry = lax.broadcasted_iota(jnp.int32, (tk, tq), 1)

    def scores(j, buf):
        start = pl.multiple_of(j * tk, tk)
        for h in range(2):
            if fox:
                kj = kaug[h, pl.ds(start, tk), :]
            else:
                kj = k_ref[0, pl.ds(start, tk), h * LANES:(h + 1) * LANES]
            buf[h] = lax.dot_general(kj, qs[h], NT_DIMS, preferred_element_type=F32)

    def absorb(j, buf, carry, masked):
        new = []
        for h in range(2):
            m, acc = carry[h]
            s = buf[h]
            if masked:
                s = jnp.where(key <= qry, s, NEG)
            m_new = jnp.maximum(m, jnp.max(s, axis=0, keepdims=True))
            p = jnp.exp2(s - m_new)
            acc = jnp.exp2(m - m_new) * acc + _dot(vaug_t[h, j], p.astype(BF16))
            new.append((m_new, acc))
        return tuple(new)

    def finish(carry):
        (_, acc0), (_, acc1) = carry
        feat_q = lax.broadcasted_iota(jnp.int32, (LANES, tq), 0)
        out_t = jnp.where(feat_q < HEAD_DIM, acc0 / acc0[HEAD_DIM:HEAD_DIM + 1, :], acc1 / acc1[0:1, :])
        o_ref[0] = jnp.transpose(out_t).astype(o_ref.dtype)

    def pair_step(t, carry):
        scores(2 * t + 1, s_odd)
        carry = absorb(2 * t, s_even, carry, False)
        scores(2 * t + 2, s_even)
        return absorb(2 * t + 1, s_odd, carry, False)

    init = tuple((jnp.full((1, tq), NEG, F32), jnp.zeros((LANES, tq), F32)) for _ in range(2))
    scores(0, s_even)
    carry = lax.fori_loop(0, qi // 2, pair_step, init)

    @pl.when(qi % 2 == 0)
    def _():
        finish(absorb(qi, s_even, carry, True))

    @pl.when(qi % 2 == 1)
    def _():
        scores(qi, s_odd)
        finish(absorb(qi, s_odd, absorb(qi - 1, s_even, carry, False), True))


def _causal_attention(q_arr, k_arr, v_arr, c_arr, *, q_blk0, k_blk0, v_blk0, n_pairs, tq=512):
    b, s, _ = q_arr.shape
    fox = c_arr is not None
    qk_w = LANES if fox else 2 * LANES
    in_specs = [
        pl.BlockSpec((1, tq, qk_w), lambda bi, p, qi: (bi, qi, q_blk0 + p)),
        pl.BlockSpec((1, s, qk_w), lambda bi, p, qi: (bi, 0, k_blk0 + p)),
        pl.BlockSpec((1, s, LANES), lambda bi, p, qi: (bi, 0, v_blk0 + p)),
    ]
    args = [q_arr, k_arr, v_arr]
    scratch = [pltpu.VMEM((2, s // tq, LANES, tq), BF16), pltpu.VMEM((2, tq, tq), F32), pltpu.VMEM((2, tq, tq), F32)]
    if fox:
        in_specs.append(pl.BlockSpec((1, s, LANES), lambda bi, p, qi: (bi, 0, 0)))
        args.append(c_arr)
        scratch.append(pltpu.VMEM((2, s, LANES), BF16))
    return pl.pallas_call(
        functools.partial(_causal_kernel, fox=fox, tq=tq, tk=tq),
        grid=(b, n_pairs, s // tq),
        in_specs=in_specs,
        out_specs=pl.BlockSpec((1, tq, LANES), lambda bi, p, qi: (bi, qi, p)),
        out_shape=jax.ShapeDtypeStruct((b, s, n_pairs * LANES), BF16),
        scratch_shapes=scratch,
        compiler_params=_params("parallel", "parallel", "arbitrary"),
        name="causal_attention",
    )(*args)


DIL_PAD = BLK * max(d for _, d in DIL_BRANCHES)
DIL_UNROLL = 4


def _dilated_kernel(q_ref, k_ref, v_ref, bias_ref, o_ref, qf, kf, vf, acc_s, m_s, l_s, s_even, s_odd, *, seq):
    qf[...] = q_ref[0].astype(F32)
    kf[0:DIL_PAD, :] = jnp.zeros((DIL_PAD, LANES), F32)
    vf[0:DIL_PAD, :] = jnp.zeros((DIL_PAD, LANES), F32)
    kf[DIL_PAD:, :] = k_ref[0].astype(F32)
    vf[DIL_PAD:, :] = v_ref[0].astype(F32)
    head0 = lax.broadcasted_iota(jnp.int32, (BLK, LANES), 1) < HEAD_DIM
    head0_k = lax.broadcasted_iota(jnp.int32, (2 * BLK, LANES), 1) < HEAD_DIM
    ones0 = jnp.where(head0_k, 1.0, 0.0).astype(BF16)
    ones1 = jnp.where(head0_k, 0.0, 1.0).astype(BF16)

    def place(dil, u):
        bi = u // dil
        q_start = (u % dil) + (dil * BLK) * bi
        return bi, q_start, q_start + (DIL_PAD - dil * BLK)

    def scores(br, dil, g, buf):
        for i in range(DIL_UNROLL):
            bi, q_start, k_start = place(dil, g * DIL_UNROLL + i)
            q2 = qf[pl.ds(q_start, BLK, stride=dil), :]
            q_st = jnp.concatenate([jnp.where(head0, q2, 0.0), jnp.where(head0, 0.0, q2)], axis=0).astype(BF16)
            kb = kf[pl.ds(k_start, 2 * BLK, stride=dil), :].astype(BF16)
            s = lax.dot_general(q_st, kb, NT_DIMS, preferred_element_type=F32)
            buf[i] = s + bias_ref[br, 0, jnp.where(bi == 0, 1, 0)]

    def absorb(br, dil, g, buf, first):
        for i in range(DIL_UNROLL):
            _, q_start, k_start = place(dil, g * DIL_UNROLL + i)
            s = buf[i]
            m = jnp.max(s, axis=-1, keepdims=True)
            p = jnp.exp2(s - m).astype(BF16)
            p_cat = jnp.concatenate([p[:BLK], p[BLK:]], axis=1)
            v2 = vf[pl.ds(k_start, 2 * BLK, stride=dil), :]
            rhs = jnp.concatenate([
                jnp.concatenate([jnp.where(head0_k, v2, 0.0).astype(BF16), ones0], axis=1),
                jnp.concatenate([jnp.where(head0_k, 0.0, v2).astype(BF16), ones1], axis=1)], axis=0)
            acc2 = _dot(p_cat, rhs)
            acc_b, l_b = acc2[:, :LANES], acc2[:, LANES:]
            m_b = jnp.where(head0, jnp.broadcast_to(m[:BLK], (BLK, LANES)), jnp.broadcast_to(m[BLK:], (BLK, LANES)))
            idx = pl.ds(q_start, BLK, stride=dil)
            if first:
                m_s[idx, :] = m_b
                l_s[idx, :] = l_b
                acc_s[idx, :] = acc_b
                continue
            m_old = m_s[idx, :]
            m_new = jnp.maximum(m_old, m_b)
            a_old = jnp.exp2(m_old - m_new)
            a_b = jnp.exp2(m_b - m_new)
            m_s[idx, :] = m_new
            l_s[idx, :] = a_old * l_s[idx, :] + a_b * l_b
            acc_s[idx, :] = a_old * acc_s[idx, :] + a_b * acc_b

    n_groups = seq // (BLK * DIL_UNROLL)
    order = sorted(range(len(DIL_BRANCHES)), key=lambda i: -DIL_BRANCHES[i][1])
    for pos, br in enumerate(order):
        dil = DIL_BRANCHES[br][1]
        first = pos == 0

        def trip(t, carry, br=br, dil=dil, first=first):
            scores(br, dil, 2 * t + 1, s_odd)
            absorb(br, dil, 2 * t, s_even, first)
            scores(br, dil, 2 * t + 2, s_even)
            absorb(br, dil, 2 * t + 1, s_odd, first)
            return carry

        scores(br, dil, 0, s_even)
        lax.fori_loop(0, n_groups // 2 - 1, trip, 0)
        scores(br, dil, n_groups - 1, s_odd)
        absorb(br, dil, n_groups - 2, s_even, first)
        absorb(br, dil, n_groups - 1, s_odd, first)

    o_ref[0] = (acc_s[...] / l_s[...]).astype(o_ref.dtype)


def _dilated_attention(proj, bias, *, q_blk0, k_blk0, v_blk0):
    b, s, _ = proj.shape
    n_pairs = DIL_HEADS // 2
    assert (s // (BLK * DIL_UNROLL)) % 2 == 0
    blk = lambda off: pl.BlockSpec((1, s, LANES), lambda bi, p: (bi, 0, off + p))
    state = [pltpu.VMEM((s, LANES), F32) for _ in range(4)]
    padded = [pltpu.VMEM((DIL_PAD + s, LANES), F32) for _ in range(2)]
    score_bufs = [pltpu.VMEM((DIL_UNROLL, 2 * BLK, 2 * BLK), F32) for _ in range(2)]
    return pl.pallas_call(
        functools.partial(_dilated_kernel, seq=s),
        grid=(b, n_pairs),
        in_specs=[blk(q_blk0), blk(k_blk0), blk(v_blk0),
                  pl.BlockSpec((len(DIL_BRANCHES), 1, 2, 2 * BLK, 2 * BLK), lambda bi, p: (0, p, 0, 0, 0))],
        out_specs=pl.BlockSpec((1, s, LANES), lambda bi, p: (bi, 0, p)),
        out_shape=jax.ShapeDtypeStruct((b, s, n_pairs * LANES), BF16),
        scratch_shapes=[state[0], padded[0], padded[1], state[1], state[2], state[3]] + score_bufs,
        compiler_params=_params("parallel", "parallel"),
        name="dilated_attention",
    )(proj, proj, proj, bias)


def _swa_kernel(q_ref, k_ref, v_ref, bias_ref, sink_ref, o_ref, vaug_t, s_even, s_odd, *, tc):
    ci = pl.program_id(1)
    seq = k_ref.shape[1]
    grp = SWA_Q_HEADS // SWA_KV_HEADS
    n_blk = tc // BLK

    @pl.when(ci == 0)
    def _():
        feat = lax.broadcasted_iota(jnp.int32, (LANES, BLK), 0)

        def fill(t, carry):
            vt = jnp.transpose(v_ref[0, pl.ds(pl.multiple_of(t * BLK, BLK), BLK), :].astype(F32))
            for kv in range(SWA_KV_HEADS):
                vaug_t[kv, t] = jnp.where(_own_lanes(feat, kv), vt, 1.0).astype(BF16)
            return carry

        lax.fori_loop(0, seq // BLK, fill, 0)

    lane_q = lax.broadcasted_iota(jnp.int32, (BLK, LANES), 1)
    feat_o = lax.broadcasted_iota(jnp.int32, (LANES, grp * BLK), 0)

    def blocks(bb):
        own = ci * n_blk + bb
        return jnp.maximum(own - 1, 0), own

    def scores(bb, kv, buf):
        prev, own = blocks(bb)
        k2 = jnp.concatenate([k_ref[0, pl.ds(pl.multiple_of(prev * BLK, BLK), BLK), :],
                              k_ref[0, pl.ds(pl.multiple_of(own * BLK, BLK), BLK), :]], axis=0)
        q4 = jnp.concatenate(
            [jnp.where(_own_lanes(lane_q, kv), q_ref[0, bb * BLK:(bb + 1) * BLK, g * LANES:(g + 1) * LANES].astype(F32),
                       0.0) for g in range(grp)], axis=0).astype(BF16)
        s = lax.dot_general(k2, q4, NT_DIMS, preferred_element_type=F32)
        buf[...] = s + bias_ref[kv, jnp.where(own == 0, 1, 0)]

    def absorb(bb, kv, buf):
        prev, own = blocks(bb)
        s = buf[...]
        sink = sink_ref[kv]
        m = jnp.maximum(jnp.max(s, axis=0, keepdims=True), sink)
        p = jnp.exp2(s - m).astype(BF16)
        v2 = jnp.concatenate([vaug_t[kv, prev], vaug_t[kv, own]], axis=1)
        acc = _dot(v2, p)
        den = (acc[HEAD_DIM:HEAD_DIM + 1, :] if kv == 0 else acc[0:1, :]) + jnp.exp2(sink - m)
        return acc / den

    units = [(bb, kv) for bb in range(n_blk) for kv in range(SWA_KV_HEADS)]
    bufs = (s_even, s_odd)
    scores(*units[0], bufs[0])
    outs = {}
    for n, unit in enumerate(units):
        if n + 1 < len(units):
            scores(*units[n + 1], bufs[(n + 1) % 2])
        outs[unit] = absorb(*unit, bufs[n % 2])
    for bb in range(n_blk):
        out_t = jnp.where(feat_o < HEAD_DIM, outs[(bb, 0)], outs[(bb, 1)])
        for g in range(grp):
            o_ref[0, bb * BLK:(bb + 1) * BLK, g * LANES:(g + 1) * LANES] = jnp.transpose(
                out_t[:, g * BLK:(g + 1) * BLK]).astype(o_ref.dtype)


def _swa_attention(proj, bias, sink_row, *, q_blk0, k_blk0, v_blk0, tc=512):
    b, s, _ = proj.shape
    qw = SWA_Q_HEADS * HEAD_DIM
    grp = SWA_Q_HEADS // SWA_KV_HEADS
    return pl.pallas_call(
        functools.partial(_swa_kernel, tc=tc),
        grid=(b, s // tc),
        in_specs=[
            pl.BlockSpec((1, tc, qw), lambda bi, ci: (bi, ci, q_blk0 * LANES // qw)),
            pl.BlockSpec((1, s, LANES), lambda bi, ci: (bi, 0, k_blk0)),
            pl.BlockSpec((1, s, LANES), lambda bi, ci: (bi, 0, v_blk0)),
            pl.BlockSpec(bias.shape, lambda bi, ci: (0, 0, 0, 0)),
            pl.BlockSpec(sink_row.shape, lambda bi, ci: (0, 0, 0)),
        ],
        out_specs=pl.BlockSpec((1, tc, qw), lambda bi, ci: (bi, ci, 0)),
        out_shape=jax.ShapeDtypeStruct((b, s, qw), BF16),
        scratch_shapes=[pltpu.VMEM((SWA_KV_HEADS, s // BLK, LANES, BLK), BF16),
                        pltpu.VMEM((2 * BLK, grp * BLK), F32), pltpu.VMEM((2 * BLK, grp * BLK), F32)],
        compiler_params=_params("parallel", "arbitrary"),
        name="swa_attention",
    )(proj, proj, proj, bias, sink_row)


def _rope128(x, a, bm, bp):
    return x * a + pltpu.roll(x, LANES - 16, axis=1) * bm + pltpu.roll(x, 16, axis=1) * bp


def _proj_odd_kernel(x_ref, g_ref, w_ref, qn_ref, kvn_ref, wuq_ref, wuk_ref, wuv_ref, a_ref, bm_ref, bp_ref,
                     q_out, k_out, v_out, swa_out):
    xb = _rms(x_ref[...], g_ref[...]).astype(BF16)
    w = w_ref
    c_q = _dot(xb, w[:, 0:MLA_Q_RANK])
    c_kv = _dot(xb, w[:, MLA_Q_RANK:MLA_Q_RANK + MLA_KV_RANK])
    o1 = MLA_Q_RANK + MLA_KV_RANK
    kpe = _dot(xb, w[:, o1:o1 + LANES])
    o2 = o1 + LANES
    swa_out[...] = _dot(xb, w[:, o2:]).astype(swa_out.dtype)
    a, bm, bp = a_ref[...], bm_ref[...], bp_ref[...]
    kpe = _rope128(kpe, a, bm, bp)
    cqn = _rms(c_q, qn_ref[...]).astype(BF16)
    ckvn = _rms(c_kv, kvn_ref[...]).astype(BF16)
    v_out[...] = _dot(ckvn, wuv_ref[...]).astype(v_out.dtype)
    q_raw = _dot(cqn, wuq_ref[...])
    k_raw = _dot(ckvn, wuk_ref[...])
    for h in range(MLA_HEADS):
        sl = slice(h * LANES, (h + 1) * LANES)
        q_out[:, sl] = _rope128(q_raw[:, sl], a, bm, bp).astype(q_out.dtype)
        k_out[:, sl] = (k_raw[:, sl] + kpe).astype(k_out.dtype)


def _proj_odd(h, g, w, qn, kvn, wuq, wuk, wuv, rope_a, rope_bm, rope_bp, seq, tm=512):
    n, d = h.shape
    n_in = w.shape[1]
    n_swa = n_in - (MLA_Q_RANK + MLA_KV_RANK + LANES)
    tiles_per_seq = seq // tm
    full = lambda shape: pl.BlockSpec(shape, lambda i: (0, 0))
    rope = pl.BlockSpec((tm, LANES), lambda i: (i % tiles_per_seq, 0))
    return pl.pallas_call(
        _proj_odd_kernel,
        grid=(n // tm,),
        in_specs=[
            pl.BlockSpec((tm, d), lambda i: (i, 0)), full((1, d)), full(w.shape),
            full(qn.shape), full(kvn.shape), full(wuq.shape), full(wuk.shape), full(wuv.shape),
            rope, rope, rope,
        ],
        out_specs=[
            pl.BlockSpec((tm, MLA_HEADS * LANES), lambda i: (i, 0)),
            pl.BlockSpec((tm, MLA_HEADS * LANES), lambda i: (i, 0)),
            pl.BlockSpec((tm, MLA_HEADS * MLA_V), lambda i: (i, 0)),
            pl.BlockSpec((tm, n_swa), lambda i: (i, 0)),
        ],
        out_shape=[
            jax.ShapeDtypeStruct((n, MLA_HEADS * LANES), BF16),
            jax.ShapeDtypeStruct((n, MLA_HEADS * LANES), BF16),
            jax.ShapeDtypeStruct((n, MLA_HEADS * MLA_V), BF16),
            jax.ShapeDtypeStruct((n, n_swa), BF16),
        ],
        compiler_params=_params("parallel"),
        name="proj_odd",
    )(h, g, w, qn, kvn, wuq, wuk, wuv, rope_a, rope_bm, rope_bp)


MOE_TILE = 1024
ROUTER_TILE = 512


def _out_router_kernel(h_ref, oa_ref, ob_ref, wo_ref, g_ref, wr_hi_ref, wr_lo_ref, br_ref,
                       h_out, hn_out, gates_out, counts_out, pos_col_out, pos_row_out, tri_ref, seen_ref):
    i = pl.program_id(0)
    tm = h_ref.shape[0]
    half = oa_ref.shape[1]
    h = h_ref[...] + _dot(oa_ref[...], wo_ref[0:half, :]) + _dot(ob_ref[...], wo_ref[half:, :])
    h_out[...] = h
    hn = _rms(h, g_ref[...])
    hn_out[...] = hn.astype(hn_out.dtype)
    z = _dot_hi(hn, wr_hi_ref[...], wr_lo_ref[...]) + br_ref[...]
    lane = lax.broadcasted_iota(jnp.int32, z.shape, 1)
    big = jnp.int32(LANES)
    is_grp = (lane >= N_EXPERTS) & (lane < N_EXPERTS + N_GROUPS)
    zg = jnp.where(is_grp, z, -jnp.inf)
    g_max = jnp.max(zg, axis=-1, keepdims=True)
    g_w = 1.0 / jnp.sum(jnp.exp(zg - g_max), axis=-1, keepdims=True)
    g_idx = jnp.min(jnp.where(zg == g_max, lane - N_EXPERTS, big), axis=-1, keepdims=True)
    in_grp = (lane < N_EXPERTS) & ((lane // EXPERTS_PER_GROUP) == g_idx)
    ze = jnp.where(in_grp, z, -jnp.inf)
    v1 = jnp.max(ze, axis=-1, keepdims=True)
    i1 = jnp.min(jnp.where(ze == v1, lane, big), axis=-1, keepdims=True)
    ze2 = jnp.where(lane == i1, -jnp.inf, ze)
    v2 = jnp.max(ze2, axis=-1, keepdims=True)
    i2 = jnp.min(jnp.where(ze2 == v2, lane, big), axis=-1, keepdims=True)
    e2 = jnp.exp(v2 - v1)
    w1 = g_w / (1.0 + e2)
    w2 = g_w * e2 / (1.0 + e2)
    for grp in range(N_GROUPS):
        first = grp * EXPERTS_PER_GROUP
        slot = jnp.where(lane == i1 - first, w1, 0.0) + jnp.where(lane == i2 - first, w2, 0.0)
        gates_out[:, grp * LANES:(grp + 1) * LANES] = jnp.where(lane < EXPERTS_PER_GROUP, slot, 0.0)

    @pl.when(i == 0)
    def _():
        ri = lax.broadcasted_iota(jnp.int32, (tm, tm), 0)
        ci = lax.broadcasted_iota(jnp.int32, (tm, tm), 1)
        tri_ref[...] = jnp.where(ci < ri, 1.0, 0.0).astype(BF16)

    @pl.when(i % (MOE_TILE // tm) == 0)
    def _():
        seen_ref[...] = jnp.zeros_like(seen_ref)

    routed = jnp.where((lane == g_idx) & (lane < N_GROUPS), 1.0, 0.0)
    pos = jnp.where(routed > 0.0, _dot(tri_ref[...], routed.astype(BF16)) + seen_ref[...], -1.0)
    pos_col_out[...] = pos
    pos_row_out[0] = jnp.transpose(pos)[0:8, :]
    here = jnp.sum(routed, axis=0, keepdims=True)
    seen_ref[...] += here
    counts_out[0] = jnp.broadcast_to(here, counts_out.shape[1:])


def _out_router(h, oa, ob, wo, g, wr_hi, wr_lo, br):
    n, d = h.shape
    tm = ROUTER_TILE
    per_tile = MOE_TILE // tm
    half = oa.shape[1]
    full = lambda shape: pl.BlockSpec(shape, lambda i: (0, 0))
    tile = lambda w: pl.BlockSpec((tm, w), lambda i: (i, 0))
    return pl.pallas_call(
        _out_router_kernel,
        grid=(n // tm,),
        in_specs=[tile(d), tile(half), tile(half), full(wo.shape), full((1, d)),
                  full(wr_hi.shape), full(wr_lo.shape), full((1, LANES))],
        out_specs=[tile(d), tile(d), tile(N_GROUPS * LANES), pl.BlockSpec((1, 8, LANES), lambda i: (i, 0, 0)),
                   tile(LANES), pl.BlockSpec((1, 8, tm), lambda i: (i // per_tile, 0, i % per_tile))],
        out_shape=[
            jax.ShapeDtypeStruct((n, d), F32),
            jax.ShapeDtypeStruct((n, d), BF16),
            jax.ShapeDtypeStruct((n, N_GROUPS * LANES), F32),
            jax.ShapeDtypeStruct((n // tm, 8, LANES), F32),
            jax.ShapeDtypeStruct((n, LANES), F32),
            jax.ShapeDtypeStruct((n // MOE_TILE, 8, MOE_TILE), F32),
        ],
        scratch_shapes=[pltpu.VMEM((tm, tm), BF16), pltpu.VMEM((1, LANES), F32)],
        compiler_params=_params("arbitrary"),
        name="out_router",
    )(h, oa, ob, wo, g, wr_hi, wr_lo, br)


MOE_CHUNK = 320
MOE_ROWS = 1024
MOE_ALIGN = 16
MOE_EXPERTS_PER_STEP = 4


def _moe_sorted_tiles(n):
    n_seg = (n // MOE_TILE) * N_GROUPS
    rows = n + n_seg * (MOE_ALIGN - 1) + N_GROUPS * (MOE_CHUNK + MOE_ROWS - 1)
    return -(-rows // MOE_ROWS)


def _moe_tables(counts, n):
    n_rt = _moe_sorted_tiles(n)
    seg_len = (counts + (MOE_ALIGN - 1)) // MOE_ALIGN * MOE_ALIGN
    group_len = seg_len.sum(axis=0)
    group_span = (group_len + MOE_CHUNK + MOE_ROWS - 1) // MOE_ROWS * MOE_ROWS
    group_end = jnp.cumsum(group_span)
    group_start = group_end - group_span
    seg_off = group_start[None, :] + jnp.cumsum(seg_len, axis=0) - seg_len
    rt_start = jnp.arange(n_rt, dtype=jnp.int32) * MOE_ROWS
    rt_group = jnp.minimum(jnp.sum(rt_start[:, None] >= group_end[None, :], axis=1), N_GROUPS - 1)
    rt_valid = jnp.clip(group_start[rt_group] + group_len[rt_group] - rt_start, 0, MOE_ROWS)
    n_used = (group_end[-1] // MOE_ROWS).reshape(1)
    i32 = lambda a: a.astype(jnp.int32)
    return i32(seg_off.reshape(-1)), i32(rt_group), i32(rt_valid), i32(n_used)


def _moe_chunks(count, body):
    def one_chunk(k, carry):
        body(k * MOE_CHUNK, MOE_CHUNK)
        return carry

    lax.fori_loop(0, (count + (MOE_CHUNK - 1)) // MOE_CHUNK, one_chunk, 0)


def _moe_pack_kernel(counts_ref, off_ref, hn_ref, gates_ref, pos_ref, xs_in, xs_ref, xbuf, sem, issued_ref):
    del xs_in
    i = pl.program_id(0)
    d = hn_ref.shape[1]

    @pl.when(i == 0)
    def _():
        issued_ref[0] = 0

    def write(slot):
        return pltpu.make_async_copy(xbuf.at[slot], xs_ref.at[pl.ds(0, MOE_CHUNK)], sem.at[slot])

    lane = lax.broadcasted_iota(jnp.int32, (hn_ref.shape[0], LANES), 1)
    for grp in range(N_GROUPS):
        seg = i * N_GROUPS + grp
        gates = gates_ref[:, grp * LANES:(grp + 1) * LANES]
        g_hi = gates.astype(BF16).astype(F32)
        g_lo = pltpu.roll(gates - g_hi, EXPERTS_PER_GROUP, axis=1)
        gate_cols = jnp.where(lane < EXPERTS_PER_GROUP, g_hi, jnp.where(lane < 2 * EXPERTS_PER_GROUP, g_lo, 0.0))
        gate_cols = gate_cols.astype(BF16)
        pos_row = pos_ref[0, grp:grp + 1, :]

        def chunk(first_pos, rows, seg=seg, gate_cols=gate_cols, pos_row=pos_row):
            n_done = issued_ref[0]
            slot = n_done % 2
            slot_pos = lax.broadcasted_iota(jnp.int32, (rows, 1), 0).astype(F32) + first_pos.astype(F32)
            gather = jnp.where(pos_row == slot_pos, 1.0, 0.0).astype(BF16)
            xbuf[slot, :, 0:d] = _dot(gather, hn_ref[...]).astype(BF16)
            xbuf[slot, :, d:] = _dot(gather, gate_cols).astype(BF16)

            @pl.when(n_done > 0)
            def _():
                write(1 - slot).wait()

            dst = pl.multiple_of(off_ref[seg] + first_pos, MOE_ALIGN)
            pltpu.make_async_copy(xbuf.at[slot], xs_ref.at[pl.ds(dst, MOE_CHUNK)], sem.at[slot]).start()
            issued_ref[0] = n_done + 1

        _moe_chunks(counts_ref[seg], chunk)

    @pl.when((i == pl.num_programs(0) - 1) & (issued_ref[0] > 0))
    def _():
        write((issued_ref[0] - 1) % 2).wait()


def _moe_pack(counts, seg_off, hn, gates, pos_row, n_rt):
    n, d = hn.shape
    rows = n_rt * MOE_ROWS
    grid_spec = pltpu.PrefetchScalarGridSpec(
        num_scalar_prefetch=2,
        grid=(n // MOE_TILE,),
        in_specs=[
            pl.BlockSpec((MOE_TILE, d), lambda i, c, o: (i, 0)),
            pl.BlockSpec((MOE_TILE, N_GROUPS * LANES), lambda i, c, o: (i, 0)),
            pl.BlockSpec((1, 8, MOE_TILE), lambda i, c, o: (i, 0, 0)),
            pl.BlockSpec(memory_space=pl.ANY),
        ],
        out_specs=pl.BlockSpec(memory_space=pl.ANY),
        scratch_shapes=[pltpu.VMEM((2, MOE_CHUNK, d + LANES), BF16), pltpu.SemaphoreType.DMA((2,)),
                        pltpu.SMEM((1,), jnp.int32)],
    )
    return pl.pallas_call(
        _moe_pack_kernel,
        grid_spec=grid_spec,
        out_shape=jax.ShapeDtypeStruct((rows, d + LANES), BF16),
        input_output_aliases={5: 0},
        compiler_params=_params("arbitrary"),
        name="moe_pack",
    )(counts, seg_off, hn, gates, pos_row, jnp.zeros((rows, d + LANES), BF16))


def _moe_experts_kernel(grp_ref, valid_ref, used_ref, xs_ref, wg_ref, wu_ref, wd_ref, ys_ref, acc_ref):
    rt = pl.program_id(0)
    j = pl.program_id(1)
    d = ys_ref.shape[1]

    @pl.when(rt < used_ref[0])
    def _():
        @pl.when(j == 0)
        def _():
            acc_ref[...] = jnp.zeros_like(acc_ref)

        x = xs_ref[:, 0:d]
        gate_cols = xs_ref[:, d:].astype(F32)
        lane = lax.broadcasted_iota(jnp.int32, gate_cols.shape, 1)
        acts = []
        for e in range(MOE_EXPERTS_PER_STEP):
            idx = j * MOE_EXPERTS_PER_STEP + e
            mine = (lane == idx) | (lane == idx + EXPERTS_PER_GROUP)
            gate = jnp.sum(jnp.where(mine, gate_cols, 0.0), axis=-1, keepdims=True)
            a = _dot(x, wg_ref[0, 0, e].astype(BF16))
            u = _dot(x, wu_ref[0, 0, e].astype(BF16))
            acts.append(((a * jax.nn.sigmoid(a)) * u * gate).astype(BF16))
        w_down = wd_ref[0, 0].astype(BF16).reshape(MOE_EXPERTS_PER_STEP * D_EXPERT, d)
        acc_ref[...] += _dot(jnp.concatenate(acts, axis=1), w_down)

        @pl.when(j == pl.num_programs(1) - 1)
        def _():
            row = lax.broadcasted_iota(jnp.int32, (acc_ref.shape[0], 1), 0)
            ys_ref[...] = jnp.where(row < valid_ref[rt], acc_ref[...], 0.0).astype(ys_ref.dtype)

    @pl.when((rt >= used_ref[0]) & (j == pl.num_programs(1) - 1))
    def _():
        ys_ref[...] = jnp.zeros_like(ys_ref)


def _moe_experts(rt_group, rt_valid, n_used, xs, w_gate, w_up, w_down, layer):
    rows = xs.shape[0]
    d = w_gate.shape[-2]
    n_rt = rows // MOE_ROWS
    eps = MOE_EXPERTS_PER_STEP
    n_steps = EXPERTS_PER_GROUP // eps

    def tile_idx(rt, j, grp, valid, used):
        return jnp.minimum(rt, used[0] - 1)

    def w_idx(rt, j, grp, valid, used):
        live = rt < used[0]
        return (layer, grp[tile_idx(rt, j, grp, valid, used)], jnp.where(live, j, n_steps - 1), 0, 0)

    grid_spec = pltpu.PrefetchScalarGridSpec(
        num_scalar_prefetch=3,
        grid=(n_rt, n_steps),
        in_specs=[
            pl.BlockSpec((MOE_ROWS, d + LANES), lambda rt, j, grp, valid, used: (tile_idx(rt, j, grp, valid, used), 0)),
            pl.BlockSpec((1, 1, eps, d, D_EXPERT), w_idx),
            pl.BlockSpec((1, 1, eps, d, D_EXPERT), w_idx),
            pl.BlockSpec((1, 1, eps, D_EXPERT, d), w_idx),
        ],
        out_specs=pl.BlockSpec((MOE_ROWS, d), lambda rt, j, grp, valid, used: (rt, 0)),
        scratch_shapes=[pltpu.VMEM((MOE_ROWS, d), F32)],
    )
    return pl.pallas_call(
        _moe_experts_kernel,
        grid_spec=grid_spec,
        out_shape=jax.ShapeDtypeStruct((rows, d), BF16),
        compiler_params=_params("arbitrary", "arbitrary"),
        name="moe_experts",
    )(rt_group, rt_valid, n_used, xs, w_gate, w_up, w_down)


def _moe_combine_kernel(counts_ref, off_ref, pos_ref, h_ref, fg_ref, ys_ref, o_ref, acc_ref, ybuf, ymore, sem,
                        *, final_norm):
    i = pl.program_id(0)

    def read(grp):
        src = pl.multiple_of(off_ref[i * N_GROUPS + grp], MOE_ALIGN)
        return pltpu.make_async_copy(ys_ref.at[pl.ds(src, MOE_CHUNK)], ybuf.at[pl.ds(grp * MOE_CHUNK, MOE_CHUNK)],
                                     sem.at[grp])

    @pl.when(i == 0)
    def _():
        ybuf[...] = jnp.zeros_like(ybuf)

    for grp in range(N_GROUPS):
        @pl.when(counts_ref[i * N_GROUPS + grp] > 0)
        def _(grp=grp):
            read(grp).start()

    slot_pos = lax.broadcasted_iota(jnp.int32, (1, MOE_CHUNK), 1).astype(F32)
    scatter = jnp.concatenate(
        [jnp.where(pos_ref[:, grp:grp + 1] == slot_pos, 1.0, 0.0).astype(BF16) for grp in range(N_GROUPS)], axis=1)
    for grp in range(N_GROUPS):
        @pl.when(counts_ref[i * N_GROUPS + grp] > 0)
        def _(grp=grp):
            read(grp).wait()

    acc_ref[...] = h_ref[...] + _dot(scatter, ybuf[...])
    for grp in range(N_GROUPS):
        count = counts_ref[i * N_GROUPS + grp]
        pos_col = pos_ref[:, grp:grp + 1]

        def more(k, carry, grp=grp, pos_col=pos_col):
            src = pl.multiple_of(off_ref[i * N_GROUPS + grp] + k * MOE_CHUNK, MOE_ALIGN)
            pltpu.sync_copy(ys_ref.at[pl.ds(src, MOE_CHUNK)], ymore)
            scatter = jnp.where(pos_col == slot_pos + (k * MOE_CHUNK).astype(F32), 1.0, 0.0).astype(BF16)
            acc_ref[...] += _dot(scatter, ymore[...])
            return carry

        lax.fori_loop(1, (count + (MOE_CHUNK - 1)) // MOE_CHUNK, more, 0)

    h = acc_ref[...]
    o_ref[...] = _rms(h, fg_ref[...]) if final_norm else h


def _moe_combine(counts, seg_off, pos_col, h, fg, ys, final_norm):
    n, d = h.shape
    grid_spec = pltpu.PrefetchScalarGridSpec(
        num_scalar_prefetch=2,
        grid=(n // MOE_TILE,),
        in_specs=[
            pl.BlockSpec((MOE_TILE, LANES), lambda i, c, o: (i, 0)),
            pl.BlockSpec((MOE_TILE, d), lambda i, c, o: (i, 0)),
            pl.BlockSpec((1, d), lambda i, c, o: (0, 0)),
            pl.BlockSpec(memory_space=pl.ANY),
        ],
        out_specs=pl.BlockSpec((MOE_TILE, d), lambda i, c, o: (i, 0)),
        scratch_shapes=[pltpu.VMEM((MOE_TILE, d), F32), pltpu.VMEM((N_GROUPS * MOE_CHUNK, d), BF16),
                        pltpu.VMEM((MOE_CHUNK, d), BF16), pltpu.SemaphoreType.DMA((N_GROUPS,))],
    )
    return pl.pallas_call(
        functools.partial(_moe_combine_kernel, final_norm=final_norm),
        grid_spec=grid_spec,
        out_shape=jax.ShapeDtypeStruct((n, d), F32),
        compiler_params=_params("arbitrary"),
        name="moe_combine",
    )(counts, seg_off, pos_col, h, fg, ys)


def _alibi_slopes(n):
    return jnp.exp2(-8.0 * jnp.arange(1, n + 1, dtype=F32) / n)


def _band_bias(slopes, max_steps, step_dist):
    steps = (jnp.arange(BLK)[:, None] + BLK) - jnp.arange(2 * BLK)[None, :]
    in_band = (steps >= 0) & (steps <= max_steps)
    dist = (steps * step_dist).astype(F32)
    return jnp.where(in_band[None], -slopes.astype(F32)[:, None, None] * dist[None], NEG)


def _rope_tables(s):
    inv = ROPE_THETA ** (-jnp.arange(0, MLA_ROPE, 2, dtype=F32) / MLA_ROPE)
    ang = jnp.arange(s, dtype=F32)[:, None] * inv[None, :]
    cos, sin = jnp.cos(ang), jnp.sin(ang)
    half = MLA_ROPE // 2
    zeros_tail = jnp.zeros((s, LANES - MLA_NOPE - MLA_ROPE), F32)
    a = jnp.concatenate([jnp.ones((s, MLA_NOPE), F32), cos, cos, zeros_tail], axis=1)
    zeros_nope = jnp.zeros((s, MLA_NOPE), F32)
    zeros_half = jnp.zeros((s, half), F32)
    bm = jnp.concatenate([zeros_nope, -sin, zeros_half, zeros_tail], axis=1)
    bp = jnp.concatenate([zeros_nope, zeros_half, sin, zeros_tail], axis=1)
    return a, bm, bp


def _swa_head_order(w, axis):
    grp = SWA_Q_HEADS // SWA_KV_HEADS
    shape = w.shape
    w = w.reshape(shape[:axis] + (SWA_KV_HEADS, grp, HEAD_DIM) + shape[axis + 1:])
    return jnp.swapaxes(w, axis, axis + 1).reshape(shape)


def _pad_cols(w, width):
    return jnp.pad(w, ((0, 0), (0, width - w.shape[1])))


def _router_weights(w_group, b_group, w_router, b_router):
    w = _pad_cols(jnp.concatenate([w_router, w_group], axis=1), LANES)
    b = _pad_cols(jnp.concatenate([b_router, b_group])[None, :], LANES)
    hi, lo = _hi_lo(w)
    return hi, lo, b


def kernel(x, attn_norm, ffn_norm, final_norm, e_w_in, e_b_f, e_w_out, o_w_in, o_q_norm, o_kv_norm, o_w_uq,
           o_w_ukv, o_sinks, o_w_out, moe_w_group, moe_b_group, moe_w_router, moe_b_router, moe_w_gate,
           moe_w_up, moe_w_down):
    b, s, d = x.shape
    n = b * s
    depth = attn_norm.shape[0]
    assert s % (BLK * DIL_BRANCHES[-1][1]) == 0 and d == D_MODEL
    h = x.reshape(n, d)

    dil_slopes = _alibi_slopes(DIL_HEADS)
    dil_bias = jnp.stack([_band_bias(dil_slopes, w // dl, dl) for (w, dl) in DIL_BRANCHES]) * LOG2E
    own_half = jnp.arange(2 * BLK) >= BLK
    dil_bias = jnp.stack([dil_bias, jnp.where(own_half, dil_bias, NEG)], axis=2)
    dil_bias = dil_bias.reshape(len(DIL_BRANCHES), DIL_HEADS // 2, 2, 2, BLK, 2 * BLK)
    dil_bias = dil_bias.transpose(0, 1, 3, 2, 4, 5).reshape(len(DIL_BRANCHES), DIL_HEADS // 2, 2, 2 * BLK, 2 * BLK)
    swa_grp = SWA_Q_HEADS // SWA_KV_HEADS
    swa_bias = _band_bias(_alibi_slopes(SWA_Q_HEADS), SWA_WINDOW - 1, 1) * LOG2E
    swa_bias = swa_bias.reshape(SWA_KV_HEADS, swa_grp, BLK, 2 * BLK).transpose(0, 3, 1, 2)
    swa_bias = swa_bias.reshape(SWA_KV_HEADS, 2 * BLK, swa_grp * BLK)
    swa_bias = jnp.stack([swa_bias, jnp.where(own_half[None, :, None], swa_bias, NEG)], axis=1)
    rope_a, rope_bm, rope_bp = _rope_tables(s)

    for layer in range(depth):
        i = layer // 2
        g_attn = attn_norm[layer][None, :]
        if layer % 2 == 0:
            w_in = e_w_in[i]
            hq = FOX_HEADS * HEAD_DIM
            scale = HEAD_DIM ** -0.5
            cols = [w_in[:, 0:hq] * (scale * LOG2E), w_in[:, hq:2 * hq], w_in[:, 2 * hq:3 * hq]]
            o = 3 * hq + FOX_HEADS
            cols += [w_in[:, o:o + hq] * (scale * LOG2E), w_in[:, o + hq:o + 2 * hq], w_in[:, o + 2 * hq:o + 3 * hq]]
            w_main = jnp.concatenate(cols, axis=1).astype(BF16)
            wf_hi, wf_lo = _hi_lo(_pad_cols(w_in[:, 3 * hq:o], LANES))
            b_f = _pad_cols(e_b_f[i][None, :], LANES)
            proj, c = _proj_even(h, g_attn, w_main, wf_hi, wf_lo, b_f, s)
            proj = proj.reshape(b, s, -1)
            o_a = _causal_attention(proj, proj, proj, c.reshape(b, s, LANES), q_blk0=0, k_blk0=4, v_blk0=8,
                                    n_pairs=FOX_HEADS // 2)
            o_b = _dilated_attention(proj, dil_bias, q_blk0=12, k_blk0=16, v_blk0=20)
            w_out = e_w_out[i].astype(BF16)
        else:
            w_in = o_w_in[i]
            o1 = MLA_Q_RANK + MLA_KV_RANK
            o2 = o1 + MLA_ROPE
            sq = SWA_Q_HEADS * HEAD_DIM
            kpe_cols = jnp.pad(w_in[:, o1:o2], ((0, 0), (MLA_NOPE, LANES - MLA_NOPE - MLA_ROPE)))
            w_main = jnp.concatenate(
                [w_in[:, :o1], kpe_cols, _swa_head_order(w_in[:, o2:o2 + sq] * (HEAD_DIM ** -0.5 * LOG2E), axis=1),
                 w_in[:, o2 + sq:]],
                axis=1).astype(BF16)
            dq = MLA_NOPE + MLA_ROPE
            wuq = o_w_uq[i].reshape(MLA_Q_RANK, MLA_HEADS, dq) * (dq ** -0.5 * LOG2E)
            wuq = jnp.pad(wuq, ((0, 0), (0, 0), (0, LANES - dq))).reshape(MLA_Q_RANK, MLA_HEADS * LANES)
            wukv = o_w_ukv[i].reshape(MLA_KV_RANK, MLA_HEADS, MLA_NOPE + MLA_V)
            wuk = jnp.pad(wukv[:, :, :MLA_NOPE], ((0, 0), (0, 0), (0, LANES - MLA_NOPE)))
            wuk = wuk.reshape(MLA_KV_RANK, MLA_HEADS * LANES)
            wuv = wukv[:, :, MLA_NOPE:].reshape(MLA_KV_RANK, MLA_HEADS * MLA_V)
            q_full, k_full, v_mla, swa = _proj_odd(
                h, g_attn, w_main, o_q_norm[i][None, :], o_kv_norm[i][None, :], wuq.astype(BF16),
                wuk.astype(BF16), wuv.astype(BF16), rope_a, rope_bm, rope_bp, s)
            o_a = _causal_attention(q_full.reshape(b, s, -1), k_full.reshape(b, s, -1), v_mla.reshape(b, s, -1),
                                    None, q_blk0=0, k_blk0=0, v_blk0=0, n_pairs=MLA_HEADS // 2)
            grp = SWA_Q_HEADS // SWA_KV_HEADS
            sink_row = jnp.repeat(o_sinks[i].reshape(SWA_KV_HEADS, grp) * LOG2E, BLK, axis=1)[:, None, :]
            o_b = _swa_attention(swa.reshape(b, s, -1), swa_bias, sink_row, q_blk0=0, k_blk0=4, v_blk0=5)
            half = MLA_HEADS * MLA_V
            w_out = jnp.concatenate([o_w_out[i][:half], _swa_head_order(o_w_out[i][half:], axis=0)], axis=0)
            w_out = w_out.astype(BF16)

        wr_hi, wr_lo, b_r = _router_weights(moe_w_group[layer], moe_b_group[layer], moe_w_router[layer],
                                            moe_b_router[layer])
        h, hn, gates, counts, pos_col, pos_row = _out_router(
            h, o_a.reshape(n, -1), o_b.reshape(n, -1), w_out, ffn_norm[layer][None, :], wr_hi, wr_lo, b_r)
        counts = counts[:, 0, :N_GROUPS].reshape(n // MOE_TILE, MOE_TILE // ROUTER_TILE, N_GROUPS).sum(axis=1)
        counts = counts.astype(jnp.int32)
        seg_off, rt_group, rt_valid, n_used = _moe_tables(counts, n)
        counts = counts.reshape(-1)
        xs = _moe_pack(counts, seg_off, hn, gates, pos_row, _moe_sorted_tiles(n))
        ys = _moe_experts(rt_group, rt_valid, n_used, xs, moe_w_gate, moe_w_up, moe_w_down, layer)
        h = _moe_combine(counts, seg_off, pos_col, h, final_norm[None, :], ys, final_norm=layer == depth - 1)
    return h.reshape(b, s, d)
```

```python
import functools

import jax
import jax.numpy as jnp
from jax import lax
from jax.experimental import pallas as pl
from jax.experimental.pallas import tpu as pltpu

F32 = jnp.float32
BF16 = jnp.bfloat16

D_MODEL = 1024
HEAD_DIM = 64
BLK = 128
NEG = -1e30
RMS_EPS = 1e-6
FOX_HEADS = 8
DIL_HEADS = 8
DIL_BRANCHES = ((128, 1), (512, 4), (2048, 16))
MLA_HEADS = 8
MLA_Q_RANK = 384
MLA_KV_RANK = 256
MLA_NOPE = 64
MLA_ROPE = 32
MLA_V = 64
ROPE_THETA = 10000.0
SWA_Q_HEADS = 8
SWA_KV_HEADS = 2
SWA_WINDOW = 128
N_GROUPS = 4
EXPERTS_PER_GROUP = 8
N_EXPERTS = N_GROUPS * EXPERTS_PER_GROUP
D_EXPERT = 256

LANES = 128
VMEM_LIMIT = 56 * 1024 * 1024

NT_DIMS = (((1,), (1,)), ((), ()))
LOG2E = 1.4426950408889634


def _params(*sem):
    return pltpu.CompilerParams(dimension_semantics=sem, vmem_limit_bytes=VMEM_LIMIT)


def _dot(a, b):
    return jnp.dot(a, b, preferred_element_type=F32)


def _rms(x, g):
    return x * lax.rsqrt(jnp.mean(x * x, axis=-1, keepdims=True) + RMS_EPS) * g


def _hi_lo(w):
    hi = w.astype(BF16)
    return hi, (w - hi.astype(F32)).astype(BF16)


def _dot_hi(x, w_hi, w_lo):
    x_hi = x.astype(BF16)
    x_lo = (x - x_hi.astype(F32)).astype(BF16)
    return _dot(x_hi, w_hi) + (_dot(x_hi, w_lo) + _dot(x_lo, w_hi))


def _split3(x):
    x1 = x.astype(BF16)
    r1 = x - x1.astype(F32)
    x2 = r1.astype(BF16)
    x3 = (r1 - x2.astype(F32)).astype(BF16)
    return x1, x2, x3


def _proj_even_kernel(x_ref, g_ref, w_ref, wf_hi_ref, wf_lo_ref, bf_ref, out_ref, c_ref, carry_ref, tri_ref, *,
                      tiles_per_seq):
    xn = _rms(x_ref[...], g_ref[...])
    xb = xn.astype(BF16)
    n_out = out_ref.shape[1]
    for c in range(0, n_out, 512):
        out_ref[:, c:c + 512] = _dot(xb, w_ref[:, c:c + 512]).astype(out_ref.dtype)
    z = _dot_hi(xn, wf_hi_ref[...], wf_lo_ref[...]) + bf_ref[...]
    logf = jnp.minimum(z, 0.0) - jnp.log1p(jnp.exp(-jnp.abs(z)))

    @pl.when(pl.program_id(0) % tiles_per_seq == 0)
    def _():
        carry_ref[...] = jnp.zeros_like(carry_ref)

    tm = logf.shape[0]

    @pl.when(pl.program_id(0) == 0)
    def _():
        ri = lax.broadcasted_iota(jnp.int32, (tm, tm), 0)
        ci = lax.broadcasted_iota(jnp.int32, (tm, tm), 1)
        tri_ref[...] = jnp.where(ci <= ri, 1.0, 0.0).astype(BF16)

    lower = tri_ref[...]
    l1, l2, l3 = _split3(logf)
    c = (_dot(lower, l1) + (_dot(lower, l2) + _dot(lower, l3))) + carry_ref[...]
    c_ref[...] = c
    carry_ref[...] = c[tm - 1:tm, :]


def _proj_even(h, g, w, wf_hi, wf_lo, bf, seq, tm=1024):
    n, d = h.shape
    n_out = w.shape[1]
    return pl.pallas_call(
        functools.partial(_proj_even_kernel, tiles_per_seq=seq // tm),
        grid=(n // tm,),
        in_specs=[
            pl.BlockSpec((tm, d), lambda i: (i, 0)),
            pl.BlockSpec((1, d), lambda i: (0, 0)),
            pl.BlockSpec((d, n_out), lambda i: (0, 0)),
            pl.BlockSpec((d, LANES), lambda i: (0, 0)),
            pl.BlockSpec((d, LANES), lambda i: (0, 0)),
            pl.BlockSpec((1, LANES), lambda i: (0, 0)),
        ],
        out_specs=[
            pl.BlockSpec((tm, n_out), lambda i: (i, 0)),
            pl.BlockSpec((tm, LANES), lambda i: (i, 0)),
        ],
        out_shape=[
            jax.ShapeDtypeStruct((n, n_out), BF16),
            jax.ShapeDtypeStruct((n, LANES), F32),
        ],
        scratch_shapes=[pltpu.VMEM((1, LANES), F32), pltpu.VMEM((tm, tm), BF16)],
        compiler_params=_params("arbitrary"),
        name="proj_even",
    )(h, g, w, wf_hi, wf_lo, bf)


def _own_lanes(lane, h):
    return lane < HEAD_DIM if h == 0 else lane >= HEAD_DIM


def _causal_kernel(*refs, fox, tq, tk):
    if fox:
        q_ref, k_ref, v_ref, c_ref, o_ref, vaug_t, s_even, s_odd, kaug = refs
    else:
        q_ref, k_ref, v_ref, o_ref, vaug_t, s_even, s_odd = refs
    pair = pl.program_id(1)
    qi = pl.program_id(2)
    seq = v_ref.shape[1]

    @pl.when(qi == 0)
    def _():
        lane = lax.broadcasted_iota(jnp.int32, (tk, LANES), 1)
        feat = lax.broadcasted_iota(jnp.int32, (LANES, tk), 0)

        def fill(t, carry):
            rows = pl.ds(pl.multiple_of(t * tk, tk), tk)
            vt = jnp.transpose(v_ref[0, rows, :].astype(F32))
            if fox:
                kp = k_ref[0, rows, :].astype(F32)
                c = c_ref[0, rows, :]
            for h in range(2):
                vaug_t[h, t] = jnp.where(_own_lanes(feat, h), vt, 1.0).astype(BF16)
                if fox:
                    ch = jnp.sum(jnp.where(lane == 2 * pair + h, c, 0.0), axis=-1, keepdims=True)
                    c1, c2, c3 = _split3(ch * (-LOG2E))
                    base = HEAD_DIM if h == 0 else 0
                    extra = jnp.where(lane == base, c1.astype(F32),
                                      jnp.where(lane == base + 1, c2.astype(F32),
                                                jnp.where(lane == base + 2, c3.astype(F32), 0.0)))
                    kaug[h, rows, :] = jnp.where(_own_lanes(lane, h), kp, extra).astype(BF16)
            return carry

        lax.fori_loop(0, seq // tk, fill, 0)

    lane_q = lax.broadcasted_iota(jnp.int32, (tq, LANES), 1)
    qs = []
    for h in range(2):
        if fox:
            base = HEAD_DIM if h == 0 else 0
            ones = jnp.where((lane_q >= base) & (lane_q < base + 3), 1.0, 0.0)
            qs.append(jnp.where(_own_lanes(lane_q, h), q_ref[0].astype(F32), ones).astype(BF16))
        else:
            qs.append(q_ref[0, :, h * LANES:(h + 1) * LANES])

    key = lax.broadcasted_iota(jnp.int32, (tk, tq), 0)
    qry = lax.broadcasted_iota(jnp.int32, (tk, tq), 1)

    def scores(j, buf):
        start = pl.multiple_of(j * tk, tk)
        for h in range(2):
            if fox:
                kj = kaug[h, pl.ds(start, tk), :]
            else:
                kj = k_ref[0, pl.ds(start, tk), h * LANES:(h + 1) * LANES]
            buf[h] = lax.dot_general(kj, qs[h], NT_DIMS, preferred_element_type=F32)

    def absorb(j, buf, carry, masked):
        new = []
        for h in range(2):
            m, acc = carry[h]
            s = buf[h]
            if masked:
                s = jnp.where(key <= qry, s, NEG)
            m_new = jnp.maximum(m, jnp.max(s, axis=0, keepdims=True))
            p = jnp.exp2(s - m_new)
            acc = jnp.exp2(m - m_new) * acc + _dot(vaug_t[h, j], p.astype(BF16))
            new.append((m_new, acc))
        return tuple(new)

    def finish(carry):
        (_, acc0), (_, acc1) = carry
        feat_q = lax.broadcasted_iota(jnp.int32, (LANES, tq), 0)
        out_t = jnp.where(feat_q < HEAD_DIM, acc0 / acc0[HEAD_DIM:HEAD_DIM + 1, :], acc1 / acc1[0:1, :])
        o_ref[0] = jnp.transpose(out_t).astype(o_ref.dtype)

    def pair_step(t, carry):
        scores(2 * t + 1, s_odd)
        carry = absorb(2 * t, s_even, carry, False)
        scores(2 * t + 2, s_even)
        return absorb(2 * t + 1, s_odd, carry, False)

    init = tuple((jnp.full((1, tq), NEG, F32), jnp.zeros((LANES, tq), F32)) for _ in range(2))
    scores(0, s_even)
    carry = lax.fori_loop(0, qi // 2, pair_step, init)

    @pl.when(qi % 2 == 0)
    def _():
        finish(absorb(qi, s_even, carry, True))

    @pl.when(qi % 2 == 1)
    def _():
        scores(qi, s_odd)
        finish(absorb(qi, s_odd, absorb(qi - 1, s_even, carry, False), True))


def _causal_attention(q_arr, k_arr, v_arr, c_arr, *, q_blk0, k_blk0, v_blk0, n_pairs, tq=512):
    b, s, _ = q_arr.shape
    fox = c_arr is not None
    qk_w = LANES if fox else 2 * LANES
    in_specs = [
        pl.BlockSpec((1, tq, qk_w), lambda bi, p, qi: (bi, qi, q_blk0 + p)),
        pl.BlockSpec((1, s, qk_w), lambda bi, p, qi: (bi, 0, k_blk0 + p)),
        pl.BlockSpec((1, s, LANES), lambda bi, p, qi: (bi, 0, v_blk0 + p)),
    ]
    args = [q_arr, k_arr, v_arr]
    scratch = [pltpu.VMEM((2, s // tq, LANES, tq), BF16), pltpu.VMEM((2, tq, tq), F32), pltpu.VMEM((2, tq, tq), F32)]
    if fox:
        in_specs.append(pl.BlockSpec((1, s, LANES), lambda bi, p, qi: (bi, 0, 0)))
        args.append(c_arr)
        scratch.append(pltpu.VMEM((2, s, LANES), BF16))
    return pl.pallas_call(
        functools.partial(_causal_kernel, fox=fox, tq=tq, tk=tq),
        grid=(b, n_pairs, s // tq),
        in_specs=in_specs,
        out_specs=pl.BlockSpec((1, tq, LANES), lambda bi, p, qi: (bi, qi, p)),
        out_shape=jax.ShapeDtypeStruct((b, s, n_pairs * LANES), BF16),
        scratch_shapes=scratch,
        compiler_params=_params("parallel", "parallel", "arbitrary"),
        name="causal_attention",
    )(*args)


DIL_PAD = BLK * max(d for _, d in DIL_BRANCHES)
DIL_UNROLL = 4


def _dilated_kernel(q_ref, k_ref, v_ref, bias_ref, o_ref, qf, kf, vf, acc_s, m_s, l_s, s_even, s_odd, *, seq):
    qf[...] = q_ref[0].astype(F32)
    kf[0:DIL_PAD, :] = jnp.zeros((DIL_PAD, LANES), F32)
    vf[0:DIL_PAD, :] = jnp.zeros((DIL_PAD, LANES), F32)
    kf[DIL_PAD:, :] = k_ref[0].astype(F32)
    vf[DIL_PAD:, :] = v_ref[0].astype(F32)
    head0 = lax.broadcasted_iota(jnp.int32, (BLK, LANES), 1) < HEAD_DIM
    head0_k = lax.broadcasted_iota(jnp.int32, (2 * BLK, LANES), 1) < HEAD_DIM
    ones0 = jnp.where(head0_k, 1.0, 0.0).astype(BF16)
    ones1 = jnp.where(head0_k, 0.0, 1.0).astype(BF16)

    def place(dil, u):
        bi = u // dil
        q_start = (u % dil) + (dil * BLK) * bi
        return bi, q_start, q_start + (DIL_PAD - dil * BLK)

    def scores(br, dil, g, buf):
        for i in range(DIL_UNROLL):
            bi, q_start, k_start = place(dil, g * DIL_UNROLL + i)
            q2 = qf[pl.ds(q_start, BLK, stride=dil), :]
            q_st = jnp.concatenate([jnp.where(head0, q2, 0.0), jnp.where(head0, 0.0, q2)], axis=0).astype(BF16)
            kb = kf[pl.ds(k_start, 2 * BLK, stride=dil), :].astype(BF16)
            s = lax.dot_general(q_st, kb, NT_DIMS, preferred_element_type=F32)
            buf[i] = s + bias_ref[br, 0, jnp.where(bi == 0, 1, 0)]

    def absorb(br, dil, g, buf, first):
        for i in range(DIL_UNROLL):
            _, q_start, k_start = place(dil, g * DIL_UNROLL + i)
            s = buf[i]
            m = jnp.max(s, axis=-1, keepdims=True)
            p = jnp.exp2(s - m).astype(BF16)
            p_cat = jnp.concatenate([p[:BLK], p[BLK:]], axis=1)
            v2 = vf[pl.ds(k_start, 2 * BLK, stride=dil), :]
            rhs = jnp.concatenate([
                jnp.concatenate([jnp.where(head0_k, v2, 0.0).astype(BF16), ones0], axis=1),
                jnp.concatenate([jnp.where(head0_k, 0.0, v2).astype(BF16), ones1], axis=1)], axis=0)
            acc2 = _dot(p_cat, rhs)
            acc_b, l_b = acc2[:, :LANES], acc2[:, LANES:]
            m_b = jnp.where(head0, jnp.broadcast_to(m[:BLK], (BLK, LANES)), jnp.broadcast_to(m[BLK:], (BLK, LANES)))
            idx = pl.ds(q_start, BLK, stride=dil)
            if first:
                m_s[idx, :] = m_b
                l_s[idx, :] = l_b
                acc_s[idx, :] = acc_b
                continue
            m_old = m_s[idx, :]
            m_new = jnp.maximum(m_old, m_b)
            a_old = jnp.exp2(m_old - m_new)
            a_b = jnp.exp2(m_b - m_new)
            m_s[idx, :] = m_new
            l_s[idx, :] = a_old * l_s[idx, :] + a_b * l_b
            acc_s[idx, :] = a_old * acc_s[idx, :] + a_b * acc_b

    n_groups = seq // (BLK * DIL_UNROLL)
    order = sorted(range(len(DIL_BRANCHES)), key=lambda i: -DIL_BRANCHES[i][1])
    for pos, br in enumerate(order):
        dil = DIL_BRANCHES[br][1]
        first = pos == 0

        def trip(t, carry, br=br, dil=dil, first=first):
            scores(br, dil, 2 * t + 1, s_odd)
            absorb(br, dil, 2 * t, s_even, first)
            scores(br, dil, 2 * t + 2, s_even)
            absorb(br, dil, 2 * t + 1, s_odd, first)
            return carry

        scores(br, dil, 0, s_even)
        lax.fori_loop(0, n_groups // 2 - 1, trip, 0)
        scores(br, dil, n_groups - 1, s_odd)
        absorb(br, dil, n_groups - 2, s_even, first)
        absorb(br, dil, n_groups - 1, s_odd, first)

    o_ref[0] = (acc_s[...] / l_s[...]).astype(o_ref.dtype)


def _dilated_attention(proj, bias, *, q_blk0, k_blk0, v_blk0):
    b, s, _ = proj.shape
    n_pairs = DIL_HEADS // 2
    assert (s // (BLK * DIL_UNROLL)) % 2 == 0
    blk = lambda off: pl.BlockSpec((1, s, LANES), lambda bi, p: (bi, 0, off + p))
    state = [pltpu.VMEM((s, LANES), F32) for _ in range(4)]
    padded = [pltpu.VMEM((DIL_PAD + s, LANES), F32) for _ in range(2)]
    score_bufs = [pltpu.VMEM((DIL_UNROLL, 2 * BLK, 2 * BLK), F32) for _ in range(2)]
    return pl.pallas_call(
        functools.partial(_dilated_kernel, seq=s),
        grid=(b, n_pairs),
        in_specs=[blk(q_blk0), blk(k_blk0), blk(v_blk0),
                  pl.BlockSpec((len(DIL_BRANCHES), 1, 2, 2 * BLK, 2 * BLK), lambda bi, p: (0, p, 0, 0, 0))],
        out_specs=pl.BlockSpec((1, s, LANES), lambda bi, p: (bi, 0, p)),
        out_shape=jax.ShapeDtypeStruct((b, s, n_pairs * LANES), BF16),
        scratch_shapes=[state[0], padded[0], padded[1], state[1], state[2], state[3]] + score_bufs,
        compiler_params=_params("parallel", "parallel"),
        name="dilated_attention",
    )(proj, proj, proj, bias)


def _swa_kernel(q_ref, k_ref, v_ref, bias_ref, sink_ref, o_ref, vaug_t, s_even, s_odd, *, tc):
    ci = pl.program_id(1)
    seq = k_ref.shape[1]
    grp = SWA_Q_HEADS // SWA_KV_HEADS
    n_blk = tc // BLK

    @pl.when(ci == 0)
    def _():
        feat = lax.broadcasted_iota(jnp.int32, (LANES, BLK), 0)

        def fill(t, carry):
            vt = jnp.transpose(v_ref[0, pl.ds(pl.multiple_of(t * BLK, BLK), BLK), :].astype(F32))
            for kv in range(SWA_KV_HEADS):
                vaug_t[kv, t] = jnp.where(_own_lanes(feat, kv), vt, 1.0).astype(BF16)
            return carry

        lax.fori_loop(0, seq // BLK, fill, 0)

    lane_q = lax.broadcasted_iota(jnp.int32, (BLK, LANES), 1)
    feat_o = lax.broadcasted_iota(jnp.int32, (LANES, grp * BLK), 0)

    def blocks(bb):
        own = ci * n_blk + bb
        return jnp.maximum(own - 1, 0), own

    def scores(bb, kv, buf):
        prev, own = blocks(bb)
        k2 = jnp.concatenate([k_ref[0, pl.ds(pl.multiple_of(prev * BLK, BLK), BLK), :],
                              k_ref[0, pl.ds(pl.multiple_of(own * BLK, BLK), BLK), :]], axis=0)
        q4 = jnp.concatenate(
            [jnp.where(_own_lanes(lane_q, kv), q_ref[0, bb * BLK:(bb + 1) * BLK, g * LANES:(g + 1) * LANES].astype(F32),
                       0.0) for g in range(grp)], axis=0).astype(BF16)
        s = lax.dot_general(k2, q4, NT_DIMS, preferred_element_type=F32)
        buf[...] = s + bias_ref[kv, jnp.where(own == 0, 1, 0)]

    def absorb(bb, kv, buf):
        prev, own = blocks(bb)
        s = buf[...]
        sink = sink_ref[kv]
        m = jnp.maximum(jnp.max(s, axis=0, keepdims=True), sink)
        p = jnp.exp2(s - m).astype(BF16)
        v2 = jnp.concatenate([vaug_t[kv, prev], vaug_t[kv, own]], axis=1)
        acc = _dot(v2, p)
        den = (acc[HEAD_DIM:HEAD_DIM + 1, :] if kv == 0 else acc[0:1, :]) + jnp.exp2(sink - m)
        return acc / den

    units = [(bb, kv) for bb in range(n_blk) for kv in range(SWA_KV_HEADS)]
    bufs = (s_even, s_odd)
    scores(*units[0], bufs[0])
    outs = {}
    for n, unit in enumerate(units):
        if n + 1 < len(units):
            scores(*units[n + 1], bufs[(n + 1) % 2])
        outs[unit] = absorb(*unit, bufs[n % 2])
    for bb in range(n_blk):
        out_t = jnp.where(feat_o < HEAD_DIM, outs[(bb, 0)], outs[(bb, 1)])
        for g in range(grp):
            o_ref[0, bb * BLK:(bb + 1) * BLK, g * LANES:(g + 1) * LANES] = jnp.transpose(
                out_t[:, g * BLK:(g + 1) * BLK]).astype(o_ref.dtype)


def _swa_attention(proj, bias, sink_row, *, q_blk0, k_blk0, v_blk0, tc=512):
    b, s, _ = proj.shape
    qw = SWA_Q_HEADS * HEAD_DIM
    grp = SWA_Q_HEADS // SWA_KV_HEADS
    return pl.pallas_call(
        functools.partial(_swa_kernel, tc=tc),
        grid=(b, s // tc),
        in_specs=[
            pl.BlockSpec((1, tc, qw), lambda bi, ci: (bi, ci, q_blk0 * LANES // qw)),
            pl.BlockSpec((1, s, LANES), lambda bi, ci: (bi, 0, k_blk0)),
            pl.BlockSpec((1, s, LANES), lambda bi, ci: (bi, 0, v_blk0)),
            pl.BlockSpec(bias.shape, lambda bi, ci: (0, 0, 0, 0)),
            pl.BlockSpec(sink_row.shape, lambda bi, ci: (0, 0, 0)),
        ],
        out_specs=pl.BlockSpec((1, tc, qw), lambda bi, ci: (bi, ci, 0)),
        out_shape=jax.ShapeDtypeStruct((b, s, qw), BF16),
        scratch_shapes=[pltpu.VMEM((SWA_KV_HEADS, s // BLK, LANES, BLK), BF16),
                        pltpu.VMEM((2 * BLK, grp * BLK), F32), pltpu.VMEM((2 * BLK, grp * BLK), F32)],
        compiler_params=_params("parallel", "arbitrary"),
        name="swa_attention",
    )(proj, proj, proj, bias, sink_row)


def _rope128(x, a, bm, bp):
    return x * a + pltpu.roll(x, LANES - 16, axis=1) * bm + pltpu.roll(x, 16, axis=1) * bp


def _proj_odd_kernel(x_ref, g_ref, w_ref, qn_ref, kvn_ref, wuq_ref, wuk_ref, wuv_ref, a_ref, bm_ref, bp_ref,
                     q_out, k_out, v_out, swa_out):
    xb = _rms(x_ref[...], g_ref[...]).astype(BF16)
    w = w_ref
    c_q = _dot(xb, w[:, 0:MLA_Q_RANK])
    c_kv = _dot(xb, w[:, MLA_Q_RANK:MLA_Q_RANK + MLA_KV_RANK])
    o1 = MLA_Q_RANK + MLA_KV_RANK
    kpe = _dot(xb, w[:, o1:o1 + LANES])
    o2 = o1 + LANES
    swa_out[...] = _dot(xb, w[:, o2:]).astype(swa_out.dtype)
    a, bm, bp = a_ref[...], bm_ref[...], bp_ref[...]
    kpe = _rope128(kpe, a, bm, bp)
    cqn = _rms(c_q, qn_ref[...]).astype(BF16)
    ckvn = _rms(c_kv, kvn_ref[...]).astype(BF16)
    v_out[...] = _dot(ckvn, wuv_ref[...]).astype(v_out.dtype)
    q_raw = _dot(cqn, wuq_ref[...])
    k_raw = _dot(ckvn, wuk_ref[...])
    for h in range(MLA_HEADS):
        sl = slice(h * LANES, (h + 1) * LANES)
        q_out[:, sl] = _rope128(q_raw[:, sl], a, bm, bp).astype(q_out.dtype)
        k_out[:, sl] = (k_raw[:, sl] + kpe).astype(k_out.dtype)


def _proj_odd(h, g, w, qn, kvn, wuq, wuk, wuv, rope_a, rope_bm, rope_bp, seq, tm=1024):
    n, d = h.shape
    n_in = w.shape[1]
    n_swa = n_in - (MLA_Q_RANK + MLA_KV_RANK + LANES)
    tiles_per_seq = seq // tm
    full = lambda shape: pl.BlockSpec(shape, lambda i: (0, 0))
    rope = pl.BlockSpec((tm, LANES), lambda i: (i % tiles_per_seq, 0))
    return pl.pallas_call(
        _proj_odd_kernel,
        grid=(n // tm,),
        in_specs=[
            pl.BlockSpec((tm, d), lambda i: (i, 0)), full((1, d)), full(w.shape),
            full(qn.shape), full(kvn.shape), full(wuq.shape), full(wuk.shape), full(wuv.shape),
            rope, rope, rope,
        ],
        out_specs=[
            pl.BlockSpec((tm, MLA_HEADS * LANES), lambda i: (i, 0)),
            pl.BlockSpec((tm, MLA_HEADS * LANES), lambda i: (i, 0)),
            pl.BlockSpec((tm, MLA_HEADS * MLA_V), lambda i: (i, 0)),
            pl.BlockSpec((tm, n_swa), lambda i: (i, 0)),
        ],
        out_shape=[
            jax.ShapeDtypeStruct((n, MLA_HEADS * LANES), BF16),
            jax.ShapeDtypeStruct((n, MLA_HEADS * LANES), BF16),
            jax.ShapeDtypeStruct((n, MLA_HEADS * MLA_V), BF16),
            jax.ShapeDtypeStruct((n, n_swa), BF16),
        ],
        compiler_params=_params("parallel"),
        name="proj_odd",
    )(h, g, w, qn, kvn, wuq, wuk, wuv, rope_a, rope_bm, rope_bp)


MOE_TILE = 1024
ROUTER_TILE = 1024


def _out_router_kernel(h_ref, oa_ref, ob_ref, wo_ref, g_ref, wr_hi_ref, wr_lo_ref, br_ref,
                       h_out, hn_out, gates_out, counts_out, pos_col_out, pos_row_out, tri_ref, seen_ref):
    i = pl.program_id(0)
    tm = h_ref.shape[0]
    half = oa_ref.shape[1]
    h = h_ref[...] + _dot(oa_ref[...], wo_ref[0:half, :]) + _dot(ob_ref[...], wo_ref[half:, :])
    h_out[...] = h
    hn = _rms(h, g_ref[...])
    hn_out[...] = hn.astype(hn_out.dtype)
    z = _dot_hi(hn, wr_hi_ref[...], wr_lo_ref[...]) + br_ref[...]
    lane = lax.broadcasted_iota(jnp.int32, z.shape, 1)
    big = jnp.int32(LANES)
    is_grp = (lane >= N_EXPERTS) & (lane < N_EXPERTS + N_GROUPS)
    zg = jnp.where(is_grp, z, -jnp.inf)
    g_max = jnp.max(zg, axis=-1, keepdims=True)
    g_w = 1.0 / jnp.sum(jnp.exp(zg - g_max), axis=-1, keepdims=True)
    g_idx = jnp.min(jnp.where(zg == g_max, lane - N_EXPERTS, big), axis=-1, keepdims=True)
    in_grp = (lane < N_EXPERTS) & ((lane // EXPERTS_PER_GROUP) == g_idx)
    ze = jnp.where(in_grp, z, -jnp.inf)
    v1 = jnp.max(ze, axis=-1, keepdims=True)
    i1 = jnp.min(jnp.where(ze == v1, lane, big), axis=-1, keepdims=True)
    ze2 = jnp.where(lane == i1, -jnp.inf, ze)
    v2 = jnp.max(ze2, axis=-1, keepdims=True)
    i2 = jnp.min(jnp.where(ze2 == v2, lane, big), axis=-1, keepdims=True)
    e2 = jnp.exp(v2 - v1)
    w1 = g_w / (1.0 + e2)
    w2 = g_w * e2 / (1.0 + e2)
    for grp in range(N_GROUPS):
        first = grp * EXPERTS_PER_GROUP
        slot = jnp.where(lane == i1 - first, w1, 0.0) + jnp.where(lane == i2 - first, w2, 0.0)
        gates_out[:, grp * LANES:(grp + 1) * LANES] = jnp.where(lane < EXPERTS_PER_GROUP, slot, 0.0)

    @pl.when(i == 0)
    def _():
        ri = lax.broadcasted_iota(jnp.int32, (tm, tm), 0)
        ci = lax.broadcasted_iota(jnp.int32, (tm, tm), 1)
        tri_ref[...] = jnp.where(ci < ri, 1.0, 0.0).astype(BF16)

    @pl.when(i % (MOE_TILE // tm) == 0)
    def _():
        seen_ref[...] = jnp.zeros_like(seen_ref)

    routed = jnp.where((lane == g_idx) & (lane < N_GROUPS), 1.0, 0.0)
    pos = jnp.where(routed > 0.0, _dot(tri_ref[...], routed.astype(BF16)) + seen_ref[...], -1.0)
    pos_col_out[...] = pos
    pos_row_out[0] = jnp.transpose(pos)[0:8, :]
    here = jnp.sum(routed, axis=0, keepdims=True)
    seen_ref[...] += here
    counts_out[0] = jnp.broadcast_to(here, counts_out.shape[1:])


def _out_router(h, oa, ob, wo, g, wr_hi, wr_lo, br):
    n, d = h.shape
    tm = ROUTER_TILE
    per_tile = MOE_TILE // tm
    half = oa.shape[1]
    full = lambda shape: pl.BlockSpec(shape, lambda i: (0, 0))
    tile = lambda w: pl.BlockSpec((tm, w), lambda i: (i, 0))
    return pl.pallas_call(
        _out_router_kernel,
        grid=(n // tm,),
        in_specs=[tile(d), tile(half), tile(half), full(wo.shape), full((1, d)),
                  full(wr_hi.shape), full(wr_lo.shape), full((1, LANES))],
        out_specs=[tile(d), tile(d), tile(N_GROUPS * LANES), pl.BlockSpec((1, 8, LANES), lambda i: (i, 0, 0)),
                   tile(LANES), pl.BlockSpec((1, 8, tm), lambda i: (i // per_tile, 0, i % per_tile))],
        out_shape=[
            jax.ShapeDtypeStruct((n, d), F32),
            jax.ShapeDtypeStruct((n, d), BF16),
            jax.ShapeDtypeStruct((n, N_GROUPS * LANES), F32),
            jax.ShapeDtypeStruct((n // tm, 8, LANES), F32),
            jax.ShapeDtypeStruct((n, LANES), F32),
            jax.ShapeDtypeStruct((n // MOE_TILE, 8, MOE_TILE), F32),
        ],
        scratch_shapes=[pltpu.VMEM((tm, tm), BF16), pltpu.VMEM((1, LANES), F32)],
        compiler_params=_params("arbitrary"),
        name="out_router",
    )(h, oa, ob, wo, g, wr_hi, wr_lo, br)


MOE_CHUNK = 320
MOE_ROWS = 1024
MOE_ALIGN = 16
MOE_EXPERTS_PER_STEP = 4


def _moe_sorted_tiles(n):
    n_seg = (n // MOE_TILE) * N_GROUPS
    rows = n + n_seg * (MOE_ALIGN - 1) + N_GROUPS * (MOE_CHUNK + MOE_ROWS - 1)
    return -(-rows // MOE_ROWS)


def _moe_tables(counts, n):
    n_rt = _moe_sorted_tiles(n)
    seg_len = (counts + (MOE_ALIGN - 1)) // MOE_ALIGN * MOE_ALIGN
    group_len = seg_len.sum(axis=0)
    group_span = (group_len + MOE_CHUNK + MOE_ROWS - 1) // MOE_ROWS * MOE_ROWS
    group_end = jnp.cumsum(group_span)
    group_start = group_end - group_span
    seg_off = group_start[None, :] + jnp.cumsum(seg_len, axis=0) - seg_len
    rt_start = jnp.arange(n_rt, dtype=jnp.int32) * MOE_ROWS
    rt_group = jnp.minimum(jnp.sum(rt_start[:, None] >= group_end[None, :], axis=1), N_GROUPS - 1)
    rt_valid = jnp.clip(group_start[rt_group] + group_len[rt_group] - rt_start, 0, MOE_ROWS)
    n_used = (group_end[-1] // MOE_ROWS).reshape(1)
    i32 = lambda a: a.astype(jnp.int32)
    return i32(seg_off.reshape(-1)), i32(rt_group), i32(rt_valid), i32(n_used)


def _moe_chunks(count, body):
    def one_chunk(k, carry):
        body(k * MOE_CHUNK, MOE_CHUNK)
        return carry

    lax.fori_loop(0, (count + (MOE_CHUNK - 1)) // MOE_CHUNK, one_chunk, 0)


def _moe_pack_kernel(counts_ref, off_ref, hn_ref, gates_ref, pos_ref, xs_in, xs_ref, xbuf, sem, issued_ref):
    del xs_in
    i = pl.program_id(0)
    d = hn_ref.shape[1]

    @pl.when(i == 0)
    def _():
        issued_ref[0] = 0

    def write(slot):
        return pltpu.make_async_copy(xbuf.at[slot], xs_ref.at[pl.ds(0, MOE_CHUNK)], sem.at[slot])

    lane = lax.broadcasted_iota(jnp.int32, (hn_ref.shape[0], LANES), 1)
    for grp in range(N_GROUPS):
        seg = i * N_GROUPS + grp
        gates = gates_ref[:, grp * LANES:(grp + 1) * LANES]
        g_hi = gates.astype(BF16).astype(F32)
        g_lo = pltpu.roll(gates - g_hi, EXPERTS_PER_GROUP, axis=1)
        gate_cols = jnp.where(lane < EXPERTS_PER_GROUP, g_hi, jnp.where(lane < 2 * EXPERTS_PER_GROUP, g_lo, 0.0))
        gate_cols = gate_cols.astype(BF16)
        pos_row = pos_ref[0, grp:grp + 1, :]

        def chunk(first_pos, rows, seg=seg, gate_cols=gate_cols, pos_row=pos_row):
            n_done = issued_ref[0]
            slot = n_done % 2
            slot_pos = lax.broadcasted_iota(jnp.int32, (rows, 1), 0).astype(F32) + first_pos.astype(F32)
            gather = jnp.where(pos_row == slot_pos, 1.0, 0.0).astype(BF16)
            xbuf[slot, :, 0:d] = _dot(gather, hn_ref[...]).astype(BF16)
            xbuf[slot, :, d:] = _dot(gather, gate_cols).astype(BF16)

            @pl.when(n_done > 0)
            def _():
                write(1 - slot).wait()

            dst = pl.multiple_of(off_ref[seg] + first_pos, MOE_ALIGN)
            pltpu.make_async_copy(xbuf.at[slot], xs_ref.at[pl.ds(dst, MOE_CHUNK)], sem.at[slot]).start()
            issued_ref[0] = n_done + 1

        _moe_chunks(counts_ref[seg], chunk)

    @pl.when((i == pl.num_programs(0) - 1) & (issued_ref[0] > 0))
    def _():
        write((issued_ref[0] - 1) % 2).wait()


def _moe_pack(counts, seg_off, hn, gates, pos_row, n_rt):
    n, d = hn.shape
    rows = n_rt * MOE_ROWS
    grid_spec = pltpu.PrefetchScalarGridSpec(
        num_scalar_prefetch=2,
        grid=(n // MOE_TILE,),
        in_specs=[
            pl.BlockSpec((MOE_TILE, d), lambda i, c, o: (i, 0)),
            pl.BlockSpec((MOE_TILE, N_GROUPS * LANES), lambda i, c, o: (i, 0)),
            pl.BlockSpec((1, 8, MOE_TILE), lambda i, c, o: (i, 0, 0)),
            pl.BlockSpec(memory_space=pl.ANY),
        ],
        out_specs=pl.BlockSpec(memory_space=pl.ANY),
        scratch_shapes=[pltpu.VMEM((2, MOE_CHUNK, d + LANES), BF16), pltpu.SemaphoreType.DMA((2,)),
                        pltpu.SMEM((1,), jnp.int32)],
    )
    return pl.pallas_call(
        _moe_pack_kernel,
        grid_spec=grid_spec,
        out_shape=jax.ShapeDtypeStruct((rows, d + LANES), BF16),
        input_output_aliases={5: 0},
        compiler_params=_params("arbitrary"),
        name="moe_pack",
    )(counts, seg_off, hn, gates, pos_row, jnp.zeros((rows, d + LANES), BF16))


def _moe_experts_kernel(grp_ref, valid_ref, used_ref, xs_ref, wg_ref, wu_ref, wd_ref, ys_ref, acc_ref):
    rt = pl.program_id(0)
    j = pl.program_id(1)
    d = ys_ref.shape[1]

    @pl.when(rt < used_ref[0])
    def _():
        @pl.when(j == 0)
        def _():
            acc_ref[...] = jnp.zeros_like(acc_ref)

        x = xs_ref[:, 0:d]
        gate_cols = xs_ref[:, d:].astype(F32)
        lane = lax.broadcasted_iota(jnp.int32, gate_cols.shape, 1)
        acts = []
        for e in range(MOE_EXPERTS_PER_STEP):
            idx = j * MOE_EXPERTS_PER_STEP + e
            mine = (lane == idx) | (lane == idx + EXPERTS_PER_GROUP)
            gate = jnp.sum(jnp.where(mine, gate_cols, 0.0), axis=-1, keepdims=True)
            a = _dot(x, wg_ref[0, 0, e].astype(BF16))
            u = _dot(x, wu_ref[0, 0, e].astype(BF16))
            acts.append(((a * jax.nn.sigmoid(a)) * u * gate).astype(BF16))
        w_down = wd_ref[0, 0].astype(BF16).reshape(MOE_EXPERTS_PER_STEP * D_EXPERT, d)
        acc_ref[...] += _dot(jnp.concatenate(acts, axis=1), w_down)

        @pl.when(j == pl.num_programs(1) - 1)
        def _():
            row = lax.broadcasted_iota(jnp.int32, (acc_ref.shape[0], 1), 0)
            ys_ref[...] = jnp.where(row < valid_ref[rt], acc_ref[...], 0.0).astype(ys_ref.dtype)

    @pl.when((rt >= used_ref[0]) & (j == pl.num_programs(1) - 1))
    def _():
        ys_ref[...] = jnp.zeros_like(ys_ref)


def _moe_experts(rt_group, rt_valid, n_used, xs, w_gate, w_up, w_down, layer):
    rows = xs.shape[0]
    d = w_gate.shape[-2]
    n_rt = rows // MOE_ROWS
    eps = MOE_EXPERTS_PER_STEP
    n_steps = EXPERTS_PER_GROUP // eps

    def tile_idx(rt, j, grp, valid, used):
        return jnp.minimum(rt, used[0] - 1)

    def w_idx(rt, j, grp, valid, used):
        live = rt < used[0]
        return (layer, grp[tile_idx(rt, j, grp, valid, used)], jnp.where(live, j, n_steps - 1), 0, 0)

    grid_spec = pltpu.PrefetchScalarGridSpec(
        num_scalar_prefetch=3,
        grid=(n_rt, n_steps),
        in_specs=[
            pl.BlockSpec((MOE_ROWS, d + LANES), lambda rt, j, grp, valid, used: (tile_idx(rt, j, grp, valid, used), 0)),
            pl.BlockSpec((1, 1, eps, d, D_EXPERT), w_idx),
            pl.BlockSpec((1, 1, eps, d, D_EXPERT), w_idx),
            pl.BlockSpec((1, 1, eps, D_EXPERT, d), w_idx),
        ],
        out_specs=pl.BlockSpec((MOE_ROWS, d), lambda rt, j, grp, valid, used: (rt, 0)),
        scratch_shapes=[pltpu.VMEM((MOE_ROWS, d), F32)],
    )
    return pl.pallas_call(
        _moe_experts_kernel,
        grid_spec=grid_spec,
        out_shape=jax.ShapeDtypeStruct((rows, d), BF16),
        compiler_params=_params("arbitrary", "arbitrary"),
        name="moe_experts",
    )(rt_group, rt_valid, n_used, xs, w_gate, w_up, w_down)


def _moe_combine_kernel(counts_ref, off_ref, pos_ref, h_ref, fg_ref, ys_ref, o_ref, ybuf, ymore, sem, *, final_norm):
    i = pl.program_id(0)

    def read(tile, grp):
        src = pl.multiple_of(off_ref[tile * N_GROUPS + grp], MOE_ALIGN)
        slot = tile % 2
        return pltpu.make_async_copy(ys_ref.at[pl.ds(src, MOE_CHUNK)],
                                     ybuf.at[slot, pl.ds(grp * MOE_CHUNK, MOE_CHUNK)], sem.at[slot, grp])

    def fetch(tile):
        for grp in range(N_GROUPS):
            @pl.when(counts_ref[tile * N_GROUPS + grp] > 0)
            def _(grp=grp):
                read(tile, grp).start()

    @pl.when(i == 0)
    def _():
        ybuf[...] = jnp.zeros_like(ybuf)
        fetch(i)

    @pl.when(i + 1 < pl.num_programs(0))
    def _():
        fetch(i + 1)

    slot_pos = lax.broadcasted_iota(jnp.int32, (1, MOE_CHUNK), 1).astype(F32)
    scatter = jnp.concatenate(
        [jnp.where(pos_ref[:, grp:grp + 1] == slot_pos, 1.0, 0.0).astype(BF16) for grp in range(N_GROUPS)], axis=1)
    for grp in range(N_GROUPS):
        @pl.when(counts_ref[i * N_GROUPS + grp] > 0)
        def _(grp=grp):
            read(i, grp).wait()

    o_ref[...] = h_ref[...] + _dot(scatter, ybuf[i % 2])
    for grp in range(N_GROUPS):
        count = counts_ref[i * N_GROUPS + grp]
        pos_col = pos_ref[:, grp:grp + 1]

        def more(k, carry, grp=grp, pos_col=pos_col):
            src = pl.multiple_of(off_ref[i * N_GROUPS + grp] + k * MOE_CHUNK, MOE_ALIGN)
            pltpu.sync_copy(ys_ref.at[pl.ds(src, MOE_CHUNK)], ymore)
            scatter = jnp.where(pos_col == slot_pos + (k * MOE_CHUNK).astype(F32), 1.0, 0.0).astype(BF16)
            o_ref[...] += _dot(scatter, ymore[...])
            return carry

        lax.fori_loop(1, (count + (MOE_CHUNK - 1)) // MOE_CHUNK, more, 0)

    if final_norm:
        o_ref[...] = _rms(o_ref[...], fg_ref[...])


def _moe_combine(counts, seg_off, pos_col, h, fg, ys, final_norm):
    n, d = h.shape
    grid_spec = pltpu.PrefetchScalarGridSpec(
        num_scalar_prefetch=2,
        grid=(n // MOE_TILE,),
        in_specs=[
            pl.BlockSpec((MOE_TILE, LANES), lambda i, c, o: (i, 0)),
            pl.BlockSpec((MOE_TILE, d), lambda i, c, o: (i, 0)),
            pl.BlockSpec((1, d), lambda i, c, o: (0, 0)),
            pl.BlockSpec(memory_space=pl.ANY),
        ],
        out_specs=pl.BlockSpec((MOE_TILE, d), lambda i, c, o: (i, 0)),
        scratch_shapes=[pltpu.VMEM((2, N_GROUPS * MOE_CHUNK, d), BF16), pltpu.VMEM((MOE_CHUNK, d), BF16),
                        pltpu.SemaphoreType.DMA((2, N_GROUPS))],
    )
    return pl.pallas_call(
        functools.partial(_moe_combine_kernel, final_norm=final_norm),
        grid_spec=grid_spec,
        out_shape=jax.ShapeDtypeStruct((n, d), F32),
        compiler_params=_params("arbitrary"),
        name="moe_combine",
    )(counts, seg_off, pos_col, h, fg, ys)


def _alibi_slopes(n):
    return jnp.exp2(-8.0 * jnp.arange(1, n + 1, dtype=F32) / n)


def _band_bias(slopes, max_steps, step_dist):
    steps = (jnp.arange(BLK)[:, None] + BLK) - jnp.arange(2 * BLK)[None, :]
    in_band = (steps >= 0) & (steps <= max_steps)
    dist = (steps * step_dist).astype(F32)
    return jnp.where(in_band[None], -slopes.astype(F32)[:, None, None] * dist[None], NEG)


def _rope_tables(s):
    inv = ROPE_THETA ** (-jnp.arange(0, MLA_ROPE, 2, dtype=F32) / MLA_ROPE)
    ang = jnp.arange(s, dtype=F32)[:, None] * inv[None, :]
    cos, sin = jnp.cos(ang), jnp.sin(ang)
    half = MLA_ROPE // 2
    zeros_tail = jnp.zeros((s, LANES - MLA_NOPE - MLA_ROPE), F32)
    a = jnp.concatenate([jnp.ones((s, MLA_NOPE), F32), cos, cos, zeros_tail], axis=1)
    zeros_nope = jnp.zeros((s, MLA_NOPE), F32)
    zeros_half = jnp.zeros((s, half), F32)
    bm = jnp.concatenate([zeros_nope, -sin, zeros_half, zeros_tail], axis=1)
    bp = jnp.concatenate([zeros_nope, zeros_half, sin, zeros_tail], axis=1)
    return a, bm, bp


def _swa_head_order(w, axis):
    grp = SWA_Q_HEADS // SWA_KV_HEADS
    shape = w.shape
    w = w.reshape(shape[:axis] + (SWA_KV_HEADS, grp, HEAD_DIM) + shape[axis + 1:])
    return jnp.swapaxes(w, axis, axis + 1).reshape(shape)


def _pad_cols(w, width):
    return jnp.pad(w, ((0, 0), (0, width - w.shape[1])))


def _router_weights(w_group, b_group, w_router, b_router):
    w = _pad_cols(jnp.concatenate([w_router, w_group], axis=1), LANES)
    b = _pad_cols(jnp.concatenate([b_router, b_group])[None, :], LANES)
    hi, lo = _hi_lo(w)
    return hi, lo, b


def kernel(x, attn_norm, ffn_norm, final_norm, e_w_in, e_b_f, e_w_out, o_w_in, o_q_norm, o_kv_norm, o_w_uq,
           o_w_ukv, o_sinks, o_w_out, moe_w_group, moe_b_group, moe_w_router, moe_b_router, moe_w_gate,
           moe_w_up, moe_w_down):
    b, s, d = x.shape
    n = b * s
    depth = attn_norm.shape[0]
    assert s % (BLK * DIL_BRANCHES[-1][1]) == 0 and d == D_MODEL
    h = x.reshape(n, d)

    dil_slopes = _alibi_slopes(DIL_HEADS)
    dil_bias = jnp.stack([_band_bias(dil_slopes, w // dl, dl) for (w, dl) in DIL_BRANCHES]) * LOG2E
    own_half = jnp.arange(2 * BLK) >= BLK
    dil_bias = jnp.stack([dil_bias, jnp.where(own_half, dil_bias, NEG)], axis=2)
    dil_bias = dil_bias.reshape(len(DIL_BRANCHES), DIL_HEADS // 2, 2, 2, BLK, 2 * BLK)
    dil_bias = dil_bias.transpose(0, 1, 3, 2, 4, 5).reshape(len(DIL_BRANCHES), DIL_HEADS // 2, 2, 2 * BLK, 2 * BLK)
    swa_grp = SWA_Q_HEADS // SWA_KV_HEADS
    swa_bias = _band_bias(_alibi_slopes(SWA_Q_HEADS), SWA_WINDOW - 1, 1) * LOG2E
    swa_bias = swa_bias.reshape(SWA_KV_HEADS, swa_grp, BLK, 2 * BLK).transpose(0, 3, 1, 2)
    swa_bias = swa_bias.reshape(SWA_KV_HEADS, 2 * BLK, swa_grp * BLK)
    swa_bias = jnp.stack([swa_bias, jnp.where(own_half[None, :, None], swa_bias, NEG)], axis=1)
    rope_a, rope_bm, rope_bp = _rope_tables(s)

    for layer in range(depth):
        i = layer // 2
        g_attn = attn_norm[layer][None, :]
        if layer % 2 == 0:
            w_in = e_w_in[i]
            hq = FOX_HEADS * HEAD_DIM
            scale = HEAD_DIM ** -0.5
            cols = [w_in[:, 0:hq] * (scale * LOG2E), w_in[:, hq:2 * hq], w_in[:, 2 * hq:3 * hq]]
            o = 3 * hq + FOX_HEADS
            cols += [w_in[:, o:o + hq] * (scale * LOG2E), w_in[:, o + hq:o + 2 * hq], w_in[:, o + 2 * hq:o + 3 * hq]]
            w_main = jnp.concatenate(cols, axis=1).astype(BF16)
            wf_hi, wf_lo = _hi_lo(_pad_cols(w_in[:, 3 * hq:o], LANES))
            b_f = _pad_cols(e_b_f[i][None, :], LANES)
            proj, c = _proj_even(h, g_attn, w_main, wf_hi, wf_lo, b_f, s)
            proj = proj.reshape(b, s, -1)
            o_a = _causal_attention(proj, proj, proj, c.reshape(b, s, LANES), q_blk0=0, k_blk0=4, v_blk0=8,
                                    n_pairs=FOX_HEADS // 2)
            o_b = _dilated_attention(proj, dil_bias, q_blk0=12, k_blk0=16, v_blk0=20)
            w_out = e_w_out[i].astype(BF16)
        else:
            w_in = o_w_in[i]
            o1 = MLA_Q_RANK + MLA_KV_RANK
            o2 = o1 + MLA_ROPE
            sq = SWA_Q_HEADS * HEAD_DIM
            kpe_cols = jnp.pad(w_in[:, o1:o2], ((0, 0), (MLA_NOPE, LANES - MLA_NOPE - MLA_ROPE)))
            w_main = jnp.concatenate(
                [w_in[:, :o1], kpe_cols, _swa_head_order(w_in[:, o2:o2 + sq] * (HEAD_DIM ** -0.5 * LOG2E), axis=1),
                 w_in[:, o2 + sq:]],
                axis=1).astype(BF16)
            dq = MLA_NOPE + MLA_ROPE
            wuq = o_w_uq[i].reshape(MLA_Q_RANK, MLA_HEADS, dq) * (dq ** -0.5 * LOG2E)
            wuq = jnp.pad(wuq, ((0, 0), (0, 0), (0, LANES - dq))).reshape(MLA_Q_RANK, MLA_HEADS * LANES)
            wukv = o_w_ukv[i].reshape(MLA_KV_RANK, MLA_HEADS, MLA_NOPE + MLA_V)
            wuk = jnp.pad(wukv[:, :, :MLA_NOPE], ((0, 0), (0, 0), (0, LANES - MLA_NOPE)))
            wuk = wuk.reshape(MLA_KV_RANK, MLA_HEADS * LANES)
            wuv = wukv[:, :, MLA_NOPE:].reshape(MLA_KV_RANK, MLA_HEADS * MLA_V)
            q_full, k_full, v_mla, swa = _proj_odd(
                h, g_attn, w_main, o_q_norm[i][None, :], o_kv_norm[i][None, :], wuq.astype(BF16),
                wuk.astype(BF16), wuv.astype(BF16), rope_a, rope_bm, rope_bp, s)
            o_a = _causal_attention(q_full.reshape(b, s, -1), k_full.reshape(b, s, -1), v_mla.reshape(b, s, -1),
                                    None, q_blk0=0, k_blk0=0, v_blk0=0, n_pairs=MLA_HEADS // 2)
            grp = SWA_Q_HEADS // SWA_KV_HEADS
            sink_row = jnp.repeat(o_sinks[i].reshape(SWA_KV_HEADS, grp) * LOG2E, BLK, axis=1)[:, None, :]
            o_b = _swa_attention(swa.reshape(b, s, -1), swa_bias, sink_row, q_blk0=0, k_blk0=4, v_blk0=5)
            half = MLA_HEADS * MLA_V
            w_out = jnp.concatenate([o_w_out[i][:half], _swa_head_order(o_w_out[i][half:], axis=0)], axis=0)
            w_out = w_out.astype(BF16)

        wr_hi, wr_lo, b_r = _router_weights(moe_w_group[layer], moe_b_group[layer], moe_w_router[layer],
                                            moe_b_router[layer])
        h, hn, gates, counts, pos_col, pos_row = _out_router(
            h, o_a.reshape(n, -1), o_b.reshape(n, -1), w_out, ffn_norm[layer][None, :], wr_hi, wr_lo, b_r)
        counts = counts[:, 0, :N_GROUPS].reshape(n // MOE_TILE, MOE_TILE // ROUTER_TILE, N_GROUPS).sum(axis=1)
        counts = counts.astype(jnp.int32)
        seg_off, rt_group, rt_valid, n_used = _moe_tables(counts, n)
        counts = counts.reshape(-1)
        xs = _moe_pack(counts, seg_off, hn, gates, pos_row, _moe_sorted_tiles(n))
        ys = _moe_experts(rt_group, rt_valid, n_used, xs, moe_w_gate, moe_w_up, moe_w_down, layer)
        h = _moe_combine(counts, seg_off, pos_col, h, final_norm[None, :], ys, final_norm=layer == depth - 1)
    return h.reshape(b, s, d)
```

```python
import functools

import jax
import jax.numpy as jnp
from jax import lax
from jax.experimental import pallas as pl
from jax.experimental.pallas import tpu as pltpu

F32 = jnp.float32
BF16 = jnp.bfloat16

D_MODEL = 1024
HEAD_DIM = 64
BLK = 128
NEG = -1e30
RMS_EPS = 1e-6
FOX_HEADS = 8
DIL_HEADS = 8
DIL_BRANCHES = ((128, 1), (512, 4), (2048, 16))
MLA_HEADS = 8
MLA_Q_RANK = 384
MLA_KV_RANK = 256
MLA_NOPE = 64
MLA_ROPE = 32
MLA_V = 64
ROPE_THETA = 10000.0
SWA_Q_HEADS = 8
SWA_KV_HEADS = 2
SWA_WINDOW = 128
N_GROUPS = 4
EXPERTS_PER_GROUP = 8
N_EXPERTS = N_GROUPS * EXPERTS_PER_GROUP
D_EXPERT = 256

LANES = 128
MXU_TILE = 256
VMEM_LIMIT = 56 * 1024 * 1024

NT_DIMS = (((1,), (1,)), ((), ()))
LOG2E = 1.4426950408889634


def _params(*sem):
    return pltpu.CompilerParams(dimension_semantics=sem, vmem_limit_bytes=VMEM_LIMIT)


def _dot(a, b):
    return jnp.dot(a, b, preferred_element_type=F32)


def _rms(x, g):
    return x * lax.rsqrt(jnp.mean(x * x, axis=-1, keepdims=True) + RMS_EPS) * g


def _hi_lo(w):
    hi = w.astype(BF16)
    return hi, (w - hi.astype(F32)).astype(BF16)


def _dot_hi(x, w_hi, w_lo):
    x_hi = x.astype(BF16)
    x_lo = (x - x_hi.astype(F32)).astype(BF16)
    n = w_hi.shape[1]
    both = _dot(x_hi, jnp.concatenate([w_hi, w_lo], axis=1))
    return both[:, :n] + (both[:, n:] + _dot(x_lo, w_hi))


def _split3(x):
    x1 = x.astype(BF16)
    r1 = x - x1.astype(F32)
    x2 = r1.astype(BF16)
    x3 = (r1 - x2.astype(F32)).astype(BF16)
    return x1, x2, x3


def _proj_even_kernel(x_ref, g_ref, w_ref, wf_hi_ref, wf_lo_ref, bf_ref, out_ref, c_ref, carry_ref, tri_ref, *,
                      tiles_per_seq):
    xn = _rms(x_ref[...], g_ref[...])
    xb = xn.astype(BF16)
    n_out = out_ref.shape[1]
    for c in range(0, n_out, 512):
        out_ref[:, c:c + 512] = _dot(xb, w_ref[:, c:c + 512]).astype(out_ref.dtype)
    z = _dot_hi(xn, wf_hi_ref[...], wf_lo_ref[...]) + bf_ref[...]
    logf = jnp.minimum(z, 0.0) - jnp.log1p(jnp.exp(-jnp.abs(z)))

    @pl.when(pl.program_id(0) % tiles_per_seq == 0)
    def _():
        carry_ref[...] = jnp.zeros_like(carry_ref)

    sub = tri_ref.shape[0]

    @pl.when(pl.program_id(0) == 0)
    def _():
        ri = lax.broadcasted_iota(jnp.int32, (sub, sub), 0)
        ci = lax.broadcasted_iota(jnp.int32, (sub, sub), 1)
        tri_ref[...] = jnp.where(ci <= ri, 1.0, 0.0).astype(BF16)

    lower = tri_ref[...]
    carry = carry_ref[...]
    for r in range(0, logf.shape[0], sub):
        l1, l2, l3 = _split3(logf[r:r + sub])
        c = (_dot(lower, l1) + (_dot(lower, l2) + _dot(lower, l3))) + carry
        c_ref[r:r + sub, :] = c
        carry = c[sub - 1:sub, :]
    carry_ref[...] = carry


def _proj_even(h, g, w, wf_hi, wf_lo, bf, seq, tm=512):
    n, d = h.shape
    n_out = w.shape[1]
    return pl.pallas_call(
        functools.partial(_proj_even_kernel, tiles_per_seq=seq // tm),
        grid=(n // tm,),
        in_specs=[
            pl.BlockSpec((tm, d), lambda i: (i, 0)),
            pl.BlockSpec((1, d), lambda i: (0, 0)),
            pl.BlockSpec((d, n_out), lambda i: (0, 0)),
            pl.BlockSpec((d, LANES), lambda i: (0, 0)),
            pl.BlockSpec((d, LANES), lambda i: (0, 0)),
            pl.BlockSpec((1, LANES), lambda i: (0, 0)),
        ],
        out_specs=[
            pl.BlockSpec((tm, n_out), lambda i: (i, 0)),
            pl.BlockSpec((tm, LANES), lambda i: (i, 0)),
        ],
        out_shape=[
            jax.ShapeDtypeStruct((n, n_out), BF16),
            jax.ShapeDtypeStruct((n, LANES), F32),
        ],
        scratch_shapes=[pltpu.VMEM((1, LANES), F32), pltpu.VMEM((MXU_TILE, MXU_TILE), BF16)],
        compiler_params=_params("arbitrary"),
        name="proj_even",
    )(h, g, w, wf_hi, wf_lo, bf)


def _own_lanes(lane, h):
    return lane < HEAD_DIM if h == 0 else lane >= HEAD_DIM


def _causal_kernel(*refs, fox, tq, tk):
    if fox:
        q_ref, k_ref, v_ref, c_ref, o_ref, vaug_t, s_even, s_odd, kaug = refs
    else:
        q_ref, k_ref, v_ref, o_ref, vaug_t, s_even, s_odd = refs
    pair = pl.program_id(1)
    qi = pl.program_id(2)
    seq = v_ref.shape[1]

    @pl.when(qi == 0)
    def _():
        lane = lax.broadcasted_iota(jnp.int32, (tk, LANES), 1)
        feat = lax.broadcasted_iota(jnp.int32, (LANES, tk), 0)

        def fill(t, carry):
            rows = pl.ds(pl.multiple_of(t * tk, tk), tk)
            vt = jnp.transpose(v_ref[0, rows, :].astype(F32))
            if fox:
                kp = k_ref[0, rows, :].astype(F32)
                c = c_ref[0, rows, :]
            for h in range(2):
                vaug_t[h, t] = jnp.where(_own_lanes(feat, h), vt, 1.0).astype(BF16)
                if fox:
                    ch = jnp.sum(jnp.where(lane == 2 * pair + h, c, 0.0), axis=-1, keepdims=True)
                    c1, c2, c3 = _split3(ch * (-LOG2E))
                    base = HEAD_DIM if h == 0 else 0
                    extra = jnp.where(lane == base, c1.astype(F32),
                                      jnp.where(lane == base + 1, c2.astype(F32),
                                                jnp.where(lane == base + 2, c3.astype(F32), 0.0)))
                    kaug[h, rows, :] = jnp.where(_own_lanes(lane, h), kp, extra).astype(BF16)
            return carry

        lax.fori_loop(0, seq // tk, fill, 0)

    lane_q = lax.broadcasted_iota(jnp.int32, (tq, LANES), 1)
    qs = []
    for h in range(2):
        if fox:
            base = HEAD_DIM if h == 0 else 0
            ones = jnp.where((lane_q >= base) & (lane_q < base + 3), 1.0, 0.0)
            qs.append(jnp.where(_own_lanes(lane_q, h), q_ref[0].astype(F32), ones).astype(BF16))
        else:
            qs.append(q_ref[0, :, h * LANES:(h + 1) * LANES])

    key = lax.broadcasted_iota(jnp.int32, (tk, tq), 0)
    qry = lax.broadcasted_iota(jnp.int32, (tk, tq), 1)

    def scores(j, buf):
        start = pl.multiple_of(j * tk, tk)
        for h in range(2):
            if fox:
                kj = kaug[h, pl.ds(start, tk), :]
            else:
                kj = k_ref[0, pl.ds(start, tk), h * LANES:(h + 1) * LANES]
            buf[h] = lax.dot_general(kj, qs[h], NT_DIMS, preferred_element_type=F32)

    def absorb(j, buf, carry, masked):
        new = []
        for h in range(2):
            m, acc = carry[h]
            s = buf[h]
            if masked:
                s = jnp.where(key <= qry, s, NEG)
            m_new = jnp.maximum(m, jnp.max(s, axis=0, keepdims=True))
            p = jnp.exp2(s - m_new)
            acc = jnp.exp2(m - m_new) * acc + _dot(vaug_t[h, j], p.astype(BF16))
            new.append((m_new, acc))
        return tuple(new)

    def finish(carry):
        (_, acc0), (_, acc1) = carry
        feat_q = lax.broadcasted_iota(jnp.int32, (LANES, tq), 0)
        out_t = jnp.where(feat_q < HEAD_DIM, acc0 / acc0[HEAD_DIM:HEAD_DIM + 1, :], acc1 / acc1[0:1, :])
        o_ref[0] = jnp.transpose(out_t).astype(o_ref.dtype)

    def pair_step(t, carry):
        scores(2 * t + 1, s_odd)
        carry = absorb(2 * t, s_even, carry, False)
        scores(2 * t + 2, s_even)
        return absorb(2 * t + 1, s_odd, carry, False)

    init = tuple((jnp.full((1, tq), NEG, F32), jnp.zeros((LANES, tq), F32)) for _ in range(2))
    scores(0, s_even)
    carry = lax.fori_loop(0, qi // 2, pair_step, init)

    @pl.when(qi % 2 == 0)
    def _():
        finish(absorb(qi, s_even, carry, True))

    @pl.when(qi % 2 == 1)
    def _():
        scores(qi, s_odd)
        finish(absorb(qi, s_odd, absorb(qi - 1, s_even, carry, False), True))


def _causal_attention(q_arr, k_arr, v_arr, c_arr, *, q_blk0, k_blk0, v_blk0, n_pairs, tq=512):
    b, s, _ = q_arr.shape
    fox = c_arr is not None
    qk_w = LANES if fox else 2 * LANES
    in_specs = [
        pl.BlockSpec((1, tq, qk_w), lambda bi, p, qi: (bi, qi, q_blk0 + p)),
        pl.BlockSpec((1, s, qk_w), lambda bi, p, qi: (bi, 0, k_blk0 + p)),
        pl.BlockSpec((1, s, LANES), lambda bi, p, qi: (bi, 0, v_blk0 + p)),
    ]
    args = [q_arr, k_arr, v_arr]
    scratch = [pltpu.VMEM((2, s // tq, LANES, tq), BF16), pltpu.VMEM((2, tq, tq), F32), pltpu.VMEM((2, tq, tq), F32)]
    if fox:
        in_specs.append(pl.BlockSpec((1, s, LANES), lambda bi, p, qi: (bi, 0, 0)))
        args.append(c_arr)
        scratch.append(pltpu.VMEM((2, s, LANES), BF16))
    return pl.pallas_call(
        functools.partial(_causal_kernel, fox=fox, tq=tq, tk=tq),
        grid=(b, n_pairs, s // tq),
        in_specs=in_specs,
        out_specs=pl.BlockSpec((1, tq, LANES), lambda bi, p, qi: (bi, qi, p)),
        out_shape=jax.ShapeDtypeStruct((b, s, n_pairs * LANES), BF16),
        scratch_shapes=scratch,
        compiler_params=_params("parallel", "parallel", "arbitrary"),
        name="causal_attention",
    )(*args)


DIL_PAD = BLK * max(d for _, d in DIL_BRANCHES)
DIL_UNROLL = 4


def _dilated_kernel(q_ref, k_ref, v_ref, bias_ref, o_ref, qf, kf, vf, acc_s, m_s, l_s, s_even, s_odd, *, seq):
    qf[...] = q_ref[0].astype(F32)
    kf[0:DIL_PAD, :] = jnp.zeros((DIL_PAD, LANES), F32)
    vf[0:DIL_PAD, :] = jnp.zeros((DIL_PAD, LANES), F32)
    kf[DIL_PAD:, :] = k_ref[0].astype(F32)
    vf[DIL_PAD:, :] = v_ref[0].astype(F32)
    head0 = lax.broadcasted_iota(jnp.int32, (BLK, LANES), 1) < HEAD_DIM
    head0_k = lax.broadcasted_iota(jnp.int32, (2 * BLK, LANES), 1) < HEAD_DIM
    ones0 = jnp.where(head0_k, 1.0, 0.0).astype(BF16)
    ones1 = jnp.where(head0_k, 0.0, 1.0).astype(BF16)

    def place(dil, u):
        bi = u // dil
        q_start = (u % dil) + (dil * BLK) * bi
        return bi, q_start, q_start + (DIL_PAD - dil * BLK)

    def scores(br, dil, g, buf):
        for i in range(DIL_UNROLL):
            bi, q_start, k_start = place(dil, g * DIL_UNROLL + i)
            q2 = qf[pl.ds(q_start, BLK, stride=dil), :]
            q_st = jnp.concatenate([jnp.where(head0, q2, 0.0), jnp.where(head0, 0.0, q2)], axis=0).astype(BF16)
            kb = kf[pl.ds(k_start, 2 * BLK, stride=dil), :].astype(BF16)
            s = lax.dot_general(q_st, kb, NT_DIMS, preferred_element_type=F32)
            buf[i] = s + bias_ref[br, 0, jnp.where(bi == 0, 1, 0)]

    def absorb(br, dil, g, buf, first):
        for i in range(DIL_UNROLL):
            _, q_start, k_start = place(dil, g * DIL_UNROLL + i)
            s = buf[i]
            m = jnp.max(s, axis=-1, keepdims=True)
            p = jnp.exp2(s - m).astype(BF16)
            p_cat = jnp.concatenate([p[:BLK], p[BLK:]], axis=1)
            v2 = vf[pl.ds(k_start, 2 * BLK, stride=dil), :]
            rhs = jnp.concatenate([
                jnp.concatenate([jnp.where(head0_k, v2, 0.0).astype(BF16), ones0], axis=1),
                jnp.concatenate([jnp.where(head0_k, 0.0, v2).astype(BF16), ones1], axis=1)], axis=0)
            acc2 = _dot(p_cat, rhs)
            acc_b, l_b = acc2[:, :LANES], acc2[:, LANES:]
            m_b = jnp.where(head0, jnp.broadcast_to(m[:BLK], (BLK, LANES)), jnp.broadcast_to(m[BLK:], (BLK, LANES)))
            idx = pl.ds(q_start, BLK, stride=dil)
            if first:
                m_s[idx, :] = m_b
                l_s[idx, :] = l_b
                acc_s[idx, :] = acc_b
                continue
            m_old = m_s[idx, :]
            m_new = jnp.maximum(m_old, m_b)
            a_old = jnp.exp2(m_old - m_new)
            a_b = jnp.exp2(m_b - m_new)
            m_s[idx, :] = m_new
            l_s[idx, :] = a_old * l_s[idx, :] + a_b * l_b
            acc_s[idx, :] = a_old * acc_s[idx, :] + a_b * acc_b

    n_groups = seq // (BLK * DIL_UNROLL)
    order = sorted(range(len(DIL_BRANCHES)), key=lambda i: -DIL_BRANCHES[i][1])
    for pos, br in enumerate(order):
        dil = DIL_BRANCHES[br][1]
        first = pos == 0

        def trip(t, carry, br=br, dil=dil, first=first):
            scores(br, dil, 2 * t + 1, s_odd)
            absorb(br, dil, 2 * t, s_even, first)
            scores(br, dil, 2 * t + 2, s_even)
            absorb(br, dil, 2 * t + 1, s_odd, first)
            return carry

        scores(br, dil, 0, s_even)
        lax.fori_loop(0, n_groups // 2 - 1, trip, 0)
        scores(br, dil, n_groups - 1, s_odd)
        absorb(br, dil, n_groups - 2, s_even, first)
        absorb(br, dil, n_groups - 1, s_odd, first)

    o_ref[0] = (acc_s[...] / l_s[...]).astype(o_ref.dtype)


def _dilated_attention(proj, bias, *, q_blk0, k_blk0, v_blk0):
    b, s, _ = proj.shape
    n_pairs = DIL_HEADS // 2
    assert (s // (BLK * DIL_UNROLL)) % 2 == 0
    blk = lambda off: pl.BlockSpec((1, s, LANES), lambda bi, p: (bi, 0, off + p))
    state = [pltpu.VMEM((s, LANES), F32) for _ in range(4)]
    padded = [pltpu.VMEM((DIL_PAD + s, LANES), F32) for _ in range(2)]
    score_bufs = [pltpu.VMEM((DIL_UNROLL, 2 * BLK, 2 * BLK), F32) for _ in range(2)]
    return pl.pallas_call(
        functools.partial(_dilated_kernel, seq=s),
        grid=(b, n_pairs),
        in_specs=[blk(q_blk0), blk(k_blk0), blk(v_blk0),
                  pl.BlockSpec((len(DIL_BRANCHES), 1, 2, 2 * BLK, 2 * BLK), lambda bi, p: (0, p, 0, 0, 0))],
        out_specs=pl.BlockSpec((1, s, LANES), lambda bi, p: (bi, 0, p)),
        out_shape=jax.ShapeDtypeStruct((b, s, n_pairs * LANES), BF16),
        scratch_shapes=[state[0], padded[0], padded[1], state[1], state[2], state[3]] + score_bufs,
        compiler_params=_params("parallel", "parallel"),
        name="dilated_attention",
    )(proj, proj, proj, bias)


def _swa_kernel(q_ref, k_ref, v_ref, bias_ref, sink_ref, o_ref, vaug_t, s_even, s_odd, *, tc):
    ci = pl.program_id(1)
    seq = k_ref.shape[1]
    grp = SWA_Q_HEADS // SWA_KV_HEADS
    n_blk = tc // BLK

    @pl.when(ci == 0)
    def _():
        feat = lax.broadcasted_iota(jnp.int32, (LANES, BLK), 0)

        def fill(t, carry):
            vt = jnp.transpose(v_ref[0, pl.ds(pl.multiple_of(t * BLK, BLK), BLK), :].astype(F32))
            for kv in range(SWA_KV_HEADS):
                vaug_t[kv, t] = jnp.where(_own_lanes(feat, kv), vt, 1.0).astype(BF16)
            return carry

        lax.fori_loop(0, seq // BLK, fill, 0)

    lane_q = lax.broadcasted_iota(jnp.int32, (BLK, LANES), 1)
    feat_o = lax.broadcasted_iota(jnp.int32, (LANES, grp * BLK), 0)

    def blocks(bb):
        own = ci * n_blk + bb
        return jnp.maximum(own - 1, 0), own

    def scores(bb, kv, buf):
        prev, own = blocks(bb)
        k2 = jnp.concatenate([k_ref[0, pl.ds(pl.multiple_of(prev * BLK, BLK), BLK), :],
                              k_ref[0, pl.ds(pl.multiple_of(own * BLK, BLK), BLK), :]], axis=0)
        q4 = jnp.concatenate(
            [jnp.where(_own_lanes(lane_q, kv), q_ref[0, bb * BLK:(bb + 1) * BLK, g * LANES:(g + 1) * LANES].astype(F32),
                       0.0) for g in range(grp)], axis=0).astype(BF16)
        s = lax.dot_general(k2, q4, NT_DIMS, preferred_element_type=F32)
        buf[...] = s + bias_ref[kv, jnp.where(own == 0, 1, 0)]

    def absorb(bb, kv, buf):
        prev, own = blocks(bb)
        s = buf[...]
        sink = sink_ref[kv]
        m = jnp.maximum(jnp.max(s, axis=0, keepdims=True), sink)
        p = jnp.exp2(s - m).astype(BF16)
        v2 = jnp.concatenate([vaug_t[kv, prev], vaug_t[kv, own]], axis=1)
        acc = _dot(v2, p)
        den = (acc[HEAD_DIM:HEAD_DIM + 1, :] if kv == 0 else acc[0:1, :]) + jnp.exp2(sink - m)
        return acc / den

    units = [(bb, kv) for bb in range(n_blk) for kv in range(SWA_KV_HEADS)]
    bufs = (s_even, s_odd)
    scores(*units[0], bufs[0])
    outs = {}
    for n, unit in enumerate(units):
        if n + 1 < len(units):
            scores(*units[n + 1], bufs[(n + 1) % 2])
        outs[unit] = absorb(*unit, bufs[n % 2])
    for bb in range(n_blk):
        out_t = jnp.where(feat_o < HEAD_DIM, outs[(bb, 0)], outs[(bb, 1)])
        for g in range(grp):
            o_ref[0, bb * BLK:(bb + 1) * BLK, g * LANES:(g + 1) * LANES] = jnp.transpose(
                out_t[:, g * BLK:(g + 1) * BLK]).astype(o_ref.dtype)


def _swa_attention(proj, bias, sink_row, *, q_blk0, k_blk0, v_blk0, tc=512):
    b, s, _ = proj.shape
    qw = SWA_Q_HEADS * HEAD_DIM
    grp = SWA_Q_HEADS // SWA_KV_HEADS
    return pl.pallas_call(
        functools.partial(_swa_kernel, tc=tc),
        grid=(b, s // tc),
        in_specs=[
            pl.BlockSpec((1, tc, qw), lambda bi, ci: (bi, ci, q_blk0 * LANES // qw)),
            pl.BlockSpec((1, s, LANES), lambda bi, ci: (bi, 0, k_blk0)),
            pl.BlockSpec((1, s, LANES), lambda bi, ci: (bi, 0, v_blk0)),
            pl.BlockSpec(bias.shape, lambda bi, ci: (0, 0, 0, 0)),
            pl.BlockSpec(sink_row.shape, lambda bi, ci: (0, 0, 0)),
        ],
        out_specs=pl.BlockSpec((1, tc, qw), lambda bi, ci: (bi, ci, 0)),
        out_shape=jax.ShapeDtypeStruct((b, s, qw), BF16),
        scratch_shapes=[pltpu.VMEM((SWA_KV_HEADS, s // BLK, LANES, BLK), BF16),
                        pltpu.VMEM((2 * BLK, grp * BLK), F32), pltpu.VMEM((2 * BLK, grp * BLK), F32)],
        compiler_params=_params("parallel", "arbitrary"),
        name="swa_attention",
    )(proj, proj, proj, bias, sink_row)


def _rope128(x, a, bm, bp):
    return x * a + pltpu.roll(x, LANES - 16, axis=1) * bm + pltpu.roll(x, 16, axis=1) * bp


def _proj_odd_kernel(x_ref, g_ref, w_ref, qn_ref, kvn_ref, wuq_ref, wuk_ref, wuv_ref, a_ref, bm_ref, bp_ref,
                     q_out, k_out, v_out, swa_out):
    xb = _rms(x_ref[...], g_ref[...]).astype(BF16)
    w = w_ref
    c_q = _dot(xb, w[:, 0:MLA_Q_RANK])
    c_kv = _dot(xb, w[:, MLA_Q_RANK:MLA_Q_RANK + MLA_KV_RANK])
    o1 = MLA_Q_RANK + MLA_KV_RANK
    kpe = _dot(xb, w[:, o1:o1 + LANES])
    o2 = o1 + LANES
    swa_out[...] = _dot(xb, w[:, o2:]).astype(swa_out.dtype)
    a, bm, bp = a_ref[...], bm_ref[...], bp_ref[...]
    kpe = _rope128(kpe, a, bm, bp)
    cqn = _rms(c_q, qn_ref[...]).astype(BF16)
    ckvn = _rms(c_kv, kvn_ref[...]).astype(BF16)
    v_out[...] = _dot(ckvn, wuv_ref[...]).astype(v_out.dtype)
    q_raw = _dot(cqn, wuq_ref[...])
    k_raw = _dot(ckvn, wuk_ref[...])
    for h in range(MLA_HEADS):
        sl = slice(h * LANES, (h + 1) * LANES)
        q_out[:, sl] = _rope128(q_raw[:, sl], a, bm, bp).astype(q_out.dtype)
        k_out[:, sl] = (k_raw[:, sl] + kpe).astype(k_out.dtype)


def _proj_odd(h, g, w, qn, kvn, wuq, wuk, wuv, rope_a, rope_bm, rope_bp, seq, tm=1024):
    n, d = h.shape
    n_in = w.shape[1]
    n_swa = n_in - (MLA_Q_RANK + MLA_KV_RANK + LANES)
    tiles_per_seq = seq // tm
    full = lambda shape: pl.BlockSpec(shape, lambda i: (0, 0))
    rope = pl.BlockSpec((tm, LANES), lambda i: (i % tiles_per_seq, 0))
    return pl.pallas_call(
        _proj_odd_kernel,
        grid=(n // tm,),
        in_specs=[
            pl.BlockSpec((tm, d), lambda i: (i, 0)), full((1, d)), full(w.shape),
            full(qn.shape), full(kvn.shape), full(wuq.shape), full(wuk.shape), full(wuv.shape),
            rope, rope, rope,
        ],
        out_specs=[
            pl.BlockSpec((tm, MLA_HEADS * LANES), lambda i: (i, 0)),
            pl.BlockSpec((tm, MLA_HEADS * LANES), lambda i: (i, 0)),
            pl.BlockSpec((tm, MLA_HEADS * MLA_V), lambda i: (i, 0)),
            pl.BlockSpec((tm, n_swa), lambda i: (i, 0)),
        ],
        out_shape=[
            jax.ShapeDtypeStruct((n, MLA_HEADS * LANES), BF16),
            jax.ShapeDtypeStruct((n, MLA_HEADS * LANES), BF16),
            jax.ShapeDtypeStruct((n, MLA_HEADS * MLA_V), BF16),
            jax.ShapeDtypeStruct((n, n_swa), BF16),
        ],
        compiler_params=_params("parallel"),
        name="proj_odd",
    )(h, g, w, qn, kvn, wuq, wuk, wuv, rope_a, rope_bm, rope_bp)


MOE_TILE = 1024
ROUTER_TILE = 512


def _out_router_kernel(h_ref, oa_ref, ob_ref, wo_ref, g_ref, wr_hi_ref, wr_lo_ref, br_ref,
                       h_out, hn_out, gates_out, counts_out, pos_col_out, pos_row_out, tri_ref, seen_ref):
    i = pl.program_id(0)
    tm = h_ref.shape[0]
    half = oa_ref.shape[1]
    h = h_ref[...] + _dot(oa_ref[...], wo_ref[0:half, :]) + _dot(ob_ref[...], wo_ref[half:, :])
    h_out[...] = h
    hn = _rms(h, g_ref[...])
    hn_out[...] = hn.astype(hn_out.dtype)
    z = _dot_hi(hn, wr_hi_ref[...], wr_lo_ref[...]) + br_ref[...]
    lane = lax.broadcasted_iota(jnp.int32, z.shape, 1)
    big = jnp.int32(LANES)
    is_grp = (lane >= N_EXPERTS) & (lane < N_EXPERTS + N_GROUPS)
    zg = jnp.where(is_grp, z, -jnp.inf)
    g_max = jnp.max(zg, axis=-1, keepdims=True)
    g_w = 1.0 / jnp.sum(jnp.exp(zg - g_max), axis=-1, keepdims=True)
    g_idx = jnp.min(jnp.where(zg == g_max, lane - N_EXPERTS, big), axis=-1, keepdims=True)
    in_grp = (lane < N_EXPERTS) & ((lane // EXPERTS_PER_GROUP) == g_idx)
    ze = jnp.where(in_grp, z, -jnp.inf)
    v1 = jnp.max(ze, axis=-1, keepdims=True)
    i1 = jnp.min(jnp.where(ze == v1, lane, big), axis=-1, keepdims=True)
    ze2 = jnp.where(lane == i1, -jnp.inf, ze)
    v2 = jnp.max(ze2, axis=-1, keepdims=True)
    i2 = jnp.min(jnp.where(ze2 == v2, lane, big), axis=-1, keepdims=True)
    e2 = jnp.exp(v2 - v1)
    w1 = g_w / (1.0 + e2)
    w2 = g_w * e2 / (1.0 + e2)
    gates_out[...] = jnp.where(lane == i1, w1, 0.0) + jnp.where(lane == i2, w2, 0.0)

    @pl.when(i == 0)
    def _():
        ri = lax.broadcasted_iota(jnp.int32, (tm, tm), 0)
        ci = lax.broadcasted_iota(jnp.int32, (tm, tm), 1)
        tri_ref[...] = jnp.where(ci < ri, 1.0, 0.0).astype(BF16)

    @pl.when(i % (MOE_TILE // tm) == 0)
    def _():
        seen_ref[...] = jnp.zeros_like(seen_ref)

    routed = jnp.where((lane == g_idx) & (lane < N_GROUPS), 1.0, 0.0)
    pos = jnp.where(routed > 0.0, _dot(tri_ref[...], routed.astype(BF16)) + seen_ref[...], -1.0)
    pos_col_out[...] = pos
    pos_row_out[0] = jnp.transpose(pos)[0:8, :]
    here = jnp.sum(routed, axis=0, keepdims=True)
    seen_ref[...] += here
    counts_out[0] = jnp.broadcast_to(here, counts_out.shape[1:])


def _out_router(h, oa, ob, wo, g, wr_hi, wr_lo, br):
    n, d = h.shape
    tm = ROUTER_TILE
    per_tile = MOE_TILE // tm
    half = oa.shape[1]
    full = lambda shape: pl.BlockSpec(shape, lambda i: (0, 0))
    tile = lambda w: pl.BlockSpec((tm, w), lambda i: (i, 0))
    return pl.pallas_call(
        _out_router_kernel,
        grid=(n // tm,),
        in_specs=[tile(d), tile(half), tile(half), full(wo.shape), full((1, d)),
                  full(wr_hi.shape), full(wr_lo.shape), full((1, LANES))],
        out_specs=[tile(d), tile(d), tile(LANES), pl.BlockSpec((1, 8, LANES), lambda i: (i, 0, 0)),
                   tile(LANES), pl.BlockSpec((1, 8, tm), lambda i: (i // per_tile, 0, i % per_tile))],
        out_shape=[
            jax.ShapeDtypeStruct((n, d), F32),
            jax.ShapeDtypeStruct((n, d), BF16),
            jax.ShapeDtypeStruct((n, LANES), F32),
            jax.ShapeDtypeStruct((n // tm, 8, LANES), F32),
            jax.ShapeDtypeStruct((n, LANES), F32),
            jax.ShapeDtypeStruct((n // MOE_TILE, 8, MOE_TILE), F32),
        ],
        scratch_shapes=[pltpu.VMEM((tm, tm), BF16), pltpu.VMEM((1, LANES), F32)],
        compiler_params=_params("arbitrary"),
        name="out_router",
    )(h, oa, ob, wo, g, wr_hi, wr_lo, br)


MOE_CHUNK = 320
MOE_ROWS = 1024
MOE_ALIGN = 16
MOE_EXPERTS_PER_STEP = 4


def _moe_sorted_tiles(n):
    n_seg = (n // MOE_TILE) * N_GROUPS
    rows = n + n_seg * (MOE_ALIGN - 1) + N_GROUPS * (MOE_CHUNK + MOE_ROWS - 1)
    return -(-rows // MOE_ROWS)


def _moe_tables(counts, n):
    n_rt = _moe_sorted_tiles(n)
    seg_len = (counts + (MOE_ALIGN - 1)) // MOE_ALIGN * MOE_ALIGN
    group_len = seg_len.sum(axis=0)
    group_span = (group_len + MOE_CHUNK + MOE_ROWS - 1) // MOE_ROWS * MOE_ROWS
    group_end = jnp.cumsum(group_span)
    group_start = group_end - group_span
    seg_off = group_start[None, :] + jnp.cumsum(seg_len, axis=0) - seg_len
    rt_start = jnp.arange(n_rt, dtype=jnp.int32) * MOE_ROWS
    rt_group = jnp.minimum(jnp.sum(rt_start[:, None] >= group_end[None, :], axis=1), N_GROUPS - 1)
    rt_valid = jnp.clip(group_start[rt_group] + group_len[rt_group] - rt_start, 0, MOE_ROWS)
    n_used = (group_end[-1] // MOE_ROWS).reshape(1)
    i32 = lambda a: a.astype(jnp.int32)
    return i32(seg_off.reshape(-1)), i32(rt_group), i32(rt_valid), i32(n_used)


def _moe_chunks(count, body):
    def one_chunk(k, carry):
        body(k * MOE_CHUNK, MOE_CHUNK)
        return carry

    lax.fori_loop(0, (count + (MOE_CHUNK - 1)) // MOE_CHUNK, one_chunk, 0)


def _moe_pack_kernel(counts_ref, off_ref, hn_ref, gates_ref, pos_ref, xs_in, xs_ref, xbuf, sem, issued_ref):
    del xs_in
    i = pl.program_id(0)
    d = hn_ref.shape[1]

    @pl.when(i == 0)
    def _():
        issued_ref[0] = 0

    def write(slot):
        return pltpu.make_async_copy(xbuf.at[slot], xs_ref.at[pl.ds(0, MOE_CHUNK)], sem.at[slot])

    lane = lax.broadcasted_iota(jnp.int32, (hn_ref.shape[0], LANES), 1)
    gates = gates_ref[...]
    gates_hi = gates.astype(BF16).astype(F32)
    gates_lo = gates - gates_hi
    for grp in range(N_GROUPS):
        seg = i * N_GROUPS + grp
        first = grp * EXPERTS_PER_GROUP
        g_hi = gates_hi if first == 0 else pltpu.roll(gates_hi, LANES - first, axis=1)
        g_lo = pltpu.roll(gates_lo, (LANES - first + EXPERTS_PER_GROUP) % LANES, axis=1)
        gate_cols = jnp.where(lane < EXPERTS_PER_GROUP, g_hi, jnp.where(lane < 2 * EXPERTS_PER_GROUP, g_lo, 0.0))
        gate_cols = gate_cols.astype(BF16)
        pos_row = pos_ref[0, grp:grp + 1, :]

        def chunk(first_pos, rows, seg=seg, gate_cols=gate_cols, pos_row=pos_row):
            n_done = issued_ref[0]
            slot = n_done % 2
            slot_pos = lax.broadcasted_iota(jnp.int32, (rows, 1), 0).astype(F32) + first_pos.astype(F32)
            gather = jnp.where(pos_row == slot_pos, 1.0, 0.0).astype(BF16)
            xbuf[slot, :, 0:d] = _dot(gather, hn_ref[...]).astype(BF16)
            xbuf[slot, :, d:] = _dot(gather, gate_cols).astype(BF16)

            @pl.when(n_done > 0)
            def _():
                write(1 - slot).wait()

            dst = pl.multiple_of(off_ref[seg] + first_pos, MOE_ALIGN)
            pltpu.make_async_copy(xbuf.at[slot], xs_ref.at[pl.ds(dst, MOE_CHUNK)], sem.at[slot]).start()
            issued_ref[0] = n_done + 1

        _moe_chunks(counts_ref[seg], chunk)

    @pl.when((i == pl.num_programs(0) - 1) & (issued_ref[0] > 0))
    def _():
        write((issued_ref[0] - 1) % 2).wait()


def _moe_pack(counts, seg_off, hn, gates, pos_row, xs_init):
    n, d = hn.shape
    rows = xs_init.shape[0]
    grid_spec = pltpu.PrefetchScalarGridSpec(
        num_scalar_prefetch=2,
        grid=(n // MOE_TILE,),
        in_specs=[
            pl.BlockSpec((MOE_TILE, d), lambda i, c, o: (i, 0)),
            pl.BlockSpec((MOE_TILE, LANES), lambda i, c, o: (i, 0)),
            pl.BlockSpec((1, 8, MOE_TILE), lambda i, c, o: (i, 0, 0)),
            pl.BlockSpec(memory_space=pl.ANY),
        ],
        out_specs=pl.BlockSpec(memory_space=pl.ANY),
        scratch_shapes=[pltpu.VMEM((2, MOE_CHUNK, d + LANES), BF16), pltpu.SemaphoreType.DMA((2,)),
                        pltpu.SMEM((1,), jnp.int32)],
    )
    return pl.pallas_call(
        _moe_pack_kernel,
        grid_spec=grid_spec,
        out_shape=jax.ShapeDtypeStruct((rows, d + LANES), BF16),
        input_output_aliases={5: 0},
        compiler_params=_params("arbitrary"),
        name="moe_pack",
    )(counts, seg_off, hn, gates, pos_row, xs_init)


def _moe_experts_kernel(grp_ref, valid_ref, used_ref, xs_ref, wg_ref, wu_ref, wd_ref, ys_ref, acc_ref,
                        wg_b, wu_b, wd_b):
    del grp_ref, used_ref
    rt = pl.program_id(0)
    j = pl.program_id(1)
    d = ys_ref.shape[1]
    valid = valid_ref[rt]

    @pl.when(valid > 0)
    def _():
        @pl.when(j == 0)
        def _():
            acc_ref[...] = jnp.zeros_like(acc_ref)

        wg_b[...] = wg_ref[0, 0].astype(BF16)
        wu_b[...] = wu_ref[0, 0].astype(BF16)
        wd_b[...] = wd_ref[0, 0].astype(BF16).reshape(wd_b.shape)

        def rows_block(t, carry):
            rows = pl.ds(pl.multiple_of(t * MXU_TILE, MXU_TILE), MXU_TILE)
            x = xs_ref[rows, 0:d]
            gate_cols = xs_ref[rows, d:].astype(F32)
            lane = lax.broadcasted_iota(jnp.int32, gate_cols.shape, 1)
            acts = []
            for e in range(MOE_EXPERTS_PER_STEP):
                idx = j * MOE_EXPERTS_PER_STEP + e
                mine = (lane == idx) | (lane == idx + EXPERTS_PER_GROUP)
                gate = jnp.sum(jnp.where(mine, gate_cols, 0.0), axis=-1, keepdims=True)
                a = _dot(x, wg_b[e])
                u = _dot(x, wu_b[e])
                acts.append(((a * jax.nn.sigmoid(a)) * u * gate).astype(BF16))
            acc_ref[rows, :] += _dot(jnp.concatenate(acts, axis=1), wd_b[...])
            return carry

        lax.fori_loop(0, (valid + (MXU_TILE - 1)) // MXU_TILE, rows_block, 0)

        @pl.when(j == pl.num_programs(1) - 1)
        def _():
            row = lax.broadcasted_iota(jnp.int32, (acc_ref.shape[0], 1), 0)
            ys_ref[...] = jnp.where(row < valid, acc_ref[...], 0.0).astype(ys_ref.dtype)

    @pl.when((valid == 0) & (j == pl.num_programs(1) - 1))
    def _():
        ys_ref[...] = jnp.zeros_like(ys_ref)


def _moe_experts(rt_group, rt_valid, n_used, xs, w_gate, w_up, w_down, layer):
    rows = xs.shape[0]
    d = w_gate.shape[-2]
    n_rt = rows // MOE_ROWS
    eps = MOE_EXPERTS_PER_STEP
    n_steps = EXPERTS_PER_GROUP // eps

    def tile_idx(rt, j, grp, valid, used):
        return jnp.minimum(rt, used[0] - 1)

    def w_idx(rt, j, grp, valid, used):
        live = rt < used[0]
        return (layer, grp[tile_idx(rt, j, grp, valid, used)], jnp.where(live, j, n_steps - 1), 0, 0)

    grid_spec = pltpu.PrefetchScalarGridSpec(
        num_scalar_prefetch=3,
        grid=(n_rt, n_steps),
        in_specs=[
            pl.BlockSpec((MOE_ROWS, d + LANES), lambda rt, j, grp, valid, used: (tile_idx(rt, j, grp, valid, used), 0)),
            pl.BlockSpec((1, 1, eps, d, D_EXPERT), w_idx),
            pl.BlockSpec((1, 1, eps, d, D_EXPERT), w_idx),
            pl.BlockSpec((1, 1, eps, D_EXPERT, d), w_idx),
        ],
        out_specs=pl.BlockSpec((MOE_ROWS, d), lambda rt, j, grp, valid, used: (rt, 0)),
        scratch_shapes=[pltpu.VMEM((MOE_ROWS, d), F32), pltpu.VMEM((eps, d, D_EXPERT), BF16),
                        pltpu.VMEM((eps, d, D_EXPERT), BF16), pltpu.VMEM((eps * D_EXPERT, d), BF16)],
    )
    return pl.pallas_call(
        _moe_experts_kernel,
        grid_spec=grid_spec,
        out_shape=jax.ShapeDtypeStruct((rows, d), BF16),
        compiler_params=_params("arbitrary", "arbitrary"),
        name="moe_experts",
    )(rt_group, rt_valid, n_used, xs, w_gate, w_up, w_down)


def _moe_combine_kernel(counts_ref, off_ref, pos_ref, h_ref, fg_ref, ys_ref, o_ref, ybuf, ymore, sem, *, final_norm):
    i = pl.program_id(0)

    def read(tile, grp):
        src = pl.multiple_of(off_ref[tile * N_GROUPS + grp], MOE_ALIGN)
        slot = tile % 2
        return pltpu.make_async_copy(ys_ref.at[pl.ds(src, MOE_CHUNK)],
                                     ybuf.at[slot, pl.ds(grp * MOE_CHUNK, MOE_CHUNK)], sem.at[slot, grp])

    def fetch(tile):
        for grp in range(N_GROUPS):
            @pl.when(counts_ref[tile * N_GROUPS + grp] > 0)
            def _(grp=grp):
                read(tile, grp).start()

    @pl.when(i == 0)
    def _():
        ybuf[...] = jnp.zeros_like(ybuf)
        fetch(i)

    @pl.when(i + 1 < pl.num_programs(0))
    def _():
        fetch(i + 1)

    slot_pos = lax.broadcasted_iota(jnp.int32, (1, MOE_CHUNK), 1).astype(F32)
    scatter = jnp.concatenate(
        [jnp.where(pos_ref[:, grp:grp + 1] == slot_pos, 1.0, 0.0).astype(BF16) for grp in range(N_GROUPS)], axis=1)
    for grp in range(N_GROUPS):
        @pl.when(counts_ref[i * N_GROUPS + grp] > 0)
        def _(grp=grp):
            read(i, grp).wait()

    o_ref[...] = h_ref[...] + _dot(scatter, ybuf[i % 2])
    for grp in range(N_GROUPS):
        count = counts_ref[i * N_GROUPS + grp]
        pos_col = pos_ref[:, grp:grp + 1]

        def more(k, carry, grp=grp, pos_col=pos_col):
            src = pl.multiple_of(off_ref[i * N_GROUPS + grp] + k * MOE_CHUNK, MOE_ALIGN)
            pltpu.sync_copy(ys_ref.at[pl.ds(src, MOE_CHUNK)], ymore)
            scatter = jnp.where(pos_col == slot_pos + (k * MOE_CHUNK).astype(F32), 1.0, 0.0).astype(BF16)
            o_ref[...] += _dot(scatter, ymore[...])
            return carry

        lax.fori_loop(1, (count + (MOE_CHUNK - 1)) // MOE_CHUNK, more, 0)

    if final_norm:
        o_ref[...] = _rms(o_ref[...], fg_ref[...])


def _moe_combine(counts, seg_off, pos_col, h, fg, ys, final_norm):
    n, d = h.shape
    grid_spec = pltpu.PrefetchScalarGridSpec(
        num_scalar_prefetch=2,
        grid=(n // MOE_TILE,),
        in_specs=[
            pl.BlockSpec((MOE_TILE, LANES), lambda i, c, o: (i, 0)),
            pl.BlockSpec((MOE_TILE, d), lambda i, c, o: (i, 0)),
            pl.BlockSpec((1, d), lambda i, c, o: (0, 0)),
            pl.BlockSpec(memory_space=pl.ANY),
        ],
        out_specs=pl.BlockSpec((MOE_TILE, d), lambda i, c, o: (i, 0)),
        scratch_shapes=[pltpu.VMEM((2, N_GROUPS * MOE_CHUNK, d), BF16), pltpu.VMEM((MOE_CHUNK, d), BF16),
                        pltpu.SemaphoreType.DMA((2, N_GROUPS))],
    )
    return pl.pallas_call(
        functools.partial(_moe_combine_kernel, final_norm=final_norm),
        grid_spec=grid_spec,
        out_shape=jax.ShapeDtypeStruct((n, d), F32),
        compiler_params=_params("arbitrary"),
        name="moe_combine",
    )(counts, seg_off, pos_col, h, fg, ys)


def _alibi_slopes(n):
    return jnp.exp2(-8.0 * jnp.arange(1, n + 1, dtype=F32) / n)


def _band_bias(slopes, max_steps, step_dist):
    steps = (jnp.arange(BLK)[:, None] + BLK) - jnp.arange(2 * BLK)[None, :]
    in_band = (steps >= 0) & (steps <= max_steps)
    dist = (steps * step_dist).astype(F32)
    return jnp.where(in_band[None], -slopes.astype(F32)[:, None, None] * dist[None], NEG)


def _rope_tables(s):
    inv = ROPE_THETA ** (-jnp.arange(0, MLA_ROPE, 2, dtype=F32) / MLA_ROPE)
    ang = jnp.arange(s, dtype=F32)[:, None] * inv[None, :]
    cos, sin = jnp.cos(ang), jnp.sin(ang)
    half = MLA_ROPE // 2
    zeros_tail = jnp.zeros((s, LANES - MLA_NOPE - MLA_ROPE), F32)
    a = jnp.concatenate([jnp.ones((s, MLA_NOPE), F32), cos, cos, zeros_tail], axis=1)
    zeros_nope = jnp.zeros((s, MLA_NOPE), F32)
    zeros_half = jnp.zeros((s, half), F32)
    bm = jnp.concatenate([zeros_nope, -sin, zeros_half, zeros_tail], axis=1)
    bp = jnp.concatenate([zeros_nope, zeros_half, sin, zeros_tail], axis=1)
    return a, bm, bp


def _swa_head_order(w, axis):
    grp = SWA_Q_HEADS // SWA_KV_HEADS
    shape = w.shape
    w = w.reshape(shape[:axis] + (SWA_KV_HEADS, grp, HEAD_DIM) + shape[axis + 1:])
    return jnp.swapaxes(w, axis, axis + 1).reshape(shape)


def _pad_cols(w, width):
    return jnp.pad(w, ((0, 0), (0, width - w.shape[1])))


def _router_weights(w_group, b_group, w_router, b_router):
    w = _pad_cols(jnp.concatenate([w_router, w_group], axis=1), LANES)
    b = _pad_cols(jnp.concatenate([b_router, b_group])[None, :], LANES)
    hi, lo = _hi_lo(w)
    return hi, lo, b


def kernel(x, attn_norm, ffn_norm, final_norm, e_w_in, e_b_f, e_w_out, o_w_in, o_q_norm, o_kv_norm, o_w_uq,
           o_w_ukv, o_sinks, o_w_out, moe_w_group, moe_b_group, moe_w_router, moe_b_router, moe_w_gate,
           moe_w_up, moe_w_down):
    b, s, d = x.shape
    n = b * s
    depth = attn_norm.shape[0]
    assert s % (BLK * DIL_BRANCHES[-1][1]) == 0 and d == D_MODEL
    h = x.reshape(n, d)

    dil_slopes = _alibi_slopes(DIL_HEADS)
    dil_bias = jnp.stack([_band_bias(dil_slopes, w // dl, dl) for (w, dl) in DIL_BRANCHES]) * LOG2E
    own_half = jnp.arange(2 * BLK) >= BLK
    dil_bias = jnp.stack([dil_bias, jnp.where(own_half, dil_bias, NEG)], axis=2)
    dil_bias = dil_bias.reshape(len(DIL_BRANCHES), DIL_HEADS // 2, 2, 2, BLK, 2 * BLK)
    dil_bias = dil_bias.transpose(0, 1, 3, 2, 4, 5).reshape(len(DIL_BRANCHES), DIL_HEADS // 2, 2, 2 * BLK, 2 * BLK)
    swa_grp = SWA_Q_HEADS // SWA_KV_HEADS
    swa_bias = _band_bias(_alibi_slopes(SWA_Q_HEADS), SWA_WINDOW - 1, 1) * LOG2E
    swa_bias = swa_bias.reshape(SWA_KV_HEADS, swa_grp, BLK, 2 * BLK).transpose(0, 3, 1, 2)
    swa_bias = swa_bias.reshape(SWA_KV_HEADS, 2 * BLK, swa_grp * BLK)
    swa_bias = jnp.stack([swa_bias, jnp.where(own_half[None, :, None], swa_bias, NEG)], axis=1)
    rope_a, rope_bm, rope_bp = _rope_tables(s)

    xs = jnp.zeros((_moe_sorted_tiles(n) * MOE_ROWS, d + LANES), BF16)
    for layer in range(depth):
        i = layer // 2
        g_attn = attn_norm[layer][None, :]
        if layer % 2 == 0:
            w_in = e_w_in[i]
            hq = FOX_HEADS * HEAD_DIM
            scale = HEAD_DIM ** -0.5
            cols = [w_in[:, 0:hq] * (scale * LOG2E), w_in[:, hq:2 * hq], w_in[:, 2 * hq:3 * hq]]
            o = 3 * hq + FOX_HEADS
            cols += [w_in[:, o:o + hq] * (scale * LOG2E), w_in[:, o + hq:o + 2 * hq], w_in[:, o + 2 * hq:o + 3 * hq]]
            w_main = jnp.concatenate(cols, axis=1).astype(BF16)
            wf_hi, wf_lo = _hi_lo(_pad_cols(w_in[:, 3 * hq:o], LANES))
            b_f = _pad_cols(e_b_f[i][None, :], LANES)
            proj, c = _proj_even(h, g_attn, w_main, wf_hi, wf_lo, b_f, s)
            proj = proj.reshape(b, s, -1)
            o_a = _causal_attention(proj, proj, proj, c.reshape(b, s, LANES), q_blk0=0, k_blk0=4, v_blk0=8,
                                    n_pairs=FOX_HEADS // 2)
            o_b = _dilated_attention(proj, dil_bias, q_blk0=12, k_blk0=16, v_blk0=20)
            w_out = e_w_out[i].astype(BF16)
        else:
            w_in = o_w_in[i]
            o1 = MLA_Q_RANK + MLA_KV_RANK
            o2 = o1 + MLA_ROPE
            sq = SWA_Q_HEADS * HEAD_DIM
            kpe_cols = jnp.pad(w_in[:, o1:o2], ((0, 0), (MLA_NOPE, LANES - MLA_NOPE - MLA_ROPE)))
            w_main = jnp.concatenate(
                [w_in[:, :o1], kpe_cols, _swa_head_order(w_in[:, o2:o2 + sq] * (HEAD_DIM ** -0.5 * LOG2E), axis=1),
                 w_in[:, o2 + sq:]],
                axis=1).astype(BF16)
            dq = MLA_NOPE + MLA_ROPE
            wuq = o_w_uq[i].reshape(MLA_Q_RANK, MLA_HEADS, dq) * (dq ** -0.5 * LOG2E)
            wuq = jnp.pad(wuq, ((0, 0), (0, 0), (0, LANES - dq))).reshape(MLA_Q_RANK, MLA_HEADS * LANES)
            wukv = o_w_ukv[i].reshape(MLA_KV_RANK, MLA_HEADS, MLA_NOPE + MLA_V)
            wuk = jnp.pad(wukv[:, :, :MLA_NOPE], ((0, 0), (0, 0), (0, LANES - MLA_NOPE)))
            wuk = wuk.reshape(MLA_KV_RANK, MLA_HEADS * LANES)
            wuv = wukv[:, :, MLA_NOPE:].reshape(MLA_KV_RANK, MLA_HEADS * MLA_V)
            q_full, k_full, v_mla, swa = _proj_odd(
                h, g_attn, w_main, o_q_norm[i][None, :], o_kv_norm[i][None, :], wuq.astype(BF16),
                wuk.astype(BF16), wuv.astype(BF16), rope_a, rope_bm, rope_bp, s)
            o_a = _causal_attention(q_full.reshape(b, s, -1), k_full.reshape(b, s, -1), v_mla.reshape(b, s, -1),
                                    None, q_blk0=0, k_blk0=0, v_blk0=0, n_pairs=MLA_HEADS // 2)
            grp = SWA_Q_HEADS // SWA_KV_HEADS
            sink_row = jnp.repeat(o_sinks[i].reshape(SWA_KV_HEADS, grp) * LOG2E, BLK, axis=1)[:, None, :]
            o_b = _swa_attention(swa.reshape(b, s, -1), swa_bias, sink_row, q_blk0=0, k_blk0=4, v_blk0=5)
            half = MLA_HEADS * MLA_V
            w_out = jnp.concatenate([o_w_out[i][:half], _swa_head_order(o_w_out[i][half:], axis=0)], axis=0)
            w_out = w_out.astype(BF16)

        wr_hi, wr_lo, b_r = _router_weights(moe_w_group[layer], moe_b_group[layer], moe_w_router[layer],
                                            moe_b_router[layer])
        h, hn, gates, counts, pos_col, pos_row = _out_router(
            h, o_a.reshape(n, -1), o_b.reshape(n, -1), w_out, ffn_norm[layer][None, :], wr_hi, wr_lo, b_r)
        counts = counts[:, 0, :N_GROUPS].reshape(n // MOE_TILE, MOE_TILE // ROUTER_TILE, N_GROUPS).sum(axis=1)
        counts = counts.astype(jnp.int32)
        seg_off, rt_group, rt_valid, n_used = _moe_tables(counts, n)
        counts = counts.reshape(-1)
        xs = _moe_pack(counts, seg_off, hn, gates, pos_row, xs)
        ys = _moe_experts(rt_group, rt_valid, n_used, xs, moe_w_gate, moe_w_up, moe_w_down, layer)
        h = _moe_combine(counts, seg_off, pos_col, h, final_norm[None, :], ys, final_norm=layer == depth - 1)
    return h.reshape(b, s, d)
```

```python
import functools

import jax
import jax.numpy as jnp
from jax import lax
from jax.experimental import pallas as pl
from jax.experimental.pallas import tpu as pltpu

F32 = jnp.float32
BF16 = jnp.bfloat16

D_MODEL = 1024
HEAD_DIM = 64
BLK = 128
NEG = -1e30
RMS_EPS = 1e-6
FOX_HEADS = 8
DIL_HEADS = 8
DIL_BRANCHES = ((128, 1), (512, 4), (2048, 16))
MLA_HEADS = 8
MLA_Q_RANK = 384
MLA_KV_RANK = 256
MLA_NOPE = 64
MLA_ROPE = 32
MLA_V = 64
ROPE_THETA = 10000.0
SWA_Q_HEADS = 8
SWA_KV_HEADS = 2
SWA_WINDOW = 128
N_GROUPS = 4
EXPERTS_PER_GROUP = 8
N_EXPERTS = N_GROUPS * EXPERTS_PER_GROUP
D_EXPERT = 256

LANES = 128
MXU_TILE = 256
VMEM_LIMIT = 56 * 1024 * 1024

NT_DIMS = (((1,), (1,)), ((), ()))
LOG2E = 1.4426950408889634


def _params(*sem):
    return pltpu.CompilerParams(dimension_semantics=sem, vmem_limit_bytes=VMEM_LIMIT)


def _dot(a, b):
    return jnp.dot(a, b, preferred_element_type=F32)


def _rms(x, g):
    return x * lax.rsqrt(jnp.mean(x * x, axis=-1, keepdims=True) + RMS_EPS) * g


def _hi_lo(w):
    hi = w.astype(BF16)
    return hi, (w - hi.astype(F32)).astype(BF16)


def _dot_hi(x, w_hi, w_lo):
    x_hi = x.astype(BF16)
    x_lo = (x - x_hi.astype(F32)).astype(BF16)
    n = w_hi.shape[1]
    both = _dot(x_hi, jnp.concatenate([w_hi, w_lo], axis=1))
    return both[:, :n] + (both[:, n:] + _dot(x_lo, w_hi))


def _split3(x):
    x1 = x.astype(BF16)
    r1 = x - x1.astype(F32)
    x2 = r1.astype(BF16)
    x3 = (r1 - x2.astype(F32)).astype(BF16)
    return x1, x2, x3


def _proj_even_kernel(x_ref, g_ref, w_ref, wf_hi_ref, wf_lo_ref, bf_ref, out_ref, c_ref, carry_ref, tri_ref, *,
                      tiles_per_seq):
    xn = _rms(x_ref[...], g_ref[...])
    xb = xn.astype(BF16)
    n_out = out_ref.shape[1]
    for c in range(0, n_out, 512):
        out_ref[:, c:c + 512] = _dot(xb, w_ref[:, c:c + 512]).astype(out_ref.dtype)
    z = _dot_hi(xn, wf_hi_ref[...], wf_lo_ref[...]) + bf_ref[...]
    logf = jnp.minimum(z, 0.0) - jnp.log1p(jnp.exp(-jnp.abs(z)))

    @pl.when(pl.program_id(0) % tiles_per_seq == 0)
    def _():
        carry_ref[...] = jnp.zeros_like(carry_ref)

    sub = tri_ref.shape[0]

    @pl.when(pl.program_id(0) == 0)
    def _():
        ri = lax.broadcasted_iota(jnp.int32, (sub, sub), 0)
        ci = lax.broadcasted_iota(jnp.int32, (sub, sub), 1)
        tri_ref[...] = jnp.where(ci <= ri, 1.0, 0.0).astype(BF16)

    lower = tri_ref[...]
    carry = carry_ref[...]
    for r in range(0, logf.shape[0], sub):
        l1, l2, l3 = _split3(logf[r:r + sub])
        c = (_dot(lower, l1) + (_dot(lower, l2) + _dot(lower, l3))) + carry
        c_ref[r:r + sub, :] = c
        carry = c[sub - 1:sub, :]
    carry_ref[...] = carry


def _proj_even(h, g, w, wf_hi, wf_lo, bf, seq, tm=512):
    n, d = h.shape
    n_out = w.shape[1]
    return pl.pallas_call(
        functools.partial(_proj_even_kernel, tiles_per_seq=seq // tm),
        grid=(n // tm,),
        in_specs=[
            pl.BlockSpec((tm, d), lambda i: (i, 0)),
            pl.BlockSpec((1, d), lambda i: (0, 0)),
            pl.BlockSpec((d, n_out), lambda i: (0, 0)),
            pl.BlockSpec((d, LANES), lambda i: (0, 0)),
            pl.BlockSpec((d, LANES), lambda i: (0, 0)),
            pl.BlockSpec((1, LANES), lambda i: (0, 0)),
        ],
        out_specs=[
            pl.BlockSpec((tm, n_out), lambda i: (i, 0)),
            pl.BlockSpec((tm, LANES), lambda i: (i, 0)),
        ],
        out_shape=[
            jax.ShapeDtypeStruct((n, n_out), BF16),
            jax.ShapeDtypeStruct((n, LANES), F32),
        ],
        scratch_shapes=[pltpu.VMEM((1, LANES), F32), pltpu.VMEM((MXU_TILE, MXU_TILE), BF16)],
        compiler_params=_params("arbitrary"),
        name="proj_even",
    )(h, g, w, wf_hi, wf_lo, bf)


def _own_lanes(lane, h):
    return lane < HEAD_DIM if h == 0 else lane >= HEAD_DIM


def _causal_kernel(*refs, fox, tq, tk):
    if fox:
        q_ref, k_ref, v_ref, c_ref, o_ref, vaug_t, s_even, s_odd, kaug = refs
    else:
        q_ref, k_ref, v_ref, o_ref, vaug_t, s_even, s_odd = refs
    pair = pl.program_id(1)
    qi = pl.program_id(2)
    seq = v_ref.shape[1]

    @pl.when(qi == 0)
    def _():
        lane = lax.broadcasted_iota(jnp.int32, (tk, LANES), 1)
        feat = lax.broadcasted_iota(jnp.int32, (LANES, tk), 0)

        def fill(t, carry):
            rows = pl.ds(pl.multiple_of(t * tk, tk), tk)
            vt = jnp.transpose(v_ref[0, rows, :].astype(F32))
            if fox:
                kp = k_ref[0, rows, :].astype(F32)
                c = c_ref[0, rows, :]
            for h in range(2):
                vaug_t[h, t] = jnp.where(_own_lanes(feat, h), vt, 1.0).astype(BF16)
                if fox:
                    ch = jnp.sum(jnp.where(lane == 2 * pair + h, c, 0.0), axis=-1, keepdims=True)
                    c1, c2, c3 = _split3(ch * (-LOG2E))
                    base = HEAD_DIM if h == 0 else 0
                    extra = jnp.where(lane == base, c1.astype(F32),
                                      jnp.where(lane == base + 1, c2.astype(F32),
                                                jnp.where(lane == base + 2, c3.astype(F32), 0.0)))
                    kaug[h, rows, :] = jnp.where(_own_lanes(lane, h), kp, extra).astype(BF16)
            return carry

        lax.fori_loop(0, seq // tk, fill, 0)

    lane_q = lax.broadcasted_iota(jnp.int32, (tq, LANES), 1)
    qs = []
    for h in range(2):
        if fox:
            base = HEAD_DIM if h == 0 else 0
            ones = jnp.where((lane_q >= base) & (lane_q < base + 3), 1.0, 0.0)
            qs.append(jnp.where(_own_lanes(lane_q, h), q_ref[0].astype(F32), ones).astype(BF16))
        else:
            qs.append(q_ref[0, :, h * LANES:(h + 1) * LANES])

    key = lax.broadcasted_iota(jnp.int32, (tk, tq), 0)
    qry = lax.broadcasted_iota(jnp.int32, (tk, tq), 1)

    def scores(j, buf):
        start = pl.multiple_of(j * tk, tk)
        for h in range(2):
            if fox:
                kj = kaug[h, pl.ds(start, tk), :]
            else:
                kj = k_ref[0, pl.ds(start, tk), h * LANES:(h + 1) * LANES]
            buf[h] = lax.dot_general(kj, qs[h], NT_DIMS, preferred_element_type=F32)

    def absorb(j, buf, carry, masked):
        new = []
        for h in range(2):
            m, acc = carry[h]
            s = buf[h]
            if masked:
                s = jnp.where(key <= qry, s, NEG)
            m_new = jnp.maximum(m, jnp.max(s, axis=0, keepdims=True))
            p = jnp.exp2(s - m_new)
            acc = jnp.exp2(m - m_new) * acc + _dot(vaug_t[h, j], p.astype(BF16))
            new.append((m_new, acc))
        return tuple(new)

    def finish(carry):
        (_, acc0), (_, acc1) = carry
        feat_q = lax.broadcasted_iota(jnp.int32, (LANES, tq), 0)
        out_t = jnp.where(feat_q < HEAD_DIM, acc0 / acc0[HEAD_DIM:HEAD_DIM + 1, :], acc1 / acc1[0:1, :])
        o_ref[0] = jnp.transpose(out_t).astype(o_ref.dtype)

    def pair_step(t, carry):
        scores(2 * t + 1, s_odd)
        carry = absorb(2 * t, s_even, carry, False)
        scores(2 * t + 2, s_even)
        return absorb(2 * t + 1, s_odd, carry, False)

    init = tuple((jnp.full((1, tq), NEG, F32), jnp.zeros((LANES, tq), F32)) for _ in range(2))
    scores(0, s_even)
    carry = lax.fori_loop(0, qi // 2, pair_step, init)

    @pl.when(qi % 2 == 0)
    def _():
        finish(absorb(qi, s_even, carry, True))

    @pl.when(qi % 2 == 1)
    def _():
        scores(qi, s_odd)
        finish(absorb(qi, s_odd, absorb(qi - 1, s_even, carry, False), True))


def _causal_attention(q_arr, k_arr, v_arr, c_arr, *, q_blk0, k_blk0, v_blk0, n_pairs, tq=512):
    b, s, _ = q_arr.shape
    fox = c_arr is not None
    qk_w = LANES if fox else 2 * LANES
    in_specs = [
        pl.BlockSpec((1, tq, qk_w), lambda bi, p, qi: (bi, qi, q_blk0 + p)),
        pl.BlockSpec((1, s, qk_w), lambda bi, p, qi: (bi, 0, k_blk0 + p)),
        pl.BlockSpec((1, s, LANES), lambda bi, p, qi: (bi, 0, v_blk0 + p)),
    ]
    args = [q_arr, k_arr, v_arr]
    scratch = [pltpu.VMEM((2, s // tq, LANES, tq), BF16), pltpu.VMEM((2, tq, tq), F32), pltpu.VMEM((2, tq, tq), F32)]
    if fox:
        in_specs.append(pl.BlockSpec((1, s, LANES), lambda bi, p, qi: (bi, 0, 0)))
        args.append(c_arr)
        scratch.append(pltpu.VMEM((2, s, LANES), BF16))
    return pl.pallas_call(
        functools.partial(_causal_kernel, fox=fox, tq=tq, tk=tq),
        grid=(b, n_pairs, s // tq),
        in_specs=in_specs,
        out_specs=pl.BlockSpec((1, tq, LANES), lambda bi, p, qi: (bi, qi, p)),
        out_shape=jax.ShapeDtypeStruct((b, s, n_pairs * LANES), BF16),
        scratch_shapes=scratch,
        compiler_params=_params("parallel", "parallel", "arbitrary"),
        name="causal_attention",
    )(*args)


DIL_PAD = BLK * max(d for _, d in DIL_BRANCHES)
DIL_UNROLL = 4


def _dilated_kernel(q_ref, k_ref, v_ref, bias_ref, o_ref, qf, kf, vf, acc_s, m_s, l_s, s_even, s_odd, *, seq):
    qf[...] = q_ref[0].astype(F32)
    kf[0:DIL_PAD, :] = jnp.zeros((DIL_PAD, LANES), F32)
    vf[0:DIL_PAD, :] = jnp.zeros((DIL_PAD, LANES), F32)
    kf[DIL_PAD:, :] = k_ref[0].astype(F32)
    vf[DIL_PAD:, :] = v_ref[0].astype(F32)
    head0 = lax.broadcasted_iota(jnp.int32, (BLK, LANES), 1) < HEAD_DIM
    head0_k = lax.broadcasted_iota(jnp.int32, (2 * BLK, LANES), 1) < HEAD_DIM
    ones0 = jnp.where(head0_k, 1.0, 0.0).astype(BF16)
    ones1 = jnp.where(head0_k, 0.0, 1.0).astype(BF16)

    def place(dil, u):
        bi = u // dil
        q_start = (u % dil) + (dil * BLK) * bi
        return bi, q_start, q_start + (DIL_PAD - dil * BLK)

    def scores(br, dil, g, buf):
        for i in range(DIL_UNROLL):
            bi, q_start, k_start = place(dil, g * DIL_UNROLL + i)
            q2 = qf[pl.ds(q_start, BLK, stride=dil), :]
            q_st = jnp.concatenate([jnp.where(head0, q2, 0.0), jnp.where(head0, 0.0, q2)], axis=0).astype(BF16)
            kb = kf[pl.ds(k_start, 2 * BLK, stride=dil), :].astype(BF16)
            s = lax.dot_general(q_st, kb, NT_DIMS, preferred_element_type=F32)
            buf[i] = s + bias_ref[br, 0, jnp.where(bi == 0, 1, 0)]

    def absorb(br, dil, g, buf, first):
        for i in range(DIL_UNROLL):
            _, q_start, k_start = place(dil, g * DIL_UNROLL + i)
            s = buf[i]
            m = jnp.max(s, axis=-1, keepdims=True)
            p = jnp.exp2(s - m).astype(BF16)
            p_cat = jnp.concatenate([p[:BLK], p[BLK:]], axis=1)
            v2 = vf[pl.ds(k_start, 2 * BLK, stride=dil), :]
            rhs = jnp.concatenate([
                jnp.concatenate([jnp.where(head0_k, v2, 0.0).astype(BF16), ones0], axis=1),
                jnp.concatenate([jnp.where(head0_k, 0.0, v2).astype(BF16), ones1], axis=1)], axis=0)
            acc2 = _dot(p_cat, rhs)
            acc_b, l_b = acc2[:, :LANES], acc2[:, LANES:]
            m_b = jnp.where(head0, jnp.broadcast_to(m[:BLK], (BLK, LANES)), jnp.broadcast_to(m[BLK:], (BLK, LANES)))
            idx = pl.ds(q_start, BLK, stride=dil)
            if first:
                m_s[idx, :] = m_b
                l_s[idx, :] = l_b
                acc_s[idx, :] = acc_b
                continue
            m_old = m_s[idx, :]
            m_new = jnp.maximum(m_old, m_b)
            a_old = jnp.exp2(m_old - m_new)
            a_b = jnp.exp2(m_b - m_new)
            m_s[idx, :] = m_new
            l_s[idx, :] = a_old * l_s[idx, :] + a_b * l_b
            acc_s[idx, :] = a_old * acc_s[idx, :] + a_b * acc_b

    n_groups = seq // (BLK * DIL_UNROLL)
    order = sorted(range(len(DIL_BRANCHES)), key=lambda i: -DIL_BRANCHES[i][1])
    for pos, br in enumerate(order):
        dil = DIL_BRANCHES[br][1]
        first = pos == 0

        def trip(t, carry, br=br, dil=dil, first=first):
            scores(br, dil, 2 * t + 1, s_odd)
            absorb(br, dil, 2 * t, s_even, first)
            scores(br, dil, 2 * t + 2, s_even)
            absorb(br, dil, 2 * t + 1, s_odd, first)
            return carry

        scores(br, dil, 0, s_even)
        lax.fori_loop(0, n_groups // 2 - 1, trip, 0)
        scores(br, dil, n_groups - 1, s_odd)
        absorb(br, dil, n_groups - 2, s_even, first)
        absorb(br, dil, n_groups - 1, s_odd, first)

    o_ref[0] = (acc_s[...] / l_s[...]).astype(o_ref.dtype)


def _dilated_attention(proj, bias, *, q_blk0, k_blk0, v_blk0):
    b, s, _ = proj.shape
    n_pairs = DIL_HEADS // 2
    assert (s // (BLK * DIL_UNROLL)) % 2 == 0
    blk = lambda off: pl.BlockSpec((1, s, LANES), lambda bi, p: (bi, 0, off + p))
    state = [pltpu.VMEM((s, LANES), F32) for _ in range(4)]
    padded = [pltpu.VMEM((DIL_PAD + s, LANES), F32) for _ in range(2)]
    score_bufs = [pltpu.VMEM((DIL_UNROLL, 2 * BLK, 2 * BLK), F32) for _ in range(2)]
    return pl.pallas_call(
        functools.partial(_dilated_kernel, seq=s),
        grid=(b, n_pairs),
        in_specs=[blk(q_blk0), blk(k_blk0), blk(v_blk0),
                  pl.BlockSpec((len(DIL_BRANCHES), 1, 2, 2 * BLK, 2 * BLK), lambda bi, p: (0, p, 0, 0, 0))],
        out_specs=pl.BlockSpec((1, s, LANES), lambda bi, p: (bi, 0, p)),
        out_shape=jax.ShapeDtypeStruct((b, s, n_pairs * LANES), BF16),
        scratch_shapes=[state[0], padded[0], padded[1], state[1], state[2], state[3]] + score_bufs,
        compiler_params=_params("parallel", "parallel"),
        name="dilated_attention",
    )(proj, proj, proj, bias)


def _swa_kernel(q_ref, k_ref, v_ref, bias_ref, sink_ref, o_ref, vaug_t, s_even, s_odd, *, tc):
    ci = pl.program_id(1)
    seq = k_ref.shape[1]
    grp = SWA_Q_HEADS // SWA_KV_HEADS
    n_blk = tc // BLK

    @pl.when(ci == 0)
    def _():
        feat = lax.broadcasted_iota(jnp.int32, (LANES, BLK), 0)

        def fill(t, carry):
            vt = jnp.transpose(v_ref[0, pl.ds(pl.multiple_of(t * BLK, BLK), BLK), :].astype(F32))
            for kv in range(SWA_KV_HEADS):
                vaug_t[kv, t] = jnp.where(_own_lanes(feat, kv), vt, 1.0).astype(BF16)
            return carry

        lax.fori_loop(0, seq // BLK, fill, 0)

    lane_q = lax.broadcasted_iota(jnp.int32, (BLK, LANES), 1)
    feat_o = lax.broadcasted_iota(jnp.int32, (LANES, grp * BLK), 0)

    def blocks(bb):
        own = ci * n_blk + bb
        return jnp.maximum(own - 1, 0), own

    def scores(bb, kv, buf):
        prev, own = blocks(bb)
        k2 = jnp.concatenate([k_ref[0, pl.ds(pl.multiple_of(prev * BLK, BLK), BLK), :],
                              k_ref[0, pl.ds(pl.multiple_of(own * BLK, BLK), BLK), :]], axis=0)
        q4 = jnp.concatenate(
            [jnp.where(_own_lanes(lane_q, kv), q_ref[0, bb * BLK:(bb + 1) * BLK, g * LANES:(g + 1) * LANES].astype(F32),
                       0.0) for g in range(grp)], axis=0).astype(BF16)
        s = lax.dot_general(k2, q4, NT_DIMS, preferred_element_type=F32)
        buf[...] = s + bias_ref[kv, jnp.where(own == 0, 1, 0)]

    def absorb(bb, kv, buf):
        prev, own = blocks(bb)
        s = buf[...]
        sink = sink_ref[kv]
        m = jnp.maximum(jnp.max(s, axis=0, keepdims=True), sink)
        p = jnp.exp2(s - m).astype(BF16)
        v2 = jnp.concatenate([vaug_t[kv, prev], vaug_t[kv, own]], axis=1)
        acc = _dot(v2, p)
        den = (acc[HEAD_DIM:HEAD_DIM + 1, :] if kv == 0 else acc[0:1, :]) + jnp.exp2(sink - m)
        return acc / den

    units = [(bb, kv) for bb in range(n_blk) for kv in range(SWA_KV_HEADS)]
    bufs = (s_even, s_odd)
    scores(*units[0], bufs[0])
    outs = {}
    for n, unit in enumerate(units):
        if n + 1 < len(units):
            scores(*units[n + 1], bufs[(n + 1) % 2])
        outs[unit] = absorb(*unit, bufs[n % 2])
    for bb in range(n_blk):
        out_t = jnp.where(feat_o < HEAD_DIM, outs[(bb, 0)], outs[(bb, 1)])
        for g in range(grp):
            o_ref[0, bb * BLK:(bb + 1) * BLK, g * LANES:(g + 1) * LANES] = jnp.transpose(
                out_t[:, g * BLK:(g + 1) * BLK]).astype(o_ref.dtype)


def _swa_attention(proj, bias, sink_row, *, q_blk0, k_blk0, v_blk0, tc=512):
    b, s, _ = proj.shape
    qw = SWA_Q_HEADS * HEAD_DIM
    grp = SWA_Q_HEADS // SWA_KV_HEADS
    return pl.pallas_call(
        functools.partial(_swa_kernel, tc=tc),
        grid=(b, s // tc),
        in_specs=[
            pl.BlockSpec((1, tc, qw), lambda bi, ci: (bi, ci, q_blk0 * LANES // qw)),
            pl.BlockSpec((1, s, LANES), lambda bi, ci: (bi, 0, k_blk0)),
            pl.BlockSpec((1, s, LANES), lambda bi, ci: (bi, 0, v_blk0)),
            pl.BlockSpec(bias.shape, lambda bi, ci: (0, 0, 0, 0)),
            pl.BlockSpec(sink_row.shape, lambda bi, ci: (0, 0, 0)),
        ],
        out_specs=pl.BlockSpec((1, tc, qw), lambda bi, ci: (bi, ci, 0)),
        out_shape=jax.ShapeDtypeStruct((b, s, qw), BF16),
        scratch_shapes=[pltpu.VMEM((SWA_KV_HEADS, s // BLK, LANES, BLK), BF16),
                        pltpu.VMEM((2 * BLK, grp * BLK), F32), pltpu.VMEM((2 * BLK, grp * BLK), F32)],
        compiler_params=_params("parallel", "arbitrary"),
        name="swa_attention",
    )(proj, proj, proj, bias, sink_row)


def _rope128(x, a, bm, bp):
    return x * a + pltpu.roll(x, LANES - 16, axis=1) * bm + pltpu.roll(x, 16, axis=1) * bp


def _proj_odd_kernel(x_ref, g_ref, w_ref, qn_ref, kvn_ref, wuq_ref, wuk_ref, wuv_ref, a_ref, bm_ref, bp_ref,
                     q_out, k_out, v_out, swa_out):
    xb = _rms(x_ref[...], g_ref[...]).astype(BF16)
    w = w_ref
    c_q = _dot(xb, w[:, 0:MLA_Q_RANK])
    c_kv = _dot(xb, w[:, MLA_Q_RANK:MLA_Q_RANK + MLA_KV_RANK])
    o1 = MLA_Q_RANK + MLA_KV_RANK
    kpe = _dot(xb, w[:, o1:o1 + LANES])
    o2 = o1 + LANES
    swa_out[...] = _dot(xb, w[:, o2:]).astype(swa_out.dtype)
    a, bm, bp = a_ref[...], bm_ref[...], bp_ref[...]
    kpe = _rope128(kpe, a, bm, bp)
    cqn = _rms(c_q, qn_ref[...]).astype(BF16)
    ckvn = _rms(c_kv, kvn_ref[...]).astype(BF16)
    v_out[...] = _dot(ckvn, wuv_ref[...]).astype(v_out.dtype)
    q_raw = _dot(cqn, wuq_ref[...])
    k_raw = _dot(ckvn, wuk_ref[...])
    for h in range(MLA_HEADS):
        sl = slice(h * LANES, (h + 1) * LANES)
        q_out[:, sl] = _rope128(q_raw[:, sl], a, bm, bp).astype(q_out.dtype)
        k_out[:, sl] = (k_raw[:, sl] + kpe).astype(k_out.dtype)


def _proj_odd(h, g, w, qn, kvn, wuq, wuk, wuv, rope_a, rope_bm, rope_bp, seq, tm=1024):
    n, d = h.shape
    n_in = w.shape[1]
    n_swa = n_in - (MLA_Q_RANK + MLA_KV_RANK + LANES)
    tiles_per_seq = seq // tm
    full = lambda shape: pl.BlockSpec(shape, lambda i: (0, 0))
    rope = pl.BlockSpec((tm, LANES), lambda i: (i % tiles_per_seq, 0))
    return pl.pallas_call(
        _proj_odd_kernel,
        grid=(n // tm,),
        in_specs=[
            pl.BlockSpec((tm, d), lambda i: (i, 0)), full((1, d)), full(w.shape),
            full(qn.shape), full(kvn.shape), full(wuq.shape), full(wuk.shape), full(wuv.shape),
            rope, rope, rope,
        ],
        out_specs=[
            pl.BlockSpec((tm, MLA_HEADS * LANES), lambda i: (i, 0)),
            pl.BlockSpec((tm, MLA_HEADS * LANES), lambda i: (i, 0)),
            pl.BlockSpec((tm, MLA_HEADS * MLA_V), lambda i: (i, 0)),
            pl.BlockSpec((tm, n_swa), lambda i: (i, 0)),
        ],
        out_shape=[
            jax.ShapeDtypeStruct((n, MLA_HEADS * LANES), BF16),
            jax.ShapeDtypeStruct((n, MLA_HEADS * LANES), BF16),
            jax.ShapeDtypeStruct((n, MLA_HEADS * MLA_V), BF16),
            jax.ShapeDtypeStruct((n, n_swa), BF16),
        ],
        compiler_params=_params("parallel"),
        name="proj_odd",
    )(h, g, w, qn, kvn, wuq, wuk, wuv, rope_a, rope_bm, rope_bp)


MOE_TILE = 1024
ROUTER_TILE = 512


def _out_router_kernel(h_ref, oa_ref, ob_ref, wo_ref, g_ref, wr_hi_ref, wr_lo_ref, br_ref,
                       h_out, hn_out, gates_out, counts_out, pos_col_out, pos_row_out, tri_ref, seen_ref):
    i = pl.program_id(0)
    tm = h_ref.shape[0]
    half = oa_ref.shape[1]
    h = h_ref[...] + _dot(oa_ref[...], wo_ref[0:half, :]) + _dot(ob_ref[...], wo_ref[half:, :])
    h_out[...] = h
    hn = _rms(h, g_ref[...])
    hn_out[...] = hn.astype(hn_out.dtype)
    z = _dot_hi(hn, wr_hi_ref[...], wr_lo_ref[...]) + br_ref[...]
    lane = lax.broadcasted_iota(jnp.int32, z.shape, 1)
    big = jnp.int32(LANES)
    is_grp = (lane >= N_EXPERTS) & (lane < N_EXPERTS + N_GROUPS)
    zg = jnp.where(is_grp, z, -jnp.inf)
    g_max = jnp.max(zg, axis=-1, keepdims=True)
    g_w = 1.0 / jnp.sum(jnp.exp(zg - g_max), axis=-1, keepdims=True)
    g_idx = jnp.min(jnp.where(zg == g_max, lane - N_EXPERTS, big), axis=-1, keepdims=True)
    in_grp = (lane < N_EXPERTS) & ((lane // EXPERTS_PER_GROUP) == g_idx)
    ze = jnp.where(in_grp, z, -jnp.inf)
    v1 = jnp.max(ze, axis=-1, keepdims=True)
    i1 = jnp.min(jnp.where(ze == v1, lane, big), axis=-1, keepdims=True)
    ze2 = jnp.where(lane == i1, -jnp.inf, ze)
    v2 = jnp.max(ze2, axis=-1, keepdims=True)
    i2 = jnp.min(jnp.where(ze2 == v2, lane, big), axis=-1, keepdims=True)
    e2 = jnp.exp(v2 - v1)
    w1 = g_w / (1.0 + e2)
    w2 = g_w * e2 / (1.0 + e2)
    gates_out[...] = jnp.where(lane == i1, w1, 0.0) + jnp.where(lane == i2, w2, 0.0)

    @pl.when(i == 0)
    def _():
        ri = lax.broadcasted_iota(jnp.int32, (tm, tm), 0)
        ci = lax.broadcasted_iota(jnp.int32, (tm, tm), 1)
        tri_ref[...] = jnp.where(ci < ri, 1.0, 0.0).astype(BF16)

    @pl.when(i % (MOE_TILE // tm) == 0)
    def _():
        seen_ref[...] = jnp.zeros_like(seen_ref)

    routed = jnp.where((lane == g_idx) & (lane < N_GROUPS), 1.0, 0.0)
    pos = jnp.where(routed > 0.0, _dot(tri_ref[...], routed.astype(BF16)) + seen_ref[...], -1.0)
    pos_col_out[...] = pos
    pos_row_out[0] = jnp.transpose(pos)[0:8, :]
    here = jnp.sum(routed, axis=0, keepdims=True)
    seen_ref[...] += here
    counts_out[0] = jnp.broadcast_to(here, counts_out.shape[1:])


def _out_router(h, oa, ob, wo, g, wr_hi, wr_lo, br):
    n, d = h.shape
    tm = ROUTER_TILE
    per_tile = MOE_TILE // tm
    half = oa.shape[1]
    full = lambda shape: pl.BlockSpec(shape, lambda i: (0, 0))
    tile = lambda w: pl.BlockSpec((tm, w), lambda i: (i, 0))
    return pl.pallas_call(
        _out_router_kernel,
        grid=(n // tm,),
        in_specs=[tile(d), tile(half), tile(half), full(wo.shape), full((1, d)),
                  full(wr_hi.shape), full(wr_lo.shape), full((1, LANES))],
        out_specs=[tile(d), tile(d), tile(LANES), pl.BlockSpec((1, 8, LANES), lambda i: (i, 0, 0)),
                   tile(LANES), pl.BlockSpec((1, 8, tm), lambda i: (i // per_tile, 0, i % per_tile))],
        out_shape=[
            jax.ShapeDtypeStruct((n, d), F32),
            jax.ShapeDtypeStruct((n, d), BF16),
            jax.ShapeDtypeStruct((n, LANES), F32),
            jax.ShapeDtypeStruct((n // tm, 8, LANES), F32),
            jax.ShapeDtypeStruct((n, LANES), F32),
            jax.ShapeDtypeStruct((n // MOE_TILE, 8, MOE_TILE), F32),
        ],
        scratch_shapes=[pltpu.VMEM((tm, tm), BF16), pltpu.VMEM((1, LANES), F32)],
        compiler_params=_params("arbitrary"),
        name="out_router",
    )(h, oa, ob, wo, g, wr_hi, wr_lo, br)


MOE_CHUNK = 320
MOE_ROWS = 1024
MOE_ALIGN = 16
MOE_EXPERTS_PER_STEP = 4


def _moe_sorted_tiles(n):
    n_seg = (n // MOE_TILE) * N_GROUPS
    rows = n + n_seg * (MOE_ALIGN - 1) + N_GROUPS * (MOE_CHUNK + MOE_ROWS - 1)
    return -(-rows // MOE_ROWS)


def _moe_tables(counts, n):
    n_rt = _moe_sorted_tiles(n)
    seg_len = (counts + (MOE_ALIGN - 1)) // MOE_ALIGN * MOE_ALIGN
    group_len = seg_len.sum(axis=0)
    group_span = (group_len + MOE_CHUNK + MOE_ROWS - 1) // MOE_ROWS * MOE_ROWS
    group_end = jnp.cumsum(group_span)
    group_start = group_end - group_span
    seg_off = group_start[None, :] + jnp.cumsum(seg_len, axis=0) - seg_len
    rt_start = jnp.arange(n_rt, dtype=jnp.int32) * MOE_ROWS
    rt_group = jnp.minimum(jnp.sum(rt_start[:, None] >= group_end[None, :], axis=1), N_GROUPS - 1)
    rt_valid = jnp.clip(group_start[rt_group] + group_len[rt_group] - rt_start, 0, MOE_ROWS)
    n_used = (group_end[-1] // MOE_ROWS).reshape(1)
    i32 = lambda a: a.astype(jnp.int32)
    return i32(seg_off.reshape(-1)), i32(rt_group), i32(rt_valid), i32(n_used)


def _moe_chunks(count, body):
    def one_chunk(k, carry):
        body(k * MOE_CHUNK, MOE_CHUNK)
        return carry

    lax.fori_loop(0, (count + (MOE_CHUNK - 1)) // MOE_CHUNK, one_chunk, 0)


def _moe_pack_kernel(counts_ref, off_ref, hn_ref, gates_ref, pos_ref, xs_in, xs_ref, xbuf, sem, issued_ref):
    del xs_in
    i = pl.program_id(0)
    d = hn_ref.shape[1]

    @pl.when(i == 0)
    def _():
        issued_ref[0] = 0

    def write(slot):
        return pltpu.make_async_copy(xbuf.at[slot], xs_ref.at[pl.ds(0, MOE_CHUNK)], sem.at[slot])

    lane = lax.broadcasted_iota(jnp.int32, (hn_ref.shape[0], LANES), 1)
    gates = gates_ref[...]
    gates_hi = gates.astype(BF16).astype(F32)
    gates_lo = gates - gates_hi
    for grp in range(N_GROUPS):
        seg = i * N_GROUPS + grp
        first = grp * EXPERTS_PER_GROUP
        g_hi = gates_hi if first == 0 else pltpu.roll(gates_hi, LANES - first, axis=1)
        g_lo = pltpu.roll(gates_lo, (LANES - first + EXPERTS_PER_GROUP) % LANES, axis=1)
        gate_cols = jnp.where(lane < EXPERTS_PER_GROUP, g_hi, jnp.where(lane < 2 * EXPERTS_PER_GROUP, g_lo, 0.0))
        gate_cols = gate_cols.astype(BF16)
        pos_row = pos_ref[0, grp:grp + 1, :]

        def chunk(first_pos, rows, seg=seg, gate_cols=gate_cols, pos_row=pos_row):
            n_done = issued_ref[0]
            slot = n_done % 2
            slot_pos = lax.broadcasted_iota(jnp.int32, (rows, 1), 0).astype(F32) + first_pos.astype(F32)
            gather = jnp.where(pos_row == slot_pos, 1.0, 0.0).astype(BF16)
            xbuf[slot, :, 0:d] = _dot(gather, hn_ref[...]).astype(BF16)
            xbuf[slot, :, d:] = _dot(gather, gate_cols).astype(BF16)

            @pl.when(n_done > 0)
            def _():
                write(1 - slot).wait()

            dst = pl.multiple_of(off_ref[seg] + first_pos, MOE_ALIGN)
            pltpu.make_async_copy(xbuf.at[slot], xs_ref.at[pl.ds(dst, MOE_CHUNK)], sem.at[slot]).start()
            issued_ref[0] = n_done + 1

        _moe_chunks(counts_ref[seg], chunk)

    @pl.when((i == pl.num_programs(0) - 1) & (issued_ref[0] > 0))
    def _():
        write((issued_ref[0] - 1) % 2).wait()


def _moe_pack(counts, seg_off, hn, gates, pos_row, xs_init):
    n, d = hn.shape
    rows = xs_init.shape[0]
    grid_spec = pltpu.PrefetchScalarGridSpec(
        num_scalar_prefetch=2,
        grid=(n // MOE_TILE,),
        in_specs=[
            pl.BlockSpec((MOE_TILE, d), lambda i, c, o: (i, 0)),
            pl.BlockSpec((MOE_TILE, LANES), lambda i, c, o: (i, 0)),
            pl.BlockSpec((1, 8, MOE_TILE), lambda i, c, o: (i, 0, 0)),
            pl.BlockSpec(memory_space=pl.ANY),
        ],
        out_specs=pl.BlockSpec(memory_space=pl.ANY),
        scratch_shapes=[pltpu.VMEM((2, MOE_CHUNK, d + LANES), BF16), pltpu.SemaphoreType.DMA((2,)),
                        pltpu.SMEM((1,), jnp.int32)],
    )
    return pl.pallas_call(
        _moe_pack_kernel,
        grid_spec=grid_spec,
        out_shape=jax.ShapeDtypeStruct((rows, d + LANES), BF16),
        input_output_aliases={5: 0},
        compiler_params=_params("arbitrary"),
        name="moe_pack",
    )(counts, seg_off, hn, gates, pos_row, xs_init)


def _moe_experts_kernel(grp_ref, valid_ref, used_ref, xs_ref, wg_ref, wu_ref, wd_ref, ys_ref, acc_ref,
                        wg_b, wu_b, wd_b):
    del grp_ref, used_ref
    rt = pl.program_id(0)
    j = pl.program_id(1)
    d = ys_ref.shape[1]
    valid = valid_ref[rt]

    @pl.when((valid > 0) & (j == 0))
    def _():
        acc_ref[...] = jnp.zeros_like(acc_ref)

    def experts(x, gate_cols, w_gate, w_up, w_down):
        lane = lax.broadcasted_iota(jnp.int32, gate_cols.shape, 1)
        acts = []
        for e in range(MOE_EXPERTS_PER_STEP):
            idx = j * MOE_EXPERTS_PER_STEP + e
            mine = (lane == idx) | (lane == idx + EXPERTS_PER_GROUP)
            gate = jnp.sum(jnp.where(mine, gate_cols, 0.0), axis=-1, keepdims=True)
            a = _dot(x, w_gate(e))
            u = _dot(x, w_up(e))
            acts.append(((a * jax.nn.sigmoid(a)) * u * gate).astype(BF16))
        return _dot(jnp.concatenate(acts, axis=1), w_down())

    @pl.when(valid == MOE_ROWS)
    def _():
        acc_ref[...] += experts(
            xs_ref[:, 0:d], xs_ref[:, d:].astype(F32), lambda e: wg_ref[0, 0, e].astype(BF16),
            lambda e: wu_ref[0, 0, e].astype(BF16), lambda: wd_ref[0, 0].astype(BF16).reshape(wd_b.shape))

    @pl.when((valid > 0) & (valid < MOE_ROWS))
    def _():
        wg_b[...] = wg_ref[0, 0].astype(BF16)
        wu_b[...] = wu_ref[0, 0].astype(BF16)
        wd_b[...] = wd_ref[0, 0].astype(BF16).reshape(wd_b.shape)

        def rows_block(t, carry):
            rows = pl.ds(pl.multiple_of(t * MXU_TILE, MXU_TILE), MXU_TILE)
            acc_ref[rows, :] += experts(xs_ref[rows, 0:d], xs_ref[rows, d:].astype(F32), lambda e: wg_b[e],
                                        lambda e: wu_b[e], lambda: wd_b[...])
            return carry

        lax.fori_loop(0, (valid + (MXU_TILE - 1)) // MXU_TILE, rows_block, 0)

    @pl.when((valid > 0) & (j == pl.num_programs(1) - 1))
    def _():
        row = lax.broadcasted_iota(jnp.int32, (acc_ref.shape[0], 1), 0)
        ys_ref[...] = jnp.where(row < valid, acc_ref[...], 0.0).astype(ys_ref.dtype)

    @pl.when((valid == 0) & (j == pl.num_programs(1) - 1))
    def _():
        ys_ref[...] = jnp.zeros_like(ys_ref)


def _moe_experts(rt_group, rt_valid, n_used, xs, w_gate, w_up, w_down, layer):
    rows = xs.shape[0]
    d = w_gate.shape[-2]
    n_rt = rows // MOE_ROWS
    eps = MOE_EXPERTS_PER_STEP
    n_steps = EXPERTS_PER_GROUP // eps

    def tile_idx(rt, j, grp, valid, used):
        return jnp.minimum(rt, used[0] - 1)

    def w_idx(rt, j, grp, valid, used):
        live = rt < used[0]
        return (layer, grp[tile_idx(rt, j, grp, valid, used)], jnp.where(live, j, n_steps - 1), 0, 0)

    grid_spec = pltpu.PrefetchScalarGridSpec(
        num_scalar_prefetch=3,
        grid=(n_rt, n_steps),
        in_specs=[
            pl.BlockSpec((MOE_ROWS, d + LANES), lambda rt, j, grp, valid, used: (tile_idx(rt, j, grp, valid, used), 0)),
            pl.BlockSpec((1, 1, eps, d, D_EXPERT), w_idx),
            pl.BlockSpec((1, 1, eps, d, D_EXPERT), w_idx),
            pl.BlockSpec((1, 1, eps, D_EXPERT, d), w_idx),
        ],
        out_specs=pl.BlockSpec((MOE_ROWS, d), lambda rt, j, grp, valid, used: (rt, 0)),
        scratch_shapes=[pltpu.VMEM((MOE_ROWS, d), F32), pltpu.VMEM((eps, d, D_EXPERT), BF16),
                        pltpu.VMEM((eps, d, D_EXPERT), BF16), pltpu.VMEM((eps * D_EXPERT, d), BF16)],
    )
    return pl.pallas_call(
        _moe_experts_kernel,
        grid_spec=grid_spec,
        out_shape=jax.ShapeDtypeStruct((rows, d), BF16),
        compiler_params=_params("arbitrary", "arbitrary"),
        name="moe_experts",
    )(rt_group, rt_valid, n_used, xs, w_gate, w_up, w_down)


def _moe_combine_kernel(counts_ref, off_ref, pos_ref, h_ref, fg_ref, ys_ref, o_ref, ybuf, ymore, sem, *, final_norm):
    i = pl.program_id(0)

    def read(tile, grp):
        src = pl.multiple_of(off_ref[tile * N_GROUPS + grp], MOE_ALIGN)
        slot = tile % 2
        return pltpu.make_async_copy(ys_ref.at[pl.ds(src, MOE_CHUNK)],
                                     ybuf.at[slot, pl.ds(grp * MOE_CHUNK, MOE_CHUNK)], sem.at[slot, grp])

    def fetch(tile):
        for grp in range(N_GROUPS):
            @pl.when(counts_ref[tile * N_GROUPS + grp] > 0)
            def _(grp=grp):
                read(tile, grp).start()

    @pl.when(i == 0)
    def _():
        ybuf[...] = jnp.zeros_like(ybuf)
        fetch(i)

    @pl.when(i + 1 < pl.num_programs(0))
    def _():
        fetch(i + 1)

    slot_pos = lax.broadcasted_iota(jnp.int32, (1, MOE_CHUNK), 1).astype(F32)
    scatter = jnp.concatenate(
        [jnp.where(pos_ref[:, grp:grp + 1] == slot_pos, 1.0, 0.0).astype(BF16) for grp in range(N_GROUPS)], axis=1)
    for grp in range(N_GROUPS):
        @pl.when(counts_ref[i * N_GROUPS + grp] > 0)
        def _(grp=grp):
            read(i, grp).wait()

    o_ref[...] = h_ref[...] + _dot(scatter, ybuf[i % 2])
    for grp in range(N_GROUPS):
        count = counts_ref[i * N_GROUPS + grp]
        pos_col = pos_ref[:, grp:grp + 1]

        def more(k, carry, grp=grp, pos_col=pos_col):
            src = pl.multiple_of(off_ref[i * N_GROUPS + grp] + k * MOE_CHUNK, MOE_ALIGN)
            pltpu.sync_copy(ys_ref.at[pl.ds(src, MOE_CHUNK)], ymore)
            scatter = jnp.where(pos_col == slot_pos + (k * MOE_CHUNK).astype(F32), 1.0, 0.0).astype(BF16)
            o_ref[...] += _dot(scatter, ymore[...])
            return carry

        lax.fori_loop(1, (count + (MOE_CHUNK - 1)) // MOE_CHUNK, more, 0)

    if final_norm:
        o_ref[...] = _rms(o_ref[...], fg_ref[...])


def _moe_combine(counts, seg_off, pos_col, h, fg, ys, final_norm):
    n, d = h.shape
    grid_spec = pltpu.PrefetchScalarGridSpec(
        num_scalar_prefetch=2,
        grid=(n // MOE_TILE,),
        in_specs=[
            pl.BlockSpec((MOE_TILE, LANES), lambda i, c, o: (i, 0)),
            pl.BlockSpec((MOE_TILE, d), lambda i, c, o: (i, 0)),
            pl.BlockSpec((1, d), lambda i, c, o: (0, 0)),
            pl.BlockSpec(memory_space=pl.ANY),
        ],
        out_specs=pl.BlockSpec((MOE_TILE, d), lambda i, c, o: (i, 0)),
        scratch_shapes=[pltpu.VMEM((2, N_GROUPS * MOE_CHUNK, d), BF16), pltpu.VMEM((MOE_CHUNK, d), BF16),
                        pltpu.SemaphoreType.DMA((2, N_GROUPS))],
    )
    return pl.pallas_call(
        functools.partial(_moe_combine_kernel, final_norm=final_norm),
        grid_spec=grid_spec,
        out_shape=jax.ShapeDtypeStruct((n, d), F32),
        compiler_params=_params("arbitrary"),
        name="moe_combine",
    )(counts, seg_off, pos_col, h, fg, ys)


def _alibi_slopes(n):
    return jnp.exp2(-8.0 * jnp.arange(1, n + 1, dtype=F32) / n)


def _band_bias(slopes, max_steps, step_dist):
    steps = (jnp.arange(BLK)[:, None] + BLK) - jnp.arange(2 * BLK)[None, :]
    in_band = (steps >= 0) & (steps <= max_steps)
    dist = (steps * step_dist).astype(F32)
    return jnp.where(in_band[None], -slopes.astype(F32)[:, None, None] * dist[None], NEG)


def _rope_tables(s):
    inv = ROPE_THETA ** (-jnp.arange(0, MLA_ROPE, 2, dtype=F32) / MLA_ROPE)
    ang = jnp.arange(s, dtype=F32)[:, None] * inv[None, :]
    cos, sin = jnp.cos(ang), jnp.sin(ang)
    half = MLA_ROPE // 2
    zeros_tail = jnp.zeros((s, LANES - MLA_NOPE - MLA_ROPE), F32)
    a = jnp.concatenate([jnp.ones((s, MLA_NOPE), F32), cos, cos, zeros_tail], axis=1)
    zeros_nope = jnp.zeros((s, MLA_NOPE), F32)
    zeros_half = jnp.zeros((s, half), F32)
    bm = jnp.concatenate([zeros_nope, -sin, zeros_half, zeros_tail], axis=1)
    bp = jnp.concatenate([zeros_nope, zeros_half, sin, zeros_tail], axis=1)
    return a, bm, bp


def _swa_head_order(w, axis):
    grp = SWA_Q_HEADS // SWA_KV_HEADS
    shape = w.shape
    w = w.reshape(shape[:axis] + (SWA_KV_HEADS, grp, HEAD_DIM) + shape[axis + 1:])
    return jnp.swapaxes(w, axis, axis + 1).reshape(shape)


def _pad_cols(w, width):
    return jnp.pad(w, ((0, 0), (0, width - w.shape[1])))


def _router_weights(w_group, b_group, w_router, b_router):
    w = _pad_cols(jnp.concatenate([w_router, w_group], axis=1), LANES)
    b = _pad_cols(jnp.concatenate([b_router, b_group])[None, :], LANES)
    hi, lo = _hi_lo(w)
    return hi, lo, b


def kernel(x, attn_norm, ffn_norm, final_norm, e_w_in, e_b_f, e_w_out, o_w_in, o_q_norm, o_kv_norm, o_w_uq,
           o_w_ukv, o_sinks, o_w_out, moe_w_group, moe_b_group, moe_w_router, moe_b_router, moe_w_gate,
           moe_w_up, moe_w_down):
    b, s, d = x.shape
    n = b * s
    depth = attn_norm.shape[0]
    assert s % (BLK * DIL_BRANCHES[-1][1]) == 0 and d == D_MODEL
    h = x.reshape(n, d)

    dil_slopes = _alibi_slopes(DIL_HEADS)
    dil_bias = jnp.stack([_band_bias(dil_slopes, w // dl, dl) for (w, dl) in DIL_BRANCHES]) * LOG2E
    own_half = jnp.arange(2 * BLK) >= BLK
    dil_bias = jnp.stack([dil_bias, jnp.where(own_half, dil_bias, NEG)], axis=2)
    dil_bias = dil_bias.reshape(len(DIL_BRANCHES), DIL_HEADS // 2, 2, 2, BLK, 2 * BLK)
    dil_bias = dil_bias.transpose(0, 1, 3, 2, 4, 5).reshape(len(DIL_BRANCHES), DIL_HEADS // 2, 2, 2 * BLK, 2 * BLK)
    swa_grp = SWA_Q_HEADS // SWA_KV_HEADS
    swa_bias = _band_bias(_alibi_slopes(SWA_Q_HEADS), SWA_WINDOW - 1, 1) * LOG2E
    swa_bias = swa_bias.reshape(SWA_KV_HEADS, swa_grp, BLK, 2 * BLK).transpose(0, 3, 1, 2)
    swa_bias = swa_bias.reshape(SWA_KV_HEADS, 2 * BLK, swa_grp * BLK)
    swa_bias = jnp.stack([swa_bias, jnp.where(own_half[None, :, None], swa_bias, NEG)], axis=1)
    rope_a, rope_bm, rope_bp = _rope_tables(s)

    xs = jnp.zeros((_moe_sorted_tiles(n) * MOE_ROWS, d + LANES), BF16)
    for layer in range(depth):
        i = layer // 2
        g_attn = attn_norm[layer][None, :]
        if layer % 2 == 0:
            w_in = e_w_in[i]
            hq = FOX_HEADS * HEAD_DIM
            scale = HEAD_DIM ** -0.5
            cols = [w_in[:, 0:hq] * (scale * LOG2E), w_in[:, hq:2 * hq], w_in[:, 2 * hq:3 * hq]]
            o = 3 * hq + FOX_HEADS
            cols += [w_in[:, o:o + hq] * (scale * LOG2E), w_in[:, o + hq:o + 2 * hq], w_in[:, o + 2 * hq:o + 3 * hq]]
            w_main = jnp.concatenate(cols, axis=1).astype(BF16)
            wf_hi, wf_lo = _hi_lo(_pad_cols(w_in[:, 3 * hq:o], LANES))
            b_f = _pad_cols(e_b_f[i][None, :], LANES)
            proj, c = _proj_even(h, g_attn, w_main, wf_hi, wf_lo, b_f, s)
            proj = proj.reshape(b, s, -1)
            o_a = _causal_attention(proj, proj, proj, c.reshape(b, s, LANES), q_blk0=0, k_blk0=4, v_blk0=8,
                                    n_pairs=FOX_HEADS // 2)
            o_b = _dilated_attention(proj, dil_bias, q_blk0=12, k_blk0=16, v_blk0=20)
            w_out = e_w_out[i].astype(BF16)
        else:
            w_in = o_w_in[i]
            o1 = MLA_Q_RANK + MLA_KV_RANK
            o2 = o1 + MLA_ROPE
            sq = SWA_Q_HEADS * HEAD_DIM
            kpe_cols = jnp.pad(w_in[:, o1:o2], ((0, 0), (MLA_NOPE, LANES - MLA_NOPE - MLA_ROPE)))
            w_main = jnp.concatenate(
                [w_in[:, :o1], kpe_cols, _swa_head_order(w_in[:, o2:o2 + sq] * (HEAD_DIM ** -0.5 * LOG2E), axis=1),
                 w_in[:, o2 + sq:]],
                axis=1).astype(BF16)
            dq = MLA_NOPE + MLA_ROPE
            wuq = o_w_uq[i].reshape(MLA_Q_RANK, MLA_HEADS, dq) * (dq ** -0.5 * LOG2E)
            wuq = jnp.pad(wuq, ((0, 0), (0, 0), (0, LANES - dq))).reshape(MLA_Q_RANK, MLA_HEADS * LANES)
            wukv = o_w_ukv[i].reshape(MLA_KV_RANK, MLA_HEADS, MLA_NOPE + MLA_V)
            wuk = jnp.pad(wukv[:, :, :MLA_NOPE], ((0, 0), (0, 0), (0, LANES - MLA_NOPE)))
            wuk = wuk.reshape(MLA_KV_RANK, MLA_HEADS * LANES)
            wuv = wukv[:, :, MLA_NOPE:].reshape(MLA_KV_RANK, MLA_HEADS * MLA_V)
            q_full, k_full, v_mla, swa = _proj_odd(
                h, g_attn, w_main, o_q_norm[i][None, :], o_kv_norm[i][None, :], wuq.astype(BF16),
                wuk.astype(BF16), wuv.astype(BF16), rope_a, rope_bm, rope_bp, s)
            o_a = _causal_attention(q_full.reshape(b, s, -1), k_full.reshape(b, s, -1), v_mla.reshape(b, s, -1),
                                    None, q_blk0=0, k_blk0=0, v_blk0=0, n_pairs=MLA_HEADS // 2)
            grp = SWA_Q_HEADS // SWA_KV_HEADS
            sink_row = jnp.repeat(o_sinks[i].reshape(SWA_KV_HEADS, grp) * LOG2E, BLK, axis=1)[:, None, :]
            o_b = _swa_attention(swa.reshape(b, s, -1), swa_bias, sink_row, q_blk0=0, k_blk0=4, v_blk0=5)
            half = MLA_HEADS * MLA_V
            w_out = jnp.concatenate([o_w_out[i][:half], _swa_head_order(o_w_out[i][half:], axis=0)], axis=0)
            w_out = w_out.astype(BF16)

        wr_hi, wr_lo, b_r = _router_weights(moe_w_group[layer], moe_b_group[layer], moe_w_router[layer],
                                            moe_b_router[layer])
        h, hn, gates, counts, pos_col, pos_row = _out_router(
            h, o_a.reshape(n, -1), o_b.reshape(n, -1), w_out, ffn_norm[layer][None, :], wr_hi, wr_lo, b_r)
        counts = counts[:, 0, :N_GROUPS].reshape(n // MOE_TILE, MOE_TILE // ROUTER_TILE, N_GROUPS).sum(axis=1)
        counts = counts.astype(jnp.int32)
        seg_off, rt_group, rt_valid, n_used = _moe_tables(counts, n)
        counts = counts.reshape(-1)
        xs = _moe_pack(counts, seg_off, hn, gates, pos_row, xs)
        ys = _moe_experts(rt_group, rt_valid, n_used, xs, moe_w_gate, moe_w_up, moe_w_down, layer)
        h = _moe_combine(counts, seg_off, pos_col, h, final_norm[None, :], ys, final_norm=layer == depth - 1)
    return h.reshape(b, s, d)
```

```python
import functools

import jax
import jax.numpy as jnp
from jax import lax
from jax.experimental import pallas as pl
from jax.experimental.pallas import tpu as pltpu

F32 = jnp.float32
BF16 = jnp.bfloat16

D_MODEL = 1024
HEAD_DIM = 64
BLK = 128
NEG = -1e30
RMS_EPS = 1e-6
FOX_HEADS = 8
DIL_HEADS = 8
DIL_BRANCHES = ((128, 1), (512, 4), (2048, 16))
MLA_HEADS = 8
MLA_Q_RANK = 384
MLA_KV_RANK = 256
MLA_NOPE = 64
MLA_ROPE = 32
MLA_V = 64
ROPE_THETA = 10000.0
SWA_Q_HEADS = 8
SWA_KV_HEADS = 2
SWA_WINDOW = 128
N_GROUPS = 4
EXPERTS_PER_GROUP = 8
N_EXPERTS = N_GROUPS * EXPERTS_PER_GROUP
D_EXPERT = 256

LANES = 128
MXU_TILE = 256

PROJ_EVEN_TILE = 512
PROJ_ODD_TILE = 1024
PROJ_COLS = 2 * MXU_TILE
ATTN_BLOCK = 512
SWA_CHUNK = 512
ROUTER_TILE = 512
MOE_TILE = 1024
VMEM_LIMIT = 56 * 1024 * 1024

NT_DIMS = (((1,), (1,)), ((), ()))
LOG2E = 1.4426950408889634


def _params(*sem):
    return pltpu.CompilerParams(dimension_semantics=sem, vmem_limit_bytes=VMEM_LIMIT)


def _dot(a, b):
    return jnp.dot(a, b, preferred_element_type=F32)


def _rms(x, g):
    return x * lax.rsqrt(jnp.mean(x * x, axis=-1, keepdims=True) + RMS_EPS) * g


def _hi_lo(w):
    hi = w.astype(BF16)
    return hi, (w - hi.astype(F32)).astype(BF16)


def _dot_hi(x, w_hi, w_lo):
    x_hi = x.astype(BF16)
    x_lo = (x - x_hi.astype(F32)).astype(BF16)
    n = w_hi.shape[1]
    both = _dot(x_hi, jnp.concatenate([w_hi, w_lo], axis=1))
    return both[:, :n] + (both[:, n:] + _dot(x_lo, w_hi))


def _split3(x):
    x1 = x.astype(BF16)
    r1 = x - x1.astype(F32)
    x2 = r1.astype(BF16)
    x3 = (r1 - x2.astype(F32)).astype(BF16)
    return x1, x2, x3


def _proj_even_kernel(x_ref, g_ref, w_ref, wf_hi_ref, wf_lo_ref, bf_ref, out_ref, c_ref, carry_ref, tri_ref, *,
                      tiles_per_seq):
    xn = _rms(x_ref[...], g_ref[...])
    xb = xn.astype(BF16)
    n_out = out_ref.shape[1]
    for c in range(0, n_out, PROJ_COLS):
        out_ref[:, c:c + PROJ_COLS] = _dot(xb, w_ref[:, c:c + PROJ_COLS]).astype(out_ref.dtype)
    z = _dot_hi(xn, wf_hi_ref[...], wf_lo_ref[...]) + bf_ref[...]
    logf = jnp.minimum(z, 0.0) - jnp.log1p(jnp.exp(-jnp.abs(z)))

    @pl.when(pl.program_id(0) % tiles_per_seq == 0)
    def _():
        carry_ref[...] = jnp.zeros_like(carry_ref)

    sub = tri_ref.shape[0]

    @pl.when(pl.program_id(0) == 0)
    def _():
        ri = lax.broadcasted_iota(jnp.int32, (sub, sub), 0)
        ci = lax.broadcasted_iota(jnp.int32, (sub, sub), 1)
        tri_ref[...] = jnp.where(ci <= ri, 1.0, 0.0).astype(BF16)

    lower = tri_ref[...]
    carry = carry_ref[...]
    for r in range(0, logf.shape[0], sub):
        l1, l2, l3 = _split3(logf[r:r + sub])
        c = (_dot(lower, l1) + (_dot(lower, l2) + _dot(lower, l3))) + carry
        c_ref[r:r + sub, :] = c
        carry = c[sub - 1:sub, :]
    carry_ref[...] = carry


def _proj_even(h, g, w, wf_hi, wf_lo, bf, seq):
    n, d = h.shape
    tm = PROJ_EVEN_TILE
    n_out = w.shape[1]
    return pl.pallas_call(
        functools.partial(_proj_even_kernel, tiles_per_seq=seq // tm),
        grid=(n // tm,),
        in_specs=[
            pl.BlockSpec((tm, d), lambda i: (i, 0)),
            pl.BlockSpec((1, d), lambda i: (0, 0)),
            pl.BlockSpec((d, n_out), lambda i: (0, 0)),
            pl.BlockSpec((d, LANES), lambda i: (0, 0)),
            pl.BlockSpec((d, LANES), lambda i: (0, 0)),
            pl.BlockSpec((1, LANES), lambda i: (0, 0)),
        ],
        out_specs=[
            pl.BlockSpec((tm, n_out), lambda i: (i, 0)),
            pl.BlockSpec((tm, LANES), lambda i: (i, 0)),
        ],
        out_shape=[
            jax.ShapeDtypeStruct((n, n_out), BF16),
            jax.ShapeDtypeStruct((n, LANES), F32),
        ],
        scratch_shapes=[pltpu.VMEM((1, LANES), F32), pltpu.VMEM((MXU_TILE, MXU_TILE), BF16)],
        compiler_params=_params("arbitrary"),
        name="proj_even",
    )(h, g, w, wf_hi, wf_lo, bf)


def _own_lanes(lane, h):
    return lane < HEAD_DIM if h == 0 else lane >= HEAD_DIM


def _causal_kernel(*refs, fox, tq, tk):
    if fox:
        q_ref, k_ref, v_ref, c_ref, o_ref, vaug_t, s_even, s_odd, kaug = refs
    else:
        q_ref, k_ref, v_ref, o_ref, vaug_t, s_even, s_odd = refs
    pair = pl.program_id(1)
    qi = pl.program_id(2)
    seq = v_ref.shape[1]

    @pl.when(qi == 0)
    def _():
        lane = lax.broadcasted_iota(jnp.int32, (tk, LANES), 1)
        feat = lax.broadcasted_iota(jnp.int32, (LANES, tk), 0)

        def fill(t, carry):
            rows = pl.ds(pl.multiple_of(t * tk, tk), tk)
            vt = jnp.transpose(v_ref[0, rows, :].astype(F32))
            if fox:
                kp = k_ref[0, rows, :].astype(F32)
                c = c_ref[0, rows, :]
            for h in range(2):
                vaug_t[h, t] = jnp.where(_own_lanes(feat, h), vt, 1.0).astype(BF16)
                if fox:
                    ch = jnp.sum(jnp.where(lane == 2 * pair + h, c, 0.0), axis=-1, keepdims=True)
                    c1, c2, c3 = _split3(ch * (-LOG2E))
                    base = HEAD_DIM if h == 0 else 0
                    extra = jnp.where(lane == base, c1.astype(F32),
                                      jnp.where(lane == base + 1, c2.astype(F32),
                                                jnp.where(lane == base + 2, c3.astype(F32), 0.0)))
                    kaug[h, rows, :] = jnp.where(_own_lanes(lane, h), kp, extra).astype(BF16)
            return carry

        lax.fori_loop(0, seq // tk, fill, 0)

    lane_q = lax.broadcasted_iota(jnp.int32, (tq, LANES), 1)
    qs = []
    for h in range(2):
        if fox:
            base = HEAD_DIM if h == 0 else 0
            ones = jnp.where((lane_q >= base) & (lane_q < base + 3), 1.0, 0.0)
            qs.append(jnp.where(_own_lanes(lane_q, h), q_ref[0].astype(F32), ones).astype(BF16))
        else:
            qs.append(q_ref[0, :, h * LANES:(h + 1) * LANES])

    key = lax.broadcasted_iota(jnp.int32, (tk, tq), 0)
    qry = lax.broadcasted_iota(jnp.int32, (tk, tq), 1)

    def scores(j, buf):
        start = pl.multiple_of(j * tk, tk)
        for h in range(2):
            if fox:
                kj = kaug[h, pl.ds(start, tk), :]
            else:
                kj = k_ref[0, pl.ds(start, tk), h * LANES:(h + 1) * LANES]
            buf[h] = lax.dot_general(kj, qs[h], NT_DIMS, preferred_element_type=F32)

    def absorb(j, buf, carry, masked):
        new = []
        for h in range(2):
            m, acc = carry[h]
            s = buf[h]
            if masked:
                s = jnp.where(key <= qry, s, NEG)
            m_new = jnp.maximum(m, jnp.max(s, axis=0, keepdims=True))
            p = jnp.exp2(s - m_new)
            acc = jnp.exp2(m - m_new) * acc + _dot(vaug_t[h, j], p.astype(BF16))
            new.append((m_new, acc))
        return tuple(new)

    def finish(carry):
        (_, acc0), (_, acc1) = carry
        feat_q = lax.broadcasted_iota(jnp.int32, (LANES, tq), 0)
        out_t = jnp.where(feat_q < HEAD_DIM, acc0 / acc0[HEAD_DIM:HEAD_DIM + 1, :], acc1 / acc1[0:1, :])
        o_ref[0] = jnp.transpose(out_t).astype(o_ref.dtype)

    def pair_step(t, carry):
        scores(2 * t + 1, s_odd)
        carry = absorb(2 * t, s_even, carry, False)
        scores(2 * t + 2, s_even)
        return absorb(2 * t + 1, s_odd, carry, False)

    init = tuple((jnp.full((1, tq), NEG, F32), jnp.zeros((LANES, tq), F32)) for _ in range(2))
    scores(0, s_even)
    carry = lax.fori_loop(0, qi // 2, pair_step, init)

    @pl.when(qi % 2 == 0)
    def _():
        finish(absorb(qi, s_even, carry, True))

    @pl.when(qi % 2 == 1)
    def _():
        scores(qi, s_odd)
        finish(absorb(qi, s_odd, absorb(qi - 1, s_even, carry, False), True))


def _causal_attention(q_arr, k_arr, v_arr, c_arr, *, q_blk0, k_blk0, v_blk0, n_pairs):
    b, s, _ = q_arr.shape
    tq = ATTN_BLOCK
    fox = c_arr is not None
    qk_w = LANES if fox else 2 * LANES
    in_specs = [
        pl.BlockSpec((1, tq, qk_w), lambda bi, p, qi: (bi, qi, q_blk0 + p)),
        pl.BlockSpec((1, s, qk_w), lambda bi, p, qi: (bi, 0, k_blk0 + p)),
        pl.BlockSpec((1, s, LANES), lambda bi, p, qi: (bi, 0, v_blk0 + p)),
    ]
    args = [q_arr, k_arr, v_arr]
    scratch = [pltpu.VMEM((2, s // tq, LANES, tq), BF16), pltpu.VMEM((2, tq, tq), F32), pltpu.VMEM((2, tq, tq), F32)]
    if fox:
        in_specs.append(pl.BlockSpec((1, s, LANES), lambda bi, p, qi: (bi, 0, 0)))
        args.append(c_arr)
        scratch.append(pltpu.VMEM((2, s, LANES), BF16))
    return pl.pallas_call(
        functools.partial(_causal_kernel, fox=fox, tq=tq, tk=tq),
        grid=(b, n_pairs, s // tq),
        in_specs=in_specs,
        out_specs=pl.BlockSpec((1, tq, LANES), lambda bi, p, qi: (bi, qi, p)),
        out_shape=jax.ShapeDtypeStruct((b, s, n_pairs * LANES), BF16),
        scratch_shapes=scratch,
        compiler_params=_params("parallel", "parallel", "arbitrary"),
        name="causal_attention",
    )(*args)


DIL_PAD = BLK * max(d for _, d in DIL_BRANCHES)
DIL_UNROLL = 4


def _dilated_kernel(q_ref, k_ref, v_ref, bias_ref, o_ref, qf, kf, vf, acc_s, m_s, l_s, s_even, s_odd, *, seq):
    qf[...] = q_ref[0].astype(F32)
    kf[0:DIL_PAD, :] = jnp.zeros((DIL_PAD, LANES), F32)
    vf[0:DIL_PAD, :] = jnp.zeros((DIL_PAD, LANES), F32)
    kf[DIL_PAD:, :] = k_ref[0].astype(F32)
    vf[DIL_PAD:, :] = v_ref[0].astype(F32)
    head0 = lax.broadcasted_iota(jnp.int32, (BLK, LANES), 1) < HEAD_DIM
    head0_k = lax.broadcasted_iota(jnp.int32, (2 * BLK, LANES), 1) < HEAD_DIM
    ones0 = jnp.where(head0_k, 1.0, 0.0).astype(BF16)
    ones1 = jnp.where(head0_k, 0.0, 1.0).astype(BF16)

    def place(dil, u):
        bi = u // dil
        q_start = (u % dil) + (dil * BLK) * bi
        return bi, q_start, q_start + (DIL_PAD - dil * BLK)

    def scores(br, dil, g, buf):
        for i in range(DIL_UNROLL):
            bi, q_start, k_start = place(dil, g * DIL_UNROLL + i)
            q2 = qf[pl.ds(q_start, BLK, stride=dil), :]
            q_st = jnp.concatenate([jnp.where(head0, q2, 0.0), jnp.where(head0, 0.0, q2)], axis=0).astype(BF16)
            kb = kf[pl.ds(k_start, 2 * BLK, stride=dil), :].astype(BF16)
            s = lax.dot_general(q_st, kb, NT_DIMS, preferred_element_type=F32)
            buf[i] = s + bias_ref[br, 0, jnp.where(bi == 0, 1, 0)]

    def absorb(br, dil, g, buf, first):
        for i in range(DIL_UNROLL):
            _, q_start, k_start = place(dil, g * DIL_UNROLL + i)
            s = buf[i]
            m = jnp.max(s, axis=-1, keepdims=True)
            p = jnp.exp2(s - m).astype(BF16)
            p_cat = jnp.concatenate([p[:BLK], p[BLK:]], axis=1)
            v2 = vf[pl.ds(k_start, 2 * BLK, stride=dil), :]
            rhs = jnp.concatenate([
                jnp.concatenate([jnp.where(head0_k, v2, 0.0).astype(BF16), ones0], axis=1),
                jnp.concatenate([jnp.where(head0_k, 0.0, v2).astype(BF16), ones1], axis=1)], axis=0)
            acc2 = _dot(p_cat, rhs)
            acc_b, l_b = acc2[:, :LANES], acc2[:, LANES:]
            m_b = jnp.where(head0, jnp.broadcast_to(m[:BLK], (BLK, LANES)), jnp.broadcast_to(m[BLK:], (BLK, LANES)))
            idx = pl.ds(q_start, BLK, stride=dil)
            if first:
                m_s[idx, :] = m_b
                l_s[idx, :] = l_b
                acc_s[idx, :] = acc_b
                continue
            m_old = m_s[idx, :]
            m_new = jnp.maximum(m_old, m_b)
            a_old = jnp.exp2(m_old - m_new)
            a_b = jnp.exp2(m_b - m_new)
            m_s[idx, :] = m_new
            l_s[idx, :] = a_old * l_s[idx, :] + a_b * l_b
            acc_s[idx, :] = a_old * acc_s[idx, :] + a_b * acc_b

    n_groups = seq // (BLK * DIL_UNROLL)
    order = sorted(range(len(DIL_BRANCHES)), key=lambda i: -DIL_BRANCHES[i][1])
    for pos, br in enumerate(order):
        dil = DIL_BRANCHES[br][1]
        first = pos == 0

        def trip(t, carry, br=br, dil=dil, first=first):
            scores(br, dil, 2 * t + 1, s_odd)
            absorb(br, dil, 2 * t, s_even, first)
            scores(br, dil, 2 * t + 2, s_even)
            absorb(br, dil, 2 * t + 1, s_odd, first)
            return carry

        scores(br, dil, 0, s_even)
        lax.fori_loop(0, n_groups // 2 - 1, trip, 0)
        scores(br, dil, n_groups - 1, s_odd)
        absorb(br, dil, n_groups - 2, s_even, first)
        absorb(br, dil, n_groups - 1, s_odd, first)

    o_ref[0] = (acc_s[...] / l_s[...]).astype(o_ref.dtype)


def _dilated_attention(proj, bias, *, q_blk0, k_blk0, v_blk0):
    b, s, _ = proj.shape
    n_pairs = DIL_HEADS // 2
    assert (s // (BLK * DIL_UNROLL)) % 2 == 0
    blk = lambda off: pl.BlockSpec((1, s, LANES), lambda bi, p: (bi, 0, off + p))
    state = [pltpu.VMEM((s, LANES), F32) for _ in range(4)]
    padded = [pltpu.VMEM((DIL_PAD + s, LANES), F32) for _ in range(2)]
    score_bufs = [pltpu.VMEM((DIL_UNROLL, 2 * BLK, 2 * BLK), F32) for _ in range(2)]
    return pl.pallas_call(
        functools.partial(_dilated_kernel, seq=s),
        grid=(b, n_pairs),
        in_specs=[blk(q_blk0), blk(k_blk0), blk(v_blk0),
                  pl.BlockSpec((len(DIL_BRANCHES), 1, 2, 2 * BLK, 2 * BLK), lambda bi, p: (0, p, 0, 0, 0))],
        out_specs=pl.BlockSpec((1, s, LANES), lambda bi, p: (bi, 0, p)),
        out_shape=jax.ShapeDtypeStruct((b, s, n_pairs * LANES), BF16),
        scratch_shapes=[state[0], padded[0], padded[1], state[1], state[2], state[3]] + score_bufs,
        compiler_params=_params("parallel", "parallel"),
        name="dilated_attention",
    )(proj, proj, proj, bias)


def _swa_kernel(q_ref, k_ref, v_ref, bias_ref, sink_ref, o_ref, vaug_t, s_even, s_odd, *, tc):
    ci = pl.program_id(1)
    seq = k_ref.shape[1]
    grp = SWA_Q_HEADS // SWA_KV_HEADS
    n_blk = tc // BLK

    @pl.when(ci == 0)
    def _():
        feat = lax.broadcasted_iota(jnp.int32, (LANES, BLK), 0)

        def fill(t, carry):
            vt = jnp.transpose(v_ref[0, pl.ds(pl.multiple_of(t * BLK, BLK), BLK), :].astype(F32))
            for kv in range(SWA_KV_HEADS):
                vaug_t[kv, t] = jnp.where(_own_lanes(feat, kv), vt, 1.0).astype(BF16)
            return carry

        lax.fori_loop(0, seq // BLK, fill, 0)

    lane_q = lax.broadcasted_iota(jnp.int32, (BLK, LANES), 1)
    feat_o = lax.broadcasted_iota(jnp.int32, (LANES, grp * BLK), 0)

    def blocks(bb):
        own = ci * n_blk + bb
        return jnp.maximum(own - 1, 0), own

    def scores(bb, kv, buf):
        prev, own = blocks(bb)
        k2 = jnp.concatenate([k_ref[0, pl.ds(pl.multiple_of(prev * BLK, BLK), BLK), :],
                              k_ref[0, pl.ds(pl.multiple_of(own * BLK, BLK), BLK), :]], axis=0)
        q4 = jnp.concatenate(
            [jnp.where(_own_lanes(lane_q, kv), q_ref[0, bb * BLK:(bb + 1) * BLK, g * LANES:(g + 1) * LANES].astype(F32),
                       0.0) for g in range(grp)], axis=0).astype(BF16)
        s = lax.dot_general(k2, q4, NT_DIMS, preferred_element_type=F32)
        buf[...] = s + bias_ref[kv, jnp.where(own == 0, 1, 0)]

    def absorb(bb, kv, buf):
        prev, own = blocks(bb)
        s = buf[...]
        sink = sink_ref[kv]
        m = jnp.maximum(jnp.max(s, axis=0, keepdims=True), sink)
        p = jnp.exp2(s - m).astype(BF16)
        v2 = jnp.concatenate([vaug_t[kv, prev], vaug_t[kv, own]], axis=1)
        acc = _dot(v2, p)
        den = (acc[HEAD_DIM:HEAD_DIM + 1, :] if kv == 0 else acc[0:1, :]) + jnp.exp2(sink - m)
        return acc / den

    units = [(bb, kv) for bb in range(n_blk) for kv in range(SWA_KV_HEADS)]
    bufs = (s_even, s_odd)
    scores(*units[0], bufs[0])
    outs = {}
    for n, unit in enumerate(units):
        if n + 1 < len(units):
            scores(*units[n + 1], bufs[(n + 1) % 2])
        outs[unit] = absorb(*unit, bufs[n % 2])
    for bb in range(n_blk):
        out_t = jnp.where(feat_o < HEAD_DIM, outs[(bb, 0)], outs[(bb, 1)])
        for g in range(grp):
            o_ref[0, bb * BLK:(bb + 1) * BLK, g * LANES:(g + 1) * LANES] = jnp.transpose(
                out_t[:, g * BLK:(g + 1) * BLK]).astype(o_ref.dtype)


def _swa_attention(proj, bias, sink_row, *, q_blk0, k_blk0, v_blk0):
    b, s, _ = proj.shape
    tc = SWA_CHUNK
    qw = SWA_Q_HEADS * HEAD_DIM
    grp = SWA_Q_HEADS // SWA_KV_HEADS
    return pl.pallas_call(
        functools.partial(_swa_kernel, tc=tc),
        grid=(b, s // tc),
        in_specs=[
            pl.BlockSpec((1, tc, qw), lambda bi, ci: (bi, ci, q_blk0 * LANES // qw)),
            pl.BlockSpec((1, s, LANES), lambda bi, ci: (bi, 0, k_blk0)),
            pl.BlockSpec((1, s, LANES), lambda bi, ci: (bi, 0, v_blk0)),
            pl.BlockSpec(bias.shape, lambda bi, ci: (0, 0, 0, 0)),
            pl.BlockSpec(sink_row.shape, lambda bi, ci: (0, 0, 0)),
        ],
        out_specs=pl.BlockSpec((1, tc, qw), lambda bi, ci: (bi, ci, 0)),
        out_shape=jax.ShapeDtypeStruct((b, s, qw), BF16),
        scratch_shapes=[pltpu.VMEM((SWA_KV_HEADS, s // BLK, LANES, BLK), BF16),
                        pltpu.VMEM((2 * BLK, grp * BLK), F32), pltpu.VMEM((2 * BLK, grp * BLK), F32)],
        compiler_params=_params("parallel", "arbitrary"),
        name="swa_attention",
    )(proj, proj, proj, bias, sink_row)


def _rope128(x, a, bm, bp):
    return x * a + pltpu.roll(x, LANES - 16, axis=1) * bm + pltpu.roll(x, 16, axis=1) * bp


def _proj_odd_kernel(x_ref, g_ref, w_ref, qn_ref, kvn_ref, wuq_ref, wuk_ref, wuv_ref, a_ref, bm_ref, bp_ref,
                     q_out, k_out, v_out, swa_out):
    xb = _rms(x_ref[...], g_ref[...]).astype(BF16)
    o1 = MLA_Q_RANK + LANES
    o2 = o1 + MLA_KV_RANK
    cq_kpe = _dot(xb, w_ref[:, 0:o1])
    c_q, kpe = cq_kpe[:, :MLA_Q_RANK], cq_kpe[:, MLA_Q_RANK:]
    c_kv = _dot(xb, w_ref[:, o1:o2])
    swa_out[...] = _dot(xb, w_ref[:, o2:]).astype(swa_out.dtype)
    a, bm, bp = a_ref[...], bm_ref[...], bp_ref[...]
    kpe = _rope128(kpe, a, bm, bp)
    cqn = _rms(c_q, qn_ref[...]).astype(BF16)
    ckvn = _rms(c_kv, kvn_ref[...]).astype(BF16)
    v_out[...] = _dot(ckvn, wuv_ref[...]).astype(v_out.dtype)
    q_raw = _dot(cqn, wuq_ref[...])
    k_raw = _dot(ckvn, wuk_ref[...])
    for h in range(MLA_HEADS):
        sl = slice(h * LANES, (h + 1) * LANES)
        q_out[:, sl] = _rope128(q_raw[:, sl], a, bm, bp).astype(q_out.dtype)
        k_out[:, sl] = (k_raw[:, sl] + kpe).astype(k_out.dtype)


def _proj_odd(h, g, w, qn, kvn, wuq, wuk, wuv, rope_a, rope_bm, rope_bp, seq):
    n, d = h.shape
    tm = PROJ_ODD_TILE
    n_in = w.shape[1]
    n_swa = n_in - (MLA_Q_RANK + MLA_KV_RANK + LANES)
    tiles_per_seq = seq // tm
    full = lambda shape: pl.BlockSpec(shape, lambda i: (0, 0))
    rope = pl.BlockSpec((tm, LANES), lambda i: (i % tiles_per_seq, 0))
    return pl.pallas_call(
        _proj_odd_kernel,
        grid=(n // tm,),
        in_specs=[
            pl.BlockSpec((tm, d), lambda i: (i, 0)), full((1, d)), full(w.shape),
            full(qn.shape), full(kvn.shape), full(wuq.shape), full(wuk.shape), full(wuv.shape),
            rope, rope, rope,
        ],
        out_specs=[
            pl.BlockSpec((tm, MLA_HEADS * LANES), lambda i: (i, 0)),
            pl.BlockSpec((tm, MLA_HEADS * LANES), lambda i: (i, 0)),
            pl.BlockSpec((tm, MLA_HEADS * MLA_V), lambda i: (i, 0)),
            pl.BlockSpec((tm, n_swa), lambda i: (i, 0)),
        ],
        out_shape=[
            jax.ShapeDtypeStruct((n, MLA_HEADS * LANES), BF16),
            jax.ShapeDtypeStruct((n, MLA_HEADS * LANES), BF16),
            jax.ShapeDtypeStruct((n, MLA_HEADS * MLA_V), BF16),
            jax.ShapeDtypeStruct((n, n_swa), BF16),
        ],
        compiler_params=_params("parallel"),
        name="proj_odd",
    )(h, g, w, qn, kvn, wuq, wuk, wuv, rope_a, rope_bm, rope_bp)


def _out_router_kernel(h_ref, oa_ref, ob_ref, wo_ref, g_ref, wr_hi_ref, wr_lo_ref, br_ref,
                       h_out, hn_out, gates_out, counts_out, pos_col_out, pos_row_out, tri_ref, seen_ref):
    i = pl.program_id(0)
    tm = h_ref.shape[0]
    half = oa_ref.shape[1]

    @pl.when(i == 0)
    def _():
        ri = lax.broadcasted_iota(jnp.int32, (tm, tm), 0)
        ci = lax.broadcasted_iota(jnp.int32, (tm, tm), 1)
        tri_ref[...] = jnp.where(ci < ri, 1.0, 0.0).astype(BF16)

    @pl.when(i % (MOE_TILE // tm) == 0)
    def _():
        seen_ref[...] = jnp.zeros_like(seen_ref)

    lane = lax.broadcasted_iota(jnp.int32, (MXU_TILE, LANES), 1)
    big = jnp.int32(LANES)
    is_grp = (lane >= N_EXPERTS) & (lane < N_EXPERTS + N_GROUPS)
    routed_parts = []
    for r in range(0, tm, MXU_TILE):
        rows = slice(r, r + MXU_TILE)
        h = h_ref[rows, :] + _dot(oa_ref[rows, :], wo_ref[0:half, :]) + _dot(ob_ref[rows, :], wo_ref[half:, :])
        h_out[rows, :] = h
        hn = _rms(h, g_ref[...])
        hn_out[rows, :] = hn.astype(hn_out.dtype)
        z = _dot_hi(hn, wr_hi_ref[...], wr_lo_ref[...]) + br_ref[...]
        zg = jnp.where(is_grp, z, -jnp.inf)
        g_max = jnp.max(zg, axis=-1, keepdims=True)
        g_w = 1.0 / jnp.sum(jnp.exp(zg - g_max), axis=-1, keepdims=True)
        g_idx = jnp.min(jnp.where(zg == g_max, lane - N_EXPERTS, big), axis=-1, keepdims=True)
        in_grp = (lane < N_EXPERTS) & ((lane // EXPERTS_PER_GROUP) == g_idx)
        ze = jnp.where(in_grp, z, -jnp.inf)
        v1 = jnp.max(ze, axis=-1, keepdims=True)
        i1 = jnp.min(jnp.where(ze == v1, lane, big), axis=-1, keepdims=True)
        ze2 = jnp.where(lane == i1, -jnp.inf, ze)
        v2 = jnp.max(ze2, axis=-1, keepdims=True)
        i2 = jnp.min(jnp.where(ze2 == v2, lane, big), axis=-1, keepdims=True)
        e2 = jnp.exp(v2 - v1)
        w1 = g_w / (1.0 + e2)
        w2 = g_w * e2 / (1.0 + e2)
        gates_out[rows, :] = jnp.where(lane == i1, w1, 0.0) + jnp.where(lane == i2, w2, 0.0)
        routed_parts.append(jnp.where((lane == g_idx) & (lane < N_GROUPS), 1.0, 0.0))

    routed = jnp.concatenate(routed_parts, axis=0)
    pos = jnp.where(routed > 0.0, _dot(tri_ref[...], routed.astype(BF16)) + seen_ref[...], -1.0)
    pos_col_out[...] = pos
    pos_row_out[0] = jnp.transpose(pos)[0:8, :]
    here = jnp.sum(routed, axis=0, keepdims=True)
    seen_ref[...] += here
    counts_out[0] = jnp.broadcast_to(here, counts_out.shape[1:])


def _out_router(h, oa, ob, wo, g, wr_hi, wr_lo, br):
    n, d = h.shape
    tm = ROUTER_TILE
    per_tile = MOE_TILE // tm
    half = oa.shape[1]
    full = lambda shape: pl.BlockSpec(shape, lambda i: (0, 0))
    tile = lambda w: pl.BlockSpec((tm, w), lambda i: (i, 0))
    return pl.pallas_call(
        _out_router_kernel,
        grid=(n // tm,),
        in_specs=[tile(d), tile(half), tile(half), full(wo.shape), full((1, d)),
                  full(wr_hi.shape), full(wr_lo.shape), full((1, LANES))],
        out_specs=[tile(d), tile(d), tile(LANES), pl.BlockSpec((1, 8, LANES), lambda i: (i, 0, 0)),
                   tile(LANES), pl.BlockSpec((1, 8, tm), lambda i: (i // per_tile, 0, i % per_tile))],
        out_shape=[
            jax.ShapeDtypeStruct((n, d), F32),
            jax.ShapeDtypeStruct((n, d), BF16),
            jax.ShapeDtypeStruct((n, LANES), F32),
            jax.ShapeDtypeStruct((n // tm, 8, LANES), F32),
            jax.ShapeDtypeStruct((n, LANES), F32),
            jax.ShapeDtypeStruct((n // MOE_TILE, 8, MOE_TILE), F32),
        ],
        scratch_shapes=[pltpu.VMEM((tm, tm), BF16), pltpu.VMEM((1, LANES), F32)],
        compiler_params=_params("arbitrary"),
        name="out_router",
    )(h, oa, ob, wo, g, wr_hi, wr_lo, br)


MOE_CHUNK = 320
MOE_ROWS = 1024
MOE_ALIGN = 16
MOE_EXPERTS_PER_STEP = 4


def _moe_sorted_tiles(n):
    n_seg = (n // MOE_TILE) * N_GROUPS
    rows = n + n_seg * (MOE_ALIGN - 1) + N_GROUPS * (MOE_CHUNK + MOE_ROWS - 1)
    return -(-rows // MOE_ROWS)


def _moe_tables(counts, n):
    n_rt = _moe_sorted_tiles(n)
    seg_len = (counts + (MOE_ALIGN - 1)) // MOE_ALIGN * MOE_ALIGN
    group_len = seg_len.sum(axis=0)
    group_span = (group_len + MOE_CHUNK + MOE_ROWS - 1) // MOE_ROWS * MOE_ROWS
    group_end = jnp.cumsum(group_span)
    group_start = group_end - group_span
    seg_off = group_start[None, :] + jnp.cumsum(seg_len, axis=0) - seg_len
    rt_start = jnp.arange(n_rt, dtype=jnp.int32) * MOE_ROWS
    rt_group = jnp.minimum(jnp.sum(rt_start[:, None] >= group_end[None, :], axis=1), N_GROUPS - 1)
    rt_valid = jnp.clip(group_start[rt_group] + group_len[rt_group] - rt_start, 0, MOE_ROWS)
    n_used = (group_end[-1] // MOE_ROWS).reshape(1)
    i32 = lambda a: a.astype(jnp.int32)
    return i32(seg_off.reshape(-1)), i32(rt_group), i32(rt_valid), i32(n_used)


def _moe_chunks(count, body):
    def one_chunk(k, carry):
        body(k * MOE_CHUNK, MOE_CHUNK)
        return carry

    lax.fori_loop(0, (count + (MOE_CHUNK - 1)) // MOE_CHUNK, one_chunk, 0)


def _moe_pack_kernel(counts_ref, off_ref, hn_ref, gates_ref, pos_ref, xs_in, xs_ref, xbuf, sem, issued_ref):
    del xs_in
    i = pl.program_id(0)
    d = hn_ref.shape[1]

    @pl.when(i == 0)
    def _():
        issued_ref[0] = 0

    def write(slot):
        return pltpu.make_async_copy(xbuf.at[slot], xs_ref.at[pl.ds(0, MOE_CHUNK)], sem.at[slot])

    lane = lax.broadcasted_iota(jnp.int32, (hn_ref.shape[0], LANES), 1)
    gates = gates_ref[...]
    gates_hi = gates.astype(BF16).astype(F32)
    gates_lo = gates - gates_hi
    for grp in range(N_GROUPS):
        seg = i * N_GROUPS + grp
        first = grp * EXPERTS_PER_GROUP
        g_hi = gates_hi if first == 0 else pltpu.roll(gates_hi, LANES - first, axis=1)
        g_lo = pltpu.roll(gates_lo, (LANES - first + EXPERTS_PER_GROUP) % LANES, axis=1)
        gate_cols = jnp.where(lane < EXPERTS_PER_GROUP, g_hi, jnp.where(lane < 2 * EXPERTS_PER_GROUP, g_lo, 0.0))
        gate_cols = gate_cols.astype(BF16)
        pos_row = pos_ref[0, grp:grp + 1, :]

        def chunk(first_pos, rows, seg=seg, gate_cols=gate_cols, pos_row=pos_row):
            n_done = issued_ref[0]
            slot = n_done % 2
            slot_pos = lax.broadcasted_iota(jnp.int32, (rows, 1), 0).astype(F32) + first_pos.astype(F32)
            gather = jnp.where(pos_row == slot_pos, 1.0, 0.0).astype(BF16)
            xbuf[slot, :, 0:d] = _dot(gather, hn_ref[...]).astype(BF16)
            xbuf[slot, :, d:] = _dot(gather, gate_cols).astype(BF16)

            @pl.when(n_done > 0)
            def _():
                write(1 - slot).wait()

            dst = pl.multiple_of(off_ref[seg] + first_pos, MOE_ALIGN)
            pltpu.make_async_copy(xbuf.at[slot], xs_ref.at[pl.ds(dst, MOE_CHUNK)], sem.at[slot]).start()
            issued_ref[0] = n_done + 1

        _moe_chunks(counts_ref[seg], chunk)

    @pl.when((i == pl.num_programs(0) - 1) & (issued_ref[0] > 0))
    def _():
        write((issued_ref[0] - 1) % 2).wait()


def _moe_pack(counts, seg_off, hn, gates, pos_row, xs_init):
    n, d = hn.shape
    rows = xs_init.shape[0]
    grid_spec = pltpu.PrefetchScalarGridSpec(
        num_scalar_prefetch=2,
        grid=(n // MOE_TILE,),
        in_specs=[
            pl.BlockSpec((MOE_TILE, d), lambda i, c, o: (i, 0)),
            pl.BlockSpec((MOE_TILE, LANES), lambda i, c, o: (i, 0)),
            pl.BlockSpec((1, 8, MOE_TILE), lambda i, c, o: (i, 0, 0)),
            pl.BlockSpec(memory_space=pl.ANY),
        ],
        out_specs=pl.BlockSpec(memory_space=pl.ANY),
        scratch_shapes=[pltpu.VMEM((2, MOE_CHUNK, d + LANES), BF16), pltpu.SemaphoreType.DMA((2,)),
                        pltpu.SMEM((1,), jnp.int32)],
    )
    return pl.pallas_call(
        _moe_pack_kernel,
        grid_spec=grid_spec,
        out_shape=jax.ShapeDtypeStruct((rows, d + LANES), BF16),
        input_output_aliases={5: 0},
        compiler_params=_params("arbitrary"),
        name="moe_pack",
    )(counts, seg_off, hn, gates, pos_row, xs_init)


def _moe_experts_kernel(grp_ref, valid_ref, used_ref, xs_ref, wg_ref, wu_ref, wd_ref, ys_ref, acc_ref,
                        wg_b, wu_b, wd_b):
    del grp_ref, used_ref
    rt = pl.program_id(0)
    j = pl.program_id(1)
    d = ys_ref.shape[1]
    valid = valid_ref[rt]

    @pl.when((valid > 0) & (j == 0))
    def _():
        acc_ref[...] = jnp.zeros_like(acc_ref)

    def experts(x, gate_cols, w_gate, w_up, w_down):
        lane = lax.broadcasted_iota(jnp.int32, gate_cols.shape, 1)
        acts = []
        for e in range(MOE_EXPERTS_PER_STEP):
            idx = j * MOE_EXPERTS_PER_STEP + e
            mine = (lane == idx) | (lane == idx + EXPERTS_PER_GROUP)
            gate = jnp.sum(jnp.where(mine, gate_cols, 0.0), axis=-1, keepdims=True)
            a = _dot(x, w_gate(e))
            u = _dot(x, w_up(e))
            acts.append(((a * jax.nn.sigmoid(a)) * u * gate).astype(BF16))
        return _dot(jnp.concatenate(acts, axis=1), w_down())

    @pl.when(valid == MOE_ROWS)
    def _():
        acc_ref[...] += experts(
            xs_ref[:, 0:d], xs_ref[:, d:].astype(F32), lambda e: wg_ref[0, 0, e].astype(BF16),
            lambda e: wu_ref[0, 0, e].astype(BF16), lambda: wd_ref[0, 0].astype(BF16).reshape(wd_b.shape))

    @pl.when((valid > 0) & (valid < MOE_ROWS))
    def _():
        wg_b[...] = wg_ref[0, 0].astype(BF16)
        wu_b[...] = wu_ref[0, 0].astype(BF16)
        wd_b[...] = wd_ref[0, 0].astype(BF16).reshape(wd_b.shape)

        def rows_block(t, carry):
            rows = pl.ds(pl.multiple_of(t * MXU_TILE, MXU_TILE), MXU_TILE)
            acc_ref[rows, :] += experts(xs_ref[rows, 0:d], xs_ref[rows, d:].astype(F32), lambda e: wg_b[e],
                                        lambda e: wu_b[e], lambda: wd_b[...])
            return carry

        lax.fori_loop(0, (valid + (MXU_TILE - 1)) // MXU_TILE, rows_block, 0)

    @pl.when((valid > 0) & (j == pl.num_programs(1) - 1))
    def _():
        row = lax.broadcasted_iota(jnp.int32, (acc_ref.shape[0], 1), 0)
        ys_ref[...] = jnp.where(row < valid, acc_ref[...], 0.0).astype(ys_ref.dtype)

    @pl.when((valid == 0) & (j == pl.num_programs(1) - 1))
    def _():
        ys_ref[...] = jnp.zeros_like(ys_ref)


def _moe_experts(rt_group, rt_valid, n_used, xs, w_gate, w_up, w_down, layer):
    rows = xs.shape[0]
    d = w_gate.shape[-2]
    n_rt = rows // MOE_ROWS
    eps = MOE_EXPERTS_PER_STEP
    n_steps = EXPERTS_PER_GROUP // eps

    def tile_idx(rt, j, grp, valid, used):
        return jnp.minimum(rt, used[0] - 1)

    def w_idx(rt, j, grp, valid, used):
        live = rt < used[0]
        return (layer, grp[tile_idx(rt, j, grp, valid, used)], jnp.where(live, j, n_steps - 1), 0, 0)

    grid_spec = pltpu.PrefetchScalarGridSpec(
        num_scalar_prefetch=3,
        grid=(n_rt, n_steps),
        in_specs=[
            pl.BlockSpec((MOE_ROWS, d + LANES), lambda rt, j, grp, valid, used: (tile_idx(rt, j, grp, valid, used), 0)),
            pl.BlockSpec((1, 1, eps, d, D_EXPERT), w_idx),
            pl.BlockSpec((1, 1, eps, d, D_EXPERT), w_idx),
            pl.BlockSpec((1, 1, eps, D_EXPERT, d), w_idx),
        ],
        out_specs=pl.BlockSpec((MOE_ROWS, d), lambda rt, j, grp, valid, used: (rt, 0)),
        scratch_shapes=[pltpu.VMEM((MOE_ROWS, d), F32), pltpu.VMEM((eps, d, D_EXPERT), BF16),
                        pltpu.VMEM((eps, d, D_EXPERT), BF16), pltpu.VMEM((eps * D_EXPERT, d), BF16)],
    )
    return pl.pallas_call(
        _moe_experts_kernel,
        grid_spec=grid_spec,
        out_shape=jax.ShapeDtypeStruct((rows, d), BF16),
        compiler_params=_params("arbitrary", "arbitrary"),
        name="moe_experts",
    )(rt_group, rt_valid, n_used, xs, w_gate, w_up, w_down)


def _moe_combine_kernel(counts_ref, off_ref, pos_ref, h_ref, fg_ref, ys_ref, o_ref, ybuf, ymore, sem, *, final_norm):
    i = pl.program_id(0)

    def read(tile, grp):
        src = pl.multiple_of(off_ref[tile * N_GROUPS + grp], MOE_ALIGN)
        slot = tile % 2
        return pltpu.make_async_copy(ys_ref.at[pl.ds(src, MOE_CHUNK)],
                                     ybuf.at[slot, pl.ds(grp * MOE_CHUNK, MOE_CHUNK)], sem.at[slot, grp])

    def fetch(tile):
        for grp in range(N_GROUPS):
            @pl.when(counts_ref[tile * N_GROUPS + grp] > 0)
            def _(grp=grp):
                read(tile, grp).start()

    @pl.when(i == 0)
    def _():
        ybuf[...] = jnp.zeros_like(ybuf)
        fetch(i)

    @pl.when(i + 1 < pl.num_programs(0))
    def _():
        fetch(i + 1)

    slot_pos = lax.broadcasted_iota(jnp.int32, (1, MOE_CHUNK), 1).astype(F32)
    scatter = jnp.concatenate(
        [jnp.where(pos_ref[:, grp:grp + 1] == slot_pos, 1.0, 0.0).astype(BF16) for grp in range(N_GROUPS)], axis=1)
    for grp in range(N_GROUPS):
        @pl.when(counts_ref[i * N_GROUPS + grp] > 0)
        def _(grp=grp):
            read(i, grp).wait()

    o_ref[...] = h_ref[...] + _dot(scatter, ybuf[i % 2])
    for grp in range(N_GROUPS):
        count = counts_ref[i * N_GROUPS + grp]
        pos_col = pos_ref[:, grp:grp + 1]

        def more(k, carry, grp=grp, pos_col=pos_col):
            src = pl.multiple_of(off_ref[i * N_GROUPS + grp] + k * MOE_CHUNK, MOE_ALIGN)
            pltpu.sync_copy(ys_ref.at[pl.ds(src, MOE_CHUNK)], ymore)
            scatter = jnp.where(pos_col == slot_pos + (k * MOE_CHUNK).astype(F32), 1.0, 0.0).astype(BF16)
            o_ref[...] += _dot(scatter, ymore[...])
            return carry

        lax.fori_loop(1, (count + (MOE_CHUNK - 1)) // MOE_CHUNK, more, 0)

    if final_norm:
        o_ref[...] = _rms(o_ref[...], fg_ref[...])


def _moe_combine(counts, seg_off, pos_col, h, fg, ys, final_norm):
    n, d = h.shape
    grid_spec = pltpu.PrefetchScalarGridSpec(
        num_scalar_prefetch=2,
        grid=(n // MOE_TILE,),
        in_specs=[
            pl.BlockSpec((MOE_TILE, LANES), lambda i, c, o: (i, 0)),
            pl.BlockSpec((MOE_TILE, d), lambda i, c, o: (i, 0)),
            pl.BlockSpec((1, d), lambda i, c, o: (0, 0)),
            pl.BlockSpec(memory_space=pl.ANY),
        ],
        out_specs=pl.BlockSpec((MOE_TILE, d), lambda i, c, o: (i, 0)),
        scratch_shapes=[pltpu.VMEM((2, N_GROUPS * MOE_CHUNK, d), BF16), pltpu.VMEM((MOE_CHUNK, d), BF16),
                        pltpu.SemaphoreType.DMA((2, N_GROUPS))],
    )
    return pl.pallas_call(
        functools.partial(_moe_combine_kernel, final_norm=final_norm),
        grid_spec=grid_spec,
        out_shape=jax.ShapeDtypeStruct((n, d), F32),
        compiler_params=_params("arbitrary"),
        name="moe_combine",
    )(counts, seg_off, pos_col, h, fg, ys)


def _alibi_slopes(n):
    return jnp.exp2(-8.0 * jnp.arange(1, n + 1, dtype=F32) / n)


def _band_bias(slopes, max_steps, step_dist):
    steps = (jnp.arange(BLK)[:, None] + BLK) - jnp.arange(2 * BLK)[None, :]
    in_band = (steps >= 0) & (steps <= max_steps)
    dist = (steps * step_dist).astype(F32)
    return jnp.where(in_band[None], -slopes.astype(F32)[:, None, None] * dist[None], NEG)


def _rope_tables(s):
    inv = ROPE_THETA ** (-jnp.arange(0, MLA_ROPE, 2, dtype=F32) / MLA_ROPE)
    ang = jnp.arange(s, dtype=F32)[:, None] * inv[None, :]
    cos, sin = jnp.cos(ang), jnp.sin(ang)
    half = MLA_ROPE // 2
    zeros_tail = jnp.zeros((s, LANES - MLA_NOPE - MLA_ROPE), F32)
    a = jnp.concatenate([jnp.ones((s, MLA_NOPE), F32), cos, cos, zeros_tail], axis=1)
    zeros_nope = jnp.zeros((s, MLA_NOPE), F32)
    zeros_half = jnp.zeros((s, half), F32)
    bm = jnp.concatenate([zeros_nope, -sin, zeros_half, zeros_tail], axis=1)
    bp = jnp.concatenate([zeros_nope, zeros_half, sin, zeros_tail], axis=1)
    return a, bm, bp


def _swa_head_order(w, axis):
    grp = SWA_Q_HEADS // SWA_KV_HEADS
    shape = w.shape
    w = w.reshape(shape[:axis] + (SWA_KV_HEADS, grp, HEAD_DIM) + shape[axis + 1:])
    return jnp.swapaxes(w, axis, axis + 1).reshape(shape)


def _pad_cols(w, width):
    return jnp.pad(w, ((0, 0), (0, width - w.shape[1])))


def _router_weights(w_group, b_group, w_router, b_router):
    w = _pad_cols(jnp.concatenate([w_router, w_group], axis=1), LANES)
    b = _pad_cols(jnp.concatenate([b_router, b_group])[None, :], LANES)
    hi, lo = _hi_lo(w)
    return hi, lo, b


def kernel(x, attn_norm, ffn_norm, final_norm, e_w_in, e_b_f, e_w_out, o_w_in, o_q_norm, o_kv_norm, o_w_uq,
           o_w_ukv, o_sinks, o_w_out, moe_w_group, moe_b_group, moe_w_router, moe_b_router, moe_w_gate,
           moe_w_up, moe_w_down):
    b, s, d = x.shape
    n = b * s
    depth = attn_norm.shape[0]
    assert s % (BLK * DIL_BRANCHES[-1][1]) == 0 and d == D_MODEL
    assert all(s % t == 0 for t in (PROJ_EVEN_TILE, PROJ_ODD_TILE, ATTN_BLOCK, SWA_CHUNK, MOE_TILE))
    assert MOE_TILE % ROUTER_TILE == 0
    h = x.reshape(n, d)

    dil_slopes = _alibi_slopes(DIL_HEADS)
    dil_bias = jnp.stack([_band_bias(dil_slopes, w // dl, dl) for (w, dl) in DIL_BRANCHES]) * LOG2E
    own_half = jnp.arange(2 * BLK) >= BLK
    dil_bias = jnp.stack([dil_bias, jnp.where(own_half, dil_bias, NEG)], axis=2)
    dil_bias = dil_bias.reshape(len(DIL_BRANCHES), DIL_HEADS // 2, 2, 2, BLK, 2 * BLK)
    dil_bias = dil_bias.transpose(0, 1, 3, 2, 4, 5).reshape(len(DIL_BRANCHES), DIL_HEADS // 2, 2, 2 * BLK, 2 * BLK)
    swa_grp = SWA_Q_HEADS // SWA_KV_HEADS
    swa_bias = _band_bias(_alibi_slopes(SWA_Q_HEADS), SWA_WINDOW - 1, 1) * LOG2E
    swa_bias = swa_bias.reshape(SWA_KV_HEADS, swa_grp, BLK, 2 * BLK).transpose(0, 3, 1, 2)
    swa_bias = swa_bias.reshape(SWA_KV_HEADS, 2 * BLK, swa_grp * BLK)
    swa_bias = jnp.stack([swa_bias, jnp.where(own_half[None, :, None], swa_bias, NEG)], axis=1)
    rope_a, rope_bm, rope_bp = _rope_tables(s)

    xs = jnp.zeros((_moe_sorted_tiles(n) * MOE_ROWS, d + LANES), BF16)
    for layer in range(depth):
        i = layer // 2
        g_attn = attn_norm[layer][None, :]
        if layer % 2 == 0:
            w_in = e_w_in[i]
            hq = FOX_HEADS * HEAD_DIM
            scale = HEAD_DIM ** -0.5
            cols = [w_in[:, 0:hq] * (scale * LOG2E), w_in[:, hq:2 * hq], w_in[:, 2 * hq:3 * hq]]
            o = 3 * hq + FOX_HEADS
            cols += [w_in[:, o:o + hq] * (scale * LOG2E), w_in[:, o + hq:o + 2 * hq], w_in[:, o + 2 * hq:o + 3 * hq]]
            w_main = jnp.concatenate(cols, axis=1).astype(BF16)
            wf_hi, wf_lo = _hi_lo(_pad_cols(w_in[:, 3 * hq:o], LANES))
            b_f = _pad_cols(e_b_f[i][None, :], LANES)
            proj, c = _proj_even(h, g_attn, w_main, wf_hi, wf_lo, b_f, s)
            proj = proj.reshape(b, s, -1)
            o_a = _causal_attention(proj, proj, proj, c.reshape(b, s, LANES), q_blk0=0, k_blk0=4, v_blk0=8,
                                    n_pairs=FOX_HEADS // 2)
            o_b = _dilated_attention(proj, dil_bias, q_blk0=12, k_blk0=16, v_blk0=20)
            w_out = e_w_out[i].astype(BF16)
        else:
            w_in = o_w_in[i]
            o1 = MLA_Q_RANK + MLA_KV_RANK
            o2 = o1 + MLA_ROPE
            sq = SWA_Q_HEADS * HEAD_DIM
            kpe_cols = jnp.pad(w_in[:, o1:o2], ((0, 0), (MLA_NOPE, LANES - MLA_NOPE - MLA_ROPE)))
            w_main = jnp.concatenate(
                [w_in[:, :MLA_Q_RANK], kpe_cols, w_in[:, MLA_Q_RANK:o1],
                 _swa_head_order(w_in[:, o2:o2 + sq] * (HEAD_DIM ** -0.5 * LOG2E), axis=1),
                 w_in[:, o2 + sq:]],
                axis=1).astype(BF16)
            dq = MLA_NOPE + MLA_ROPE
            wuq = o_w_uq[i].reshape(MLA_Q_RANK, MLA_HEADS, dq) * (dq ** -0.5 * LOG2E)
            wuq = jnp.pad(wuq, ((0, 0), (0, 0), (0, LANES - dq))).reshape(MLA_Q_RANK, MLA_HEADS * LANES)
            wukv = o_w_ukv[i].reshape(MLA_KV_RANK, MLA_HEADS, MLA_NOPE + MLA_V)
            wuk = jnp.pad(wukv[:, :, :MLA_NOPE], ((0, 0), (0, 0), (0, LANES - MLA_NOPE)))
            wuk = wuk.reshape(MLA_KV_RANK, MLA_HEADS * LANES)
            wuv = wukv[:, :, MLA_NOPE:].reshape(MLA_KV_RANK, MLA_HEADS * MLA_V)
            q_full, k_full, v_mla, swa = _proj_odd(
                h, g_attn, w_main, o_q_norm[i][None, :], o_kv_norm[i][None, :], wuq.astype(BF16),
                wuk.astype(BF16), wuv.astype(BF16), rope_a, rope_bm, rope_bp, s)
            o_a = _causal_attention(q_full.reshape(b, s, -1), k_full.reshape(b, s, -1), v_mla.reshape(b, s, -1),
                                    None, q_blk0=0, k_blk0=0, v_blk0=0, n_pairs=MLA_HEADS // 2)
            grp = SWA_Q_HEADS // SWA_KV_HEADS
            sink_row = jnp.repeat(o_sinks[i].reshape(SWA_KV_HEADS, grp) * LOG2E, BLK, axis=1)[:, None, :]
            o_b = _swa_attention(swa.reshape(b, s, -1), swa_bias, sink_row, q_blk0=0, k_blk0=4, v_blk0=5)
            half = MLA_HEADS * MLA_V
            w_out = jnp.concatenate([o_w_out[i][:half], _swa_head_order(o_w_out[i][half:], axis=0)], axis=0)
            w_out = w_out.astype(BF16)

        wr_hi, wr_lo, b_r = _router_weights(moe_w_group[layer], moe_b_group[layer], moe_w_router[layer],
                                            moe_b_router[layer])
        h, hn, gates, counts, pos_col, pos_row = _out_router(
            h, o_a.reshape(n, -1), o_b.reshape(n, -1), w_out, ffn_norm[layer][None, :], wr_hi, wr_lo, b_r)
        counts = counts[:, 0, :N_GROUPS].reshape(n // MOE_TILE, MOE_TILE // ROUTER_TILE, N_GROUPS).sum(axis=1)
        counts = counts.astype(jnp.int32)
        seg_off, rt_group, rt_valid, n_used = _moe_tables(counts, n)
        counts = counts.reshape(-1)
        xs = _moe_pack(counts, seg_off, hn, gates, pos_row, xs)
        ys = _moe_experts(rt_group, rt_valid, n_used, xs, moe_w_gate, moe_w_up, moe_w_down, layer)
        h = _moe_combine(counts, seg_off, pos_col, h, final_norm[None, :], ys, final_norm=layer == depth - 1)
    return h.reshape(b, s, d)
```

```python
import functools

import numpy as np
import jax
import jax.numpy as jnp
from jax import lax
from jax.experimental import pallas as pl
from jax.experimental.pallas import tpu as pltpu

F32 = jnp.float32
BF16 = jnp.bfloat16

D_MODEL = 1024
HEAD_DIM = 64
BLK = 128
NEG = -1e30
RMS_EPS = 1e-6
FOX_HEADS = 8
DIL_HEADS = 8
DIL_BRANCHES = ((128, 1), (512, 4), (2048, 16))
MLA_HEADS = 8
MLA_Q_RANK = 384
MLA_KV_RANK = 256
MLA_NOPE = 64
MLA_ROPE = 32
MLA_V = 64
ROPE_THETA = 10000.0
SWA_Q_HEADS = 8
SWA_KV_HEADS = 2
SWA_WINDOW = 128
N_GROUPS = 4
EXPERTS_PER_GROUP = 8
N_EXPERTS = N_GROUPS * EXPERTS_PER_GROUP
D_EXPERT = 256

LANES = 128
MXU_TILE = 256

PROJ_EVEN_TILE = 512
PROJ_ODD_TILE = 1024
PROJ_COLS = 2 * MXU_TILE
ATTN_BLOCK = 512
SWA_CHUNK = 512
ROUTER_TILE = 512
MOE_TILE = 1024
VMEM_LIMIT = 56 * 1024 * 1024

NT_DIMS = (((1,), (1,)), ((), ()))
LOG2E = 1.4426950408889634


def _params(*sem):
    return pltpu.CompilerParams(dimension_semantics=sem, vmem_limit_bytes=VMEM_LIMIT)


def _dot(a, b):
    return jnp.dot(a, b, preferred_element_type=F32)


def _rms(x, g):
    return x * lax.rsqrt(jnp.mean(x * x, axis=-1, keepdims=True) + RMS_EPS) * g


def _hi_lo(w):
    hi = w.astype(BF16)
    return hi, (w - hi.astype(F32)).astype(BF16)


def _dot_hi(x, w_hi, w_lo):
    x_hi = x.astype(BF16)
    x_lo = (x - x_hi.astype(F32)).astype(BF16)
    n = w_hi.shape[1]
    both = _dot(x_hi, jnp.concatenate([w_hi, w_lo], axis=1))
    return both[:, :n] + (both[:, n:] + _dot(x_lo, w_hi))


def _split3(x):
    x1 = x.astype(BF16)
    r1 = x - x1.astype(F32)
    x2 = r1.astype(BF16)
    x3 = (r1 - x2.astype(F32)).astype(BF16)
    return x1, x2, x3


def _proj_even_kernel(x_ref, g_ref, w_ref, wf_hi_ref, wf_lo_ref, bf_ref, out_ref, c_ref, carry_ref, tri_ref, *,
                      tiles_per_seq):
    xn = _rms(x_ref[...], g_ref[...])
    xb = xn.astype(BF16)
    n_out = out_ref.shape[1]
    for c in range(0, n_out, PROJ_COLS):
        out_ref[:, c:c + PROJ_COLS] = _dot(xb, w_ref[:, c:c + PROJ_COLS]).astype(out_ref.dtype)
    z = _dot_hi(xn, wf_hi_ref[...], wf_lo_ref[...]) + bf_ref[...]
    logf = jnp.minimum(z, 0.0) - jnp.log1p(jnp.exp(-jnp.abs(z)))

    @pl.when(pl.program_id(0) % tiles_per_seq == 0)
    def _():
        carry_ref[...] = jnp.zeros_like(carry_ref)

    sub = tri_ref.shape[0]

    @pl.when(pl.program_id(0) == 0)
    def _():
        ri = lax.broadcasted_iota(jnp.int32, (sub, sub), 0)
        ci = lax.broadcasted_iota(jnp.int32, (sub, sub), 1)
        tri_ref[...] = jnp.where(ci <= ri, 1.0, 0.0).astype(BF16)

    lower = tri_ref[...]
    carry = carry_ref[...]
    for r in range(0, logf.shape[0], sub):
        l1, l2, l3 = _split3(logf[r:r + sub])
        c = (_dot(lower, l1) + (_dot(lower, l2) + _dot(lower, l3))) + carry
        c_ref[r:r + sub, :] = c
        carry = c[sub - 1:sub, :]
    carry_ref[...] = carry


def _proj_even(h, g, w, wf_hi, wf_lo, bf, seq):
    n, d = h.shape
    tm = PROJ_EVEN_TILE
    n_out = w.shape[1]
    return pl.pallas_call(
        functools.partial(_proj_even_kernel, tiles_per_seq=seq // tm),
        grid=(n // tm,),
        in_specs=[
            pl.BlockSpec((tm, d), lambda i: (i, 0)),
            pl.BlockSpec((1, d), lambda i: (0, 0)),
            pl.BlockSpec((d, n_out), lambda i: (0, 0)),
            pl.BlockSpec((d, LANES), lambda i: (0, 0)),
            pl.BlockSpec((d, LANES), lambda i: (0, 0)),
            pl.BlockSpec((1, LANES), lambda i: (0, 0)),
        ],
        out_specs=[
            pl.BlockSpec((tm, n_out), lambda i: (i, 0)),
            pl.BlockSpec((tm, LANES), lambda i: (i, 0)),
        ],
        out_shape=[
            jax.ShapeDtypeStruct((n, n_out), BF16),
            jax.ShapeDtypeStruct((n, LANES), F32),
        ],
        scratch_shapes=[pltpu.VMEM((1, LANES), F32), pltpu.VMEM((MXU_TILE, MXU_TILE), BF16)],
        compiler_params=_params("arbitrary"),
        name="proj_even",
    )(h, g, w, wf_hi, wf_lo, bf)


def _own_lanes(lane, h):
    return lane < HEAD_DIM if h == 0 else lane >= HEAD_DIM


def _causal_kernel(*refs, fox, tq, tk):
    if fox:
        q_ref, k_ref, v_ref, c_ref, o_ref, vaug_t, s_even, s_odd, kaug = refs
    else:
        q_ref, k_ref, v_ref, o_ref, vaug_t, s_even, s_odd = refs
    pair = pl.program_id(1)
    qi = pl.program_id(2)
    seq = v_ref.shape[1]

    @pl.when(qi == 0)
    def _():
        lane = lax.broadcasted_iota(jnp.int32, (tk, LANES), 1)
        feat = lax.broadcasted_iota(jnp.int32, (LANES, tk), 0)

        def fill(t, carry):
            rows = pl.ds(pl.multiple_of(t * tk, tk), tk)
            vt = jnp.transpose(v_ref[0, rows, :].astype(F32))
            if fox:
                kp = k_ref[0, rows, :].astype(F32)
                c = c_ref[0, rows, :]
            for h in range(2):
                vaug_t[h, t] = jnp.where(_own_lanes(feat, h), vt, 1.0).astype(BF16)
                if fox:
                    ch = jnp.sum(jnp.where(lane == 2 * pair + h, c, 0.0), axis=-1, keepdims=True)
                    c1, c2, c3 = _split3(ch * (-LOG2E))
                    base = HEAD_DIM if h == 0 else 0
                    extra = jnp.where(lane == base, c1.astype(F32),
                                      jnp.where(lane == base + 1, c2.astype(F32),
                                                jnp.where(lane == base + 2, c3.astype(F32), 0.0)))
                    kaug[h, rows, :] = jnp.where(_own_lanes(lane, h), kp, extra).astype(BF16)
            return carry

        lax.fori_loop(0, seq // tk, fill, 0)

    lane_q = lax.broadcasted_iota(jnp.int32, (tq, LANES), 1)
    qs = []
    for h in range(2):
        if fox:
            base = HEAD_DIM if h == 0 else 0
            ones = jnp.where((lane_q >= base) & (lane_q < base + 3), 1.0, 0.0)
            qs.append(jnp.where(_own_lanes(lane_q, h), q_ref[0].astype(F32), ones).astype(BF16))
        else:
            qs.append(q_ref[0, :, h * LANES:(h + 1) * LANES])

    key = lax.broadcasted_iota(jnp.int32, (tk, tq), 0)
    qry = lax.broadcasted_iota(jnp.int32, (tk, tq), 1)

    def scores(j, buf):
        start = pl.multiple_of(j * tk, tk)
        for h in range(2):
            if fox:
                kj = kaug[h, pl.ds(start, tk), :]
            else:
                kj = k_ref[0, pl.ds(start, tk), h * LANES:(h + 1) * LANES]
            buf[h] = lax.dot_general(kj, qs[h], NT_DIMS, preferred_element_type=F32)

    def absorb(j, buf, carry, masked):
        new = []
        for h in range(2):
            m, acc = carry[h]
            s = buf[h]
            if masked:
                s = jnp.where(key <= qry, s, NEG)
            m_new = jnp.maximum(m, jnp.max(s, axis=0, keepdims=True))
            p = jnp.exp2(s - m_new)
            acc = jnp.exp2(m - m_new) * acc + _dot(vaug_t[h, j], p.astype(BF16))
            new.append((m_new, acc))
        return tuple(new)

    def finish(carry):
        (_, acc0), (_, acc1) = carry
        feat_q = lax.broadcasted_iota(jnp.int32, (LANES, tq), 0)
        out_t = jnp.where(feat_q < HEAD_DIM, acc0 / acc0[HEAD_DIM:HEAD_DIM + 1, :], acc1 / acc1[0:1, :])
        o_ref[0] = jnp.transpose(out_t).astype(o_ref.dtype)

    def pair_step(t, carry):
        scores(2 * t + 1, s_odd)
        carry = absorb(2 * t, s_even, carry, False)
        scores(2 * t + 2, s_even)
        return absorb(2 * t + 1, s_odd, carry, False)

    init = tuple((jnp.full((1, tq), NEG, F32), jnp.zeros((LANES, tq), F32)) for _ in range(2))
    scores(0, s_even)
    carry = lax.fori_loop(0, qi // 2, pair_step, init)

    @pl.when(qi % 2 == 0)
    def _():
        finish(absorb(qi, s_even, carry, True))

    @pl.when(qi % 2 == 1)
    def _():
        scores(qi, s_odd)
        finish(absorb(qi, s_odd, absorb(qi - 1, s_even, carry, False), True))


def _causal_attention(q_arr, k_arr, v_arr, c_arr, *, q_blk0, k_blk0, v_blk0, n_pairs):
    b, s, _ = q_arr.shape
    tq = ATTN_BLOCK
    fox = c_arr is not None
    qk_w = LANES if fox else 2 * LANES
    in_specs = [
        pl.BlockSpec((1, tq, qk_w), lambda bi, p, qi: (bi, qi, q_blk0 + p)),
        pl.BlockSpec((1, s, qk_w), lambda bi, p, qi: (bi, 0, k_blk0 + p)),
        pl.BlockSpec((1, s, LANES), lambda bi, p, qi: (bi, 0, v_blk0 + p)),
    ]
    args = [q_arr, k_arr, v_arr]
    scratch = [pltpu.VMEM((2, s // tq, LANES, tq), BF16), pltpu.VMEM((2, tq, tq), F32), pltpu.VMEM((2, tq, tq), F32)]
    if fox:
        in_specs.append(pl.BlockSpec((1, s, LANES), lambda bi, p, qi: (bi, 0, 0)))
        args.append(c_arr)
        scratch.append(pltpu.VMEM((2, s, LANES), BF16))
    return pl.pallas_call(
        functools.partial(_causal_kernel, fox=fox, tq=tq, tk=tq),
        grid=(b, n_pairs, s // tq),
        in_specs=in_specs,
        out_specs=pl.BlockSpec((1, tq, LANES), lambda bi, p, qi: (bi, qi, p)),
        out_shape=jax.ShapeDtypeStruct((b, s, n_pairs * LANES), BF16),
        scratch_shapes=scratch,
        compiler_params=_params("parallel", "parallel", "arbitrary"),
        name="causal_attention",
    )(*args)


DIL_PAD = BLK * max(d for _, d in DIL_BRANCHES)
DIL_UNROLL = 4


def _dilated_kernel(q_ref, k_ref, v_ref, bias_ref, o_ref, qf, kf, vf, acc_s, m_s, l_s, s_even, s_odd, *, seq):
    qf[...] = q_ref[0].astype(F32)
    kf[0:DIL_PAD, :] = jnp.zeros((DIL_PAD, LANES), F32)
    vf[0:DIL_PAD, :] = jnp.zeros((DIL_PAD, LANES), F32)
    kf[DIL_PAD:, :] = k_ref[0].astype(F32)
    vf[DIL_PAD:, :] = v_ref[0].astype(F32)
    head0 = lax.broadcasted_iota(jnp.int32, (BLK, LANES), 1) < HEAD_DIM
    head0_k = lax.broadcasted_iota(jnp.int32, (2 * BLK, LANES), 1) < HEAD_DIM
    ones0 = jnp.where(head0_k, 1.0, 0.0).astype(BF16)
    ones1 = jnp.where(head0_k, 0.0, 1.0).astype(BF16)

    def place(dil, u):
        bi = u // dil
        q_start = (u % dil) + (dil * BLK) * bi
        return bi, q_start, q_start + (DIL_PAD - dil * BLK)

    def scores(br, dil, g, buf):
        for i in range(DIL_UNROLL):
            bi, q_start, k_start = place(dil, g * DIL_UNROLL + i)
            q2 = qf[pl.ds(q_start, BLK, stride=dil), :]
            q_st = jnp.concatenate([jnp.where(head0, q2, 0.0), jnp.where(head0, 0.0, q2)], axis=0).astype(BF16)
            kb = kf[pl.ds(k_start, 2 * BLK, stride=dil), :].astype(BF16)
            s = lax.dot_general(q_st, kb, NT_DIMS, preferred_element_type=F32)
            buf[i] = s + bias_ref[br, 0, jnp.where(bi == 0, 1, 0)]

    def absorb(br, dil, g, buf, first):
        for i in range(DIL_UNROLL):
            _, q_start, k_start = place(dil, g * DIL_UNROLL + i)
            s = buf[i]
            m = jnp.max(s, axis=-1, keepdims=True)
            p = jnp.exp2(s - m).astype(BF16)
            p_cat = jnp.concatenate([p[:BLK], p[BLK:]], axis=1)
            v2 = vf[pl.ds(k_start, 2 * BLK, stride=dil), :]
            rhs = jnp.concatenate([
                jnp.concatenate([jnp.where(head0_k, v2, 0.0).astype(BF16), ones0], axis=1),
                jnp.concatenate([jnp.where(head0_k, 0.0, v2).astype(BF16), ones1], axis=1)], axis=0)
            acc2 = _dot(p_cat, rhs)
            acc_b, l_b = acc2[:, :LANES], acc2[:, LANES:]
            m_b = jnp.where(head0, jnp.broadcast_to(m[:BLK], (BLK, LANES)), jnp.broadcast_to(m[BLK:], (BLK, LANES)))
            idx = pl.ds(q_start, BLK, stride=dil)
            if first:
                m_s[idx, :] = m_b
                l_s[idx, :] = l_b
                acc_s[idx, :] = acc_b
                continue
            m_old = m_s[idx, :]
            m_new = jnp.maximum(m_old, m_b)
            a_old = jnp.exp2(m_old - m_new)
            a_b = jnp.exp2(m_b - m_new)
            m_s[idx, :] = m_new
            l_s[idx, :] = a_old * l_s[idx, :] + a_b * l_b
            acc_s[idx, :] = a_old * acc_s[idx, :] + a_b * acc_b

    n_groups = seq // (BLK * DIL_UNROLL)
    order = sorted(range(len(DIL_BRANCHES)), key=lambda i: -DIL_BRANCHES[i][1])
    for pos, br in enumerate(order):
        dil = DIL_BRANCHES[br][1]
        first = pos == 0

        def trip(t, carry, br=br, dil=dil, first=first):
            scores(br, dil, 2 * t + 1, s_odd)
            absorb(br, dil, 2 * t, s_even, first)
            scores(br, dil, 2 * t + 2, s_even)
            absorb(br, dil, 2 * t + 1, s_odd, first)
            return carry

        scores(br, dil, 0, s_even)
        lax.fori_loop(0, n_groups // 2 - 1, trip, 0)
        scores(br, dil, n_groups - 1, s_odd)
        absorb(br, dil, n_groups - 2, s_even, first)
        absorb(br, dil, n_groups - 1, s_odd, first)

    o_ref[0] = (acc_s[...] / l_s[...]).astype(o_ref.dtype)


def _dilated_attention(proj, bias, *, q_blk0, k_blk0, v_blk0):
    b, s, _ = proj.shape
    n_pairs = DIL_HEADS // 2
    assert (s // (BLK * DIL_UNROLL)) % 2 == 0
    blk = lambda off: pl.BlockSpec((1, s, LANES), lambda bi, p: (bi, 0, off + p))
    state = [pltpu.VMEM((s, LANES), F32) for _ in range(4)]
    padded = [pltpu.VMEM((DIL_PAD + s, LANES), F32) for _ in range(2)]
    score_bufs = [pltpu.VMEM((DIL_UNROLL, 2 * BLK, 2 * BLK), F32) for _ in range(2)]
    return pl.pallas_call(
        functools.partial(_dilated_kernel, seq=s),
        grid=(b, n_pairs),
        in_specs=[blk(q_blk0), blk(k_blk0), blk(v_blk0),
                  pl.BlockSpec((len(DIL_BRANCHES), 1, 2, 2 * BLK, 2 * BLK), lambda bi, p: (0, p, 0, 0, 0))],
        out_specs=pl.BlockSpec((1, s, LANES), lambda bi, p: (bi, 0, p)),
        out_shape=jax.ShapeDtypeStruct((b, s, n_pairs * LANES), BF16),
        scratch_shapes=[state[0], padded[0], padded[1], state[1], state[2], state[3]] + score_bufs,
        compiler_params=_params("parallel", "parallel"),
        name="dilated_attention",
    )(proj, proj, proj, bias)


def _swa_kernel(q_ref, k_ref, v_ref, bias_ref, sink_ref, o_ref, vaug_t, s_even, s_odd, *, tc):
    ci = pl.program_id(1)
    seq = k_ref.shape[1]
    grp = SWA_Q_HEADS // SWA_KV_HEADS
    n_blk = tc // BLK

    @pl.when(ci == 0)
    def _():
        feat = lax.broadcasted_iota(jnp.int32, (LANES, BLK), 0)

        def fill(t, carry):
            vt = jnp.transpose(v_ref[0, pl.ds(pl.multiple_of(t * BLK, BLK), BLK), :].astype(F32))
            for kv in range(SWA_KV_HEADS):
                vaug_t[kv, t] = jnp.where(_own_lanes(feat, kv), vt, 1.0).astype(BF16)
            return carry

        lax.fori_loop(0, seq // BLK, fill, 0)

    lane_q = lax.broadcasted_iota(jnp.int32, (BLK, LANES), 1)
    feat_o = lax.broadcasted_iota(jnp.int32, (LANES, grp * BLK), 0)

    def blocks(bb):
        own = ci * n_blk + bb
        return jnp.maximum(own - 1, 0), own

    def scores(bb, kv, buf):
        prev, own = blocks(bb)
        k2 = jnp.concatenate([k_ref[0, pl.ds(pl.multiple_of(prev * BLK, BLK), BLK), :],
                              k_ref[0, pl.ds(pl.multiple_of(own * BLK, BLK), BLK), :]], axis=0)
        q4 = jnp.concatenate(
            [jnp.where(_own_lanes(lane_q, kv), q_ref[0, bb * BLK:(bb + 1) * BLK, g * LANES:(g + 1) * LANES].astype(F32),
                       0.0) for g in range(grp)], axis=0).astype(BF16)
        s = lax.dot_general(k2, q4, NT_DIMS, preferred_element_type=F32)
        buf[...] = s + bias_ref[kv, jnp.where(own == 0, 1, 0)]

    def absorb(bb, kv, buf):
        prev, own = blocks(bb)
        s = buf[...]
        sink = sink_ref[kv]
        m = jnp.maximum(jnp.max(s, axis=0, keepdims=True), sink)
        p = jnp.exp2(s - m).astype(BF16)
        v2 = jnp.concatenate([vaug_t[kv, prev], vaug_t[kv, own]], axis=1)
        acc = _dot(v2, p)
        den = (acc[HEAD_DIM:HEAD_DIM + 1, :] if kv == 0 else acc[0:1, :]) + jnp.exp2(sink - m)
        return acc / den

    units = [(bb, kv) for bb in range(n_blk) for kv in range(SWA_KV_HEADS)]
    bufs = (s_even, s_odd)
    scores(*units[0], bufs[0])
    outs = {}
    for n, unit in enumerate(units):
        if n + 1 < len(units):
            scores(*units[n + 1], bufs[(n + 1) % 2])
        outs[unit] = absorb(*unit, bufs[n % 2])
    for bb in range(n_blk):
        out_t = jnp.where(feat_o < HEAD_DIM, outs[(bb, 0)], outs[(bb, 1)])
        for g in range(grp):
            o_ref[0, bb * BLK:(bb + 1) * BLK, g * LANES:(g + 1) * LANES] = jnp.transpose(
                out_t[:, g * BLK:(g + 1) * BLK]).astype(o_ref.dtype)


def _swa_attention(proj, bias, sink_row, *, q_blk0, k_blk0, v_blk0):
    b, s, _ = proj.shape
    tc = SWA_CHUNK
    qw = SWA_Q_HEADS * HEAD_DIM
    grp = SWA_Q_HEADS // SWA_KV_HEADS
    return pl.pallas_call(
        functools.partial(_swa_kernel, tc=tc),
        grid=(b, s // tc),
        in_specs=[
            pl.BlockSpec((1, tc, qw), lambda bi, ci: (bi, ci, q_blk0 * LANES // qw)),
            pl.BlockSpec((1, s, LANES), lambda bi, ci: (bi, 0, k_blk0)),
            pl.BlockSpec((1, s, LANES), lambda bi, ci: (bi, 0, v_blk0)),
            pl.BlockSpec(bias.shape, lambda bi, ci: (0, 0, 0, 0)),
            pl.BlockSpec(sink_row.shape, lambda bi, ci: (0, 0, 0)),
        ],
        out_specs=pl.BlockSpec((1, tc, qw), lambda bi, ci: (bi, ci, 0)),
        out_shape=jax.ShapeDtypeStruct((b, s, qw), BF16),
        scratch_shapes=[pltpu.VMEM((SWA_KV_HEADS, s // BLK, LANES, BLK), BF16),
                        pltpu.VMEM((2 * BLK, grp * BLK), F32), pltpu.VMEM((2 * BLK, grp * BLK), F32)],
        compiler_params=_params("parallel", "arbitrary"),
        name="swa_attention",
    )(proj, proj, proj, bias, sink_row)


def _rope128(x, a, bm, bp):
    return x * a + pltpu.roll(x, LANES - 16, axis=1) * bm + pltpu.roll(x, 16, axis=1) * bp


def _proj_odd_kernel(x_ref, g_ref, w_ref, qn_ref, kvn_ref, wuq_ref, wuk_ref, wuv_ref, a_ref, bm_ref, bp_ref,
                     q_out, k_out, v_out, swa_out):
    xb = _rms(x_ref[...], g_ref[...]).astype(BF16)
    o1 = MLA_Q_RANK + LANES
    o2 = o1 + MLA_KV_RANK
    cq_kpe = _dot(xb, w_ref[:, 0:o1])
    c_q, kpe = cq_kpe[:, :MLA_Q_RANK], cq_kpe[:, MLA_Q_RANK:]
    c_kv = _dot(xb, w_ref[:, o1:o2])
    swa_out[...] = _dot(xb, w_ref[:, o2:]).astype(swa_out.dtype)
    a, bm, bp = a_ref[...], bm_ref[...], bp_ref[...]
    kpe = _rope128(kpe, a, bm, bp)
    cqn = _rms(c_q, qn_ref[...]).astype(BF16)
    ckvn = _rms(c_kv, kvn_ref[...]).astype(BF16)
    v_out[...] = _dot(ckvn, wuv_ref[...]).astype(v_out.dtype)
    q_raw = _dot(cqn, wuq_ref[...])
    k_raw = _dot(ckvn, wuk_ref[...])
    for h in range(MLA_HEADS):
        sl = slice(h * LANES, (h + 1) * LANES)
        q_out[:, sl] = _rope128(q_raw[:, sl], a, bm, bp).astype(q_out.dtype)
        k_out[:, sl] = (k_raw[:, sl] + kpe).astype(k_out.dtype)


def _proj_odd(h, g, w, qn, kvn, wuq, wuk, wuv, rope_a, rope_bm, rope_bp, seq):
    n, d = h.shape
    tm = PROJ_ODD_TILE
    n_in = w.shape[1]
    n_swa = n_in - (MLA_Q_RANK + MLA_KV_RANK + LANES)
    tiles_per_seq = seq // tm
    full = lambda shape: pl.BlockSpec(shape, lambda i: (0, 0))
    rope = pl.BlockSpec((tm, LANES), lambda i: (i % tiles_per_seq, 0))
    return pl.pallas_call(
        _proj_odd_kernel,
        grid=(n // tm,),
        in_specs=[
            pl.BlockSpec((tm, d), lambda i: (i, 0)), full((1, d)), full(w.shape),
            full(qn.shape), full(kvn.shape), full(wuq.shape), full(wuk.shape), full(wuv.shape),
            rope, rope, rope,
        ],
        out_specs=[
            pl.BlockSpec((tm, MLA_HEADS * LANES), lambda i: (i, 0)),
            pl.BlockSpec((tm, MLA_HEADS * LANES), lambda i: (i, 0)),
            pl.BlockSpec((tm, MLA_HEADS * MLA_V), lambda i: (i, 0)),
            pl.BlockSpec((tm, n_swa), lambda i: (i, 0)),
        ],
        out_shape=[
            jax.ShapeDtypeStruct((n, MLA_HEADS * LANES), BF16),
            jax.ShapeDtypeStruct((n, MLA_HEADS * LANES), BF16),
            jax.ShapeDtypeStruct((n, MLA_HEADS * MLA_V), BF16),
            jax.ShapeDtypeStruct((n, n_swa), BF16),
        ],
        compiler_params=_params("parallel"),
        name="proj_odd",
    )(h, g, w, qn, kvn, wuq, wuk, wuv, rope_a, rope_bm, rope_bp)


def _out_router_kernel(h_ref, oa_ref, ob_ref, wo_ref, g_ref, wr_hi_ref, wr_lo_ref, br_ref,
                       h_out, hn_out, gates_out, counts_out, pos_col_out, pos_row_out, tri_ref, seen_ref):
    i = pl.program_id(0)
    tm = h_ref.shape[0]
    half = oa_ref.shape[1]

    @pl.when(i == 0)
    def _():
        ri = lax.broadcasted_iota(jnp.int32, (tm, tm), 0)
        ci = lax.broadcasted_iota(jnp.int32, (tm, tm), 1)
        tri_ref[...] = jnp.where(ci < ri, 1.0, 0.0).astype(BF16)

    @pl.when(i % (MOE_TILE // tm) == 0)
    def _():
        seen_ref[...] = jnp.zeros_like(seen_ref)

    lane = lax.broadcasted_iota(jnp.int32, (MXU_TILE, LANES), 1)
    big = jnp.int32(LANES)
    is_grp = (lane >= N_EXPERTS) & (lane < N_EXPERTS + N_GROUPS)
    routed_parts = []
    for r in range(0, tm, MXU_TILE):
        rows = slice(r, r + MXU_TILE)
        h = h_ref[rows, :] + _dot(oa_ref[rows, :], wo_ref[0:half, :]) + _dot(ob_ref[rows, :], wo_ref[half:, :])
        h_out[rows, :] = h
        hn = _rms(h, g_ref[...])
        hn_out[rows, :] = hn.astype(hn_out.dtype)
        z = _dot_hi(hn, wr_hi_ref[...], wr_lo_ref[...]) + br_ref[...]
        zg = jnp.where(is_grp, z, -jnp.inf)
        g_max = jnp.max(zg, axis=-1, keepdims=True)
        g_w = 1.0 / jnp.sum(jnp.exp(zg - g_max), axis=-1, keepdims=True)
        g_idx = jnp.min(jnp.where(zg == g_max, lane - N_EXPERTS, big), axis=-1, keepdims=True)
        in_grp = (lane < N_EXPERTS) & ((lane // EXPERTS_PER_GROUP) == g_idx)
        ze = jnp.where(in_grp, z, -jnp.inf)
        v1 = jnp.max(ze, axis=-1, keepdims=True)
        i1 = jnp.min(jnp.where(ze == v1, lane, big), axis=-1, keepdims=True)
        ze2 = jnp.where(lane == i1, -jnp.inf, ze)
        v2 = jnp.max(ze2, axis=-1, keepdims=True)
        i2 = jnp.min(jnp.where(ze2 == v2, lane, big), axis=-1, keepdims=True)
        e2 = jnp.exp(v2 - v1)
        w1 = g_w / (1.0 + e2)
        w2 = g_w * e2 / (1.0 + e2)
        gates_out[rows, :] = jnp.where(lane == i1, w1, 0.0) + jnp.where(lane == i2, w2, 0.0)
        routed_parts.append(jnp.where((lane == g_idx) & (lane < N_GROUPS), 1.0, 0.0))

    routed = jnp.concatenate(routed_parts, axis=0)
    pos = jnp.where(routed > 0.0, _dot(tri_ref[...], routed.astype(BF16)) + seen_ref[...], -1.0)
    pos_col_out[...] = pos
    pos_row_out[0] = jnp.transpose(pos)[0:8, :]
    here = jnp.sum(routed, axis=0, keepdims=True)
    seen_ref[...] += here
    counts_out[0] = jnp.broadcast_to(here, counts_out.shape[1:])


def _out_router(h, oa, ob, wo, g, wr_hi, wr_lo, br):
    n, d = h.shape
    tm = ROUTER_TILE
    per_tile = MOE_TILE // tm
    half = oa.shape[1]
    full = lambda shape: pl.BlockSpec(shape, lambda i: (0, 0))
    tile = lambda w: pl.BlockSpec((tm, w), lambda i: (i, 0))
    return pl.pallas_call(
        _out_router_kernel,
        grid=(n // tm,),
        in_specs=[tile(d), tile(half), tile(half), full(wo.shape), full((1, d)),
                  full(wr_hi.shape), full(wr_lo.shape), full((1, LANES))],
        out_specs=[tile(d), tile(d), tile(LANES), pl.BlockSpec((1, 8, LANES), lambda i: (i, 0, 0)),
                   tile(LANES), pl.BlockSpec((1, 8, tm), lambda i: (i // per_tile, 0, i % per_tile))],
        out_shape=[
            jax.ShapeDtypeStruct((n, d), F32),
            jax.ShapeDtypeStruct((n, d), BF16),
            jax.ShapeDtypeStruct((n, LANES), F32),
            jax.ShapeDtypeStruct((n // tm, 8, LANES), F32),
            jax.ShapeDtypeStruct((n, LANES), F32),
            jax.ShapeDtypeStruct((n // MOE_TILE, 8, MOE_TILE), F32),
        ],
        scratch_shapes=[pltpu.VMEM((tm, tm), BF16), pltpu.VMEM((1, LANES), F32)],
        compiler_params=_params("arbitrary"),
        name="out_router",
    )(h, oa, ob, wo, g, wr_hi, wr_lo, br)


MOE_CHUNK = 288
MOE_ROWS = 1024
MOE_ALIGN = 16
MOE_EXPERTS_PER_STEP = 4


def _moe_sorted_tiles(n):
    n_seg = (n // MOE_TILE) * N_GROUPS
    rows = n + n_seg * (MOE_ALIGN - 1) + N_GROUPS * (MOE_CHUNK + MOE_ROWS - 1)
    return -(-rows // MOE_ROWS)


def _moe_tables(counts, n):
    n_rt = _moe_sorted_tiles(n)
    seg_len = (counts + (MOE_ALIGN - 1)) // MOE_ALIGN * MOE_ALIGN
    group_len = seg_len.sum(axis=0)
    group_span = (group_len + MOE_CHUNK + MOE_ROWS - 1) // MOE_ROWS * MOE_ROWS
    group_end = jnp.cumsum(group_span)
    group_start = group_end - group_span
    seg_off = group_start[None, :] + jnp.cumsum(seg_len, axis=0) - seg_len
    rt_start = jnp.arange(n_rt, dtype=jnp.int32) * MOE_ROWS
    rt_group = jnp.minimum(jnp.sum(rt_start[:, None] >= group_end[None, :], axis=1), N_GROUPS - 1)
    rt_valid = jnp.clip(group_start[rt_group] + group_len[rt_group] - rt_start, 0, MOE_ROWS)
    n_used = (group_end[-1] // MOE_ROWS).reshape(1)
    i32 = lambda a: a.astype(jnp.int32)
    return i32(seg_off.reshape(-1)), i32(rt_group), i32(rt_valid), i32(n_used)


def _moe_chunks(count, body):
    def one_chunk(k, carry):
        body(k * MOE_CHUNK, MOE_CHUNK)
        return carry

    lax.fori_loop(0, (count + (MOE_CHUNK - 1)) // MOE_CHUNK, one_chunk, 0)


def _moe_pack_kernel(counts_ref, off_ref, hn_ref, gates_ref, pos_ref, xs_in, xs_ref, xbuf, sem, issued_ref):
    del xs_in
    i = pl.program_id(0)
    d = hn_ref.shape[1]

    @pl.when(i == 0)
    def _():
        issued_ref[0] = 0

    def write(slot):
        return pltpu.make_async_copy(xbuf.at[slot], xs_ref.at[pl.ds(0, MOE_CHUNK)], sem.at[slot])

    lane = lax.broadcasted_iota(jnp.int32, (hn_ref.shape[0], LANES), 1)
    gates = gates_ref[...]
    gates_hi = gates.astype(BF16).astype(F32)
    gates_lo = gates - gates_hi
    for grp in range(N_GROUPS):
        seg = i * N_GROUPS + grp
        first = grp * EXPERTS_PER_GROUP
        g_hi = gates_hi if first == 0 else pltpu.roll(gates_hi, LANES - first, axis=1)
        g_lo = pltpu.roll(gates_lo, (LANES - first + EXPERTS_PER_GROUP) % LANES, axis=1)
        gate_cols = jnp.where(lane < EXPERTS_PER_GROUP, g_hi, jnp.where(lane < 2 * EXPERTS_PER_GROUP, g_lo, 0.0))
        gate_cols = gate_cols.astype(BF16)
        pos_row = pos_ref[0, grp:grp + 1, :]

        def chunk(first_pos, rows, seg=seg, gate_cols=gate_cols, pos_row=pos_row):
            n_done = issued_ref[0]
            slot = n_done % 2
            slot_pos = lax.broadcasted_iota(jnp.int32, (rows, 1), 0).astype(F32) + first_pos.astype(F32)
            gather = jnp.where(pos_row == slot_pos, 1.0, 0.0).astype(BF16)
            xbuf[slot, :, 0:d] = _dot(gather, hn_ref[...]).astype(BF16)
            xbuf[slot, :, d:] = _dot(gather, gate_cols).astype(BF16)

            @pl.when(n_done > 0)
            def _():
                write(1 - slot).wait()

            dst = pl.multiple_of(off_ref[seg] + first_pos, MOE_ALIGN)
            pltpu.make_async_copy(xbuf.at[slot], xs_ref.at[pl.ds(dst, MOE_CHUNK)], sem.at[slot]).start()
            issued_ref[0] = n_done + 1

        _moe_chunks(counts_ref[seg], chunk)

    @pl.when((i == pl.num_programs(0) - 1) & (issued_ref[0] > 0))
    def _():
        write((issued_ref[0] - 1) % 2).wait()


def _moe_pack(counts, seg_off, hn, gates, pos_row, xs_init):
    n, d = hn.shape
    rows = xs_init.shape[0]
    grid_spec = pltpu.PrefetchScalarGridSpec(
        num_scalar_prefetch=2,
        grid=(n // MOE_TILE,),
        in_specs=[
            pl.BlockSpec((MOE_TILE, d), lambda i, c, o: (i, 0)),
            pl.BlockSpec((MOE_TILE, LANES), lambda i, c, o: (i, 0)),
            pl.BlockSpec((1, 8, MOE_TILE), lambda i, c, o: (i, 0, 0)),
            pl.BlockSpec(memory_space=pl.ANY),
        ],
        out_specs=pl.BlockSpec(memory_space=pl.ANY),
        scratch_shapes=[pltpu.VMEM((2, MOE_CHUNK, d + LANES), BF16), pltpu.SemaphoreType.DMA((2,)),
                        pltpu.SMEM((1,), jnp.int32)],
    )
    return pl.pallas_call(
        _moe_pack_kernel,
        grid_spec=grid_spec,
        out_shape=jax.ShapeDtypeStruct((rows, d + LANES), BF16),
        input_output_aliases={5: 0},
        compiler_params=_params("arbitrary"),
        name="moe_pack",
    )(counts, seg_off, hn, gates, pos_row, xs_init)


def _moe_experts_kernel(grp_ref, valid_ref, used_ref, xs_ref, wg_ref, wu_ref, wd_ref, ys_ref, acc_ref,
                        wg_b, wu_b, wd_b):
    del grp_ref, used_ref
    rt = pl.program_id(0)
    j = pl.program_id(1)
    d = ys_ref.shape[1]
    valid = valid_ref[rt]

    @pl.when((valid > 0) & (j == 0))
    def _():
        acc_ref[...] = jnp.zeros_like(acc_ref)

    def experts(x, gate_cols, w_gate, w_up, w_down):
        lane = lax.broadcasted_iota(jnp.int32, gate_cols.shape, 1)
        acts = []
        for e in range(MOE_EXPERTS_PER_STEP):
            idx = j * MOE_EXPERTS_PER_STEP + e
            mine = (lane == idx) | (lane == idx + EXPERTS_PER_GROUP)
            gate = jnp.sum(jnp.where(mine, gate_cols, 0.0), axis=-1, keepdims=True)
            a = _dot(x, w_gate(e))
            u = _dot(x, w_up(e))
            acts.append(((a * jax.nn.sigmoid(a)) * u * gate).astype(BF16))
        return _dot(jnp.concatenate(acts, axis=1), w_down())

    @pl.when(valid == MOE_ROWS)
    def _():
        acc_ref[...] += experts(
            xs_ref[:, 0:d], xs_ref[:, d:].astype(F32), lambda e: wg_ref[0, 0, e].astype(BF16),
            lambda e: wu_ref[0, 0, e].astype(BF16), lambda: wd_ref[0, 0].astype(BF16).reshape(wd_b.shape))

    @pl.when((valid > 0) & (valid < MOE_ROWS))
    def _():
        wg_b[...] = wg_ref[0, 0].astype(BF16)
        wu_b[...] = wu_ref[0, 0].astype(BF16)
        wd_b[...] = wd_ref[0, 0].astype(BF16).reshape(wd_b.shape)

        def rows_block(t, carry):
            rows = pl.ds(pl.multiple_of(t * MXU_TILE, MXU_TILE), MXU_TILE)
            acc_ref[rows, :] += experts(xs_ref[rows, 0:d], xs_ref[rows, d:].astype(F32), lambda e: wg_b[e],
                                        lambda e: wu_b[e], lambda: wd_b[...])
            return carry

        lax.fori_loop(0, (valid + (MXU_TILE - 1)) // MXU_TILE, rows_block, 0)

    @pl.when((valid > 0) & (j == pl.num_programs(1) - 1))
    def _():
        row = lax.broadcasted_iota(jnp.int32, (acc_ref.shape[0], 1), 0)
        ys_ref[...] = jnp.where(row < valid, acc_ref[...], 0.0).astype(ys_ref.dtype)

    @pl.when((valid == 0) & (j == pl.num_programs(1) - 1))
    def _():
        ys_ref[...] = jnp.zeros_like(ys_ref)


def _moe_experts(rt_group, rt_valid, n_used, xs, w_gate, w_up, w_down, layer):
    rows = xs.shape[0]
    d = w_gate.shape[-2]
    n_rt = rows // MOE_ROWS
    eps = MOE_EXPERTS_PER_STEP
    n_steps = EXPERTS_PER_GROUP // eps

    def tile_idx(rt, j, grp, valid, used):
        return jnp.minimum(rt, used[0] - 1)

    def w_idx(rt, j, grp, valid, used):
        live = rt < used[0]
        return (layer, grp[tile_idx(rt, j, grp, valid, used)], jnp.where(live, j, n_steps - 1), 0, 0)

    grid_spec = pltpu.PrefetchScalarGridSpec(
        num_scalar_prefetch=3,
        grid=(n_rt, n_steps),
        in_specs=[
            pl.BlockSpec((MOE_ROWS, d + LANES), lambda rt, j, grp, valid, used: (tile_idx(rt, j, grp, valid, used), 0)),
            pl.BlockSpec((1, 1, eps, d, D_EXPERT), w_idx),
            pl.BlockSpec((1, 1, eps, d, D_EXPERT), w_idx),
            pl.BlockSpec((1, 1, eps, D_EXPERT, d), w_idx),
        ],
        out_specs=pl.BlockSpec((MOE_ROWS, d), lambda rt, j, grp, valid, used: (rt, 0)),
        scratch_shapes=[pltpu.VMEM((MOE_ROWS, d), F32), pltpu.VMEM((eps, d, D_EXPERT), BF16),
                        pltpu.VMEM((eps, d, D_EXPERT), BF16), pltpu.VMEM((eps * D_EXPERT, d), BF16)],
    )
    return pl.pallas_call(
        _moe_experts_kernel,
        grid_spec=grid_spec,
        out_shape=jax.ShapeDtypeStruct((rows, d), BF16),
        compiler_params=_params("arbitrary", "arbitrary"),
        name="moe_experts",
    )(rt_group, rt_valid, n_used, xs, w_gate, w_up, w_down)


def _moe_combine_kernel(counts_ref, off_ref, pos_ref, h_ref, fg_ref, ys_ref, o_ref, ybuf, ymore, sem, *, final_norm):
    i = pl.program_id(0)

    def read(tile, grp):
        src = pl.multiple_of(off_ref[tile * N_GROUPS + grp], MOE_ALIGN)
        slot = tile % 2
        return pltpu.make_async_copy(ys_ref.at[pl.ds(src, MOE_CHUNK)],
                                     ybuf.at[slot, pl.ds(grp * MOE_CHUNK, MOE_CHUNK)], sem.at[slot, grp])

    def fetch(tile):
        for grp in range(N_GROUPS):
            @pl.when(counts_ref[tile * N_GROUPS + grp] > 0)
            def _(grp=grp):
                read(tile, grp).start()

    @pl.when(i == 0)
    def _():
        ybuf[...] = jnp.zeros_like(ybuf)
        fetch(i)

    @pl.when(i + 1 < pl.num_programs(0))
    def _():
        fetch(i + 1)

    slot_pos = lax.broadcasted_iota(jnp.int32, (1, MOE_CHUNK), 1).astype(F32)
    scatter = jnp.concatenate(
        [jnp.where(pos_ref[:, grp:grp + 1] == slot_pos, 1.0, 0.0).astype(BF16) for grp in range(N_GROUPS)], axis=1)
    for grp in range(N_GROUPS):
        @pl.when(counts_ref[i * N_GROUPS + grp] > 0)
        def _(grp=grp):
            read(i, grp).wait()

    o_ref[...] = h_ref[...] + _dot(scatter, ybuf[i % 2])
    for grp in range(N_GROUPS):
        count = counts_ref[i * N_GROUPS + grp]
        pos_col = pos_ref[:, grp:grp + 1]

        def more(k, carry, grp=grp, pos_col=pos_col):
            src = pl.multiple_of(off_ref[i * N_GROUPS + grp] + k * MOE_CHUNK, MOE_ALIGN)
            pltpu.sync_copy(ys_ref.at[pl.ds(src, MOE_CHUNK)], ymore)
            scatter = jnp.where(pos_col == slot_pos + (k * MOE_CHUNK).astype(F32), 1.0, 0.0).astype(BF16)
            o_ref[...] += _dot(scatter, ymore[...])
            return carry

        lax.fori_loop(1, (count + (MOE_CHUNK - 1)) // MOE_CHUNK, more, 0)

    if final_norm:
        o_ref[...] = _rms(o_ref[...], fg_ref[...])


def _moe_combine(counts, seg_off, pos_col, h, fg, ys, final_norm):
    n, d = h.shape
    grid_spec = pltpu.PrefetchScalarGridSpec(
        num_scalar_prefetch=2,
        grid=(n // MOE_TILE,),
        in_specs=[
            pl.BlockSpec((MOE_TILE, LANES), lambda i, c, o: (i, 0)),
            pl.BlockSpec((MOE_TILE, d), lambda i, c, o: (i, 0)),
            pl.BlockSpec((1, d), lambda i, c, o: (0, 0)),
            pl.BlockSpec(memory_space=pl.ANY),
        ],
        out_specs=pl.BlockSpec((MOE_TILE, d), lambda i, c, o: (i, 0)),
        scratch_shapes=[pltpu.VMEM((2, N_GROUPS * MOE_CHUNK, d), BF16), pltpu.VMEM((MOE_CHUNK, d), BF16),
                        pltpu.SemaphoreType.DMA((2, N_GROUPS))],
    )
    return pl.pallas_call(
        functools.partial(_moe_combine_kernel, final_norm=final_norm),
        grid_spec=grid_spec,
        out_shape=jax.ShapeDtypeStruct((n, d), F32),
        compiler_params=_params("arbitrary"),
        name="moe_combine",
    )(counts, seg_off, pos_col, h, fg, ys)


def _alibi_slopes(n):
    return np.exp2(-8.0 * np.arange(1, n + 1, dtype=np.float64) / n)


def _band_bias(slopes, max_steps, step_dist):
    steps = (np.arange(BLK)[:, None] + BLK) - np.arange(2 * BLK)[None, :]
    in_band = (steps >= 0) & (steps <= max_steps)
    dist = (steps * step_dist).astype(np.float64)
    return np.where(in_band[None], -slopes[:, None, None] * dist[None], NEG)


def _attention_bias_tables():
    own_half = np.arange(2 * BLK) >= BLK
    dil = np.stack([_band_bias(_alibi_slopes(DIL_HEADS), w // dl, dl) for (w, dl) in DIL_BRANCHES]) * LOG2E
    dil = np.stack([dil, np.where(own_half, dil, NEG)], axis=2)
    dil = dil.reshape(len(DIL_BRANCHES), DIL_HEADS // 2, 2, 2, BLK, 2 * BLK)
    dil = dil.transpose(0, 1, 3, 2, 4, 5).reshape(len(DIL_BRANCHES), DIL_HEADS // 2, 2, 2 * BLK, 2 * BLK)
    grp = SWA_Q_HEADS // SWA_KV_HEADS
    swa = _band_bias(_alibi_slopes(SWA_Q_HEADS), SWA_WINDOW - 1, 1) * LOG2E
    swa = swa.reshape(SWA_KV_HEADS, grp, BLK, 2 * BLK).transpose(0, 3, 1, 2).reshape(SWA_KV_HEADS, 2 * BLK, grp * BLK)
    swa = np.stack([swa, np.where(own_half[None, :, None], swa, NEG)], axis=1)
    return dil.astype(np.float32), swa.astype(np.float32)


def _rope_tables(s):
    inv = ROPE_THETA ** (-np.arange(0, MLA_ROPE, 2, dtype=np.float64) / MLA_ROPE)
    ang = np.arange(s, dtype=np.float64)[:, None] * inv[None, :]
    cos, sin = np.cos(ang), np.sin(ang)
    half = MLA_ROPE // 2
    zeros_tail = np.zeros((s, LANES - MLA_NOPE - MLA_ROPE))
    a = np.concatenate([np.ones((s, MLA_NOPE)), cos, cos, zeros_tail], axis=1)
    zeros_nope = np.zeros((s, MLA_NOPE))
    zeros_half = np.zeros((s, half))
    bm = np.concatenate([zeros_nope, -sin, zeros_half, zeros_tail], axis=1)
    bp = np.concatenate([zeros_nope, zeros_half, sin, zeros_tail], axis=1)
    return a.astype(np.float32), bm.astype(np.float32), bp.astype(np.float32)


def _swa_head_order(w, axis):
    grp = SWA_Q_HEADS // SWA_KV_HEADS
    shape = w.shape
    w = w.reshape(shape[:axis] + (SWA_KV_HEADS, grp, HEAD_DIM) + shape[axis + 1:])
    return jnp.swapaxes(w, axis, axis + 1).reshape(shape)


def _pad_cols(w, width):
    return jnp.pad(w, ((0, 0), (0, width - w.shape[1])))


def _router_weights(w_group, b_group, w_router, b_router):
    w = _pad_cols(jnp.concatenate([w_router, w_group], axis=1), LANES)
    b = _pad_cols(jnp.concatenate([b_router, b_group])[None, :], LANES)
    hi, lo = _hi_lo(w)
    return hi, lo, b


def kernel(x, attn_norm, ffn_norm, final_norm, e_w_in, e_b_f, e_w_out, o_w_in, o_q_norm, o_kv_norm, o_w_uq,
           o_w_ukv, o_sinks, o_w_out, moe_w_group, moe_b_group, moe_w_router, moe_b_router, moe_w_gate,
           moe_w_up, moe_w_down):
    b, s, d = x.shape
    n = b * s
    depth = attn_norm.shape[0]
    assert s % (BLK * DIL_BRANCHES[-1][1]) == 0 and d == D_MODEL
    assert all(s % t == 0 for t in (PROJ_EVEN_TILE, PROJ_ODD_TILE, ATTN_BLOCK, SWA_CHUNK, MOE_TILE))
    assert MOE_TILE % ROUTER_TILE == 0
    h = x.reshape(n, d)

    dil_bias, swa_bias = _attention_bias_tables()
    rope_a, rope_bm, rope_bp = _rope_tables(s)

    xs = jnp.zeros((_moe_sorted_tiles(n) * MOE_ROWS, d + LANES), BF16)
    for layer in range(depth):
        i = layer // 2
        g_attn = attn_norm[layer][None, :]
        if layer % 2 == 0:
            w_in = e_w_in[i]
            hq = FOX_HEADS * HEAD_DIM
            scale = HEAD_DIM ** -0.5
            cols = [w_in[:, 0:hq] * (scale * LOG2E), w_in[:, hq:2 * hq], w_in[:, 2 * hq:3 * hq]]
            o = 3 * hq + FOX_HEADS
            cols += [w_in[:, o:o + hq] * (scale * LOG2E), w_in[:, o + hq:o + 2 * hq], w_in[:, o + 2 * hq:o + 3 * hq]]
            w_main = jnp.concatenate(cols, axis=1).astype(BF16)
            wf_hi, wf_lo = _hi_lo(_pad_cols(w_in[:, 3 * hq:o], LANES))
            b_f = _pad_cols(e_b_f[i][None, :], LANES)
            proj, c = _proj_even(h, g_attn, w_main, wf_hi, wf_lo, b_f, s)
            proj = proj.reshape(b, s, -1)
            o_a = _causal_attention(proj, proj, proj, c.reshape(b, s, LANES), q_blk0=0, k_blk0=4, v_blk0=8,
                                    n_pairs=FOX_HEADS // 2)
            o_b = _dilated_attention(proj, dil_bias, q_blk0=12, k_blk0=16, v_blk0=20)
            w_out = e_w_out[i].astype(BF16)
        else:
            w_in = o_w_in[i]
            o1 = MLA_Q_RANK + MLA_KV_RANK
            o2 = o1 + MLA_ROPE
            sq = SWA_Q_HEADS * HEAD_DIM
            kpe_cols = jnp.pad(w_in[:, o1:o2], ((0, 0), (MLA_NOPE, LANES - MLA_NOPE - MLA_ROPE)))
            w_main = jnp.concatenate(
                [w_in[:, :MLA_Q_RANK], kpe_cols, w_in[:, MLA_Q_RANK:o1],
                 _swa_head_order(w_in[:, o2:o2 + sq] * (HEAD_DIM ** -0.5 * LOG2E), axis=1),
                 w_in[:, o2 + sq:]],
                axis=1).astype(BF16)
            dq = MLA_NOPE + MLA_ROPE
            wuq = o_w_uq[i].reshape(MLA_Q_RANK, MLA_HEADS, dq) * (dq ** -0.5 * LOG2E)
            wuq = jnp.pad(wuq, ((0, 0), (0, 0), (0, LANES - dq))).reshape(MLA_Q_RANK, MLA_HEADS * LANES)
            wukv = o_w_ukv[i].reshape(MLA_KV_RANK, MLA_HEADS, MLA_NOPE + MLA_V)
            wuk = jnp.pad(wukv[:, :, :MLA_NOPE], ((0, 0), (0, 0), (0, LANES - MLA_NOPE)))
            wuk = wuk.reshape(MLA_KV_RANK, MLA_HEADS * LANES)
            wuv = wukv[:, :, MLA_NOPE:].reshape(MLA_KV_RANK, MLA_HEADS * MLA_V)
            q_full, k_full, v_mla, swa = _proj_odd(
                h, g_attn, w_main, o_q_norm[i][None, :], o_kv_norm[i][None, :], wuq.astype(BF16),
                wuk.astype(BF16), wuv.astype(BF16), rope_a, rope_bm, rope_bp, s)
            o_a = _causal_attention(q_full.reshape(b, s, -1), k_full.reshape(b, s, -1), v_mla.reshape(b, s, -1),
                                    None, q_blk0=0, k_blk0=0, v_blk0=0, n_pairs=MLA_HEADS // 2)
            grp = SWA_Q_HEADS // SWA_KV_HEADS
            sink_row = jnp.repeat(o_sinks[i].reshape(SWA_KV_HEADS, grp) * LOG2E, BLK, axis=1)[:, None, :]
            o_b = _swa_attention(swa.reshape(b, s, -1), swa_bias, sink_row, q_blk0=0, k_blk0=4, v_blk0=5)
            half = MLA_HEADS * MLA_V
            w_out = jnp.concatenate([o_w_out[i][:half], _swa_head_order(o_w_out[i][half:], axis=0)], axis=0)
            w_out = w_out.astype(BF16)

        wr_hi, wr_lo, b_r = _router_weights(moe_w_group[layer], moe_b_group[layer], moe_w_router[layer],
                                            moe_b_router[layer])
        h, hn, gates, counts, pos_col, pos_row = _out_router(
            h, o_a.reshape(n, -1), o_b.reshape(n, -1), w_out, ffn_norm[layer][None, :], wr_hi, wr_lo, b_r)
        counts = counts[:, 0, :N_GROUPS].reshape(n // MOE_TILE, MOE_TILE // ROUTER_TILE, N_GROUPS).sum(axis=1)
        counts = counts.astype(jnp.int32)
        seg_off, rt_group, rt_valid, n_used = _moe_tables(counts, n)
        counts = counts.reshape(-1)
        xs = _moe_pack(counts, seg_off, hn, gates, pos_row, xs)
        ys = _moe_experts(rt_group, rt_valid, n_used, xs, moe_w_gate, moe_w_up, moe_w_down, layer)
        h = _moe_combine(counts, seg_off, pos_col, h, final_norm[None, :], ys, final_norm=layer == depth - 1)
    return h.reshape(b, s, d)
```

```python
import functools

import numpy as np
import jax
import jax.numpy as jnp
from jax import lax
from jax.experimental import pallas as pl
from jax.experimental.pallas import tpu as pltpu

F32 = jnp.float32
BF16 = jnp.bfloat16

D_MODEL = 1024
HEAD_DIM = 64
BLK = 128
NEG = -1e30
RMS_EPS = 1e-6
FOX_HEADS = 8
DIL_HEADS = 8
DIL_BRANCHES = ((128, 1), (512, 4), (2048, 16))
MLA_HEADS = 8
MLA_Q_RANK = 384
MLA_KV_RANK = 256
MLA_NOPE = 64
MLA_ROPE = 32
MLA_V = 64
ROPE_THETA = 10000.0
SWA_Q_HEADS = 8
SWA_KV_HEADS = 2
SWA_WINDOW = 128
N_GROUPS = 4
EXPERTS_PER_GROUP = 8
N_EXPERTS = N_GROUPS * EXPERTS_PER_GROUP
D_EXPERT = 256

LANES = 128
MXU_TILE = 256

PROJ_EVEN_TILE = 512
PROJ_ODD_TILE = 1024
PROJ_COLS = 2 * MXU_TILE
ATTN_BLOCK = 512
SWA_CHUNK = 512
ROUTER_TILE = 512
MOE_TILE = 1024
VMEM_LIMIT = 56 * 1024 * 1024

NT_DIMS = (((1,), (1,)), ((), ()))
LOG2E = 1.4426950408889634


def _params(*sem):
    return pltpu.CompilerParams(dimension_semantics=sem, vmem_limit_bytes=VMEM_LIMIT)


def _dot(a, b):
    return jnp.dot(a, b, preferred_element_type=F32)


def _rms(x, g):
    return x * lax.rsqrt(jnp.mean(x * x, axis=-1, keepdims=True) + RMS_EPS) * g


def _hi_lo(w):
    hi = w.astype(BF16)
    return hi, (w - hi.astype(F32)).astype(BF16)


def _dot_hi(x, w_hi, w_lo):
    x_hi = x.astype(BF16)
    x_lo = (x - x_hi.astype(F32)).astype(BF16)
    n = w_hi.shape[1]
    both = _dot(x_hi, jnp.concatenate([w_hi, w_lo], axis=1))
    return both[:, :n] + (both[:, n:] + _dot(x_lo, w_hi))


def _split3(x):
    x1 = x.astype(BF16)
    r1 = x - x1.astype(F32)
    x2 = r1.astype(BF16)
    x3 = (r1 - x2.astype(F32)).astype(BF16)
    return x1, x2, x3


def _proj_even_kernel(x_ref, g_ref, w_ref, wf_hi_ref, wf_lo_ref, bf_ref, out_ref, c_ref, carry_ref, tri_ref, *,
                      tiles_per_seq):
    xn = _rms(x_ref[...], g_ref[...])
    xb = xn.astype(BF16)
    n_out = out_ref.shape[1]
    for c in range(0, n_out, PROJ_COLS):
        out_ref[:, c:c + PROJ_COLS] = _dot(xb, w_ref[:, c:c + PROJ_COLS]).astype(out_ref.dtype)
    z = _dot_hi(xn, wf_hi_ref[...], wf_lo_ref[...]) + bf_ref[...]
    logf = jnp.minimum(z, 0.0) - jnp.log1p(jnp.exp(-jnp.abs(z)))

    @pl.when(pl.program_id(0) % tiles_per_seq == 0)
    def _():
        carry_ref[...] = jnp.zeros_like(carry_ref)

    sub = tri_ref.shape[0]

    @pl.when(pl.program_id(0) == 0)
    def _():
        ri = lax.broadcasted_iota(jnp.int32, (sub, sub), 0)
        ci = lax.broadcasted_iota(jnp.int32, (sub, sub), 1)
        tri_ref[...] = jnp.where(ci <= ri, 1.0, 0.0).astype(BF16)

    lower = tri_ref[...]
    carry = carry_ref[...]
    for r in range(0, logf.shape[0], sub):
        l1, l2, l3 = _split3(logf[r:r + sub])
        c = (_dot(lower, l1) + (_dot(lower, l2) + _dot(lower, l3))) + carry
        c_ref[r:r + sub, :] = c
        carry = c[sub - 1:sub, :]
    carry_ref[...] = carry


def _proj_even(h, g, w, wf_hi, wf_lo, bf, seq):
    n, d = h.shape
    tm = PROJ_EVEN_TILE
    n_out = w.shape[1]
    return pl.pallas_call(
        functools.partial(_proj_even_kernel, tiles_per_seq=seq // tm),
        grid=(n // tm,),
        in_specs=[
            pl.BlockSpec((tm, d), lambda i: (i, 0)),
            pl.BlockSpec((1, d), lambda i: (0, 0)),
            pl.BlockSpec((d, n_out), lambda i: (0, 0)),
            pl.BlockSpec((d, LANES), lambda i: (0, 0)),
            pl.BlockSpec((d, LANES), lambda i: (0, 0)),
            pl.BlockSpec((1, LANES), lambda i: (0, 0)),
        ],
        out_specs=[
            pl.BlockSpec((tm, n_out), lambda i: (i, 0)),
            pl.BlockSpec((tm, LANES), lambda i: (i, 0)),
        ],
        out_shape=[
            jax.ShapeDtypeStruct((n, n_out), BF16),
            jax.ShapeDtypeStruct((n, LANES), F32),
        ],
        scratch_shapes=[pltpu.VMEM((1, LANES), F32), pltpu.VMEM((MXU_TILE, MXU_TILE), BF16)],
        compiler_params=_params("arbitrary"),
        name="proj_even",
    )(h, g, w, wf_hi, wf_lo, bf)


def _own_lanes(lane, h):
    return lane < HEAD_DIM if h == 0 else lane >= HEAD_DIM


def _causal_kernel(*refs, fox, tq, tk):
    if fox:
        q_ref, k_ref, v_ref, c_ref, o_ref, vaug_t, s_even, s_odd, kaug = refs
    else:
        q_ref, k_ref, v_ref, o_ref, vaug_t, s_even, s_odd = refs
    pair = pl.program_id(1)
    qi = pl.program_id(2)
    seq = v_ref.shape[1]

    @pl.when(qi == 0)
    def _():
        lane = lax.broadcasted_iota(jnp.int32, (tk, LANES), 1)
        feat = lax.broadcasted_iota(jnp.int32, (LANES, tk), 0)

        def fill(t, carry):
            rows = pl.ds(pl.multiple_of(t * tk, tk), tk)
            vt = jnp.transpose(v_ref[0, rows, :].astype(F32))
            if fox:
                kp = k_ref[0, rows, :].astype(F32)
                c = c_ref[0, rows, :]
            for h in range(2):
                vaug_t[h, t] = jnp.where(_own_lanes(feat, h), vt, 1.0).astype(BF16)
                if fox:
                    ch = jnp.sum(jnp.where(lane == 2 * pair + h, c, 0.0), axis=-1, keepdims=True)
                    c1, c2, c3 = _split3(ch * (-LOG2E))
                    base = HEAD_DIM if h == 0 else 0
                    extra = jnp.where(lane == base, c1.astype(F32),
                                      jnp.where(lane == base + 1, c2.astype(F32),
                                                jnp.where(lane == base + 2, c3.astype(F32), 0.0)))
                    kaug[h, rows, :] = jnp.where(_own_lanes(lane, h), kp, extra).astype(BF16)
            return carry

        lax.fori_loop(0, seq // tk, fill, 0)

    lane_q = lax.broadcasted_iota(jnp.int32, (tq, LANES), 1)
    qs = []
    for h in range(2):
        if fox:
            base = HEAD_DIM if h == 0 else 0
            ones = jnp.where((lane_q >= base) & (lane_q < base + 3), 1.0, 0.0)
            qs.append(jnp.where(_own_lanes(lane_q, h), q_ref[0].astype(F32), ones).astype(BF16))
        else:
            qs.append(q_ref[0, :, h * LANES:(h + 1) * LANES])

    key = lax.broadcasted_iota(jnp.int32, (tk, tq), 0)
    qry = lax.broadcasted_iota(jnp.int32, (tk, tq), 1)

    def scores(j, buf):
        start = pl.multiple_of(j * tk, tk)
        for h in range(2):
            if fox:
                kj = kaug[h, pl.ds(start, tk), :]
            else:
                kj = k_ref[0, pl.ds(start, tk), h * LANES:(h + 1) * LANES]
            buf[h] = lax.dot_general(kj, qs[h], NT_DIMS, preferred_element_type=F32)

    def absorb(j, buf, carry, masked):
        new = []
        for h in range(2):
            m, acc = carry[h]
            s = buf[h]
            if masked:
                s = jnp.where(key <= qry, s, NEG)
            m_new = jnp.maximum(m, jnp.max(s, axis=0, keepdims=True))
            p = jnp.exp2(s - m_new)
            acc = jnp.exp2(m - m_new) * acc + _dot(vaug_t[h, j], p.astype(BF16))
            new.append((m_new, acc))
        return tuple(new)

    def finish(carry):
        (_, acc0), (_, acc1) = carry
        feat_q = lax.broadcasted_iota(jnp.int32, (LANES, tq), 0)
        out_t = jnp.where(feat_q < HEAD_DIM, acc0 / acc0[HEAD_DIM:HEAD_DIM + 1, :], acc1 / acc1[0:1, :])
        o_ref[0] = jnp.transpose(out_t).astype(o_ref.dtype)

    def pair_step(t, carry):
        scores(2 * t + 1, s_odd)
        carry = absorb(2 * t, s_even, carry, False)
        scores(2 * t + 2, s_even)
        return absorb(2 * t + 1, s_odd, carry, False)

    init = tuple((jnp.full((1, tq), NEG, F32), jnp.zeros((LANES, tq), F32)) for _ in range(2))
    scores(0, s_even)
    carry = lax.fori_loop(0, qi // 2, pair_step, init)

    @pl.when(qi % 2 == 0)
    def _():
        finish(absorb(qi, s_even, carry, True))

    @pl.when(qi % 2 == 1)
    def _():
        scores(qi, s_odd)
        finish(absorb(qi, s_odd, absorb(qi - 1, s_even, carry, False), True))


def _causal_attention(q_arr, k_arr, v_arr, c_arr, *, q_blk0, k_blk0, v_blk0, n_pairs):
    b, s, _ = q_arr.shape
    tq = ATTN_BLOCK
    fox = c_arr is not None
    qk_w = LANES if fox else 2 * LANES
    in_specs = [
        pl.BlockSpec((1, tq, qk_w), lambda bi, p, qi: (bi, qi, q_blk0 + p)),
        pl.BlockSpec((1, s, qk_w), lambda bi, p, qi: (bi, 0, k_blk0 + p)),
        pl.BlockSpec((1, s, LANES), lambda bi, p, qi: (bi, 0, v_blk0 + p)),
    ]
    args = [q_arr, k_arr, v_arr]
    scratch = [pltpu.VMEM((2, s // tq, LANES, tq), BF16), pltpu.VMEM((2, tq, tq), F32), pltpu.VMEM((2, tq, tq), F32)]
    if fox:
        in_specs.append(pl.BlockSpec((1, s, LANES), lambda bi, p, qi: (bi, 0, 0)))
        args.append(c_arr)
        scratch.append(pltpu.VMEM((2, s, LANES), BF16))
    return pl.pallas_call(
        functools.partial(_causal_kernel, fox=fox, tq=tq, tk=tq),
        grid=(b, n_pairs, s // tq),
        in_specs=in_specs,
        out_specs=pl.BlockSpec((1, tq, LANES), lambda bi, p, qi: (bi, qi, p)),
        out_shape=jax.ShapeDtypeStruct((b, s, n_pairs * LANES), BF16),
        scratch_shapes=scratch,
        compiler_params=_params("parallel", "parallel", "arbitrary"),
        name="causal_attention",
    )(*args)


DIL_PAD = BLK * max(d for _, d in DIL_BRANCHES)
DIL_UNROLL = 4


def _dilated_kernel(q_ref, k_ref, v_ref, bias_ref, o_ref, qf, kf, vf, acc_s, m_s, l_s, s_even, s_odd, *, seq):
    qf[...] = q_ref[0].astype(F32)
    kf[0:DIL_PAD, :] = jnp.zeros((DIL_PAD, LANES), F32)
    vf[0:DIL_PAD, :] = jnp.zeros((DIL_PAD, LANES), F32)
    kf[DIL_PAD:, :] = k_ref[0].astype(F32)
    vf[DIL_PAD:, :] = v_ref[0].astype(F32)
    head0 = lax.broadcasted_iota(jnp.int32, (BLK, LANES), 1) < HEAD_DIM
    head0_k = lax.broadcasted_iota(jnp.int32, (2 * BLK, LANES), 1) < HEAD_DIM
    ones0 = jnp.where(head0_k, 1.0, 0.0).astype(BF16)
    ones1 = jnp.where(head0_k, 0.0, 1.0).astype(BF16)

    def place(dil, u):
        bi = u // dil
        q_start = (u % dil) + (dil * BLK) * bi
        return bi, q_start, q_start + (DIL_PAD - dil * BLK)

    def scores(br, dil, g, buf):
        for i in range(DIL_UNROLL):
            bi, q_start, k_start = place(dil, g * DIL_UNROLL + i)
            q2 = qf[pl.ds(q_start, BLK, stride=dil), :]
            q_st = jnp.concatenate([jnp.where(head0, q2, 0.0), jnp.where(head0, 0.0, q2)], axis=0).astype(BF16)
            kb = kf[pl.ds(k_start, 2 * BLK, stride=dil), :].astype(BF16)
            s = lax.dot_general(q_st, kb, NT_DIMS, preferred_element_type=F32)
            buf[i] = s + bias_ref[br, 0, jnp.where(bi == 0, 1, 0)]

    def absorb(br, dil, g, buf, first):
        for i in range(DIL_UNROLL):
            _, q_start, k_start = place(dil, g * DIL_UNROLL + i)
            s = buf[i]
            m = jnp.max(s, axis=-1, keepdims=True)
            p = jnp.exp2(s - m).astype(BF16)
            p_cat = jnp.concatenate([p[:BLK], p[BLK:]], axis=1)
            v2 = vf[pl.ds(k_start, 2 * BLK, stride=dil), :]
            rhs = jnp.concatenate([
                jnp.concatenate([jnp.where(head0_k, v2, 0.0).astype(BF16), ones0], axis=1),
                jnp.concatenate([jnp.where(head0_k, 0.0, v2).astype(BF16), ones1], axis=1)], axis=0)
            acc2 = _dot(p_cat, rhs)
            acc_b, l_b = acc2[:, :LANES], acc2[:, LANES:]
            m_b = jnp.where(head0, jnp.broadcast_to(m[:BLK], (BLK, LANES)), jnp.broadcast_to(m[BLK:], (BLK, LANES)))
            idx = pl.ds(q_start, BLK, stride=dil)
            if first:
                m_s[idx, :] = m_b
                l_s[idx, :] = l_b
                acc_s[idx, :] = acc_b
                continue
            m_old = m_s[idx, :]
            m_new = jnp.maximum(m_old, m_b)
            a_old = jnp.exp2(m_old - m_new)
            a_b = jnp.exp2(m_b - m_new)
            m_s[idx, :] = m_new
            l_s[idx, :] = a_old * l_s[idx, :] + a_b * l_b
            acc_s[idx, :] = a_old * acc_s[idx, :] + a_b * acc_b

    n_groups = seq // (BLK * DIL_UNROLL)
    order = sorted(range(len(DIL_BRANCHES)), key=lambda i: -DIL_BRANCHES[i][1])
    for pos, br in enumerate(order):
        dil = DIL_BRANCHES[br][1]
        first = pos == 0

        def trip(t, carry, br=br, dil=dil, first=first):
            scores(br, dil, 2 * t + 1, s_odd)
            absorb(br, dil, 2 * t, s_even, first)
            scores(br, dil, 2 * t + 2, s_even)
            absorb(br, dil, 2 * t + 1, s_odd, first)
            return carry

        scores(br, dil, 0, s_even)
        lax.fori_loop(0, n_groups // 2 - 1, trip, 0)
        scores(br, dil, n_groups - 1, s_odd)
        absorb(br, dil, n_groups - 2, s_even, first)
        absorb(br, dil, n_groups - 1, s_odd, first)

    o_ref[0] = (acc_s[...] / l_s[...]).astype(o_ref.dtype)


def _dilated_attention(proj, bias, *, q_blk0, k_blk0, v_blk0):
    b, s, _ = proj.shape
    n_pairs = DIL_HEADS // 2
    assert (s // (BLK * DIL_UNROLL)) % 2 == 0
    blk = lambda off: pl.BlockSpec((1, s, LANES), lambda bi, p: (bi, 0, off + p))
    state = [pltpu.VMEM((s, LANES), F32) for _ in range(4)]
    padded = [pltpu.VMEM((DIL_PAD + s, LANES), F32) for _ in range(2)]
    score_bufs = [pltpu.VMEM((DIL_UNROLL, 2 * BLK, 2 * BLK), F32) for _ in range(2)]
    return pl.pallas_call(
        functools.partial(_dilated_kernel, seq=s),
        grid=(b, n_pairs),
        in_specs=[blk(q_blk0), blk(k_blk0), blk(v_blk0),
                  pl.BlockSpec((len(DIL_BRANCHES), 1, 2, 2 * BLK, 2 * BLK), lambda bi, p: (0, p, 0, 0, 0))],
        out_specs=pl.BlockSpec((1, s, LANES), lambda bi, p: (bi, 0, p)),
        out_shape=jax.ShapeDtypeStruct((b, s, n_pairs * LANES), BF16),
        scratch_shapes=[state[0], padded[0], padded[1], state[1], state[2], state[3]] + score_bufs,
        compiler_params=_params("parallel", "parallel"),
        name="dilated_attention",
    )(proj, proj, proj, bias)


def _swa_kernel(q_ref, k_ref, v_ref, bias_ref, sink_ref, o_ref, vaug_t, s_even, s_odd, *, tc):
    ci = pl.program_id(1)
    seq = k_ref.shape[1]
    grp = SWA_Q_HEADS // SWA_KV_HEADS
    n_blk = tc // BLK

    @pl.when(ci == 0)
    def _():
        feat = lax.broadcasted_iota(jnp.int32, (LANES, BLK), 0)

        def fill(t, carry):
            vt = jnp.transpose(v_ref[0, pl.ds(pl.multiple_of(t * BLK, BLK), BLK), :].astype(F32))
            for kv in range(SWA_KV_HEADS):
                vaug_t[kv, t] = jnp.where(_own_lanes(feat, kv), vt, 1.0).astype(BF16)
            return carry

        lax.fori_loop(0, seq // BLK, fill, 0)

    lane_q = lax.broadcasted_iota(jnp.int32, (BLK, LANES), 1)
    feat_o = lax.broadcasted_iota(jnp.int32, (LANES, grp * BLK), 0)

    def blocks(bb):
        own = ci * n_blk + bb
        return jnp.maximum(own - 1, 0), own

    def scores(bb, kv, buf):
        prev, own = blocks(bb)
        k2 = jnp.concatenate([k_ref[0, pl.ds(pl.multiple_of(prev * BLK, BLK), BLK), :],
                              k_ref[0, pl.ds(pl.multiple_of(own * BLK, BLK), BLK), :]], axis=0)
        q4 = jnp.concatenate(
            [jnp.where(_own_lanes(lane_q, kv), q_ref[0, bb * BLK:(bb + 1) * BLK, g * LANES:(g + 1) * LANES].astype(F32),
                       0.0) for g in range(grp)], axis=0).astype(BF16)
        s = lax.dot_general(k2, q4, NT_DIMS, preferred_element_type=F32)
        buf[...] = s + bias_ref[kv, jnp.where(own == 0, 1, 0)]

    def absorb(bb, kv, buf):
        prev, own = blocks(bb)
        s = buf[...]
        sink = sink_ref[kv]
        m = jnp.maximum(jnp.max(s, axis=0, keepdims=True), sink)
        p = jnp.exp2(s - m).astype(BF16)
        v2 = jnp.concatenate([vaug_t[kv, prev], vaug_t[kv, own]], axis=1)
        acc = _dot(v2, p)
        den = (acc[HEAD_DIM:HEAD_DIM + 1, :] if kv == 0 else acc[0:1, :]) + jnp.exp2(sink - m)
        return acc / den

    units = [(bb, kv) for bb in range(n_blk) for kv in range(SWA_KV_HEADS)]
    bufs = (s_even, s_odd)
    scores(*units[0], bufs[0])
    outs = {}
    for n, unit in enumerate(units):
        if n + 1 < len(units):
            scores(*units[n + 1], bufs[(n + 1) % 2])
        outs[unit] = absorb(*unit, bufs[n % 2])
    for bb in range(n_blk):
        out_t = jnp.where(feat_o < HEAD_DIM, outs[(bb, 0)], outs[(bb, 1)])
        for g in range(grp):
            o_ref[0, bb * BLK:(bb + 1) * BLK, g * LANES:(g + 1) * LANES] = jnp.transpose(
                out_t[:, g * BLK:(g + 1) * BLK]).astype(o_ref.dtype)


def _swa_attention(proj, bias, sink_row, *, q_blk0, k_blk0, v_blk0):
    b, s, _ = proj.shape
    tc = SWA_CHUNK
    qw = SWA_Q_HEADS * HEAD_DIM
    grp = SWA_Q_HEADS // SWA_KV_HEADS
    return pl.pallas_call(
        functools.partial(_swa_kernel, tc=tc),
        grid=(b, s // tc),
        in_specs=[
            pl.BlockSpec((1, tc, qw), lambda bi, ci: (bi, ci, q_blk0 * LANES // qw)),
            pl.BlockSpec((1, s, LANES), lambda bi, ci: (bi, 0, k_blk0)),
            pl.BlockSpec((1, s, LANES), lambda bi, ci: (bi, 0, v_blk0)),
            pl.BlockSpec(bias.shape, lambda bi, ci: (0, 0, 0, 0)),
            pl.BlockSpec(sink_row.shape, lambda bi, ci: (0, 0, 0)),
        ],
        out_specs=pl.BlockSpec((1, tc, qw), lambda bi, ci: (bi, ci, 0)),
        out_shape=jax.ShapeDtypeStruct((b, s, qw), BF16),
        scratch_shapes=[pltpu.VMEM((SWA_KV_HEADS, s // BLK, LANES, BLK), BF16),
                        pltpu.VMEM((2 * BLK, grp * BLK), F32), pltpu.VMEM((2 * BLK, grp * BLK), F32)],
        compiler_params=_params("parallel", "arbitrary"),
        name="swa_attention",
    )(proj, proj, proj, bias, sink_row)


def _rope128(x, a, bm, bp):
    return x * a + pltpu.roll(x, LANES - 16, axis=1) * bm + pltpu.roll(x, 16, axis=1) * bp


def _proj_odd_kernel(x_ref, g_ref, w_ref, qn_ref, kvn_ref, wuq_ref, wuk_ref, wuv_ref, a_ref, bm_ref, bp_ref,
                     q_out, k_out, v_out, swa_out):
    xb = _rms(x_ref[...], g_ref[...]).astype(BF16)
    o1 = MLA_Q_RANK + LANES
    o2 = o1 + MLA_KV_RANK
    cq_kpe = _dot(xb, w_ref[:, 0:o1])
    c_q, kpe = cq_kpe[:, :MLA_Q_RANK], cq_kpe[:, MLA_Q_RANK:]
    c_kv = _dot(xb, w_ref[:, o1:o2])
    swa_out[...] = _dot(xb, w_ref[:, o2:]).astype(swa_out.dtype)
    a, bm, bp = a_ref[...], bm_ref[...], bp_ref[...]
    kpe = _rope128(kpe, a, bm, bp)
    cqn = _rms(c_q, qn_ref[...]).astype(BF16)
    ckvn = _rms(c_kv, kvn_ref[...]).astype(BF16)
    v_out[...] = _dot(ckvn, wuv_ref[...]).astype(v_out.dtype)
    q_raw = _dot(cqn, wuq_ref[...])
    k_raw = _dot(ckvn, wuk_ref[...])
    for h in range(MLA_HEADS):
        sl = slice(h * LANES, (h + 1) * LANES)
        q_out[:, sl] = _rope128(q_raw[:, sl], a, bm, bp).astype(q_out.dtype)
        k_out[:, sl] = (k_raw[:, sl] + kpe).astype(k_out.dtype)


def _proj_odd(h, g, w, qn, kvn, wuq, wuk, wuv, rope_a, rope_bm, rope_bp, seq):
    n, d = h.shape
    tm = PROJ_ODD_TILE
    n_in = w.shape[1]
    n_swa = n_in - (MLA_Q_RANK + MLA_KV_RANK + LANES)
    tiles_per_seq = seq // tm
    full = lambda shape: pl.BlockSpec(shape, lambda i: (0, 0))
    rope = pl.BlockSpec((tm, LANES), lambda i: (i % tiles_per_seq, 0))
    return pl.pallas_call(
        _proj_odd_kernel,
        grid=(n // tm,),
        in_specs=[
            pl.BlockSpec((tm, d), lambda i: (i, 0)), full((1, d)), full(w.shape),
            full(qn.shape), full(kvn.shape), full(wuq.shape), full(wuk.shape), full(wuv.shape),
            rope, rope, rope,
        ],
        out_specs=[
            pl.BlockSpec((tm, MLA_HEADS * LANES), lambda i: (i, 0)),
            pl.BlockSpec((tm, MLA_HEADS * LANES), lambda i: (i, 0)),
            pl.BlockSpec((tm, MLA_HEADS * MLA_V), lambda i: (i, 0)),
            pl.BlockSpec((tm, n_swa), lambda i: (i, 0)),
        ],
        out_shape=[
            jax.ShapeDtypeStruct((n, MLA_HEADS * LANES), BF16),
            jax.ShapeDtypeStruct((n, MLA_HEADS * LANES), BF16),
            jax.ShapeDtypeStruct((n, MLA_HEADS * MLA_V), BF16),
            jax.ShapeDtypeStruct((n, n_swa), BF16),
        ],
        compiler_params=_params("parallel"),
        name="proj_odd",
    )(h, g, w, qn, kvn, wuq, wuk, wuv, rope_a, rope_bm, rope_bp)


def _out_router_kernel(h_ref, oa_ref, ob_ref, wo_ref, g_ref, wr_hi_ref, wr_lo_ref, br_ref,
                       h_out, hn_out, gates_out, counts_out, pos_col_out, pos_row_out, tri_ref, seen_ref):
    i = pl.program_id(0)
    tm = h_ref.shape[0]
    half = oa_ref.shape[1]

    @pl.when(i == 0)
    def _():
        ri = lax.broadcasted_iota(jnp.int32, (tm, tm), 0)
        ci = lax.broadcasted_iota(jnp.int32, (tm, tm), 1)
        tri_ref[...] = jnp.where(ci < ri, 1.0, 0.0).astype(BF16)

    @pl.when(i % (MOE_TILE // tm) == 0)
    def _():
        seen_ref[...] = jnp.zeros_like(seen_ref)

    lane = lax.broadcasted_iota(jnp.int32, (MXU_TILE, LANES), 1)
    big = jnp.int32(LANES)
    is_grp = (lane >= N_EXPERTS) & (lane < N_EXPERTS + N_GROUPS)
    routed_parts = []
    for r in range(0, tm, MXU_TILE):
        rows = slice(r, r + MXU_TILE)
        h = h_ref[rows, :] + _dot(oa_ref[rows, :], wo_ref[0:half, :]) + _dot(ob_ref[rows, :], wo_ref[half:, :])
        h_out[rows, :] = h
        hn = _rms(h, g_ref[...])
        hn_out[rows, :] = hn.astype(hn_out.dtype)
        z = _dot_hi(hn, wr_hi_ref[...], wr_lo_ref[...]) + br_ref[...]
        zg = jnp.where(is_grp, z, -jnp.inf)
        g_max = jnp.max(zg, axis=-1, keepdims=True)
        g_w = 1.0 / jnp.sum(jnp.exp(zg - g_max), axis=-1, keepdims=True)
        g_idx = jnp.min(jnp.where(zg == g_max, lane - N_EXPERTS, big), axis=-1, keepdims=True)
        in_grp = (lane < N_EXPERTS) & ((lane // EXPERTS_PER_GROUP) == g_idx)
        ze = jnp.where(in_grp, z, -jnp.inf)
        v1 = jnp.max(ze, axis=-1, keepdims=True)
        i1 = jnp.min(jnp.where(ze == v1, lane, big), axis=-1, keepdims=True)
        ze2 = jnp.where(lane == i1, -jnp.inf, ze)
        v2 = jnp.max(ze2, axis=-1, keepdims=True)
        i2 = jnp.min(jnp.where(ze2 == v2, lane, big), axis=-1, keepdims=True)
        e2 = jnp.exp(v2 - v1)
        w1 = g_w / (1.0 + e2)
        w2 = g_w * e2 / (1.0 + e2)
        gates_out[rows, :] = jnp.where(lane == i1, w1, 0.0) + jnp.where(lane == i2, w2, 0.0)
        routed_parts.append(jnp.where((lane == g_idx) & (lane < N_GROUPS), 1.0, 0.0))

    routed = jnp.concatenate(routed_parts, axis=0)
    pos = jnp.where(routed > 0.0, _dot(tri_ref[...], routed.astype(BF16)) + seen_ref[...], -1.0)
    pos_col_out[...] = pos
    pos_row_out[0] = jnp.transpose(pos)[0:8, :]
    here = jnp.sum(routed, axis=0, keepdims=True)
    seen_ref[...] += here
    counts_out[0] = jnp.broadcast_to(here, counts_out.shape[1:])


def _out_router(h, oa, ob, wo, g, wr_hi, wr_lo, br):
    n, d = h.shape
    tm = ROUTER_TILE
    per_tile = MOE_TILE // tm
    half = oa.shape[1]
    full = lambda shape: pl.BlockSpec(shape, lambda i: (0, 0))
    tile = lambda w: pl.BlockSpec((tm, w), lambda i: (i, 0))
    return pl.pallas_call(
        _out_router_kernel,
        grid=(n // tm,),
        in_specs=[tile(d), tile(half), tile(half), full(wo.shape), full((1, d)),
                  full(wr_hi.shape), full(wr_lo.shape), full((1, LANES))],
        out_specs=[tile(d), tile(d), tile(LANES), pl.BlockSpec((1, 8, LANES), lambda i: (i, 0, 0)),
                   tile(LANES), pl.BlockSpec((1, 8, tm), lambda i: (i // per_tile, 0, i % per_tile))],
        out_shape=[
            jax.ShapeDtypeStruct((n, d), F32),
            jax.ShapeDtypeStruct((n, d), BF16),
            jax.ShapeDtypeStruct((n, LANES), F32),
            jax.ShapeDtypeStruct((n // tm, 8, LANES), F32),
            jax.ShapeDtypeStruct((n, LANES), F32),
            jax.ShapeDtypeStruct((n // MOE_TILE, 8, MOE_TILE), F32),
        ],
        scratch_shapes=[pltpu.VMEM((tm, tm), BF16), pltpu.VMEM((1, LANES), F32)],
        compiler_params=_params("arbitrary"),
        name="out_router",
    )(h, oa, ob, wo, g, wr_hi, wr_lo, br)


MOE_CHUNK = 320
MOE_ROWS = 1024
MOE_ALIGN = 16
MOE_EXPERTS_PER_STEP = 4


def _moe_sorted_tiles(n):
    n_seg = (n // MOE_TILE) * N_GROUPS
    rows = n + n_seg * (MOE_ALIGN - 1) + N_GROUPS * (MOE_CHUNK + MOE_ROWS - 1)
    return -(-rows // MOE_ROWS)


def _moe_tables(counts, n):
    n_rt = _moe_sorted_tiles(n)
    seg_len = (counts + (MOE_ALIGN - 1)) // MOE_ALIGN * MOE_ALIGN
    group_len = seg_len.sum(axis=0)
    group_span = (group_len + MOE_CHUNK + MOE_ROWS - 1) // MOE_ROWS * MOE_ROWS
    group_end = jnp.cumsum(group_span)
    group_start = group_end - group_span
    seg_off = group_start[None, :] + jnp.cumsum(seg_len, axis=0) - seg_len
    rt_start = jnp.arange(n_rt, dtype=jnp.int32) * MOE_ROWS
    rt_group = jnp.minimum(jnp.sum(rt_start[:, None] >= group_end[None, :], axis=1), N_GROUPS - 1)
    rt_valid = jnp.clip(group_start[rt_group] + group_len[rt_group] - rt_start, 0, MOE_ROWS)
    n_used = (group_end[-1] // MOE_ROWS).reshape(1)
    i32 = lambda a: a.astype(jnp.int32)
    return i32(seg_off.reshape(-1)), i32(rt_group), i32(rt_valid), i32(n_used)


def _moe_pack_kernel(counts_ref, off_ref, hn_ref, gates_ref, pos_ref, xs_in, xs_ref, xbuf, sem):
    del xs_in
    i = pl.program_id(0)
    d = hn_ref.shape[1]

    def write(grp, dst):
        return pltpu.make_async_copy(xbuf.at[grp], xs_ref.at[pl.ds(dst, MOE_CHUNK)], sem.at[grp])

    @pl.when(i > 0)
    def _():
        for grp in range(N_GROUPS):
            write(grp, 0).wait()

    lane = lax.broadcasted_iota(jnp.int32, (hn_ref.shape[0], LANES), 1)
    gates = gates_ref[...]
    gates_hi = gates.astype(BF16).astype(F32)
    gates_lo = gates - gates_hi
    chunks = []
    for grp in range(N_GROUPS):
        seg = i * N_GROUPS + grp
        first = grp * EXPERTS_PER_GROUP
        g_hi = gates_hi if first == 0 else pltpu.roll(gates_hi, LANES - first, axis=1)
        g_lo = pltpu.roll(gates_lo, (LANES - first + EXPERTS_PER_GROUP) % LANES, axis=1)
        gate_cols = jnp.where(lane < EXPERTS_PER_GROUP, g_hi, jnp.where(lane < 2 * EXPERTS_PER_GROUP, g_lo, 0.0))
        gate_cols = gate_cols.astype(BF16)
        pos_row = pos_ref[0, grp:grp + 1, :]

        def chunk(first_pos, grp=grp, seg=seg, gate_cols=gate_cols, pos_row=pos_row):
            slot_pos = lax.broadcasted_iota(jnp.int32, (MOE_CHUNK, 1), 0).astype(F32) + first_pos
            gather = jnp.where(pos_row == slot_pos, 1.0, 0.0).astype(BF16)
            xbuf[grp, :, 0:d] = _dot(gather, hn_ref[...]).astype(BF16)
            xbuf[grp, :, d:] = _dot(gather, gate_cols).astype(BF16)
            dst = off_ref[seg] + first_pos.astype(jnp.int32)
            write(grp, pl.multiple_of(dst, MOE_ALIGN)).start()

        chunk(jnp.float32(0.0))
        chunks.append(chunk)

    for grp in range(N_GROUPS):
        def more(k, carry, grp=grp):
            write(grp, 0).wait()
            chunks[grp]((k * MOE_CHUNK).astype(F32))
            return carry

        count = counts_ref[i * N_GROUPS + grp]
        lax.fori_loop(1, (count + (MOE_CHUNK - 1)) // MOE_CHUNK, more, 0)

    @pl.when(i == pl.num_programs(0) - 1)
    def _():
        for grp in range(N_GROUPS):
            write(grp, 0).wait()


def _moe_pack(counts, seg_off, hn, gates, pos_row, xs_init):
    n, d = hn.shape
    rows = xs_init.shape[0]
    grid_spec = pltpu.PrefetchScalarGridSpec(
        num_scalar_prefetch=2,
        grid=(n // MOE_TILE,),
        in_specs=[
            pl.BlockSpec((MOE_TILE, d), lambda i, c, o: (i, 0)),
            pl.BlockSpec((MOE_TILE, LANES), lambda i, c, o: (i, 0)),
            pl.BlockSpec((1, 8, MOE_TILE), lambda i, c, o: (i, 0, 0)),
            pl.BlockSpec(memory_space=pl.ANY),
        ],
        out_specs=pl.BlockSpec(memory_space=pl.ANY),
        scratch_shapes=[pltpu.VMEM((N_GROUPS, MOE_CHUNK, d + LANES), BF16), pltpu.SemaphoreType.DMA((N_GROUPS,))],
    )
    return pl.pallas_call(
        _moe_pack_kernel,
        grid_spec=grid_spec,
        out_shape=jax.ShapeDtypeStruct((rows, d + LANES), BF16),
        input_output_aliases={5: 0},
        compiler_params=_params("arbitrary"),
        name="moe_pack",
    )(counts, seg_off, hn, gates, pos_row, xs_init)


def _moe_experts_kernel(grp_ref, valid_ref, used_ref, xs_ref, wg_ref, wu_ref, wd_ref, ys_ref, acc_ref,
                        wg_b, wu_b, wd_b):
    del grp_ref, used_ref
    rt = pl.program_id(0)
    j = pl.program_id(1)
    d = ys_ref.shape[1]
    valid = valid_ref[rt]

    @pl.when((valid > 0) & (j == 0))
    def _():
        acc_ref[...] = jnp.zeros_like(acc_ref)

    def experts(x, gate_cols, w_gate, w_up, w_down):
        lane = lax.broadcasted_iota(jnp.int32, gate_cols.shape, 1)
        acts = []
        for e in range(MOE_EXPERTS_PER_STEP):
            idx = j * MOE_EXPERTS_PER_STEP + e
            mine = (lane == idx) | (lane == idx + EXPERTS_PER_GROUP)
            gate = jnp.sum(jnp.where(mine, gate_cols, 0.0), axis=-1, keepdims=True)
            a = _dot(x, w_gate(e))
            u = _dot(x, w_up(e))
            acts.append(((a * jax.nn.sigmoid(a)) * u * gate).astype(BF16))
        return _dot(jnp.concatenate(acts, axis=1), w_down())

    @pl.when(valid == MOE_ROWS)
    def _():
        acc_ref[...] += experts(
            xs_ref[:, 0:d], xs_ref[:, d:].astype(F32), lambda e: wg_ref[0, 0, e].astype(BF16),
            lambda e: wu_ref[0, 0, e].astype(BF16), lambda: wd_ref[0, 0].astype(BF16).reshape(wd_b.shape))

    @pl.when((valid > 0) & (valid < MOE_ROWS))
    def _():
        wg_b[...] = wg_ref[0, 0].astype(BF16)
        wu_b[...] = wu_ref[0, 0].astype(BF16)
        wd_b[...] = wd_ref[0, 0].astype(BF16).reshape(wd_b.shape)

        def rows_block(t, carry):
            rows = pl.ds(pl.multiple_of(t * MXU_TILE, MXU_TILE), MXU_TILE)
            acc_ref[rows, :] += experts(xs_ref[rows, 0:d], xs_ref[rows, d:].astype(F32), lambda e: wg_b[e],
                                        lambda e: wu_b[e], lambda: wd_b[...])
            return carry

        lax.fori_loop(0, (valid + (MXU_TILE - 1)) // MXU_TILE, rows_block, 0)

    @pl.when((valid > 0) & (j == pl.num_programs(1) - 1))
    def _():
        row = lax.broadcasted_iota(jnp.int32, (acc_ref.shape[0], 1), 0)
        ys_ref[...] = jnp.where(row < valid, acc_ref[...], 0.0).astype(ys_ref.dtype)

    @pl.when((valid == 0) & (j == pl.num_programs(1) - 1))
    def _():
        ys_ref[...] = jnp.zeros_like(ys_ref)


def _moe_experts(rt_group, rt_valid, n_used, xs, w_gate, w_up, w_down, layer):
    rows = xs.shape[0]
    d = w_gate.shape[-2]
    n_rt = rows // MOE_ROWS
    eps = MOE_EXPERTS_PER_STEP
    n_steps = EXPERTS_PER_GROUP // eps

    def tile_idx(rt, j, grp, valid, used):
        return jnp.minimum(rt, used[0] - 1)

    def w_idx(rt, j, grp, valid, used):
        live = rt < used[0]
        return (layer, grp[tile_idx(rt, j, grp, valid, used)], jnp.where(live, j, n_steps - 1), 0, 0)

    grid_spec = pltpu.PrefetchScalarGridSpec(
        num_scalar_prefetch=3,
        grid=(n_rt, n_steps),
        in_specs=[
            pl.BlockSpec((MOE_ROWS, d + LANES), lambda rt, j, grp, valid, used: (tile_idx(rt, j, grp, valid, used), 0)),
            pl.BlockSpec((1, 1, eps, d, D_EXPERT), w_idx),
            pl.BlockSpec((1, 1, eps, d, D_EXPERT), w_idx),
            pl.BlockSpec((1, 1, eps, D_EXPERT, d), w_idx),
        ],
        out_specs=pl.BlockSpec((MOE_ROWS, d), lambda rt, j, grp, valid, used: (rt, 0)),
        scratch_shapes=[pltpu.VMEM((MOE_ROWS, d), F32), pltpu.VMEM((eps, d, D_EXPERT), BF16),
                        pltpu.VMEM((eps, d, D_EXPERT), BF16), pltpu.VMEM((eps * D_EXPERT, d), BF16)],
    )
    return pl.pallas_call(
        _moe_experts_kernel,
        grid_spec=grid_spec,
        out_shape=jax.ShapeDtypeStruct((rows, d), BF16),
        compiler_params=_params("arbitrary", "arbitrary"),
        name="moe_experts",
    )(rt_group, rt_valid, n_used, xs, w_gate, w_up, w_down)


def _moe_combine_kernel(counts_ref, off_ref, pos_ref, h_ref, fg_ref, ys_ref, o_ref, ybuf, ymore, sem, *, final_norm):
    i = pl.program_id(0)

    def read(tile, grp):
        src = pl.multiple_of(off_ref[tile * N_GROUPS + grp], MOE_ALIGN)
        slot = tile % 2
        return pltpu.make_async_copy(ys_ref.at[pl.ds(src, MOE_CHUNK)],
                                     ybuf.at[slot, pl.ds(grp * MOE_CHUNK, MOE_CHUNK)], sem.at[slot, grp])

    def fetch(tile):
        for grp in range(N_GROUPS):
            @pl.when(counts_ref[tile * N_GROUPS + grp] > 0)
            def _(grp=grp):
                read(tile, grp).start()

    @pl.when(i == 0)
    def _():
        ybuf[...] = jnp.zeros_like(ybuf)
        fetch(i)

    @pl.when(i + 1 < pl.num_programs(0))
    def _():
        fetch(i + 1)

    for grp in range(N_GROUPS):
        @pl.when(counts_ref[i * N_GROUPS + grp] > 0)
        def _(grp=grp):
            read(i, grp).wait()

    slot_pos = lax.broadcasted_iota(jnp.int32, (1, MOE_CHUNK), 1).astype(F32)
    for r in range(0, MOE_TILE, MOE_TILE // 2):
        rows = slice(r, r + MOE_TILE // 2)
        scatter = jnp.concatenate(
            [jnp.where(pos_ref[rows, grp:grp + 1] == slot_pos, 1.0, 0.0).astype(BF16) for grp in range(N_GROUPS)],
            axis=1)
        o_ref[rows, :] = h_ref[rows, :] + _dot(scatter, ybuf[i % 2])
    for grp in range(N_GROUPS):
        count = counts_ref[i * N_GROUPS + grp]
        pos_col = pos_ref[:, grp:grp + 1]

        def more(k, carry, grp=grp, pos_col=pos_col):
            src = pl.multiple_of(off_ref[i * N_GROUPS + grp] + k * MOE_CHUNK, MOE_ALIGN)
            pltpu.sync_copy(ys_ref.at[pl.ds(src, MOE_CHUNK)], ymore)
            scatter = jnp.where(pos_col == slot_pos + (k * MOE_CHUNK).astype(F32), 1.0, 0.0).astype(BF16)
            o_ref[...] += _dot(scatter, ymore[...])
            return carry

        lax.fori_loop(1, (count + (MOE_CHUNK - 1)) // MOE_CHUNK, more, 0)

    if final_norm:
        o_ref[...] = _rms(o_ref[...], fg_ref[...])


def _moe_combine(counts, seg_off, pos_col, h, fg, ys, final_norm):
    n, d = h.shape
    grid_spec = pltpu.PrefetchScalarGridSpec(
        num_scalar_prefetch=2,
        grid=(n // MOE_TILE,),
        in_specs=[
            pl.BlockSpec((MOE_TILE, LANES), lambda i, c, o: (i, 0)),
            pl.BlockSpec((MOE_TILE, d), lambda i, c, o: (i, 0)),
            pl.BlockSpec((1, d), lambda i, c, o: (0, 0)),
            pl.BlockSpec(memory_space=pl.ANY),
        ],
        out_specs=pl.BlockSpec((MOE_TILE, d), lambda i, c, o: (i, 0)),
        scratch_shapes=[pltpu.VMEM((2, N_GROUPS * MOE_CHUNK, d), BF16), pltpu.VMEM((MOE_CHUNK, d), BF16),
                        pltpu.SemaphoreType.DMA((2, N_GROUPS))],
    )
    return pl.pallas_call(
        functools.partial(_moe_combine_kernel, final_norm=final_norm),
        grid_spec=grid_spec,
        out_shape=jax.ShapeDtypeStruct((n, d), F32),
        compiler_params=_params("arbitrary"),
        name="moe_combine",
    )(counts, seg_off, pos_col, h, fg, ys)


def _alibi_slopes(n):
    return np.exp2(-8.0 * np.arange(1, n + 1, dtype=np.float64) / n)


def _band_bias(slopes, max_steps, step_dist):
    steps = (np.arange(BLK)[:, None] + BLK) - np.arange(2 * BLK)[None, :]
    in_band = (steps >= 0) & (steps <= max_steps)
    dist = (steps * step_dist).astype(np.float64)
    return np.where(in_band[None], -slopes[:, None, None] * dist[None], NEG)


def _attention_bias_tables():
    own_half = np.arange(2 * BLK) >= BLK
    dil = np.stack([_band_bias(_alibi_slopes(DIL_HEADS), w // dl, dl) for (w, dl) in DIL_BRANCHES]) * LOG2E
    dil = np.stack([dil, np.where(own_half, dil, NEG)], axis=2)
    dil = dil.reshape(len(DIL_BRANCHES), DIL_HEADS // 2, 2, 2, BLK, 2 * BLK)
    dil = dil.transpose(0, 1, 3, 2, 4, 5).reshape(len(DIL_BRANCHES), DIL_HEADS // 2, 2, 2 * BLK, 2 * BLK)
    grp = SWA_Q_HEADS // SWA_KV_HEADS
    swa = _band_bias(_alibi_slopes(SWA_Q_HEADS), SWA_WINDOW - 1, 1) * LOG2E
    swa = swa.reshape(SWA_KV_HEADS, grp, BLK, 2 * BLK).transpose(0, 3, 1, 2).reshape(SWA_KV_HEADS, 2 * BLK, grp * BLK)
    swa = np.stack([swa, np.where(own_half[None, :, None], swa, NEG)], axis=1)
    return dil.astype(np.float32), swa.astype(np.float32)


def _rope_tables(s):
    inv = ROPE_THETA ** (-np.arange(0, MLA_ROPE, 2, dtype=np.float64) / MLA_ROPE)
    ang = np.arange(s, dtype=np.float64)[:, None] * inv[None, :]
    cos, sin = np.cos(ang), np.sin(ang)
    half = MLA_ROPE // 2
    zeros_tail = np.zeros((s, LANES - MLA_NOPE - MLA_ROPE))
    a = np.concatenate([np.ones((s, MLA_NOPE)), cos, cos, zeros_tail], axis=1)
    zeros_nope = np.zeros((s, MLA_NOPE))
    zeros_half = np.zeros((s, half))
    bm = np.concatenate([zeros_nope, -sin, zeros_half, zeros_tail], axis=1)
    bp = np.concatenate([zeros_nope, zeros_half, sin, zeros_tail], axis=1)
    return a.astype(np.float32), bm.astype(np.float32), bp.astype(np.float32)


def _swa_head_order(w, axis):
    grp = SWA_Q_HEADS // SWA_KV_HEADS
    shape = w.shape
    w = w.reshape(shape[:axis] + (SWA_KV_HEADS, grp, HEAD_DIM) + shape[axis + 1:])
    return jnp.swapaxes(w, axis, axis + 1).reshape(shape)


def _pad_cols(w, width):
    return jnp.pad(w, ((0, 0), (0, width - w.shape[1])))


def _router_weights(w_group, b_group, w_router, b_router):
    w = _pad_cols(jnp.concatenate([w_router, w_group], axis=1), LANES)
    b = _pad_cols(jnp.concatenate([b_router, b_group])[None, :], LANES)
    hi, lo = _hi_lo(w)
    return hi, lo, b


def kernel(x, attn_norm, ffn_norm, final_norm, e_w_in, e_b_f, e_w_out, o_w_in, o_q_norm, o_kv_norm, o_w_uq,
           o_w_ukv, o_sinks, o_w_out, moe_w_group, moe_b_group, moe_w_router, moe_b_router, moe_w_gate,
           moe_w_up, moe_w_down):
    b, s, d = x.shape
    n = b * s
    depth = attn_norm.shape[0]
    assert s % (BLK * DIL_BRANCHES[-1][1]) == 0 and d == D_MODEL
    assert all(s % t == 0 for t in (PROJ_EVEN_TILE, PROJ_ODD_TILE, ATTN_BLOCK, SWA_CHUNK, MOE_TILE))
    assert MOE_TILE % ROUTER_TILE == 0
    h = x.reshape(n, d)

    dil_bias, swa_bias = _attention_bias_tables()
    rope_a, rope_bm, rope_bp = _rope_tables(s)

    xs = jnp.zeros((_moe_sorted_tiles(n) * MOE_ROWS, d + LANES), BF16)
    for layer in range(depth):
        i = layer // 2
        g_attn = attn_norm[layer][None, :]
        if layer % 2 == 0:
            w_in = e_w_in[i]
            hq = FOX_HEADS * HEAD_DIM
            scale = HEAD_DIM ** -0.5
            cols = [w_in[:, 0:hq] * (scale * LOG2E), w_in[:, hq:2 * hq], w_in[:, 2 * hq:3 * hq]]
            o = 3 * hq + FOX_HEADS
            cols += [w_in[:, o:o + hq] * (scale * LOG2E), w_in[:, o + hq:o + 2 * hq], w_in[:, o + 2 * hq:o + 3 * hq]]
            w_main = jnp.concatenate(cols, axis=1).astype(BF16)
            wf_hi, wf_lo = _hi_lo(_pad_cols(w_in[:, 3 * hq:o], LANES))
            b_f = _pad_cols(e_b_f[i][None, :], LANES)
            proj, c = _proj_even(h, g_attn, w_main, wf_hi, wf_lo, b_f, s)
            proj = proj.reshape(b, s, -1)
            o_a = _causal_attention(proj, proj, proj, c.reshape(b, s, LANES), q_blk0=0, k_blk0=4, v_blk0=8,
                                    n_pairs=FOX_HEADS // 2)
            o_b = _dilated_attention(proj, dil_bias, q_blk0=12, k_blk0=16, v_blk0=20)
            w_out = e_w_out[i].astype(BF16)
        else:
            w_in = o_w_in[i]
            o1 = MLA_Q_RANK + MLA_KV_RANK
            o2 = o1 + MLA_ROPE
            sq = SWA_Q_HEADS * HEAD_DIM
            kpe_cols = jnp.pad(w_in[:, o1:o2], ((0, 0), (MLA_NOPE, LANES - MLA_NOPE - MLA_ROPE)))
            w_main = jnp.concatenate(
                [w_in[:, :MLA_Q_RANK], kpe_cols, w_in[:, MLA_Q_RANK:o1],
                 _swa_head_order(w_in[:, o2:o2 + sq] * (HEAD_DIM ** -0.5 * LOG2E), axis=1),
                 w_in[:, o2 + sq:]],
                axis=1).astype(BF16)
            dq = MLA_NOPE + MLA_ROPE
            wuq = o_w_uq[i].reshape(MLA_Q_RANK, MLA_HEADS, dq) * (dq ** -0.5 * LOG2E)
            wuq = jnp.pad(wuq, ((0, 0), (0, 0), (0, LANES - dq))).reshape(MLA_Q_RANK, MLA_HEADS * LANES)
            wukv = o_w_ukv[i].reshape(MLA_KV_RANK, MLA_HEADS, MLA_NOPE + MLA_V)
            wuk = jnp.pad(wukv[:, :, :MLA_NOPE], ((0, 0), (0, 0), (0, LANES - MLA_NOPE)))
            wuk = wuk.reshape(MLA_KV_RANK, MLA_HEADS * LANES)
            wuv = wukv[:, :, MLA_NOPE:].reshape(MLA_KV_RANK, MLA_HEADS * MLA_V)
            q_full, k_full, v_mla, swa = _proj_odd(
                h, g_attn, w_main, o_q_norm[i][None, :], o_kv_norm[i][None, :], wuq.astype(BF16),
                wuk.astype(BF16), wuv.astype(BF16), rope_a, rope_bm, rope_bp, s)
            o_a = _causal_attention(q_full.reshape(b, s, -1), k_full.reshape(b, s, -1), v_mla.reshape(b, s, -1),
                                    None, q_blk0=0, k_blk0=0, v_blk0=0, n_pairs=MLA_HEADS // 2)
            grp = SWA_Q_HEADS // SWA_KV_HEADS
            sink_row = jnp.repeat(o_sinks[i].reshape(SWA_KV_HEADS, grp) * LOG2E, BLK, axis=1)[:, None, :]
            o_b = _swa_attention(swa.reshape(b, s, -1), swa_bias, sink_row, q_blk0=0, k_blk0=4, v_blk0=5)
            half = MLA_HEADS * MLA_V
            w_out = jnp.concatenate([o_w_out[i][:half], _swa_head_order(o_w_out[i][half:], axis=0)], axis=0)
            w_out = w_out.astype(BF16)

        wr_hi, wr_lo, b_r = _router_weights(moe_w_group[layer], moe_b_group[layer], moe_w_router[layer],
                                            moe_b_router[layer])
        h, hn, gates, counts, pos_col, pos_row = _out_router(
            h, o_a.reshape(n, -1), o_b.reshape(n, -1), w_out, ffn_norm[layer][None, :], wr_hi, wr_lo, b_r)
        counts = counts[:, 0, :N_GROUPS].reshape(n // MOE_TILE, MOE_TILE // ROUTER_TILE, N_GROUPS).sum(axis=1)
        counts = counts.astype(jnp.int32)
        seg_off, rt_group, rt_valid, n_used = _moe_tables(counts, n)
        counts = counts.reshape(-1)
        xs = _moe_pack(counts, seg_off, hn, gates, pos_row, xs)
        ys = _moe_experts(rt_group, rt_valid, n_used, xs, moe_w_gate, moe_w_up, moe_w_down, layer)
        h = _moe_combine(counts, seg_off, pos_col, h, final_norm[None, :], ys, final_norm=layer == depth - 1)
    return h.reshape(b, s, d)
```

```python
import functools

import numpy as np
import jax
import jax.numpy as jnp
from jax import lax
from jax.experimental import pallas as pl
from jax.experimental.pallas import tpu as pltpu

F32 = jnp.float32
BF16 = jnp.bfloat16

D_MODEL = 1024
HEAD_DIM = 64
BLK = 128
NEG = -1e30
RMS_EPS = 1e-6
FOX_HEADS = 8
DIL_HEADS = 8
DIL_BRANCHES = ((128, 1), (512, 4), (2048, 16))
MLA_HEADS = 8
MLA_Q_RANK = 384
MLA_KV_RANK = 256
MLA_NOPE = 64
MLA_ROPE = 32
MLA_V = 64
ROPE_THETA = 10000.0
SWA_Q_HEADS = 8
SWA_KV_HEADS = 2
SWA_WINDOW = 128
N_GROUPS = 4
EXPERTS_PER_GROUP = 8
N_EXPERTS = N_GROUPS * EXPERTS_PER_GROUP
D_EXPERT = 256

LANES = 128
MXU_TILE = 256

PROJ_EVEN_TILE = 512
PROJ_ODD_TILE = 1024
PROJ_COLS = 2 * MXU_TILE
ATTN_BLOCK = 512
SWA_CHUNK = 512
ROUTER_TILE = 512
MOE_TILE = 1024
VMEM_LIMIT = 56 * 1024 * 1024

NT_DIMS = (((1,), (1,)), ((), ()))
LOG2E = 1.4426950408889634


def _params(*sem):
    return pltpu.CompilerParams(dimension_semantics=sem, vmem_limit_bytes=VMEM_LIMIT)


def _dot(a, b):
    return jnp.dot(a, b, preferred_element_type=F32)


def _rms(x, g):
    return x * lax.rsqrt(jnp.mean(x * x, axis=-1, keepdims=True) + RMS_EPS) * g


def _hi_lo(w):
    hi = w.astype(BF16)
    return hi, (w - hi.astype(F32)).astype(BF16)


def _dot_hi(x, w_hi, w_lo):
    x_hi = x.astype(BF16)
    x_lo = (x - x_hi.astype(F32)).astype(BF16)
    n = w_hi.shape[1]
    both = _dot(x_hi, jnp.concatenate([w_hi, w_lo], axis=1))
    return both[:, :n] + (both[:, n:] + _dot(x_lo, w_hi))


def _split3(x):
    x1 = x.astype(BF16)
    r1 = x - x1.astype(F32)
    x2 = r1.astype(BF16)
    x3 = (r1 - x2.astype(F32)).astype(BF16)
    return x1, x2, x3


def _proj_even_kernel(x_ref, g_ref, w_ref, wf_hi_ref, wf_lo_ref, bf_ref, out_ref, c_ref, carry_ref, tri_ref, *,
                      tiles_per_seq):
    xn = _rms(x_ref[...], g_ref[...])
    xb = xn.astype(BF16)
    n_out = out_ref.shape[1]
    for c in range(0, n_out, PROJ_COLS):
        out_ref[:, c:c + PROJ_COLS] = _dot(xb, w_ref[:, c:c + PROJ_COLS]).astype(out_ref.dtype)
    z = _dot_hi(xn, wf_hi_ref[...], wf_lo_ref[...]) + bf_ref[...]
    logf = jnp.minimum(z, 0.0) - jnp.log1p(jnp.exp(-jnp.abs(z)))

    @pl.when(pl.program_id(0) % tiles_per_seq == 0)
    def _():
        carry_ref[...] = jnp.zeros_like(carry_ref)

    sub = tri_ref.shape[0]

    @pl.when(pl.program_id(0) == 0)
    def _():
        ri = lax.broadcasted_iota(jnp.int32, (sub, sub), 0)
        ci = lax.broadcasted_iota(jnp.int32, (sub, sub), 1)
        tri_ref[...] = jnp.where(ci <= ri, 1.0, 0.0).astype(BF16)

    lower = tri_ref[...]
    carry = carry_ref[...]
    for r in range(0, logf.shape[0], sub):
        l1, l2, l3 = _split3(logf[r:r + sub])
        c = (_dot(lower, l1) + (_dot(lower, l2) + _dot(lower, l3))) + carry
        c_ref[r:r + sub, :] = c
        carry = c[sub - 1:sub, :]
    carry_ref[...] = carry


def _proj_even(h, g, w, wf_hi, wf_lo, bf, seq):
    n, d = h.shape
    tm = PROJ_EVEN_TILE
    n_out = w.shape[1]
    return pl.pallas_call(
        functools.partial(_proj_even_kernel, tiles_per_seq=seq // tm),
        grid=(n // tm,),
        in_specs=[
            pl.BlockSpec((tm, d), lambda i: (i, 0)),
            pl.BlockSpec((1, d), lambda i: (0, 0)),
            pl.BlockSpec((d, n_out), lambda i: (0, 0)),
            pl.BlockSpec((d, LANES), lambda i: (0, 0)),
            pl.BlockSpec((d, LANES), lambda i: (0, 0)),
            pl.BlockSpec((1, LANES), lambda i: (0, 0)),
        ],
        out_specs=[
            pl.BlockSpec((tm, n_out), lambda i: (i, 0)),
            pl.BlockSpec((tm, LANES), lambda i: (i, 0)),
        ],
        out_shape=[
            jax.ShapeDtypeStruct((n, n_out), BF16),
            jax.ShapeDtypeStruct((n, LANES), F32),
        ],
        scratch_shapes=[pltpu.VMEM((1, LANES), F32), pltpu.VMEM((MXU_TILE, MXU_TILE), BF16)],
        compiler_params=_params("arbitrary"),
        name="proj_even",
    )(h, g, w, wf_hi, wf_lo, bf)


def _own_lanes(lane, h):
    return lane < HEAD_DIM if h == 0 else lane >= HEAD_DIM


def _causal_kernel(*refs, fox, tq, tk):
    if fox:
        q_ref, k_ref, v_ref, c_ref, o_ref, vaug_t, s_even, s_odd, kaug = refs
    else:
        q_ref, k_ref, v_ref, o_ref, vaug_t, s_even, s_odd = refs
    pair = pl.program_id(1)
    qi = pl.program_id(2)
    seq = v_ref.shape[1]

    @pl.when(qi == 0)
    def _():
        lane = lax.broadcasted_iota(jnp.int32, (tk, LANES), 1)
        feat = lax.broadcasted_iota(jnp.int32, (LANES, tk), 0)

        def fill(t, carry):
            rows = pl.ds(pl.multiple_of(t * tk, tk), tk)
            vt = jnp.transpose(v_ref[0, rows, :].astype(F32))
            if fox:
                kp = k_ref[0, rows, :].astype(F32)
                c = c_ref[0, rows, :]
            for h in range(2):
                vaug_t[h, t] = jnp.where(_own_lanes(feat, h), vt, 1.0).astype(BF16)
                if fox:
                    ch = jnp.sum(jnp.where(lane == 2 * pair + h, c, 0.0), axis=-1, keepdims=True)
                    c1, c2, c3 = _split3(ch * (-LOG2E))
                    base = HEAD_DIM if h == 0 else 0
                    extra = jnp.where(lane == base, c1.astype(F32),
                                      jnp.where(lane == base + 1, c2.astype(F32),
                                                jnp.where(lane == base + 2, c3.astype(F32), 0.0)))
                    kaug[h, rows, :] = jnp.where(_own_lanes(lane, h), kp, extra).astype(BF16)
            return carry

        lax.fori_loop(0, seq // tk, fill, 0)

    lane_q = lax.broadcasted_iota(jnp.int32, (tq, LANES), 1)
    qs = []
    for h in range(2):
        if fox:
            base = HEAD_DIM if h == 0 else 0
            ones = jnp.where((lane_q >= base) & (lane_q < base + 3), 1.0, 0.0)
            qs.append(jnp.where(_own_lanes(lane_q, h), q_ref[0].astype(F32), ones).astype(BF16))
        else:
            qs.append(q_ref[0, :, h * LANES:(h + 1) * LANES])

    key = lax.broadcasted_iota(jnp.int32, (tk, tq), 0)
    qry = lax.broadcasted_iota(jnp.int32, (tk, tq), 1)

    def scores(j, buf):
        start = pl.multiple_of(j * tk, tk)
        for h in range(2):
            if fox:
                kj = kaug[h, pl.ds(start, tk), :]
            else:
                kj = k_ref[0, pl.ds(start, tk), h * LANES:(h + 1) * LANES]
            buf[h] = lax.dot_general(kj, qs[h], NT_DIMS, preferred_element_type=F32)

    def absorb(j, buf, carry, masked):
        new = []
        for h in range(2):
            m, acc = carry[h]
            s = buf[h]
            if masked:
                s = jnp.where(key <= qry, s, NEG)
            m_new = jnp.maximum(m, jnp.max(s, axis=0, keepdims=True))
            p = jnp.exp2(s - m_new)
            acc = jnp.exp2(m - m_new) * acc + _dot(vaug_t[h, j], p.astype(BF16))
            new.append((m_new, acc))
        return tuple(new)

    def finish(carry):
        (_, acc0), (_, acc1) = carry
        feat_q = lax.broadcasted_iota(jnp.int32, (LANES, tq), 0)
        out_t = jnp.where(feat_q < HEAD_DIM, acc0 / acc0[HEAD_DIM:HEAD_DIM + 1, :], acc1 / acc1[0:1, :])
        o_ref[0] = jnp.transpose(out_t).astype(o_ref.dtype)

    def pair_step(t, carry):
        scores(2 * t + 1, s_odd)
        carry = absorb(2 * t, s_even, carry, False)
        scores(2 * t + 2, s_even)
        return absorb(2 * t + 1, s_odd, carry, False)

    init = tuple((jnp.full((1, tq), NEG, F32), jnp.zeros((LANES, tq), F32)) for _ in range(2))
    scores(0, s_even)
    carry = lax.fori_loop(0, qi // 2, pair_step, init)

    @pl.when(qi % 2 == 0)
    def _():
        finish(absorb(qi, s_even, carry, True))

    @pl.when(qi % 2 == 1)
    def _():
        scores(qi, s_odd)
        finish(absorb(qi, s_odd, absorb(qi - 1, s_even, carry, False), True))


def _causal_attention(q_arr, k_arr, v_arr, c_arr, *, q_blk0, k_blk0, v_blk0, n_pairs):
    b, s, _ = q_arr.shape
    tq = ATTN_BLOCK
    fox = c_arr is not None
    qk_w = LANES if fox else 2 * LANES
    in_specs = [
        pl.BlockSpec((1, tq, qk_w), lambda bi, p, qi: (bi, qi, q_blk0 + p)),
        pl.BlockSpec((1, s, qk_w), lambda bi, p, qi: (bi, 0, k_blk0 + p)),
        pl.BlockSpec((1, s, LANES), lambda bi, p, qi: (bi, 0, v_blk0 + p)),
    ]
    args = [q_arr, k_arr, v_arr]
    scratch = [pltpu.VMEM((2, s // tq, LANES, tq), BF16), pltpu.VMEM((2, tq, tq), F32), pltpu.VMEM((2, tq, tq), F32)]
    if fox:
        in_specs.append(pl.BlockSpec((1, s, LANES), lambda bi, p, qi: (bi, 0, 0)))
        args.append(c_arr)
        scratch.append(pltpu.VMEM((2, s, LANES), BF16))
    return pl.pallas_call(
        functools.partial(_causal_kernel, fox=fox, tq=tq, tk=tq),
        grid=(b, n_pairs, s // tq),
        in_specs=in_specs,
        out_specs=pl.BlockSpec((1, tq, LANES), lambda bi, p, qi: (bi, qi, p)),
        out_shape=jax.ShapeDtypeStruct((b, s, n_pairs * LANES), BF16),
        scratch_shapes=scratch,
        compiler_params=_params("parallel", "parallel", "arbitrary"),
        name="causal_attention",
    )(*args)


DIL_PAD = BLK * max(d for _, d in DIL_BRANCHES)
DIL_UNROLL = 8


def _dilated_kernel(q_ref, k_ref, v_ref, bias_ref, o_ref, qf, kf, vf, acc_s, m_s, l_s, s_even, s_odd, *, seq):
    qf[...] = q_ref[0].astype(F32)
    kf[0:DIL_PAD, :] = jnp.zeros((DIL_PAD, LANES), F32)
    vf[0:DIL_PAD, :] = jnp.zeros((DIL_PAD, LANES), F32)
    kf[DIL_PAD:, :] = k_ref[0].astype(F32)
    vf[DIL_PAD:, :] = v_ref[0].astype(F32)
    head0 = lax.broadcasted_iota(jnp.int32, (BLK, LANES), 1) < HEAD_DIM
    head0_k = lax.broadcasted_iota(jnp.int32, (2 * BLK, LANES), 1) < HEAD_DIM
    ones0 = jnp.where(head0_k, 1.0, 0.0).astype(BF16)
    ones1 = jnp.where(head0_k, 0.0, 1.0).astype(BF16)

    def place(dil, u):
        bi = u // dil
        q_start = (u % dil) + (dil * BLK) * bi
        return bi, q_start, q_start + (DIL_PAD - dil * BLK)

    def scores(br, dil, g, buf):
        for i in range(DIL_UNROLL):
            bi, q_start, k_start = place(dil, g * DIL_UNROLL + i)
            q2 = qf[pl.ds(q_start, BLK, stride=dil), :]
            q_st = jnp.concatenate([jnp.where(head0, q2, 0.0), jnp.where(head0, 0.0, q2)], axis=0).astype(BF16)
            kb = kf[pl.ds(k_start, 2 * BLK, stride=dil), :].astype(BF16)
            s = lax.dot_general(q_st, kb, NT_DIMS, preferred_element_type=F32)
            buf[i] = s + bias_ref[br, 0, jnp.where(bi == 0, 1, 0)]

    def absorb(br, dil, g, buf, first):
        for i in range(DIL_UNROLL):
            _, q_start, k_start = place(dil, g * DIL_UNROLL + i)
            s = buf[i]
            m = jnp.max(s, axis=-1, keepdims=True)
            p = jnp.exp2(s - m).astype(BF16)
            p_cat = jnp.concatenate([p[:BLK], p[BLK:]], axis=1)
            v2 = vf[pl.ds(k_start, 2 * BLK, stride=dil), :]
            rhs = jnp.concatenate([
                jnp.concatenate([jnp.where(head0_k, v2, 0.0).astype(BF16), ones0], axis=1),
                jnp.concatenate([jnp.where(head0_k, 0.0, v2).astype(BF16), ones1], axis=1)], axis=0)
            acc2 = _dot(p_cat, rhs)
            acc_b, l_b = acc2[:, :LANES], acc2[:, LANES:]
            m_b = jnp.where(head0, jnp.broadcast_to(m[:BLK], (BLK, LANES)), jnp.broadcast_to(m[BLK:], (BLK, LANES)))
            idx = pl.ds(q_start, BLK, stride=dil)
            if first:
                m_s[idx, :] = m_b
                l_s[idx, :] = l_b
                acc_s[idx, :] = acc_b
                continue
            m_old = m_s[idx, :]
            m_new = jnp.maximum(m_old, m_b)
            a_old = jnp.exp2(m_old - m_new)
            a_b = jnp.exp2(m_b - m_new)
            m_s[idx, :] = m_new
            l_s[idx, :] = a_old * l_s[idx, :] + a_b * l_b
            acc_s[idx, :] = a_old * acc_s[idx, :] + a_b * acc_b

    n_groups = seq // (BLK * DIL_UNROLL)
    order = sorted(range(len(DIL_BRANCHES)), key=lambda i: -DIL_BRANCHES[i][1])
    for pos, br in enumerate(order):
        dil = DIL_BRANCHES[br][1]
        first = pos == 0

        def trip(t, carry, br=br, dil=dil, first=first):
            scores(br, dil, 2 * t + 1, s_odd)
            absorb(br, dil, 2 * t, s_even, first)
            scores(br, dil, 2 * t + 2, s_even)
            absorb(br, dil, 2 * t + 1, s_odd, first)
            return carry

        scores(br, dil, 0, s_even)
        lax.fori_loop(0, n_groups // 2 - 1, trip, 0)
        scores(br, dil, n_groups - 1, s_odd)
        absorb(br, dil, n_groups - 2, s_even, first)
        absorb(br, dil, n_groups - 1, s_odd, first)

    o_ref[0] = (acc_s[...] / l_s[...]).astype(o_ref.dtype)


def _dilated_attention(proj, bias, *, q_blk0, k_blk0, v_blk0):
    b, s, _ = proj.shape
    n_pairs = DIL_HEADS // 2
    assert (s // (BLK * DIL_UNROLL)) % 2 == 0
    blk = lambda off: pl.BlockSpec((1, s, LANES), lambda bi, p: (bi, 0, off + p))
    state = [pltpu.VMEM((s, LANES), F32) for _ in range(4)]
    padded = [pltpu.VMEM((DIL_PAD + s, LANES), F32) for _ in range(2)]
    score_bufs = [pltpu.VMEM((DIL_UNROLL, 2 * BLK, 2 * BLK), F32) for _ in range(2)]
    return pl.pallas_call(
        functools.partial(_dilated_kernel, seq=s),
        grid=(b, n_pairs),
        in_specs=[blk(q_blk0), blk(k_blk0), blk(v_blk0),
                  pl.BlockSpec((len(DIL_BRANCHES), 1, 2, 2 * BLK, 2 * BLK), lambda bi, p: (0, p, 0, 0, 0))],
        out_specs=pl.BlockSpec((1, s, LANES), lambda bi, p: (bi, 0, p)),
        out_shape=jax.ShapeDtypeStruct((b, s, n_pairs * LANES), BF16),
        scratch_shapes=[state[0], padded[0], padded[1], state[1], state[2], state[3]] + score_bufs,
        compiler_params=_params("parallel", "parallel"),
        name="dilated_attention",
    )(proj, proj, proj, bias)


def _swa_kernel(q_ref, k_ref, v_ref, bias_ref, sink_ref, o_ref, vaug_t, s_even, s_odd, *, tc):
    ci = pl.program_id(1)
    seq = k_ref.shape[1]
    grp = SWA_Q_HEADS // SWA_KV_HEADS
    n_blk = tc // BLK

    @pl.when(ci == 0)
    def _():
        feat = lax.broadcasted_iota(jnp.int32, (LANES, BLK), 0)

        def fill(t, carry):
            vt = jnp.transpose(v_ref[0, pl.ds(pl.multiple_of(t * BLK, BLK), BLK), :].astype(F32))
            for kv in range(SWA_KV_HEADS):
                vaug_t[kv, t] = jnp.where(_own_lanes(feat, kv), vt, 1.0).astype(BF16)
            return carry

        lax.fori_loop(0, seq // BLK, fill, 0)

    lane_q = lax.broadcasted_iota(jnp.int32, (BLK, LANES), 1)
    feat_o = lax.broadcasted_iota(jnp.int32, (LANES, grp * BLK), 0)

    def blocks(bb):
        own = ci * n_blk + bb
        return jnp.maximum(own - 1, 0), own

    def scores(bb, kv, buf):
        prev, own = blocks(bb)
        k2 = jnp.concatenate([k_ref[0, pl.ds(pl.multiple_of(prev * BLK, BLK), BLK), :],
                              k_ref[0, pl.ds(pl.multiple_of(own * BLK, BLK), BLK), :]], axis=0)
        q4 = jnp.concatenate(
            [jnp.where(_own_lanes(lane_q, kv), q_ref[0, bb * BLK:(bb + 1) * BLK, g * LANES:(g + 1) * LANES].astype(F32),
                       0.0) for g in range(grp)], axis=0).astype(BF16)
        s = lax.dot_general(k2, q4, NT_DIMS, preferred_element_type=F32)
        buf[...] = s + bias_ref[kv, jnp.where(own == 0, 1, 0)]

    def absorb(bb, kv, buf):
        prev, own = blocks(bb)
        s = buf[...]
        sink = sink_ref[kv]
        m = jnp.maximum(jnp.max(s, axis=0, keepdims=True), sink)
        p = jnp.exp2(s - m).astype(BF16)
        v2 = jnp.concatenate([vaug_t[kv, prev], vaug_t[kv, own]], axis=1)
        acc = _dot(v2, p)
        den = (acc[HEAD_DIM:HEAD_DIM + 1, :] if kv == 0 else acc[0:1, :]) + jnp.exp2(sink - m)
        return acc / den

    units = [(bb, kv) for bb in range(n_blk) for kv in range(SWA_KV_HEADS)]
    bufs = (s_even, s_odd)
    scores(*units[0], bufs[0])
    outs = {}
    for n, unit in enumerate(units):
        if n + 1 < len(units):
            scores(*units[n + 1], bufs[(n + 1) % 2])
        outs[unit] = absorb(*unit, bufs[n % 2])
    for bb in range(n_blk):
        out_t = jnp.where(feat_o < HEAD_DIM, outs[(bb, 0)], outs[(bb, 1)])
        for g in range(grp):
            o_ref[0, bb * BLK:(bb + 1) * BLK, g * LANES:(g + 1) * LANES] = jnp.transpose(
                out_t[:, g * BLK:(g + 1) * BLK]).astype(o_ref.dtype)


def _swa_attention(proj, bias, sink_row, *, q_blk0, k_blk0, v_blk0):
    b, s, _ = proj.shape
    tc = SWA_CHUNK
    qw = SWA_Q_HEADS * HEAD_DIM
    grp = SWA_Q_HEADS // SWA_KV_HEADS
    return pl.pallas_call(
        functools.partial(_swa_kernel, tc=tc),
        grid=(b, s // tc),
        in_specs=[
            pl.BlockSpec((1, tc, qw), lambda bi, ci: (bi, ci, q_blk0 * LANES // qw)),
            pl.BlockSpec((1, s, LANES), lambda bi, ci: (bi, 0, k_blk0)),
            pl.BlockSpec((1, s, LANES), lambda bi, ci: (bi, 0, v_blk0)),
            pl.BlockSpec(bias.shape, lambda bi, ci: (0, 0, 0, 0)),
            pl.BlockSpec(sink_row.shape, lambda bi, ci: (0, 0, 0)),
        ],
        out_specs=pl.BlockSpec((1, tc, qw), lambda bi, ci: (bi, ci, 0)),
        out_shape=jax.ShapeDtypeStruct((b, s, qw), BF16),
        scratch_shapes=[pltpu.VMEM((SWA_KV_HEADS, s // BLK, LANES, BLK), BF16),
                        pltpu.VMEM((2 * BLK, grp * BLK), F32), pltpu.VMEM((2 * BLK, grp * BLK), F32)],
        compiler_params=_params("parallel", "arbitrary"),
        name="swa_attention",
    )(proj, proj, proj, bias, sink_row)


def _rope128(x, a, bm, bp):
    return x * a + pltpu.roll(x, LANES - 16, axis=1) * bm + pltpu.roll(x, 16, axis=1) * bp


def _proj_odd_kernel(x_ref, g_ref, w_ref, qn_ref, kvn_ref, wuq_ref, wuk_ref, wuv_ref, a_ref, bm_ref, bp_ref,
                     q_out, k_out, v_out, swa_out):
    xb = _rms(x_ref[...], g_ref[...]).astype(BF16)
    o1 = MLA_Q_RANK + LANES
    o2 = o1 + MLA_KV_RANK
    cq_kpe = _dot(xb, w_ref[:, 0:o1])
    c_q, kpe = cq_kpe[:, :MLA_Q_RANK], cq_kpe[:, MLA_Q_RANK:]
    c_kv = _dot(xb, w_ref[:, o1:o2])
    swa_out[...] = _dot(xb, w_ref[:, o2:]).astype(swa_out.dtype)
    a, bm, bp = a_ref[...], bm_ref[...], bp_ref[...]
    kpe = _rope128(kpe, a, bm, bp)
    cqn = _rms(c_q, qn_ref[...]).astype(BF16)
    ckvn = _rms(c_kv, kvn_ref[...]).astype(BF16)
    v_out[...] = _dot(ckvn, wuv_ref[...]).astype(v_out.dtype)
    q_raw = _dot(cqn, wuq_ref[...])
    k_raw = _dot(ckvn, wuk_ref[...])
    for h in range(MLA_HEADS):
        sl = slice(h * LANES, (h + 1) * LANES)
        q_out[:, sl] = _rope128(q_raw[:, sl], a, bm, bp).astype(q_out.dtype)
        k_out[:, sl] = (k_raw[:, sl] + kpe).astype(k_out.dtype)


def _proj_odd(h, g, w, qn, kvn, wuq, wuk, wuv, rope_a, rope_bm, rope_bp, seq):
    n, d = h.shape
    tm = PROJ_ODD_TILE
    n_in = w.shape[1]
    n_swa = n_in - (MLA_Q_RANK + MLA_KV_RANK + LANES)
    tiles_per_seq = seq // tm
    full = lambda shape: pl.BlockSpec(shape, lambda i: (0, 0))
    rope = pl.BlockSpec((tm, LANES), lambda i: (i % tiles_per_seq, 0))
    return pl.pallas_call(
        _proj_odd_kernel,
        grid=(n // tm,),
        in_specs=[
            pl.BlockSpec((tm, d), lambda i: (i, 0)), full((1, d)), full(w.shape),
            full(qn.shape), full(kvn.shape), full(wuq.shape), full(wuk.shape), full(wuv.shape),
            rope, rope, rope,
        ],
        out_specs=[
            pl.BlockSpec((tm, MLA_HEADS * LANES), lambda i: (i, 0)),
            pl.BlockSpec((tm, MLA_HEADS * LANES), lambda i: (i, 0)),
            pl.BlockSpec((tm, MLA_HEADS * MLA_V), lambda i: (i, 0)),
            pl.BlockSpec((tm, n_swa), lambda i: (i, 0)),
        ],
        out_shape=[
            jax.ShapeDtypeStruct((n, MLA_HEADS * LANES), BF16),
            jax.ShapeDtypeStruct((n, MLA_HEADS * LANES), BF16),
            jax.ShapeDtypeStruct((n, MLA_HEADS * MLA_V), BF16),
            jax.ShapeDtypeStruct((n, n_swa), BF16),
        ],
        compiler_params=_params("parallel"),
        name="proj_odd",
    )(h, g, w, qn, kvn, wuq, wuk, wuv, rope_a, rope_bm, rope_bp)


def _out_router_kernel(h_ref, oa_ref, ob_ref, wo_ref, g_ref, wr_hi_ref, wr_lo_ref, br_ref,
                       h_out, hn_out, gates_out, counts_out, pos_col_out, pos_row_out, tri_ref, seen_ref):
    i = pl.program_id(0)
    tm = h_ref.shape[0]
    half = oa_ref.shape[1]

    @pl.when(i == 0)
    def _():
        ri = lax.broadcasted_iota(jnp.int32, (tm, tm), 0)
        ci = lax.broadcasted_iota(jnp.int32, (tm, tm), 1)
        tri_ref[...] = jnp.where(ci < ri, 1.0, 0.0).astype(BF16)

    @pl.when(i % (MOE_TILE // tm) == 0)
    def _():
        seen_ref[...] = jnp.zeros_like(seen_ref)

    lane = lax.broadcasted_iota(jnp.int32, (MXU_TILE, LANES), 1)
    big = jnp.int32(LANES)
    is_grp = (lane >= N_EXPERTS) & (lane < N_EXPERTS + N_GROUPS)
    routed_parts = []
    for r in range(0, tm, MXU_TILE):
        rows = slice(r, r + MXU_TILE)
        h = h_ref[rows, :] + _dot(oa_ref[rows, :], wo_ref[0:half, :]) + _dot(ob_ref[rows, :], wo_ref[half:, :])
        h_out[rows, :] = h
        hn = _rms(h, g_ref[...])
        hn_out[rows, :] = hn.astype(hn_out.dtype)
        z = _dot_hi(hn, wr_hi_ref[...], wr_lo_ref[...]) + br_ref[...]
        zg = jnp.where(is_grp, z, -jnp.inf)
        g_max = jnp.max(zg, axis=-1, keepdims=True)
        g_w = 1.0 / jnp.sum(jnp.exp(zg - g_max), axis=-1, keepdims=True)
        g_idx = jnp.min(jnp.where(zg == g_max, lane - N_EXPERTS, big), axis=-1, keepdims=True)
        in_grp = (lane < N_EXPERTS) & ((lane // EXPERTS_PER_GROUP) == g_idx)
        ze = jnp.where(in_grp, z, -jnp.inf)
        v1 = jnp.max(ze, axis=-1, keepdims=True)
        i1 = jnp.min(jnp.where(ze == v1, lane, big), axis=-1, keepdims=True)
        ze2 = jnp.where(lane == i1, -jnp.inf, ze)
        v2 = jnp.max(ze2, axis=-1, keepdims=True)
        i2 = jnp.min(jnp.where(ze2 == v2, lane, big), axis=-1, keepdims=True)
        e2 = jnp.exp(v2 - v1)
        w1 = g_w / (1.0 + e2)
        w2 = g_w * e2 / (1.0 + e2)
        gates_out[rows, :] = jnp.where(lane == i1, w1, 0.0) + jnp.where(lane == i2, w2, 0.0)
        routed_parts.append(jnp.where((lane == g_idx) & (lane < N_GROUPS), 1.0, 0.0))

    routed = jnp.concatenate(routed_parts, axis=0)
    pos = jnp.where(routed > 0.0, _dot(tri_ref[...], routed.astype(BF16)) + seen_ref[...], -1.0)
    pos_col_out[...] = pos
    pos_row_out[0] = jnp.transpose(pos)[0:8, :]
    here = jnp.sum(routed, axis=0, keepdims=True)
    seen_ref[...] += here
    counts_out[0] = jnp.broadcast_to(here, counts_out.shape[1:])


def _out_router(h, oa, ob, wo, g, wr_hi, wr_lo, br):
    n, d = h.shape
    tm = ROUTER_TILE
    per_tile = MOE_TILE // tm
    half = oa.shape[1]
    full = lambda shape: pl.BlockSpec(shape, lambda i: (0, 0))
    tile = lambda w: pl.BlockSpec((tm, w), lambda i: (i, 0))
    return pl.pallas_call(
        _out_router_kernel,
        grid=(n // tm,),
        in_specs=[tile(d), tile(half), tile(half), full(wo.shape), full((1, d)),
                  full(wr_hi.shape), full(wr_lo.shape), full((1, LANES))],
        out_specs=[tile(d), tile(d), tile(LANES), pl.BlockSpec((1, 8, LANES), lambda i: (i, 0, 0)),
                   tile(LANES), pl.BlockSpec((1, 8, tm), lambda i: (i // per_tile, 0, i % per_tile))],
        out_shape=[
            jax.ShapeDtypeStruct((n, d), F32),
            jax.ShapeDtypeStruct((n, d), BF16),
            jax.ShapeDtypeStruct((n, LANES), F32),
            jax.ShapeDtypeStruct((n // tm, 8, LANES), F32),
            jax.ShapeDtypeStruct((n, LANES), F32),
            jax.ShapeDtypeStruct((n // MOE_TILE, 8, MOE_TILE), F32),
        ],
        scratch_shapes=[pltpu.VMEM((tm, tm), BF16), pltpu.VMEM((1, LANES), F32)],
        compiler_params=_params("arbitrary"),
        name="out_router",
    )(h, oa, ob, wo, g, wr_hi, wr_lo, br)


MOE_CHUNK = 320
MOE_ROWS = 1024
MOE_ALIGN = 16
MOE_EXPERTS_PER_STEP = 4


def _moe_sorted_tiles(n):
    n_seg = (n // MOE_TILE) * N_GROUPS
    rows = n + n_seg * (MOE_ALIGN - 1) + N_GROUPS * (MOE_CHUNK + MOE_ROWS - 1)
    return -(-rows // MOE_ROWS)


def _moe_tables(counts, n):
    n_rt = _moe_sorted_tiles(n)
    seg_len = (counts + (MOE_ALIGN - 1)) // MOE_ALIGN * MOE_ALIGN
    group_len = seg_len.sum(axis=0)
    group_span = (group_len + MOE_CHUNK + MOE_ROWS - 1) // MOE_ROWS * MOE_ROWS
    group_end = jnp.cumsum(group_span)
    group_start = group_end - group_span
    seg_off = group_start[None, :] + jnp.cumsum(seg_len, axis=0) - seg_len
    rt_start = jnp.arange(n_rt, dtype=jnp.int32) * MOE_ROWS
    rt_group = jnp.minimum(jnp.sum(rt_start[:, None] >= group_end[None, :], axis=1), N_GROUPS - 1)
    rt_valid = jnp.clip(group_start[rt_group] + group_len[rt_group] - rt_start, 0, MOE_ROWS)
    n_used = (group_end[-1] // MOE_ROWS).reshape(1)
    i32 = lambda a: a.astype(jnp.int32)
    return i32(seg_off.reshape(-1)), i32(rt_group), i32(rt_valid), i32(n_used)


def _moe_pack_kernel(counts_ref, off_ref, hn_ref, gates_ref, pos_ref, xs_in, xs_ref, xbuf, sem):
    del xs_in
    i = pl.program_id(0)
    d = hn_ref.shape[1]

    def write(grp, dst):
        return pltpu.make_async_copy(xbuf.at[grp], xs_ref.at[pl.ds(dst, MOE_CHUNK)], sem.at[grp])

    @pl.when(i > 0)
    def _():
        for grp in range(N_GROUPS):
            write(grp, 0).wait()

    lane = lax.broadcasted_iota(jnp.int32, (hn_ref.shape[0], LANES), 1)
    gates = gates_ref[...]
    gates_hi = gates.astype(BF16).astype(F32)
    gates_lo = gates - gates_hi
    chunks = []
    for grp in range(N_GROUPS):
        seg = i * N_GROUPS + grp
        first = grp * EXPERTS_PER_GROUP
        g_hi = gates_hi if first == 0 else pltpu.roll(gates_hi, LANES - first, axis=1)
        g_lo = pltpu.roll(gates_lo, (LANES - first + EXPERTS_PER_GROUP) % LANES, axis=1)
        gate_cols = jnp.where(lane < EXPERTS_PER_GROUP, g_hi, jnp.where(lane < 2 * EXPERTS_PER_GROUP, g_lo, 0.0))
        gate_cols = gate_cols.astype(BF16)
        pos_row = pos_ref[0, grp:grp + 1, :]

        def chunk(first_pos, grp=grp, seg=seg, gate_cols=gate_cols, pos_row=pos_row):
            slot_pos = lax.broadcasted_iota(jnp.int32, (MOE_CHUNK, 1), 0).astype(F32) + first_pos
            gather = jnp.where(pos_row == slot_pos, 1.0, 0.0).astype(BF16)
            xbuf[grp, :, 0:d] = _dot(gather, hn_ref[...]).astype(BF16)
            xbuf[grp, :, d:] = _dot(gather, gate_cols).astype(BF16)
            dst = off_ref[seg] + first_pos.astype(jnp.int32)
            write(grp, pl.multiple_of(dst, MOE_ALIGN)).start()

        chunk(jnp.float32(0.0))
        chunks.append(chunk)

    for grp in range(N_GROUPS):
        def more(k, carry, grp=grp):
            write(grp, 0).wait()
            chunks[grp]((k * MOE_CHUNK).astype(F32))
            return carry

        count = counts_ref[i * N_GROUPS + grp]
        lax.fori_loop(1, (count + (MOE_CHUNK - 1)) // MOE_CHUNK, more, 0)

    @pl.when(i == pl.num_programs(0) - 1)
    def _():
        for grp in range(N_GROUPS):
            write(grp, 0).wait()


def _moe_pack(counts, seg_off, hn, gates, pos_row, xs_init):
    n, d = hn.shape
    rows = xs_init.shape[0]
    grid_spec = pltpu.PrefetchScalarGridSpec(
        num_scalar_prefetch=2,
        grid=(n // MOE_TILE,),
        in_specs=[
            pl.BlockSpec((MOE_TILE, d), lambda i, c, o: (i, 0)),
            pl.BlockSpec((MOE_TILE, LANES), lambda i, c, o: (i, 0)),
            pl.BlockSpec((1, 8, MOE_TILE), lambda i, c, o: (i, 0, 0)),
            pl.BlockSpec(memory_space=pl.ANY),
        ],
        out_specs=pl.BlockSpec(memory_space=pl.ANY),
        scratch_shapes=[pltpu.VMEM((N_GROUPS, MOE_CHUNK, d + LANES), BF16), pltpu.SemaphoreType.DMA((N_GROUPS,))],
    )
    return pl.pallas_call(
        _moe_pack_kernel,
        grid_spec=grid_spec,
        out_shape=jax.ShapeDtypeStruct((rows, d + LANES), BF16),
        input_output_aliases={5: 0},
        compiler_params=_params("arbitrary"),
        name="moe_pack",
    )(counts, seg_off, hn, gates, pos_row, xs_init)


def _moe_experts_kernel(grp_ref, valid_ref, used_ref, xs_ref, wg_ref, wu_ref, wd_ref, ys_ref, acc_ref,
                        wg_b, wu_b, wd_b):
    del grp_ref, used_ref
    rt = pl.program_id(0)
    j = pl.program_id(1)
    d = ys_ref.shape[1]
    valid = valid_ref[rt]

    @pl.when((valid > 0) & (j == 0))
    def _():
        acc_ref[...] = jnp.zeros_like(acc_ref)

    def experts(x, gate_cols, w_gate, w_up, w_down):
        lane = lax.broadcasted_iota(jnp.int32, gate_cols.shape, 1)
        acts = []
        for e in range(MOE_EXPERTS_PER_STEP):
            idx = j * MOE_EXPERTS_PER_STEP + e
            mine = (lane == idx) | (lane == idx + EXPERTS_PER_GROUP)
            gate = jnp.sum(jnp.where(mine, gate_cols, 0.0), axis=-1, keepdims=True)
            a = _dot(x, w_gate(e))
            u = _dot(x, w_up(e))
            acts.append(((a * jax.nn.sigmoid(a)) * u * gate).astype(BF16))
        return _dot(jnp.concatenate(acts, axis=1), w_down())

    @pl.when(valid == MOE_ROWS)
    def _():
        acc_ref[...] += experts(
            xs_ref[:, 0:d], xs_ref[:, d:].astype(F32), lambda e: wg_ref[0, 0, e].astype(BF16),
            lambda e: wu_ref[0, 0, e].astype(BF16), lambda: wd_ref[0, 0].astype(BF16).reshape(wd_b.shape))

    @pl.when((valid > 0) & (valid < MOE_ROWS))
    def _():
        wg_b[...] = wg_ref[0, 0].astype(BF16)
        wu_b[...] = wu_ref[0, 0].astype(BF16)
        wd_b[...] = wd_ref[0, 0].astype(BF16).reshape(wd_b.shape)

        def rows_block(t, carry):
            rows = pl.ds(pl.multiple_of(t * MXU_TILE, MXU_TILE), MXU_TILE)
            acc_ref[rows, :] += experts(xs_ref[rows, 0:d], xs_ref[rows, d:].astype(F32), lambda e: wg_b[e],
                                        lambda e: wu_b[e], lambda: wd_b[...])
            return carry

        lax.fori_loop(0, (valid + (MXU_TILE - 1)) // MXU_TILE, rows_block, 0)

    @pl.when((valid > 0) & (j == pl.num_programs(1) - 1))
    def _():
        row = lax.broadcasted_iota(jnp.int32, (acc_ref.shape[0], 1), 0)
        ys_ref[...] = jnp.where(row < valid, acc_ref[...], 0.0).astype(ys_ref.dtype)

    @pl.when((valid == 0) & (j == pl.num_programs(1) - 1))
    def _():
        ys_ref[...] = jnp.zeros_like(ys_ref)


def _moe_experts(rt_group, rt_valid, n_used, xs, w_gate, w_up, w_down, layer):
    rows = xs.shape[0]
    d = w_gate.shape[-2]
    n_rt = rows // MOE_ROWS
    eps = MOE_EXPERTS_PER_STEP
    n_steps = EXPERTS_PER_GROUP // eps

    def tile_idx(rt, j, grp, valid, used):
        return jnp.minimum(rt, used[0] - 1)

    def w_idx(rt, j, grp, valid, used):
        live = rt < used[0]
        return (layer, grp[tile_idx(rt, j, grp, valid, used)], jnp.where(live, j, n_steps - 1), 0, 0)

    grid_spec = pltpu.PrefetchScalarGridSpec(
        num_scalar_prefetch=3,
        grid=(n_rt, n_steps),
        in_specs=[
            pl.BlockSpec((MOE_ROWS, d + LANES), lambda rt, j, grp, valid, used: (tile_idx(rt, j, grp, valid, used), 0)),
            pl.BlockSpec((1, 1, eps, d, D_EXPERT), w_idx),
            pl.BlockSpec((1, 1, eps, d, D_EXPERT), w_idx),
            pl.BlockSpec((1, 1, eps, D_EXPERT, d), w_idx),
        ],
        out_specs=pl.BlockSpec((MOE_ROWS, d), lambda rt, j, grp, valid, used: (rt, 0)),
        scratch_shapes=[pltpu.VMEM((MOE_ROWS, d), F32), pltpu.VMEM((eps, d, D_EXPERT), BF16),
                        pltpu.VMEM((eps, d, D_EXPERT), BF16), pltpu.VMEM((eps * D_EXPERT, d), BF16)],
    )
    return pl.pallas_call(
        _moe_experts_kernel,
        grid_spec=grid_spec,
        out_shape=jax.ShapeDtypeStruct((rows, d), BF16),
        compiler_params=_params("arbitrary", "arbitrary"),
        name="moe_experts",
    )(rt_group, rt_valid, n_used, xs, w_gate, w_up, w_down)


def _moe_combine_kernel(counts_ref, off_ref, pos_ref, h_ref, fg_ref, ys_ref, o_ref, ybuf, ymore, sem, *, final_norm):
    i = pl.program_id(0)

    def read(tile, grp):
        src = pl.multiple_of(off_ref[tile * N_GROUPS + grp], MOE_ALIGN)
        slot = tile % 2
        return pltpu.make_async_copy(ys_ref.at[pl.ds(src, MOE_CHUNK)],
                                     ybuf.at[slot, pl.ds(grp * MOE_CHUNK, MOE_CHUNK)], sem.at[slot, grp])

    def fetch(tile):
        for grp in range(N_GROUPS):
            @pl.when(counts_ref[tile * N_GROUPS + grp] > 0)
            def _(grp=grp):
                read(tile, grp).start()

    @pl.when(i == 0)
    def _():
        ybuf[...] = jnp.zeros_like(ybuf)
        fetch(i)

    @pl.when(i + 1 < pl.num_programs(0))
    def _():
        fetch(i + 1)

    for grp in range(N_GROUPS):
        @pl.when(counts_ref[i * N_GROUPS + grp] > 0)
        def _(grp=grp):
            read(i, grp).wait()

    slot_pos = lax.broadcasted_iota(jnp.int32, (1, MOE_CHUNK), 1).astype(F32)
    for r in range(0, MOE_TILE, MOE_TILE // 2):
        rows = slice(r, r + MOE_TILE // 2)
        scatter = jnp.concatenate(
            [jnp.where(pos_ref[rows, grp:grp + 1] == slot_pos, 1.0, 0.0).astype(BF16) for grp in range(N_GROUPS)],
            axis=1)
        o_ref[rows, :] = h_ref[rows, :] + _dot(scatter, ybuf[i % 2])
    for grp in range(N_GROUPS):
        count = counts_ref[i * N_GROUPS + grp]
        pos_col = pos_ref[:, grp:grp + 1]

        def more(k, carry, grp=grp, pos_col=pos_col):
            src = pl.multiple_of(off_ref[i * N_GROUPS + grp] + k * MOE_CHUNK, MOE_ALIGN)
            pltpu.sync_copy(ys_ref.at[pl.ds(src, MOE_CHUNK)], ymore)
            scatter = jnp.where(pos_col == slot_pos + (k * MOE_CHUNK).astype(F32), 1.0, 0.0).astype(BF16)
            o_ref[...] += _dot(scatter, ymore[...])
            return carry

        lax.fori_loop(1, (count + (MOE_CHUNK - 1)) // MOE_CHUNK, more, 0)

    if final_norm:
        o_ref[...] = _rms(o_ref[...], fg_ref[...])


def _moe_combine(counts, seg_off, pos_col, h, fg, ys, final_norm):
    n, d = h.shape
    grid_spec = pltpu.PrefetchScalarGridSpec(
        num_scalar_prefetch=2,
        grid=(n // MOE_TILE,),
        in_specs=[
            pl.BlockSpec((MOE_TILE, LANES), lambda i, c, o: (i, 0)),
            pl.BlockSpec((MOE_TILE, d), lambda i, c, o: (i, 0)),
            pl.BlockSpec((1, d), lambda i, c, o: (0, 0)),
            pl.BlockSpec(memory_space=pl.ANY),
        ],
        out_specs=pl.BlockSpec((MOE_TILE, d), lambda i, c, o: (i, 0)),
        scratch_shapes=[pltpu.VMEM((2, N_GROUPS * MOE_CHUNK, d), BF16), pltpu.VMEM((MOE_CHUNK, d), BF16),
                        pltpu.SemaphoreType.DMA((2, N_GROUPS))],
    )
    return pl.pallas_call(
        functools.partial(_moe_combine_kernel, final_norm=final_norm),
        grid_spec=grid_spec,
        out_shape=jax.ShapeDtypeStruct((n, d), F32),
        compiler_params=_params("arbitrary"),
        name="moe_combine",
    )(counts, seg_off, pos_col, h, fg, ys)


def _alibi_slopes(n):
    return np.exp2(-8.0 * np.arange(1, n + 1, dtype=np.float64) / n)


def _band_bias(slopes, max_steps, step_dist):
    steps = (np.arange(BLK)[:, None] + BLK) - np.arange(2 * BLK)[None, :]
    in_band = (steps >= 0) & (steps <= max_steps)
    dist = (steps * step_dist).astype(np.float64)
    return np.where(in_band[None], -slopes[:, None, None] * dist[None], NEG)


def _attention_bias_tables():
    own_half = np.arange(2 * BLK) >= BLK
    dil = np.stack([_band_bias(_alibi_slopes(DIL_HEADS), w // dl, dl) for (w, dl) in DIL_BRANCHES]) * LOG2E
    dil = np.stack([dil, np.where(own_half, dil, NEG)], axis=2)
    dil = dil.reshape(len(DIL_BRANCHES), DIL_HEADS // 2, 2, 2, BLK, 2 * BLK)
    dil = dil.transpose(0, 1, 3, 2, 4, 5).reshape(len(DIL_BRANCHES), DIL_HEADS // 2, 2, 2 * BLK, 2 * BLK)
    grp = SWA_Q_HEADS // SWA_KV_HEADS
    swa = _band_bias(_alibi_slopes(SWA_Q_HEADS), SWA_WINDOW - 1, 1) * LOG2E
    swa = swa.reshape(SWA_KV_HEADS, grp, BLK, 2 * BLK).transpose(0, 3, 1, 2).reshape(SWA_KV_HEADS, 2 * BLK, grp * BLK)
    swa = np.stack([swa, np.where(own_half[None, :, None], swa, NEG)], axis=1)
    return dil.astype(np.float32), swa.astype(np.float32)


def _rope_tables(s):
    inv = ROPE_THETA ** (-np.arange(0, MLA_ROPE, 2, dtype=np.float64) / MLA_ROPE)
    ang = np.arange(s, dtype=np.float64)[:, None] * inv[None, :]
    cos, sin = np.cos(ang), np.sin(ang)
    half = MLA_ROPE // 2
    zeros_tail = np.zeros((s, LANES - MLA_NOPE - MLA_ROPE))
    a = np.concatenate([np.ones((s, MLA_NOPE)), cos, cos, zeros_tail], axis=1)
    zeros_nope = np.zeros((s, MLA_NOPE))
    zeros_half = np.zeros((s, half))
    bm = np.concatenate([zeros_nope, -sin, zeros_half, zeros_tail], axis=1)
    bp = np.concatenate([zeros_nope, zeros_half, sin, zeros_tail], axis=1)
    return a.astype(np.float32), bm.astype(np.float32), bp.astype(np.float32)


def _swa_head_order(w, axis):
    grp = SWA_Q_HEADS // SWA_KV_HEADS
    shape = w.shape
    w = w.reshape(shape[:axis] + (SWA_KV_HEADS, grp, HEAD_DIM) + shape[axis + 1:])
    return jnp.swapaxes(w, axis, axis + 1).reshape(shape)


def _pad_cols(w, width):
    return jnp.pad(w, ((0, 0), (0, width - w.shape[1])))


def _router_weights(w_group, b_group, w_router, b_router):
    w = _pad_cols(jnp.concatenate([w_router, w_group], axis=1), LANES)
    b = _pad_cols(jnp.concatenate([b_router, b_group])[None, :], LANES)
    hi, lo = _hi_lo(w)
    return hi, lo, b


def kernel(x, attn_norm, ffn_norm, final_norm, e_w_in, e_b_f, e_w_out, o_w_in, o_q_norm, o_kv_norm, o_w_uq,
           o_w_ukv, o_sinks, o_w_out, moe_w_group, moe_b_group, moe_w_router, moe_b_router, moe_w_gate,
           moe_w_up, moe_w_down):
    b, s, d = x.shape
    n = b * s
    depth = attn_norm.shape[0]
    assert s % (BLK * DIL_BRANCHES[-1][1]) == 0 and d == D_MODEL
    assert all(s % t == 0 for t in (PROJ_EVEN_TILE, PROJ_ODD_TILE, ATTN_BLOCK, SWA_CHUNK, MOE_TILE))
    assert MOE_TILE % ROUTER_TILE == 0
    h = x.reshape(n, d)

    dil_bias, swa_bias = _attention_bias_tables()
    rope_a, rope_bm, rope_bp = _rope_tables(s)

    xs = jnp.zeros((_moe_sorted_tiles(n) * MOE_ROWS, d + LANES), BF16)
    for layer in range(depth):
        i = layer // 2
        g_attn = attn_norm[layer][None, :]
        if layer % 2 == 0:
            w_in = e_w_in[i]
            hq = FOX_HEADS * HEAD_DIM
            scale = HEAD_DIM ** -0.5
            cols = [w_in[:, 0:hq] * (scale * LOG2E), w_in[:, hq:2 * hq], w_in[:, 2 * hq:3 * hq]]
            o = 3 * hq + FOX_HEADS
            cols += [w_in[:, o:o + hq] * (scale * LOG2E), w_in[:, o + hq:o + 2 * hq], w_in[:, o + 2 * hq:o + 3 * hq]]
            w_main = jnp.concatenate(cols, axis=1).astype(BF16)
            wf_hi, wf_lo = _hi_lo(_pad_cols(w_in[:, 3 * hq:o], LANES))
            b_f = _pad_cols(e_b_f[i][None, :], LANES)
            proj, c = _proj_even(h, g_attn, w_main, wf_hi, wf_lo, b_f, s)
            proj = proj.reshape(b, s, -1)
            o_a = _causal_attention(proj, proj, proj, c.reshape(b, s, LANES), q_blk0=0, k_blk0=4, v_blk0=8,
                                    n_pairs=FOX_HEADS // 2)
            o_b = _dilated_attention(proj, dil_bias, q_blk0=12, k_blk0=16, v_blk0=20)
            w_out = e_w_out[i].astype(BF16)
        else:
            w_in = o_w_in[i]
            o1 = MLA_Q_RANK + MLA_KV_RANK
            o2 = o1 + MLA_ROPE
            sq = SWA_Q_HEADS * HEAD_DIM
            kpe_cols = jnp.pad(w_in[:, o1:o2], ((0, 0), (MLA_NOPE, LANES - MLA_NOPE - MLA_ROPE)))
            w_main = jnp.concatenate(
                [w_in[:, :MLA_Q_RANK], kpe_cols, w_in[:, MLA_Q_RANK:o1],
                 _swa_head_order(w_in[:, o2:o2 + sq] * (HEAD_DIM ** -0.5 * LOG2E), axis=1),
                 w_in[:, o2 + sq:]],
                axis=1).astype(BF16)
            dq = MLA_NOPE + MLA_ROPE
            wuq = o_w_uq[i].reshape(MLA_Q_RANK, MLA_HEADS, dq) * (dq ** -0.5 * LOG2E)
            wuq = jnp.pad(wuq, ((0, 0), (0, 0), (0, LANES - dq))).reshape(MLA_Q_RANK, MLA_HEADS * LANES)
            wukv = o_w_ukv[i].reshape(MLA_KV_RANK, MLA_HEADS, MLA_NOPE + MLA_V)
            wuk = jnp.pad(wukv[:, :, :MLA_NOPE], ((0, 0), (0, 0), (0, LANES - MLA_NOPE)))
            wuk = wuk.reshape(MLA_KV_RANK, MLA_HEADS * LANES)
            wuv = wukv[:, :, MLA_NOPE:].reshape(MLA_KV_RANK, MLA_HEADS * MLA_V)
            q_full, k_full, v_mla, swa = _proj_odd(
                h, g_attn, w_main, o_q_norm[i][None, :], o_kv_norm[i][None, :], wuq.astype(BF16),
                wuk.astype(BF16), wuv.astype(BF16), rope_a, rope_bm, rope_bp, s)
            o_a = _causal_attention(q_full.reshape(b, s, -1), k_full.reshape(b, s, -1), v_mla.reshape(b, s, -1),
                                    None, q_blk0=0, k_blk0=0, v_blk0=0, n_pairs=MLA_HEADS // 2)
            grp = SWA_Q_HEADS // SWA_KV_HEADS
            sink_row = jnp.repeat(o_sinks[i].reshape(SWA_KV_HEADS, grp) * LOG2E, BLK, axis=1)[:, None, :]
            o_b = _swa_attention(swa.reshape(b, s, -1), swa_bias, sink_row, q_blk0=0, k_blk0=4, v_blk0=5)
            half = MLA_HEADS * MLA_V
            w_out = jnp.concatenate([o_w_out[i][:half], _swa_head_order(o_w_out[i][half:], axis=0)], axis=0)
            w_out = w_out.astype(BF16)

        wr_hi, wr_lo, b_r = _router_weights(moe_w_group[layer], moe_b_group[layer], moe_w_router[layer],
                                            moe_b_router[layer])
        h, hn, gates, counts, pos_col, pos_row = _out_router(
            h, o_a.reshape(n, -1), o_b.reshape(n, -1), w_out, ffn_norm[layer][None, :], wr_hi, wr_lo, b_r)
        counts = counts[:, 0, :N_GROUPS].reshape(n // MOE_TILE, MOE_TILE // ROUTER_TILE, N_GROUPS).sum(axis=1)
        counts = counts.astype(jnp.int32)
        seg_off, rt_group, rt_valid, n_used = _moe_tables(counts, n)
        counts = counts.reshape(-1)
        xs = _moe_pack(counts, seg_off, hn, gates, pos_row, xs)
        ys = _moe_experts(rt_group, rt_valid, n_used, xs, moe_w_gate, moe_w_up, moe_w_down, layer)
        h = _moe_combine(counts, seg_off, pos_col, h, final_norm[None, :], ys, final_norm=layer == depth - 1)
    return h.reshape(b, s, d)
```

```python
import functools

import numpy as np
import jax
import jax.numpy as jnp
from jax import lax
from jax.experimental import pallas as pl
from jax.experimental.pallas import tpu as pltpu

F32 = jnp.float32
BF16 = jnp.bfloat16

D_MODEL = 1024
HEAD_DIM = 64
BLK = 128
NEG = -1e30
RMS_EPS = 1e-6
FOX_HEADS = 8
DIL_HEADS = 8
DIL_BRANCHES = ((128, 1), (512, 4), (2048, 16))
MLA_HEADS = 8
MLA_Q_RANK = 384
MLA_KV_RANK = 256
MLA_NOPE = 64
MLA_ROPE = 32
MLA_V = 64
ROPE_THETA = 10000.0
SWA_Q_HEADS = 8
SWA_KV_HEADS = 2
SWA_WINDOW = 128
N_GROUPS = 4
EXPERTS_PER_GROUP = 8
N_EXPERTS = N_GROUPS * EXPERTS_PER_GROUP
D_EXPERT = 256

LANES = 128
MXU_TILE = 256

PROJ_EVEN_TILE = 512
PROJ_ODD_TILE = 1024
PROJ_COLS = 2 * MXU_TILE
ATTN_BLOCK = 512
SWA_CHUNK = 512
ROUTER_TILE = 512
MOE_TILE = 1024
VMEM_LIMIT = 56 * 1024 * 1024

NT_DIMS = (((1,), (1,)), ((), ()))
LOG2E = 1.4426950408889634


def _params(*sem):
    return pltpu.CompilerParams(dimension_semantics=sem, vmem_limit_bytes=VMEM_LIMIT)


def _dot(a, b):
    return jnp.dot(a, b, preferred_element_type=F32)


def _rms(x, g):
    return x * lax.rsqrt(jnp.mean(x * x, axis=-1, keepdims=True) + RMS_EPS) * g


def _hi_lo(w):
    hi = w.astype(BF16)
    return hi, (w - hi.astype(F32)).astype(BF16)


def _dot_hi(x, w_hi, w_lo):
    x_hi = x.astype(BF16)
    x_lo = (x - x_hi.astype(F32)).astype(BF16)
    n = w_hi.shape[1]
    both = _dot(x_hi, jnp.concatenate([w_hi, w_lo], axis=1))
    return both[:, :n] + (both[:, n:] + _dot(x_lo, w_hi))


def _split3(x):
    x1 = x.astype(BF16)
    r1 = x - x1.astype(F32)
    x2 = r1.astype(BF16)
    x3 = (r1 - x2.astype(F32)).astype(BF16)
    return x1, x2, x3


def _proj_even_kernel(x_ref, g_ref, w_ref, wf_hi_ref, wf_lo_ref, bf_ref, out_ref, c_ref, carry_ref, tri_ref, *,
                      tiles_per_seq):
    xn = _rms(x_ref[...], g_ref[...])
    xb = xn.astype(BF16)
    n_out = out_ref.shape[1]
    for c in range(0, n_out, PROJ_COLS):
        out_ref[:, c:c + PROJ_COLS] = _dot(xb, w_ref[:, c:c + PROJ_COLS]).astype(out_ref.dtype)
    z = _dot_hi(xn, wf_hi_ref[...], wf_lo_ref[...]) + bf_ref[...]
    logf = jnp.minimum(z, 0.0) - jnp.log1p(jnp.exp(-jnp.abs(z)))

    @pl.when(pl.program_id(0) % tiles_per_seq == 0)
    def _():
        carry_ref[...] = jnp.zeros_like(carry_ref)

    sub = tri_ref.shape[0]

    @pl.when(pl.program_id(0) == 0)
    def _():
        ri = lax.broadcasted_iota(jnp.int32, (sub, sub), 0)
        ci = lax.broadcasted_iota(jnp.int32, (sub, sub), 1)
        tri_ref[...] = jnp.where(ci <= ri, 1.0, 0.0).astype(BF16)

    lower = tri_ref[...]
    carry = carry_ref[...]
    for r in range(0, logf.shape[0], sub):
        l1, l2, l3 = _split3(logf[r:r + sub])
        c = (_dot(lower, l1) + (_dot(lower, l2) + _dot(lower, l3))) + carry
        c_ref[r:r + sub, :] = c
        carry = c[sub - 1:sub, :]
    carry_ref[...] = carry


def _proj_even(h, g, w, wf_hi, wf_lo, bf, seq):
    n, d = h.shape
    tm = PROJ_EVEN_TILE
    n_out = w.shape[1]
    return pl.pallas_call(
        functools.partial(_proj_even_kernel, tiles_per_seq=seq // tm),
        grid=(n // tm,),
        in_specs=[
            pl.BlockSpec((tm, d), lambda i: (i, 0)),
            pl.BlockSpec((1, d), lambda i: (0, 0)),
            pl.BlockSpec((d, n_out), lambda i: (0, 0)),
            pl.BlockSpec((d, LANES), lambda i: (0, 0)),
            pl.BlockSpec((d, LANES), lambda i: (0, 0)),
            pl.BlockSpec((1, LANES), lambda i: (0, 0)),
        ],
        out_specs=[
            pl.BlockSpec((tm, n_out), lambda i: (i, 0)),
            pl.BlockSpec((tm, LANES), lambda i: (i, 0)),
        ],
        out_shape=[
            jax.ShapeDtypeStruct((n, n_out), BF16),
            jax.ShapeDtypeStruct((n, LANES), F32),
        ],
        scratch_shapes=[pltpu.VMEM((1, LANES), F32), pltpu.VMEM((MXU_TILE, MXU_TILE), BF16)],
        compiler_params=_params("arbitrary"),
        name="proj_even",
    )(h, g, w, wf_hi, wf_lo, bf)


def _own_lanes(lane, h):
    return lane < HEAD_DIM if h == 0 else lane >= HEAD_DIM


def _causal_kernel(*refs, fox, tq, tk):
    if fox:
        q_ref, k_ref, v_ref, c_ref, o_ref, vaug_t, s_a0, s_a1, s_b0, s_b1, kaug = refs
    else:
        q_ref, k_ref, v_ref, o_ref, vaug_t, s_a0, s_a1, s_b0, s_b1 = refs
    pair = pl.program_id(1)
    seq = v_ref.shape[1]

    lane = lax.broadcasted_iota(jnp.int32, (tk, LANES), 1)
    feat = lax.broadcasted_iota(jnp.int32, (LANES, tk), 0)

    def fill(t, carry):
        rows = pl.ds(pl.multiple_of(t * tk, tk), tk)
        vt = jnp.transpose(v_ref[0, rows, :].astype(F32))
        if fox:
            kp = k_ref[0, rows, :].astype(F32)
            c = c_ref[0, rows, :]
        for h in range(2):
            vaug_t[h, t] = jnp.where(_own_lanes(feat, h), vt, 1.0).astype(BF16)
            if fox:
                ch = jnp.sum(jnp.where(lane == 2 * pair + h, c, 0.0), axis=-1, keepdims=True)
                c1, c2, c3 = _split3(ch * (-LOG2E))
                base = HEAD_DIM if h == 0 else 0
                extra = jnp.where(lane == base, c1.astype(F32),
                                  jnp.where(lane == base + 1, c2.astype(F32),
                                            jnp.where(lane == base + 2, c3.astype(F32), 0.0)))
                kaug[h, rows, :] = jnp.where(_own_lanes(lane, h), kp, extra).astype(BF16)
        return carry

    lax.fori_loop(0, seq // tk, fill, 0)

    lane_q = lax.broadcasted_iota(jnp.int32, (tq, LANES), 1)

    def queries(qi):
        rows = slice(qi * tq, (qi + 1) * tq)
        qs = []
        for h in range(2):
            if fox:
                base = HEAD_DIM if h == 0 else 0
                ones = jnp.where((lane_q >= base) & (lane_q < base + 3), 1.0, 0.0)
                qs.append(jnp.where(_own_lanes(lane_q, h), q_ref[0, rows, :].astype(F32), ones).astype(BF16))
            else:
                qs.append(q_ref[0, rows, h * LANES:(h + 1) * LANES])
        return qs

    key = lax.broadcasted_iota(jnp.int32, (tk, tq), 0)
    qry = lax.broadcasted_iota(jnp.int32, (tk, tq), 1)

    def scores(j, qs, buf):
        for h in range(2):
            if fox:
                kj = kaug[h, j * tk:(j + 1) * tk, :]
            else:
                kj = k_ref[0, j * tk:(j + 1) * tk, h * LANES:(h + 1) * LANES]
            buf[h] = lax.dot_general(kj, qs[h], NT_DIMS, preferred_element_type=F32)

    def absorb(j, buf, carry, masked):
        new = []
        for h in range(2):
            m, acc = carry[h]
            s = buf[h]
            if masked:
                s = jnp.where(key <= qry, s, NEG)
            m_new = jnp.maximum(m, jnp.max(s, axis=0, keepdims=True))
            p = jnp.exp2(s - m_new)
            acc = jnp.exp2(m - m_new) * acc + _dot(vaug_t[h, j], p.astype(BF16))
            new.append((m_new, acc))
        return tuple(new)

    def finish(carry, qi):
        (_, acc0), (_, acc1) = carry
        feat_q = lax.broadcasted_iota(jnp.int32, (LANES, tq), 0)
        out_t = jnp.where(feat_q < HEAD_DIM, acc0 / acc0[HEAD_DIM:HEAD_DIM + 1, :], acc1 / acc1[0:1, :])
        o_ref[0, qi * tq:(qi + 1) * tq, :] = jnp.transpose(out_t).astype(o_ref.dtype)

    for qi in range(seq // tq):
        bufs = (s_a0, s_a1) if qi % 2 == 0 else (s_b0, s_b1)
        qs = queries(qi)
        carry = tuple((jnp.full((1, tq), NEG, F32), jnp.zeros((LANES, tq), F32)) for _ in range(2))
        scores(0, qs, bufs[0])
        for j in range(qi + 1):
            if j < qi:
                scores(j + 1, qs, bufs[(j + 1) % 2])
            carry = absorb(j, bufs[j % 2], carry, j == qi)
        finish(carry, qi)


def _causal_attention(q_arr, k_arr, v_arr, c_arr, *, q_blk0, k_blk0, v_blk0, n_pairs):
    b, s, _ = q_arr.shape
    tq = ATTN_BLOCK
    fox = c_arr is not None
    qk_w = LANES if fox else 2 * LANES
    in_specs = [
        pl.BlockSpec((1, s, qk_w), lambda bi, p: (bi, 0, q_blk0 + p)),
        pl.BlockSpec((1, s, qk_w), lambda bi, p: (bi, 0, k_blk0 + p)),
        pl.BlockSpec((1, s, LANES), lambda bi, p: (bi, 0, v_blk0 + p)),
    ]
    args = [q_arr, k_arr, v_arr]
    scratch = [pltpu.VMEM((2, s // tq, LANES, tq), BF16)] + [pltpu.VMEM((2, tq, tq), F32) for _ in range(4)]
    if fox:
        in_specs.append(pl.BlockSpec((1, s, LANES), lambda bi, p: (bi, 0, 0)))
        args.append(c_arr)
        scratch.append(pltpu.VMEM((2, s, LANES), BF16))
    return pl.pallas_call(
        functools.partial(_causal_kernel, fox=fox, tq=tq, tk=tq),
        grid=(b, n_pairs),
        in_specs=in_specs,
        out_specs=pl.BlockSpec((1, s, LANES), lambda bi, p: (bi, 0, p)),
        out_shape=jax.ShapeDtypeStruct((b, s, n_pairs * LANES), BF16),
        scratch_shapes=scratch,
        compiler_params=_params("parallel", "parallel"),
        name="causal_attention",
    )(*args)


DIL_PAD = BLK * max(d for _, d in DIL_BRANCHES)
DIL_UNROLL = 8


def _dilated_kernel(q_ref, k_ref, v_ref, bias_ref, o_ref, qf, kf, vf, acc_s, m_s, l_s, s_even, s_odd, *, seq):
    qf[...] = q_ref[0].astype(F32)
    kf[0:DIL_PAD, :] = jnp.zeros((DIL_PAD, LANES), F32)
    vf[0:DIL_PAD, :] = jnp.zeros((DIL_PAD, LANES), F32)
    kf[DIL_PAD:, :] = k_ref[0].astype(F32)
    vf[DIL_PAD:, :] = v_ref[0].astype(F32)
    head0 = lax.broadcasted_iota(jnp.int32, (BLK, LANES), 1) < HEAD_DIM
    head0_k = lax.broadcasted_iota(jnp.int32, (2 * BLK, LANES), 1) < HEAD_DIM
    ones0 = jnp.where(head0_k, 1.0, 0.0).astype(BF16)
    ones1 = jnp.where(head0_k, 0.0, 1.0).astype(BF16)

    def place(dil, u):
        bi = u // dil
        q_start = (u % dil) + (dil * BLK) * bi
        return bi, q_start, q_start + (DIL_PAD - dil * BLK)

    def scores(br, dil, g, buf):
        for i in range(DIL_UNROLL):
            bi, q_start, k_start = place(dil, g * DIL_UNROLL + i)
            q2 = qf[pl.ds(q_start, BLK, stride=dil), :]
            q_st = jnp.concatenate([jnp.where(head0, q2, 0.0), jnp.where(head0, 0.0, q2)], axis=0).astype(BF16)
            kb = kf[pl.ds(k_start, 2 * BLK, stride=dil), :].astype(BF16)
            s = lax.dot_general(q_st, kb, NT_DIMS, preferred_element_type=F32)
            buf[i] = s + bias_ref[br, 0, jnp.where(bi == 0, 1, 0)]

    def absorb(br, dil, g, buf, first):
        for i in range(DIL_UNROLL):
            _, q_start, k_start = place(dil, g * DIL_UNROLL + i)
            s = buf[i]
            m = jnp.max(s, axis=-1, keepdims=True)
            p = jnp.exp2(s - m).astype(BF16)
            p_cat = jnp.concatenate([p[:BLK], p[BLK:]], axis=1)
            v2 = vf[pl.ds(k_start, 2 * BLK, stride=dil), :]
            rhs = jnp.concatenate([
                jnp.concatenate([jnp.where(head0_k, v2, 0.0).astype(BF16), ones0], axis=1),
                jnp.concatenate([jnp.where(head0_k, 0.0, v2).astype(BF16), ones1], axis=1)], axis=0)
            acc2 = _dot(p_cat, rhs)
            acc_b, l_b = acc2[:, :LANES], acc2[:, LANES:]
            m_b = jnp.where(head0, jnp.broadcast_to(m[:BLK], (BLK, LANES)), jnp.broadcast_to(m[BLK:], (BLK, LANES)))
            idx = pl.ds(q_start, BLK, stride=dil)
            if first:
                m_s[idx, :] = m_b
                l_s[idx, :] = l_b
                acc_s[idx, :] = acc_b
                continue
            m_old = m_s[idx, :]
            m_new = jnp.maximum(m_old, m_b)
            a_old = jnp.exp2(m_old - m_new)
            a_b = jnp.exp2(m_b - m_new)
            m_s[idx, :] = m_new
            l_s[idx, :] = a_old * l_s[idx, :] + a_b * l_b
            acc_s[idx, :] = a_old * acc_s[idx, :] + a_b * acc_b

    n_groups = seq // (BLK * DIL_UNROLL)
    order = sorted(range(len(DIL_BRANCHES)), key=lambda i: -DIL_BRANCHES[i][1])
    for pos, br in enumerate(order):
        dil = DIL_BRANCHES[br][1]
        first = pos == 0

        def trip(t, carry, br=br, dil=dil, first=first):
            scores(br, dil, 2 * t + 1, s_odd)
            absorb(br, dil, 2 * t, s_even, first)
            scores(br, dil, 2 * t + 2, s_even)
            absorb(br, dil, 2 * t + 1, s_odd, first)
            return carry

        scores(br, dil, 0, s_even)
        lax.fori_loop(0, n_groups // 2 - 1, trip, 0)
        scores(br, dil, n_groups - 1, s_odd)
        absorb(br, dil, n_groups - 2, s_even, first)
        absorb(br, dil, n_groups - 1, s_odd, first)

    o_ref[0] = (acc_s[...] / l_s[...]).astype(o_ref.dtype)


def _dilated_attention(proj, bias, *, q_blk0, k_blk0, v_blk0):
    b, s, _ = proj.shape
    n_pairs = DIL_HEADS // 2
    assert (s // (BLK * DIL_UNROLL)) % 2 == 0
    blk = lambda off: pl.BlockSpec((1, s, LANES), lambda bi, p: (bi, 0, off + p))
    state = [pltpu.VMEM((s, LANES), F32) for _ in range(4)]
    padded = [pltpu.VMEM((DIL_PAD + s, LANES), F32) for _ in range(2)]
    score_bufs = [pltpu.VMEM((DIL_UNROLL, 2 * BLK, 2 * BLK), F32) for _ in range(2)]
    return pl.pallas_call(
        functools.partial(_dilated_kernel, seq=s),
        grid=(b, n_pairs),
        in_specs=[blk(q_blk0), blk(k_blk0), blk(v_blk0),
                  pl.BlockSpec((len(DIL_BRANCHES), 1, 2, 2 * BLK, 2 * BLK), lambda bi, p: (0, p, 0, 0, 0))],
        out_specs=pl.BlockSpec((1, s, LANES), lambda bi, p: (bi, 0, p)),
        out_shape=jax.ShapeDtypeStruct((b, s, n_pairs * LANES), BF16),
        scratch_shapes=[state[0], padded[0], padded[1], state[1], state[2], state[3]] + score_bufs,
        compiler_params=_params("parallel", "parallel"),
        name="dilated_attention",
    )(proj, proj, proj, bias)


def _swa_kernel(q_ref, k_ref, v_ref, bias_ref, sink_ref, o_ref, vaug_t, s_even, s_odd, *, tc):
    ci = pl.program_id(1)
    seq = k_ref.shape[1]
    grp = SWA_Q_HEADS // SWA_KV_HEADS
    n_blk = tc // BLK

    @pl.when(ci == 0)
    def _():
        feat = lax.broadcasted_iota(jnp.int32, (LANES, BLK), 0)

        def fill(t, carry):
            vt = jnp.transpose(v_ref[0, pl.ds(pl.multiple_of(t * BLK, BLK), BLK), :].astype(F32))
            for kv in range(SWA_KV_HEADS):
                vaug_t[kv, t] = jnp.where(_own_lanes(feat, kv), vt, 1.0).astype(BF16)
            return carry

        lax.fori_loop(0, seq // BLK, fill, 0)

    lane_q = lax.broadcasted_iota(jnp.int32, (BLK, LANES), 1)
    feat_o = lax.broadcasted_iota(jnp.int32, (LANES, grp * BLK), 0)

    def blocks(bb):
        own = ci * n_blk + bb
        return jnp.maximum(own - 1, 0), own

    def scores(bb, kv, buf):
        prev, own = blocks(bb)
        k2 = jnp.concatenate([k_ref[0, pl.ds(pl.multiple_of(prev * BLK, BLK), BLK), :],
                              k_ref[0, pl.ds(pl.multiple_of(own * BLK, BLK), BLK), :]], axis=0)
        q4 = jnp.concatenate(
            [jnp.where(_own_lanes(lane_q, kv), q_ref[0, bb * BLK:(bb + 1) * BLK, g * LANES:(g + 1) * LANES].astype(F32),
                       0.0) for g in range(grp)], axis=0).astype(BF16)
        s = lax.dot_general(k2, q4, NT_DIMS, preferred_element_type=F32)
        buf[...] = s + bias_ref[kv, jnp.where(own == 0, 1, 0)]

    def absorb(bb, kv, buf):
        prev, own = blocks(bb)
        s = buf[...]
        sink = sink_ref[kv]
        m = jnp.maximum(jnp.max(s, axis=0, keepdims=True), sink)
        p = jnp.exp2(s - m).astype(BF16)
        v2 = jnp.concatenate([vaug_t[kv, prev], vaug_t[kv, own]], axis=1)
        acc = _dot(v2, p)
        den = (acc[HEAD_DIM:HEAD_DIM + 1, :] if kv == 0 else acc[0:1, :]) + jnp.exp2(sink - m)
        return acc / den

    units = [(bb, kv) for bb in range(n_blk) for kv in range(SWA_KV_HEADS)]
    bufs = (s_even, s_odd)
    scores(*units[0], bufs[0])
    outs = {}
    for n, unit in enumerate(units):
        if n + 1 < len(units):
            scores(*units[n + 1], bufs[(n + 1) % 2])
        outs[unit] = absorb(*unit, bufs[n % 2])
    for bb in range(n_blk):
        out_t = jnp.where(feat_o < HEAD_DIM, outs[(bb, 0)], outs[(bb, 1)])
        for g in range(grp):
            o_ref[0, bb * BLK:(bb + 1) * BLK, g * LANES:(g + 1) * LANES] = jnp.transpose(
                out_t[:, g * BLK:(g + 1) * BLK]).astype(o_ref.dtype)


def _swa_attention(proj, bias, sink_row, *, q_blk0, k_blk0, v_blk0):
    b, s, _ = proj.shape
    tc = SWA_CHUNK
    qw = SWA_Q_HEADS * HEAD_DIM
    grp = SWA_Q_HEADS // SWA_KV_HEADS
    return pl.pallas_call(
        functools.partial(_swa_kernel, tc=tc),
        grid=(b, s // tc),
        in_specs=[
            pl.BlockSpec((1, tc, qw), lambda bi, ci: (bi, ci, q_blk0 * LANES // qw)),
            pl.BlockSpec((1, s, LANES), lambda bi, ci: (bi, 0, k_blk0)),
            pl.BlockSpec((1, s, LANES), lambda bi, ci: (bi, 0, v_blk0)),
            pl.BlockSpec(bias.shape, lambda bi, ci: (0, 0, 0, 0)),
            pl.BlockSpec(sink_row.shape, lambda bi, ci: (0, 0, 0)),
        ],
        out_specs=pl.BlockSpec((1, tc, qw), lambda bi, ci: (bi, ci, 0)),
        out_shape=jax.ShapeDtypeStruct((b, s, qw), BF16),
        scratch_shapes=[pltpu.VMEM((SWA_KV_HEADS, s // BLK, LANES, BLK), BF16),
                        pltpu.VMEM((2 * BLK, grp * BLK), F32), pltpu.VMEM((2 * BLK, grp * BLK), F32)],
        compiler_params=_params("parallel", "arbitrary"),
        name="swa_attention",
    )(proj, proj, proj, bias, sink_row)


def _rope128(x, a, bm, bp):
    return x * a + pltpu.roll(x, LANES - 16, axis=1) * bm + pltpu.roll(x, 16, axis=1) * bp


def _proj_odd_kernel(x_ref, g_ref, w_ref, qn_ref, kvn_ref, wuq_ref, wuk_ref, wuv_ref, a_ref, bm_ref, bp_ref,
                     q_out, k_out, v_out, swa_out):
    xb = _rms(x_ref[...], g_ref[...]).astype(BF16)
    o1 = MLA_Q_RANK + LANES
    o2 = o1 + MLA_KV_RANK
    cq_kpe = _dot(xb, w_ref[:, 0:o1])
    c_q, kpe = cq_kpe[:, :MLA_Q_RANK], cq_kpe[:, MLA_Q_RANK:]
    c_kv = _dot(xb, w_ref[:, o1:o2])
    swa_out[...] = _dot(xb, w_ref[:, o2:]).astype(swa_out.dtype)
    a, bm, bp = a_ref[...], bm_ref[...], bp_ref[...]
    kpe = _rope128(kpe, a, bm, bp)
    cqn = _rms(c_q, qn_ref[...]).astype(BF16)
    ckvn = _rms(c_kv, kvn_ref[...]).astype(BF16)
    v_out[...] = _dot(ckvn, wuv_ref[...]).astype(v_out.dtype)
    q_raw = _dot(cqn, wuq_ref[...])
    k_raw = _dot(ckvn, wuk_ref[...])
    for h in range(MLA_HEADS):
        sl = slice(h * LANES, (h + 1) * LANES)
        q_out[:, sl] = _rope128(q_raw[:, sl], a, bm, bp).astype(q_out.dtype)
        k_out[:, sl] = (k_raw[:, sl] + kpe).astype(k_out.dtype)


def _proj_odd(h, g, w, qn, kvn, wuq, wuk, wuv, rope_a, rope_bm, rope_bp, seq):
    n, d = h.shape
    tm = PROJ_ODD_TILE
    n_in = w.shape[1]
    n_swa = n_in - (MLA_Q_RANK + MLA_KV_RANK + LANES)
    tiles_per_seq = seq // tm
    full = lambda shape: pl.BlockSpec(shape, lambda i: (0, 0))
    rope = pl.BlockSpec((tm, LANES), lambda i: (i % tiles_per_seq, 0))
    return pl.pallas_call(
        _proj_odd_kernel,
        grid=(n // tm,),
        in_specs=[
            pl.BlockSpec((tm, d), lambda i: (i, 0)), full((1, d)), full(w.shape),
            full(qn.shape), full(kvn.shape), full(wuq.shape), full(wuk.shape), full(wuv.shape),
            rope, rope, rope,
        ],
        out_specs=[
            pl.BlockSpec((tm, MLA_HEADS * LANES), lambda i: (i, 0)),
            pl.BlockSpec((tm, MLA_HEADS * LANES), lambda i: (i, 0)),
            pl.BlockSpec((tm, MLA_HEADS * MLA_V), lambda i: (i, 0)),
            pl.BlockSpec((tm, n_swa), lambda i: (i, 0)),
        ],
        out_shape=[
            jax.ShapeDtypeStruct((n, MLA_HEADS * LANES), BF16),
            jax.ShapeDtypeStruct((n, MLA_HEADS * LANES), BF16),
            jax.ShapeDtypeStruct((n, MLA_HEADS * MLA_V), BF16),
            jax.ShapeDtypeStruct((n, n_swa), BF16),
        ],
        compiler_params=_params("parallel"),
        name="proj_odd",
    )(h, g, w, qn, kvn, wuq, wuk, wuv, rope_a, rope_bm, rope_bp)


def _out_router_kernel(h_ref, oa_ref, ob_ref, wo_ref, g_ref, wr_hi_ref, wr_lo_ref, br_ref,
                       h_out, hn_out, gates_out, counts_out, pos_col_out, pos_row_out, tri_ref, seen_ref):
    i = pl.program_id(0)
    tm = h_ref.shape[0]
    half = oa_ref.shape[1]

    @pl.when(i == 0)
    def _():
        ri = lax.broadcasted_iota(jnp.int32, (tm, tm), 0)
        ci = lax.broadcasted_iota(jnp.int32, (tm, tm), 1)
        tri_ref[...] = jnp.where(ci < ri, 1.0, 0.0).astype(BF16)

    @pl.when(i % (MOE_TILE // tm) == 0)
    def _():
        seen_ref[...] = jnp.zeros_like(seen_ref)

    lane = lax.broadcasted_iota(jnp.int32, (MXU_TILE, LANES), 1)
    big = jnp.int32(LANES)
    is_grp = (lane >= N_EXPERTS) & (lane < N_EXPERTS + N_GROUPS)
    routed_parts = []
    for r in range(0, tm, MXU_TILE):
        rows = slice(r, r + MXU_TILE)
        h = h_ref[rows, :] + _dot(oa_ref[rows, :], wo_ref[0:half, :]) + _dot(ob_ref[rows, :], wo_ref[half:, :])
        h_out[rows, :] = h
        hn = _rms(h, g_ref[...])
        hn_out[rows, :] = hn.astype(hn_out.dtype)
        z = _dot_hi(hn, wr_hi_ref[...], wr_lo_ref[...]) + br_ref[...]
        zg = jnp.where(is_grp, z, -jnp.inf)
        g_max = jnp.max(zg, axis=-1, keepdims=True)
        g_w = 1.0 / jnp.sum(jnp.exp(zg - g_max), axis=-1, keepdims=True)
        g_idx = jnp.min(jnp.where(zg == g_max, lane - N_EXPERTS, big), axis=-1, keepdims=True)
        in_grp = (lane < N_EXPERTS) & ((lane // EXPERTS_PER_GROUP) == g_idx)
        ze = jnp.where(in_grp, z, -jnp.inf)
        v1 = jnp.max(ze, axis=-1, keepdims=True)
        i1 = jnp.min(jnp.where(ze == v1, lane, big), axis=-1, keepdims=True)
        ze2 = jnp.where(lane == i1, -jnp.inf, ze)
        v2 = jnp.max(ze2, axis=-1, keepdims=True)
        i2 = jnp.min(jnp.where(ze2 == v2, lane, big), axis=-1, keepdims=True)
        e2 = jnp.exp(v2 - v1)
        w1 = g_w / (1.0 + e2)
        w2 = g_w * e2 / (1.0 + e2)
        gates_out[rows, :] = jnp.where(lane == i1, w1, 0.0) + jnp.where(lane == i2, w2, 0.0)
        routed_parts.append(jnp.where((lane == g_idx) & (lane < N_GROUPS), 1.0, 0.0))

    routed = jnp.concatenate(routed_parts, axis=0)
    pos = jnp.where(routed > 0.0, _dot(tri_ref[...], routed.astype(BF16)) + seen_ref[...], -1.0)
    pos_col_out[...] = pos
    pos_row_out[0] = jnp.transpose(pos)[0:8, :]
    here = jnp.sum(routed, axis=0, keepdims=True)
    seen_ref[...] += here
    counts_out[0] = jnp.broadcast_to(here, counts_out.shape[1:])


def _out_router(h, oa, ob, wo, g, wr_hi, wr_lo, br):
    n, d = h.shape
    tm = ROUTER_TILE
    per_tile = MOE_TILE // tm
    half = oa.shape[1]
    full = lambda shape: pl.BlockSpec(shape, lambda i: (0, 0))
    tile = lambda w: pl.BlockSpec((tm, w), lambda i: (i, 0))
    return pl.pallas_call(
        _out_router_kernel,
        grid=(n // tm,),
        in_specs=[tile(d), tile(half), tile(half), full(wo.shape), full((1, d)),
                  full(wr_hi.shape), full(wr_lo.shape), full((1, LANES))],
        out_specs=[tile(d), tile(d), tile(LANES), pl.BlockSpec((1, 8, LANES), lambda i: (i, 0, 0)),
                   tile(LANES), pl.BlockSpec((1, 8, tm), lambda i: (i // per_tile, 0, i % per_tile))],
        out_shape=[
            jax.ShapeDtypeStruct((n, d), F32),
            jax.ShapeDtypeStruct((n, d), BF16),
            jax.ShapeDtypeStruct((n, LANES), F32),
            jax.ShapeDtypeStruct((n // tm, 8, LANES), F32),
            jax.ShapeDtypeStruct((n, LANES), F32),
            jax.ShapeDtypeStruct((n // MOE_TILE, 8, MOE_TILE), F32),
        ],
        scratch_shapes=[pltpu.VMEM((tm, tm), BF16), pltpu.VMEM((1, LANES), F32)],
        compiler_params=_params("arbitrary"),
        name="out_router",
    )(h, oa, ob, wo, g, wr_hi, wr_lo, br)


MOE_CHUNK = 320
MOE_ROWS = 1024
MOE_ALIGN = 16
MOE_EXPERTS_PER_STEP = 4


def _moe_sorted_tiles(n):
    n_seg = (n // MOE_TILE) * N_GROUPS
    rows = n + n_seg * (MOE_ALIGN - 1) + N_GROUPS * (MOE_CHUNK + MOE_ROWS - 1)
    return -(-rows // MOE_ROWS)


def _moe_tables(counts, n):
    n_rt = _moe_sorted_tiles(n)
    seg_len = (counts + (MOE_ALIGN - 1)) // MOE_ALIGN * MOE_ALIGN
    group_len = seg_len.sum(axis=0)
    group_span = (group_len + MOE_CHUNK + MOE_ROWS - 1) // MOE_ROWS * MOE_ROWS
    group_end = jnp.cumsum(group_span)
    group_start = group_end - group_span
    seg_off = group_start[None, :] + jnp.cumsum(seg_len, axis=0) - seg_len
    rt_start = jnp.arange(n_rt, dtype=jnp.int32) * MOE_ROWS
    rt_group = jnp.minimum(jnp.sum(rt_start[:, None] >= group_end[None, :], axis=1), N_GROUPS - 1)
    rt_valid = jnp.clip(group_start[rt_group] + group_len[rt_group] - rt_start, 0, MOE_ROWS)
    n_used = (group_end[-1] // MOE_ROWS).reshape(1)
    i32 = lambda a: a.astype(jnp.int32)
    return i32(seg_off.reshape(-1)), i32(rt_group), i32(rt_valid), i32(n_used)


def _moe_pack_kernel(counts_ref, off_ref, hn_ref, gates_ref, pos_ref, xs_in, xs_ref, xbuf, sem):
    del xs_in
    i = pl.program_id(0)
    d = hn_ref.shape[1]

    def write(grp, dst):
        return pltpu.make_async_copy(xbuf.at[grp], xs_ref.at[pl.ds(dst, MOE_CHUNK)], sem.at[grp])

    @pl.when(i > 0)
    def _():
        for grp in range(N_GROUPS):
            write(grp, 0).wait()

    lane = lax.broadcasted_iota(jnp.int32, (hn_ref.shape[0], LANES), 1)
    gates = gates_ref[...]
    gates_hi = gates.astype(BF16).astype(F32)
    gates_lo = gates - gates_hi
    chunks = []
    for grp in range(N_GROUPS):
        seg = i * N_GROUPS + grp
        first = grp * EXPERTS_PER_GROUP
        g_hi = gates_hi if first == 0 else pltpu.roll(gates_hi, LANES - first, axis=1)
        g_lo = pltpu.roll(gates_lo, (LANES - first + EXPERTS_PER_GROUP) % LANES, axis=1)
        gate_cols = jnp.where(lane < EXPERTS_PER_GROUP, g_hi, jnp.where(lane < 2 * EXPERTS_PER_GROUP, g_lo, 0.0))
        gate_cols = gate_cols.astype(BF16)
        pos_row = pos_ref[0, grp:grp + 1, :]

        def chunk(first_pos, grp=grp, seg=seg, gate_cols=gate_cols, pos_row=pos_row):
            slot_pos = lax.broadcasted_iota(jnp.int32, (MOE_CHUNK, 1), 0).astype(F32) + first_pos
            gather = jnp.where(pos_row == slot_pos, 1.0, 0.0).astype(BF16)
            xbuf[grp, :, 0:d] = _dot(gather, hn_ref[...]).astype(BF16)
            xbuf[grp, :, d:] = _dot(gather, gate_cols).astype(BF16)
            dst = off_ref[seg] + first_pos.astype(jnp.int32)
            write(grp, pl.multiple_of(dst, MOE_ALIGN)).start()

        chunk(jnp.float32(0.0))
        chunks.append(chunk)

    for grp in range(N_GROUPS):
        def more(k, carry, grp=grp):
            write(grp, 0).wait()
            chunks[grp]((k * MOE_CHUNK).astype(F32))
            return carry

        count = counts_ref[i * N_GROUPS + grp]
        lax.fori_loop(1, (count + (MOE_CHUNK - 1)) // MOE_CHUNK, more, 0)

    @pl.when(i == pl.num_programs(0) - 1)
    def _():
        for grp in range(N_GROUPS):
            write(grp, 0).wait()


def _moe_pack(counts, seg_off, hn, gates, pos_row, xs_init):
    n, d = hn.shape
    rows = xs_init.shape[0]
    grid_spec = pltpu.PrefetchScalarGridSpec(
        num_scalar_prefetch=2,
        grid=(n // MOE_TILE,),
        in_specs=[
            pl.BlockSpec((MOE_TILE, d), lambda i, c, o: (i, 0)),
            pl.BlockSpec((MOE_TILE, LANES), lambda i, c, o: (i, 0)),
            pl.BlockSpec((1, 8, MOE_TILE), lambda i, c, o: (i, 0, 0)),
            pl.BlockSpec(memory_space=pl.ANY),
        ],
        out_specs=pl.BlockSpec(memory_space=pl.ANY),
        scratch_shapes=[pltpu.VMEM((N_GROUPS, MOE_CHUNK, d + LANES), BF16), pltpu.SemaphoreType.DMA((N_GROUPS,))],
    )
    return pl.pallas_call(
        _moe_pack_kernel,
        grid_spec=grid_spec,
        out_shape=jax.ShapeDtypeStruct((rows, d + LANES), BF16),
        input_output_aliases={5: 0},
        compiler_params=_params("arbitrary"),
        name="moe_pack",
    )(counts, seg_off, hn, gates, pos_row, xs_init)


def _moe_experts_kernel(grp_ref, valid_ref, used_ref, xs_ref, wg_ref, wu_ref, wd_ref, ys_ref, acc_ref,
                        wg_b, wu_b, wd_b):
    del grp_ref, used_ref
    rt = pl.program_id(0)
    j = pl.program_id(1)
    d = ys_ref.shape[1]
    valid = valid_ref[rt]

    @pl.when((valid > 0) & (j == 0))
    def _():
        acc_ref[...] = jnp.zeros_like(acc_ref)

    def experts(x, gate_cols, w_gate, w_up, w_down):
        lane = lax.broadcasted_iota(jnp.int32, gate_cols.shape, 1)
        acts = []
        for e in range(MOE_EXPERTS_PER_STEP):
            idx = j * MOE_EXPERTS_PER_STEP + e
            mine = (lane == idx) | (lane == idx + EXPERTS_PER_GROUP)
            gate = jnp.sum(jnp.where(mine, gate_cols, 0.0), axis=-1, keepdims=True)
            a = _dot(x, w_gate(e))
            u = _dot(x, w_up(e))
            acts.append(((a * jax.nn.sigmoid(a)) * u * gate).astype(BF16))
        return _dot(jnp.concatenate(acts, axis=1), w_down())

    @pl.when(valid == MOE_ROWS)
    def _():
        acc_ref[...] += experts(
            xs_ref[:, 0:d], xs_ref[:, d:].astype(F32), lambda e: wg_ref[0, 0, e].astype(BF16),
            lambda e: wu_ref[0, 0, e].astype(BF16), lambda: wd_ref[0, 0].astype(BF16).reshape(wd_b.shape))

    @pl.when((valid > 0) & (valid < MOE_ROWS))
    def _():
        wg_b[...] = wg_ref[0, 0].astype(BF16)
        wu_b[...] = wu_ref[0, 0].astype(BF16)
        wd_b[...] = wd_ref[0, 0].astype(BF16).reshape(wd_b.shape)

        def rows_block(t, carry):
            rows = pl.ds(pl.multiple_of(t * MXU_TILE, MXU_TILE), MXU_TILE)
            acc_ref[rows, :] += experts(xs_ref[rows, 0:d], xs_ref[rows, d:].astype(F32), lambda e: wg_b[e],
                                        lambda e: wu_b[e], lambda: wd_b[...])
            return carry

        lax.fori_loop(0, (valid + (MXU_TILE - 1)) // MXU_TILE, rows_block, 0)

    @pl.when((valid > 0) & (j == pl.num_programs(1) - 1))
    def _():
        row = lax.broadcasted_iota(jnp.int32, (acc_ref.shape[0], 1), 0)
        ys_ref[...] = jnp.where(row < valid, acc_ref[...], 0.0).astype(ys_ref.dtype)

    @pl.when((valid == 0) & (j == pl.num_programs(1) - 1))
    def _():
        ys_ref[...] = jnp.zeros_like(ys_ref)


def _moe_experts(rt_group, rt_valid, n_used, xs, w_gate, w_up, w_down, layer):
    rows = xs.shape[0]
    d = w_gate.shape[-2]
    n_rt = rows // MOE_ROWS
    eps = MOE_EXPERTS_PER_STEP
    n_steps = EXPERTS_PER_GROUP // eps

    def tile_idx(rt, j, grp, valid, used):
        return jnp.minimum(rt, used[0] - 1)

    def w_idx(rt, j, grp, valid, used):
        live = rt < used[0]
        return (layer, grp[tile_idx(rt, j, grp, valid, used)], jnp.where(live, j, n_steps - 1), 0, 0)

    grid_spec = pltpu.PrefetchScalarGridSpec(
        num_scalar_prefetch=3,
        grid=(n_rt, n_steps),
        in_specs=[
            pl.BlockSpec((MOE_ROWS, d + LANES), lambda rt, j, grp, valid, used: (tile_idx(rt, j, grp, valid, used), 0)),
            pl.BlockSpec((1, 1, eps, d, D_EXPERT), w_idx),
            pl.BlockSpec((1, 1, eps, d, D_EXPERT), w_idx),
            pl.BlockSpec((1, 1, eps, D_EXPERT, d), w_idx),
        ],
        out_specs=pl.BlockSpec((MOE_ROWS, d), lambda rt, j, grp, valid, used: (rt, 0)),
        scratch_shapes=[pltpu.VMEM((MOE_ROWS, d), F32), pltpu.VMEM((eps, d, D_EXPERT), BF16),
                        pltpu.VMEM((eps, d, D_EXPERT), BF16), pltpu.VMEM((eps * D_EXPERT, d), BF16)],
    )
    return pl.pallas_call(
        _moe_experts_kernel,
        grid_spec=grid_spec,
        out_shape=jax.ShapeDtypeStruct((rows, d), BF16),
        compiler_params=_params("arbitrary", "arbitrary"),
        name="moe_experts",
    )(rt_group, rt_valid, n_used, xs, w_gate, w_up, w_down)


def _moe_combine_kernel(counts_ref, off_ref, pos_ref, h_ref, fg_ref, ys_ref, o_ref, ybuf, ymore, sem, *, final_norm):
    i = pl.program_id(0)

    def read(tile, grp):
        src = pl.multiple_of(off_ref[tile * N_GROUPS + grp], MOE_ALIGN)
        slot = tile % 2
        return pltpu.make_async_copy(ys_ref.at[pl.ds(src, MOE_CHUNK)],
                                     ybuf.at[slot, pl.ds(grp * MOE_CHUNK, MOE_CHUNK)], sem.at[slot, grp])

    def fetch(tile):
        for grp in range(N_GROUPS):
            @pl.when(counts_ref[tile * N_GROUPS + grp] > 0)
            def _(grp=grp):
                read(tile, grp).start()

    @pl.when(i == 0)
    def _():
        ybuf[...] = jnp.zeros_like(ybuf)
        fetch(i)

    @pl.when(i + 1 < pl.num_programs(0))
    def _():
        fetch(i + 1)

    for grp in range(N_GROUPS):
        @pl.when(counts_ref[i * N_GROUPS + grp] > 0)
        def _(grp=grp):
            read(i, grp).wait()

    slot_pos = lax.broadcasted_iota(jnp.int32, (1, MOE_CHUNK), 1).astype(F32)
    for r in range(0, MOE_TILE, MOE_TILE // 2):
        rows = slice(r, r + MOE_TILE // 2)
        scatter = jnp.concatenate(
            [jnp.where(pos_ref[rows, grp:grp + 1] == slot_pos, 1.0, 0.0).astype(BF16) for grp in range(N_GROUPS)],
            axis=1)
        o_ref[rows, :] = h_ref[rows, :] + _dot(scatter, ybuf[i % 2])
    for grp in range(N_GROUPS):
        count = counts_ref[i * N_GROUPS + grp]
        pos_col = pos_ref[:, grp:grp + 1]

        def more(k, carry, grp=grp, pos_col=pos_col):
            src = pl.multiple_of(off_ref[i * N_GROUPS + grp] + k * MOE_CHUNK, MOE_ALIGN)
            pltpu.sync_copy(ys_ref.at[pl.ds(src, MOE_CHUNK)], ymore)
            scatter = jnp.where(pos_col == slot_pos + (k * MOE_CHUNK).astype(F32), 1.0, 0.0).astype(BF16)
            o_ref[...] += _dot(scatter, ymore[...])
            return carry

        lax.fori_loop(1, (count + (MOE_CHUNK - 1)) // MOE_CHUNK, more, 0)

    if final_norm:
        o_ref[...] = _rms(o_ref[...], fg_ref[...])


def _moe_combine(counts, seg_off, pos_col, h, fg, ys, final_norm):
    n, d = h.shape
    grid_spec = pltpu.PrefetchScalarGridSpec(
        num_scalar_prefetch=2,
        grid=(n // MOE_TILE,),
        in_specs=[
            pl.BlockSpec((MOE_TILE, LANES), lambda i, c, o: (i, 0)),
            pl.BlockSpec((MOE_TILE, d), lambda i, c, o: (i, 0)),
            pl.BlockSpec((1, d), lambda i, c, o: (0, 0)),
            pl.BlockSpec(memory_space=pl.ANY),
        ],
        out_specs=pl.BlockSpec((MOE_TILE, d), lambda i, c, o: (i, 0)),
        scratch_shapes=[pltpu.VMEM((2, N_GROUPS * MOE_CHUNK, d), BF16), pltpu.VMEM((MOE_CHUNK, d), BF16),
                        pltpu.SemaphoreType.DMA((2, N_GROUPS))],
    )
    return pl.pallas_call(
        functools.partial(_moe_combine_kernel, final_norm=final_norm),
        grid_spec=grid_spec,
        out_shape=jax.ShapeDtypeStruct((n, d), F32),
        compiler_params=_params("arbitrary"),
        name="moe_combine",
    )(counts, seg_off, pos_col, h, fg, ys)


def _alibi_slopes(n):
    return np.exp2(-8.0 * np.arange(1, n + 1, dtype=np.float64) / n)


def _band_bias(slopes, max_steps, step_dist):
    steps = (np.arange(BLK)[:, None] + BLK) - np.arange(2 * BLK)[None, :]
    in_band = (steps >= 0) & (steps <= max_steps)
    dist = (steps * step_dist).astype(np.float64)
    return np.where(in_band[None], -slopes[:, None, None] * dist[None], NEG)


def _attention_bias_tables():
    own_half = np.arange(2 * BLK) >= BLK
    dil = np.stack([_band_bias(_alibi_slopes(DIL_HEADS), w // dl, dl) for (w, dl) in DIL_BRANCHES]) * LOG2E
    dil = np.stack([dil, np.where(own_half, dil, NEG)], axis=2)
    dil = dil.reshape(len(DIL_BRANCHES), DIL_HEADS // 2, 2, 2, BLK, 2 * BLK)
    dil = dil.transpose(0, 1, 3, 2, 4, 5).reshape(len(DIL_BRANCHES), DIL_HEADS // 2, 2, 2 * BLK, 2 * BLK)
    grp = SWA_Q_HEADS // SWA_KV_HEADS
    swa = _band_bias(_alibi_slopes(SWA_Q_HEADS), SWA_WINDOW - 1, 1) * LOG2E
    swa = swa.reshape(SWA_KV_HEADS, grp, BLK, 2 * BLK).transpose(0, 3, 1, 2).reshape(SWA_KV_HEADS, 2 * BLK, grp * BLK)
    swa = np.stack([swa, np.where(own_half[None, :, None], swa, NEG)], axis=1)
    return dil.astype(np.float32), swa.astype(np.float32)


def _rope_tables(s):
    inv = ROPE_THETA ** (-np.arange(0, MLA_ROPE, 2, dtype=np.float64) / MLA_ROPE)
    ang = np.arange(s, dtype=np.float64)[:, None] * inv[None, :]
    cos, sin = np.cos(ang), np.sin(ang)
    half = MLA_ROPE // 2
    zeros_tail = np.zeros((s, LANES - MLA_NOPE - MLA_ROPE))
    a = np.concatenate([np.ones((s, MLA_NOPE)), cos, cos, zeros_tail], axis=1)
    zeros_nope = np.zeros((s, MLA_NOPE))
    zeros_half = np.zeros((s, half))
    bm = np.concatenate([zeros_nope, -sin, zeros_half, zeros_tail], axis=1)
    bp = np.concatenate([zeros_nope, zeros_half, sin, zeros_tail], axis=1)
    return a.astype(np.float32), bm.astype(np.float32), bp.astype(np.float32)


def _swa_head_order(w, axis):
    grp = SWA_Q_HEADS // SWA_KV_HEADS
    shape = w.shape
    w = w.reshape(shape[:axis] + (SWA_KV_HEADS, grp, HEAD_DIM) + shape[axis + 1:])
    return jnp.swapaxes(w, axis, axis + 1).reshape(shape)


def _pad_cols(w, width):
    return jnp.pad(w, ((0, 0), (0, width - w.shape[1])))


def _router_weights(w_group, b_group, w_router, b_router):
    w = _pad_cols(jnp.concatenate([w_router, w_group], axis=1), LANES)
    b = _pad_cols(jnp.concatenate([b_router, b_group])[None, :], LANES)
    hi, lo = _hi_lo(w)
    return hi, lo, b


def kernel(x, attn_norm, ffn_norm, final_norm, e_w_in, e_b_f, e_w_out, o_w_in, o_q_norm, o_kv_norm, o_w_uq,
           o_w_ukv, o_sinks, o_w_out, moe_w_group, moe_b_group, moe_w_router, moe_b_router, moe_w_gate,
           moe_w_up, moe_w_down):
    b, s, d = x.shape
    n = b * s
    depth = attn_norm.shape[0]
    assert s % (BLK * DIL_BRANCHES[-1][1]) == 0 and d == D_MODEL
    assert all(s % t == 0 for t in (PROJ_EVEN_TILE, PROJ_ODD_TILE, ATTN_BLOCK, SWA_CHUNK, MOE_TILE))
    assert MOE_TILE % ROUTER_TILE == 0
    h = x.reshape(n, d)

    dil_bias, swa_bias = _attention_bias_tables()
    rope_a, rope_bm, rope_bp = _rope_tables(s)

    xs = jnp.zeros((_moe_sorted_tiles(n) * MOE_ROWS, d + LANES), BF16)
    for layer in range(depth):
        i = layer // 2
        g_attn = attn_norm[layer][None, :]
        if layer % 2 == 0:
            w_in = e_w_in[i]
            hq = FOX_HEADS * HEAD_DIM
            scale = HEAD_DIM ** -0.5
            cols = [w_in[:, 0:hq] * (scale * LOG2E), w_in[:, hq:2 * hq], w_in[:, 2 * hq:3 * hq]]
            o = 3 * hq + FOX_HEADS
            cols += [w_in[:, o:o + hq] * (scale * LOG2E), w_in[:, o + hq:o + 2 * hq], w_in[:, o + 2 * hq:o + 3 * hq]]
            w_main = jnp.concatenate(cols, axis=1).astype(BF16)
            wf_hi, wf_lo = _hi_lo(_pad_cols(w_in[:, 3 * hq:o], LANES))
            b_f = _pad_cols(e_b_f[i][None, :], LANES)
            proj, c = _proj_even(h, g_attn, w_main, wf_hi, wf_lo, b_f, s)
            proj = proj.reshape(b, s, -1)
            o_a = _causal_attention(proj, proj, proj, c.reshape(b, s, LANES), q_blk0=0, k_blk0=4, v_blk0=8,
                                    n_pairs=FOX_HEADS // 2)
            o_b = _dilated_attention(proj, dil_bias, q_blk0=12, k_blk0=16, v_blk0=20)
            w_out = e_w_out[i].astype(BF16)
        else:
            w_in = o_w_in[i]
            o1 = MLA_Q_RANK + MLA_KV_RANK
            o2 = o1 + MLA_ROPE
            sq = SWA_Q_HEADS * HEAD_DIM
            kpe_cols = jnp.pad(w_in[:, o1:o2], ((0, 0), (MLA_NOPE, LANES - MLA_NOPE - MLA_ROPE)))
            w_main = jnp.concatenate(
                [w_in[:, :MLA_Q_RANK], kpe_cols, w_in[:, MLA_Q_RANK:o1],
                 _swa_head_order(w_in[:, o2:o2 + sq] * (HEAD_DIM ** -0.5 * LOG2E), axis=1),
                 w_in[:, o2 + sq:]],
                axis=1).astype(BF16)
            dq = MLA_NOPE + MLA_ROPE
            wuq = o_w_uq[i].reshape(MLA_Q_RANK, MLA_HEADS, dq) * (dq ** -0.5 * LOG2E)
            wuq = jnp.pad(wuq, ((0, 0), (0, 0), (0, LANES - dq))).reshape(MLA_Q_RANK, MLA_HEADS * LANES)
            wukv = o_w_ukv[i].reshape(MLA_KV_RANK, MLA_HEADS, MLA_NOPE + MLA_V)
            wuk = jnp.pad(wukv[:, :, :MLA_NOPE], ((0, 0), (0, 0), (0, LANES - MLA_NOPE)))
            wuk = wuk.reshape(MLA_KV_RANK, MLA_HEADS * LANES)
            wuv = wukv[:, :, MLA_NOPE:].reshape(MLA_KV_RANK, MLA_HEADS * MLA_V)
            q_full, k_full, v_mla, swa = _proj_odd(
                h, g_attn, w_main, o_q_norm[i][None, :], o_kv_norm[i][None, :], wuq.astype(BF16),
                wuk.astype(BF16), wuv.astype(BF16), rope_a, rope_bm, rope_bp, s)
            o_a = _causal_attention(q_full.reshape(b, s, -1), k_full.reshape(b, s, -1), v_mla.reshape(b, s, -1),
                                    None, q_blk0=0, k_blk0=0, v_blk0=0, n_pairs=MLA_HEADS // 2)
            grp = SWA_Q_HEADS // SWA_KV_HEADS
            sink_row = jnp.repeat(o_sinks[i].reshape(SWA_KV_HEADS, grp) * LOG2E, BLK, axis=1)[:, None, :]
            o_b = _swa_attention(swa.reshape(b, s, -1), swa_bias, sink_row, q_blk0=0, k_blk0=4, v_blk0=5)
            half = MLA_HEADS * MLA_V
            w_out = jnp.concatenate([o_w_out[i][:half], _swa_head_order(o_w_out[i][half:], axis=0)], axis=0)
            w_out = w_out.astype(BF16)

        wr_hi, wr_lo, b_r = _router_weights(moe_w_group[layer], moe_b_group[layer], moe_w_router[layer],
                                            moe_b_router[layer])
        h, hn, gates, counts, pos_col, pos_row = _out_router(
            h, o_a.reshape(n, -1), o_b.reshape(n, -1), w_out, ffn_norm[layer][None, :], wr_hi, wr_lo, b_r)
        counts = counts[:, 0, :N_GROUPS].reshape(n // MOE_TILE, MOE_TILE // ROUTER_TILE, N_GROUPS).sum(axis=1)
        counts = counts.astype(jnp.int32)
        seg_off, rt_group, rt_valid, n_used = _moe_tables(counts, n)
        counts = counts.reshape(-1)
        xs = _moe_pack(counts, seg_off, hn, gates, pos_row, xs)
        ys = _moe_experts(rt_group, rt_valid, n_used, xs, moe_w_gate, moe_w_up, moe_w_down, layer)
        h = _moe_combine(counts, seg_off, pos_col, h, final_norm[None, :], ys, final_norm=layer == depth - 1)
    return h.reshape(b, s, d)
```
